```python
import jax
import jax.numpy as jnp
from jax import lax
import numpy as np

D_MODEL = 1024
BATCH = 4
SEQ = 4096
DEPTH = 4

GRID_W = 64
CTX_LEN = 256
HEAD_DIM = 64
MIX_WIDTH = D_MODEL
N_MIX_HEADS = MIX_WIDTH // HEAD_DIM
N_DIR = 2
RET_HEADS = N_MIX_HEADS // 4
RET_WIDTH = RET_HEADS * HEAD_DIM
RET_CHUNK = 128
GDN_HEADS = N_MIX_HEADS // 4
GDN_WIDTH = GDN_HEADS * HEAD_DIM
GDN_CHUNK = 64
GDN_CONV = 5
SWA_HEADS = N_MIX_HEADS - RET_HEADS - GDN_HEADS
SWA_KV_HEADS = SWA_HEADS // 4
SWA_GROUP = SWA_HEADS // SWA_KV_HEADS
SWA_WIDTH = SWA_HEADS * HEAD_DIM
SWA_KV_WIDTH = SWA_KV_HEADS * HEAD_DIM
WINDOW = 128
ATTN_BLOCK = 128
ROPE_BASE = 10000.0
RET_COLS = 4 * RET_WIDTH
GDN_COLS = 4 * GDN_WIDTH + 2 * N_DIR * GDN_HEADS
SWA_COLS = SWA_WIDTH + 2 * SWA_KV_WIDTH
IN_COLS = RET_COLS + GDN_COLS + SWA_COLS
DENSE_FF = 2816
N_EXPERTS = 8
TOP_K = 2
EXPERT_FF = 3584
MOE_BLOCK = 128
N_DENSE_LAYERS = (DEPTH + 1) // 2
N_MOE_LAYERS = DEPTH // 2
EPS = 1e-6

kernel_name = 'hybrid_retention_deltanet_swa_moe_dit'


def split_cols(p, sizes):
    idx = [int(s) for s in np.cumsum(sizes)[:-1]]
    return jnp.split(p, idx, axis=-1)


def rms_norm(x, gain):
    xf = x.astype(jnp.float32)
    y = xf * lax.rsqrt(jnp.mean(xf * xf, axis=-1, keepdims=True) + EPS)
    return (y * gain.astype(jnp.float32)).astype(x.dtype)


def modulate(h, shift, scale):
    return h * (1 + scale) + shift


def l2_normalize(t):
    return t * lax.rsqrt(jnp.sum(t * t, axis=-1, keepdims=True) + EPS)


def flip_dir(t, direction):
    return jnp.flip(t, axis=1) if direction == 1 else t


def axial_rope(n_tokens):
    rows = n_tokens // GRID_W
    row = jnp.repeat(jnp.arange(rows, dtype=jnp.float32), GRID_W)
    col = jnp.tile(jnp.arange(GRID_W, dtype=jnp.float32), rows)
    n_freq = HEAD_DIM // 4
    inv = ROPE_BASE ** (-jnp.arange(n_freq, dtype=jnp.float32) / n_freq)
    ang = jnp.concatenate([row[:, None] * inv, col[:, None] * inv], axis=-1)
    return jnp.cos(ang)[:, None, :], jnp.sin(ang)[:, None, :]


def apply_rope(t, cos, sin):
    half = t.shape[-1] // 2
    t1 = t[..., :half].astype(jnp.float32)
    t2 = t[..., half:].astype(jnp.float32)
    return jnp.concatenate([t1 * cos - t2 * sin, t2 * cos + t1 * sin], axis=-1).astype(t.dtype)


def depthwise_conv_centred(x, w):
    return lax.conv_general_dilated(x, w[:, None, :].astype(x.dtype), window_strides=(1,), padding='SAME',
                                    dimension_numbers=('NWC', 'WIO', 'NWC'), feature_group_count=x.shape[-1])


def retention_chunkwise(q, k, v, log_gamma, s0):
    b, t, h, dk = q.shape
    n = t // RET_CHUNK
    qc = q.reshape(b, n, RET_CHUNK, h, dk)
    kc = k.reshape(b, n, RET_CHUNK, h, dk)
    vc = v.reshape(b, n, RET_CHUNK, h, -1)
    pos = jnp.arange(RET_CHUNK, dtype=jnp.float32)
    diff = pos[:, None] - pos[None, :]
    intra_decay = jnp.exp(jnp.where(diff[None] >= 0, diff[None] * log_gamma[:, None, None], -jnp.inf))
    scores = jnp.einsum('bnihd,bnjhd->bnhij', qc, kc) * intra_decay
    o_intra = jnp.einsum('bnhij,bnjhe->bnihe', scores, vc)
    k_w = jnp.exp((RET_CHUNK - 1 - pos)[:, None] * log_gamma)
    chunk_kv = jnp.einsum('bnjhd,bnjhe->nbhde', kc * k_w[..., None], vc)
    chunk_decay = jnp.exp(RET_CHUNK * log_gamma)[:, None, None]

    def step(s, kv):
        return s * chunk_decay + kv, s

    s_final, s_prev = lax.scan(step, s0, chunk_kv)
    q_w = jnp.exp((pos + 1)[:, None] * log_gamma)
    o_inter = jnp.einsum('bnihd,nbhde->bnihe', qc * q_w[..., None], s_prev)
    return (o_intra + o_inter).reshape(b, t, h, -1), s_final


def gated_delta_chunkwise(q, k, v, g, beta, s0):
    b, t, h, dk = q.shape
    dv = v.shape[-1]
    n = t // GDN_CHUNK

    def blk(z):
        return z.reshape(b, n, GDN_CHUNK, h, -1).transpose(0, 1, 3, 2, 4)

    q, k, v = blk(q), blk(k), blk(v)
    g = g.reshape(b, n, GDN_CHUNK, h).transpose(0, 1, 3, 2)
    beta = beta.reshape(b, n, GDN_CHUNK, h).transpose(0, 1, 3, 2)
    gcs = jnp.cumsum(g, axis=-1)
    idx = jnp.arange(GDN_CHUNK)
    incl = idx[:, None] >= idx[None, :]
    strict = idx[:, None] > idx[None, :]
    decay = jnp.exp(jnp.where(incl, gcs[..., :, None] - gcs[..., None, :], -jnp.inf))
    kb = k * beta[..., None]
    a_mat = jnp.where(strict, jnp.einsum('bnhid,bnhjd->bnhij', kb, k) * decay, 0.0)
    rhs = jnp.concatenate([v * beta[..., None], kb * jnp.exp(gcs)[..., None]], axis=-1)
    sol = lax.linalg.triangular_solve(a_mat + jnp.eye(GDN_CHUNK, dtype=a_mat.dtype), rhs, left_side=True,
                                      lower=True, unit_diagonal=True)
    u, w = sol[..., :dv], sol[..., dv:]
    attn = jnp.einsum('bnhid,bnhjd->bnhij', q, k) * decay
    q_dec = q * jnp.exp(gcs)[..., None]
    k_dec = k * jnp.exp(gcs[..., -1:] - gcs)[..., None]
    blk_decay = jnp.exp(gcs[..., -1])
    xs = tuple(jnp.moveaxis(z, 1, 0) for z in (u, w, q_dec, k_dec, attn, blk_decay))

    def step(s, inp):
        u_n, w_n, qd_n, kd_n, at_n, dec_n = inp
        v_new = u_n - jnp.einsum('bhcd,bhde->bhce', w_n, s)
        o = jnp.einsum('bhcd,bhde->bhce', qd_n, s) + jnp.einsum('bhij,bhje->bhie', at_n, v_new)
        s = s * dec_n[..., None, None] + jnp.einsum('bhcd,bhce->bhde', kd_n, v_new)
        return s, o

    s_final, o = lax.scan(step, s0, xs)
    o = jnp.moveaxis(o, 0, 1).transpose(0, 1, 3, 2, 4).reshape(b, t, h, dv)
    return o, s_final


def retention_mixer(p_ctx, p_lat, log_neg_decay, gn_gain, cos, sin):
    def heads(p):
        b, t = p.shape[:2]
        q, k, v, g = split_cols(p.astype(jnp.float32), [RET_WIDTH] * 4)
        hd = lambda z: z.reshape(b, t, RET_HEADS, HEAD_DIM)
        return hd(q), hd(k) * HEAD_DIM ** -0.5, hd(v), g

    qc, kc, vc, gc = heads(p_ctx)
    ql, kl, vl, gl = heads(p_lat)
    ql, kl = apply_rope(ql, cos, sin), apply_rope(kl, cos, sin)
    log_gamma = -jnp.exp(log_neg_decay.astype(jnp.float32))
    b = p_lat.shape[0]
    s0 = jnp.zeros((b, RET_HEADS, HEAD_DIM, HEAD_DIM), jnp.float32)
    o_c = jnp.zeros_like(vc)
    o_l = jnp.zeros_like(vl)
    for d in range(N_DIR):
        oc, s_ctx = retention_chunkwise(flip_dir(qc, d), flip_dir(kc, d), flip_dir(vc, d), log_gamma[d], s0)
        ol, _ = retention_chunkwise(flip_dir(ql, d), flip_dir(kl, d), flip_dir(vl, d), log_gamma[d], s_ctx)
        o_c = o_c + flip_dir(oc, d)
        o_l = o_l + flip_dir(ol, d)

    def finish(o, g):
        mu = jnp.mean(o, axis=-1, keepdims=True)
        var = jnp.mean(jnp.square(o - mu), axis=-1, keepdims=True)
        y = ((o - mu) * lax.rsqrt(var + EPS)).reshape(o.shape[0], o.shape[1], RET_WIDTH)
        return y * gn_gain.astype(jnp.float32) * jax.nn.silu(g)

    return finish(o_c, gc), finish(o_l, gl)


def gated_deltanet_mixer(p_ctx, p_lat, conv_w, a_log, dt_bias, norm_gain):
    def prep(p):
        b, t = p.shape[:2]
        qkv, z, a, bt = split_cols(p.astype(jnp.float32),
                                   [3 * GDN_WIDTH, GDN_WIDTH, N_DIR * GDN_HEADS, N_DIR * GDN_HEADS])
        qkv = jax.nn.silu(depthwise_conv_centred(qkv, conv_w.astype(jnp.float32)))
        q, k, v = split_cols(qkv, [GDN_WIDTH] * 3)
        hd = lambda y: y.reshape(b, t, GDN_HEADS, HEAD_DIM)
        q = l2_normalize(hd(q)) * HEAD_DIM ** -0.5
        k = l2_normalize(hd(k))
        a = a.reshape(b, t, N_DIR, GDN_HEADS)
        bt = bt.reshape(b, t, N_DIR, GDN_HEADS)
        g = -jnp.exp(a_log.astype(jnp.float32)) * jax.nn.softplus(a + dt_bias.astype(jnp.float32))
        return q, k, hd(v), z, g, jax.nn.sigmoid(bt)

    qc, kc, vc, zc, gc, bc = prep(p_ctx)
    ql, kl, vl, zl, gl, bl = prep(p_lat)
    b = p_lat.shape[0]
    s0 = jnp.zeros((b, GDN_HEADS, HEAD_DIM, HEAD_DIM), jnp.float32)
    o_c = jnp.zeros_like(vc)
    o_l = jnp.zeros_like(vl)
    for d in range(N_DIR):
        oc, s_ctx = gated_delta_chunkwise(flip_dir(qc, d), flip_dir(kc, d), flip_dir(vc, d),
                                          flip_dir(gc[:, :, d], d), flip_dir(bc[:, :, d], d), s0)
        ol, _ = gated_delta_chunkwise(flip_dir(ql, d), flip_dir(kl, d), flip_dir(vl, d),
                                      flip_dir(gl[:, :, d], d), flip_dir(bl[:, :, d], d), s_ctx)
        o_c = o_c + flip_dir(oc, d)
        o_l = o_l + flip_dir(ol, d)

    def finish(o, z):
        o = o * lax.rsqrt(jnp.mean(o * o, axis=-1, keepdims=True) + EPS) * norm_gain.astype(jnp.float32)
        return o.reshape(o.shape[0], o.shape[1], GDN_WIDTH) * jax.nn.silu(z)

    return finish(o_c, zc), finish(o_l, zl)


def window_attention_mixer(p_ctx, p_lat, sink, cos, sin):
    b, n_lat = p_lat.shape[:2]
    n_ctx = p_ctx.shape[1]
    scale = HEAD_DIM ** -0.5

    def heads(p):
        t = p.shape[1]
        q, k, v = split_cols(p, [SWA_WIDTH, SWA_KV_WIDTH, SWA_KV_WIDTH])
        return (q.reshape(b, t, SWA_HEADS, HEAD_DIM), k.reshape(b, t, SWA_KV_HEADS, HEAD_DIM),
                v.reshape(b, t, SWA_KV_HEADS, HEAD_DIM))

    qc, kc, vc = heads(p_ctx)
    ql, kl, vl = heads(p_lat)
    ql = apply_rope(ql, cos, sin)
    kl = apply_rope(kl, cos, sin)
    qc = qc.reshape(b, n_ctx, SWA_KV_HEADS, SWA_GROUP, HEAD_DIM)
    ql = ql.reshape(b, n_lat, SWA_KV_HEADS, SWA_GROUP, HEAD_DIM)
    sink_g = sink.astype(jnp.float32).reshape(SWA_KV_HEADS, SWA_GROUP)
    s_cc = jnp.einsum('bqgrd,bkgd->bgrqk', qc, kc).astype(jnp.float32) * scale
    sink_c = jnp.broadcast_to(sink_g[None, :, :, None, None], s_cc.shape[:-1] + (1,))
    p_c = jax.nn.softmax(jnp.concatenate([s_cc, sink_c], axis=-1), axis=-1)[..., :n_ctx].astype(vc.dtype)
    o_c = jnp.einsum('bgrqk,bkgd->bqgrd', p_c, vc).reshape(b, n_ctx, SWA_WIDTH)
    nb = n_lat // ATTN_BLOCK
    qb = ql.reshape(b, nb, ATTN_BLOCK, SWA_KV_HEADS, SWA_GROUP, HEAD_DIM)

    def band(z):
        zp = jnp.pad(z, ((0, 0), (ATTN_BLOCK, ATTN_BLOCK), (0, 0), (0, 0)))
        zp = zp.reshape(b, nb + 2, ATTN_BLOCK, SWA_KV_HEADS, HEAD_DIM)
        return jnp.concatenate([zp[:, :nb], zp[:, 1:nb + 1], zp[:, 2:]], axis=2)

    kw, vw = band(kl), band(vl)
    s_loc = jnp.einsum('bnqgrd,bnkgd->bngrqk', qb, kw).astype(jnp.float32) * scale
    s_ctx = jnp.einsum('bnqgrd,bkgd->bngrqk', qb, kc).astype(jnp.float32) * scale
    q_pos = jnp.arange(nb)[:, None, None] * ATTN_BLOCK + jnp.arange(ATTN_BLOCK)[None, :, None]
    k_pos = (jnp.arange(nb)[:, None, None] - 1) * ATTN_BLOCK + jnp.arange(3 * ATTN_BLOCK)[None, None, :]
    valid = (jnp.abs(k_pos - q_pos) <= WINDOW) & (k_pos >= 0) & (k_pos < n_lat)
    s_loc = jnp.where(valid[None, :, None, None], s_loc, -jnp.inf)
    sink_l = jnp.broadcast_to(sink_g[None, None, :, :, None, None], s_loc.shape[:-1] + (1,))
    p_l = jax.nn.softmax(jnp.concatenate([s_loc, s_ctx, sink_l], axis=-1), axis=-1).astype(vl.dtype)
    n_band = 3 * ATTN_BLOCK
    o_l = (jnp.einsum('bngrqk,bnkgd->bnqgrd', p_l[..., :n_band], vw)
           + jnp.einsum('bngrqk,bkgd->bnqgrd', p_l[..., n_band:n_band + n_ctx], vc))
    return o_c, o_l.reshape(b, n_lat, SWA_WIDTH)


def token_mixers(p_ctx, p_lat, ret_decay, ret_gn_g, gdn_conv_w, gdn_a_log, gdn_dt_bias, gdn_norm_g, swa_sink,
                 cos, sin):
    pc = split_cols(p_ctx, [RET_COLS, GDN_COLS, SWA_COLS])
    pl = split_cols(p_lat, [RET_COLS, GDN_COLS, SWA_COLS])
    rc, rl = retention_mixer(pc[0], pl[0], ret_decay, ret_gn_g, cos, sin)
    dc, dl = gated_deltanet_mixer(pc[1], pl[1], gdn_conv_w, gdn_a_log, gdn_dt_bias, gdn_norm_g)
    wc, wl = window_attention_mixer(pc[2], pl[2], swa_sink, cos, sin)
    dt = p_lat.dtype
    o_ctx = jnp.concatenate([rc.astype(dt), dc.astype(dt), wc.astype(dt)], axis=-1)
    o_lat = jnp.concatenate([rl.astype(dt), dl.astype(dt), wl.astype(dt)], axis=-1)
    return o_ctx, o_lat


def swiglu(h, w1, w3, w2):
    return (jax.nn.silu(h @ w1) * (h @ w3)) @ w2


def moe_swiglu(h, router_w, w_gate, w_up, w_down):
    n_tok, d = h.shape
    logits = (h @ router_w).astype(jnp.float32)
    top_logit, top_idx = lax.top_k(logits, TOP_K)
    gate = jax.nn.softmax(top_logit, axis=-1)
    n_assign = n_tok * TOP_K
    expert_flat = top_idx.reshape(n_assign)
    order = jnp.argsort(expert_flat)
    expert_sorted = expert_flat[order]
    token_sorted = (order // TOP_K).astype(jnp.int32)
    gate_sorted = gate.reshape(n_assign)[order]
    counts = jnp.bincount(expert_flat, length=N_EXPERTS)
    group_start = jnp.cumsum(counts) - counts
    padded = (counts + MOE_BLOCK - 1) // MOE_BLOCK * MOE_BLOCK
    padded_end = jnp.cumsum(padded)
    padded_start = padded_end - padded
    dest = padded_start[expert_sorted] + jnp.arange(n_assign) - group_start[expert_sorted]
    n_blocks = -(-n_assign // MOE_BLOCK) + N_EXPERTS
    n_rows = n_blocks * MOE_BLOCK
    row_token = jnp.zeros((n_rows,), jnp.int32).at[dest].set(token_sorted)
    row_gate = jnp.zeros((n_rows,), jnp.float32).at[dest].set(gate_sorted)
    block_start = jnp.arange(n_blocks) * MOE_BLOCK
    block_expert = jnp.minimum(jnp.searchsorted(padded_end, block_start, side='right'), N_EXPERTS - 1)

    def expert_block(args):
        tok, e = args
        xb = h[tok]
        hid = jax.nn.silu(xb @ w_gate[e]) * (xb @ w_up[e])
        return hid @ w_down[e]

    y = lax.map(expert_block, (row_token.reshape(n_blocks, MOE_BLOCK), block_expert))
    y = y.reshape(n_rows, d) * row_gate[:, None].astype(h.dtype)
    return jnp.zeros_like(h).at[row_token].add(y)


def setup_inputs(seed: int = 0) -> dict:
    key = jax.random.key(seed)
    ks = jax.random.split(key, 32)
    f32 = jnp.float32
    d = D_MODEL

    def nrm(k, shape, scale):
        return jax.random.normal(k, shape, f32) * scale

    ret_base = np.log(-np.log(1.0 - 2.0 ** (-5.0 - np.arange(RET_HEADS))))
    dt = jnp.exp(jax.random.uniform(ks[12], (DEPTH, N_DIR, GDN_HEADS), f32, np.log(1e-3), np.log(1e-1)))
    return {
        'x': nrm(ks[0], (BATCH, SEQ, d), 1.0),
        'c': nrm(ks[1], (BATCH, d), 1.0),
        'ctx': nrm(ks[2], (BATCH, CTX_LEN, d), 1.0),
        'c_ctx': nrm(ks[3], (d,), 1.0),
        'ada_w': nrm(ks[4], (DEPTH, d, 6 * d), 0.5 * d ** -0.5),
        'ada_b': nrm(ks[5], (DEPTH, 6 * d), 0.02),
        'norm1_g': 1.0 + nrm(ks[6], (DEPTH, d), 0.05),
        'w_in': nrm(ks[7], (DEPTH, d, IN_COLS), d ** -0.5),
        'ret_decay': jnp.asarray(ret_base, f32) + nrm(ks[8], (DEPTH, N_DIR, RET_HEADS), 0.05),
        'ret_gn_g': 1.0 + nrm(ks[9], (DEPTH, RET_WIDTH), 0.05),
        'gdn_conv_w': nrm(ks[10], (DEPTH, GDN_CONV, 3 * GDN_WIDTH), GDN_CONV ** -0.5),
        'gdn_a_log': jnp.log(jax.random.uniform(ks[11], (DEPTH, N_DIR, GDN_HEADS), f32, 1.0, 16.0)),
        'gdn_dt_bias': dt + jnp.log(-jnp.expm1(-dt)),
        'gdn_norm_g': 1.0 + nrm(ks[13], (DEPTH, HEAD_DIM), 0.05),
        'swa_sink': nrm(ks[14], (DEPTH, SWA_HEADS), 0.5),
        'w_out': nrm(ks[15], (DEPTH, MIX_WIDTH, d), MIX_WIDTH ** -0.5),
        'norm2_g': 1.0 + nrm(ks[16], (DEPTH, d), 0.05),
        'ffn_w1': nrm(ks[17], (N_DENSE_LAYERS, d, DENSE_FF), d ** -0.5),
        'ffn_w3': nrm(ks[18], (N_DENSE_LAYERS, d, DENSE_FF), d ** -0.5),
        'ffn_w2': nrm(ks[19], (N_DENSE_LAYERS, DENSE_FF, d), DENSE_FF ** -0.5),
        'router_w': nrm(ks[20], (N_MOE_LAYERS, d, N_EXPERTS), d ** -0.5),
        'moe_w1': nrm(ks[21], (N_MOE_LAYERS, N_EXPERTS, d, EXPERT_FF), d ** -0.5),
        'moe_w3': nrm(ks[22], (N_MOE_LAYERS, N_EXPERTS, d, EXPERT_FF), d ** -0.5),
        'moe_w2': nrm(ks[23], (N_MOE_LAYERS, N_EXPERTS, EXPERT_FF, d), EXPERT_FF ** -0.5),
        'final_g': 1.0 + nrm(ks[24], (d,), 0.05),
    }


def reference(x, c, ctx, c_ctx, ada_w, ada_b, norm1_g, w_in, ret_decay, ret_gn_g, gdn_conv_w, gdn_a_log,
              gdn_dt_bias, gdn_norm_g, swa_sink, w_out, norm2_g, ffn_w1, ffn_w3, ffn_w2, router_w, moe_w1, moe_w3,
              moe_w2, final_g):
    b, n_lat, d = x.shape
    n_ctx = ctx.shape[1]
    cos, sin = axial_rope(n_lat)
    silu_c = jax.nn.silu(c)
    silu_cc = jax.nn.silu(c_ctx)
    for layer in range(DEPTH):
        last = layer == DEPTH - 1
        mx = jnp.split((silu_c @ ada_w[layer] + ada_b[layer])[:, None, :], 6, axis=-1)
        mc = jnp.split(silu_cc @ ada_w[layer] + ada_b[layer], 6, axis=-1)
        hx = modulate(rms_norm(x, norm1_g[layer]), mx[0], mx[1])
        hc = modulate(rms_norm(ctx, norm1_g[layer]), mc[0], mc[1])
        oc, ox = token_mixers(hc @ w_in[layer], hx @ w_in[layer], ret_decay[layer], ret_gn_g[layer],
                              gdn_conv_w[layer], gdn_a_log[layer], gdn_dt_bias[layer], gdn_norm_g[layer],
                              swa_sink[layer], cos, sin)
        x = x + mx[2] * (ox @ w_out[layer])
        tokens = modulate(rms_norm(x, norm2_g[layer]), mx[3], mx[4]).reshape(-1, d)
        if not last:
            ctx = ctx + mc[2] * (oc @ w_out[layer])
            hc2 = modulate(rms_norm(ctx, norm2_g[layer]), mc[3], mc[4])
            tokens = jnp.concatenate([hc2.reshape(-1, d), tokens], axis=0)
        i = layer // 2
        if layer % 2 == 0:
            y = swiglu(tokens, ffn_w1[i], ffn_w3[i], ffn_w2[i])
        else:
            y = moe_swiglu(tokens, router_w[i], moe_w1[i], moe_w3[i], moe_w2[i])
        x = x + mx[5] * y[tokens.shape[0] - b * n_lat:].reshape(b, n_lat, d)
        if not last:
            ctx = ctx + mc[5] * y[:b * n_ctx].reshape(b, n_ctx, d)
    return rms_norm(x, final_g)
```

```python
import functools

import jax
import jax.numpy as jnp
import numpy as np
from jax import lax
from jax.experimental import pallas as pl
from jax.experimental.pallas import tpu as pltpu

F32 = jnp.float32
BF16 = jnp.bfloat16

HEAD_DIM = 64
GRID_W = 64
ROPE_BASE = 10000.0
EPS = 1e-6
RET_HEADS = 4
GDN_HEADS = 4
GDN_CONV = 5
SWA_HEADS = 8
SWA_KV_HEADS = 2
SWA_GROUP = SWA_HEADS // SWA_KV_HEADS
WINDOW = 128
N_EXPERTS = 8
RET_W = RET_HEADS * HEAD_DIM
GDN_W = GDN_HEADS * HEAD_DIM
SWA_W = SWA_HEADS * HEAD_DIM
SWA_KV_W = SWA_KV_HEADS * HEAD_DIM
LANES = 128
VMEM_LIMIT = 56 * 1024 * 1024
NEG_BIG = -1e30


def _cparams(*sem):
    return pltpu.CompilerParams(dimension_semantics=sem, vmem_limit_bytes=VMEM_LIMIT)


def _dot(a, b):
    return jnp.dot(a, b, preferred_element_type=F32)


def _dot_nt(a, b):
    return lax.dot_general(a, b, (((1,), (1,)), ((), ())), preferred_element_type=F32)


def _dot_tn(a, b):
    return lax.dot_general(a, b, (((0,), (0,)), ((), ())), preferred_element_type=F32)


def _split_dot(x, m_bf16, terms=2):
    acc = None
    rem = x
    for _ in range(terms):
        part = rem.astype(BF16)
        d = _dot(part, m_bf16)
        acc = d if acc is None else acc + d
        rem = rem - part.astype(F32)
    return acc


def _sigmoid(x):
    return 1.0 / (1.0 + jnp.exp(-x))


def _silu(x):
    return x * _sigmoid(x)


def _rms_mod(x, gain, shift, scale):
    ms = jnp.mean(x * x, axis=-1, keepdims=True)
    y = x * lax.rsqrt(ms + EPS) * gain
    return y * (1.0 + scale) + shift


def _seg_map(n_lat_tiles, tiles_per_batch, n_batch):
    def seg(i):
        return jnp.where(i < n_lat_tiles, i // tiles_per_batch, n_batch)
    return seg


def _adaln_kernel(c_ref, w_ref, b_ref, o_ref):
    h = _silu(c_ref[...]).astype(BF16)
    o_ref[0] = _dot(h, w_ref[0].astype(BF16)) + b_ref[0]


def adaln(cvec, ada_w, ada_b):
    depth, d, n6 = ada_w.shape
    tn = 1536 if n6 % 1536 == 0 else n6
    rows = cvec.shape[0]
    return pl.pallas_call(
        _adaln_kernel,
        out_shape=jax.ShapeDtypeStruct((depth, rows, n6), F32),
        grid=(depth, n6 // tn),
        in_specs=[pl.BlockSpec((rows, d), lambda l, j: (0, 0)),
                  pl.BlockSpec((1, d, tn), lambda l, j: (l, 0, j)),
                  pl.BlockSpec((1, 1, tn), lambda l, j: (l, 0, j))],
        out_specs=pl.BlockSpec((1, rows, tn), lambda l, j: (l, 0, j)),
        compiler_params=_cparams("parallel", "parallel"),
        name="adaln",
    )(cvec, ada_w, ada_b.reshape(depth, 1, n6))


def _inproj_kernel(x_ref, mod_ref, g_ref, wr_ref, wg_ref, ws_ref, wab_ref, pr_ref, pg_ref, ps_ref, pab_ref):
    h = _rms_mod(x_ref[...], g_ref[...], mod_ref[0, 0:1, :], mod_ref[0, 1:2, :]).astype(BF16)
    for w_ref, o_ref in ((wr_ref, pr_ref), (wg_ref, pg_ref), (ws_ref, ps_ref), (wab_ref, pab_ref)):
        n = w_ref.shape[1]
        step = 256 if n % 256 == 0 else n
        for c in range(0, n, step):
            o_ref[:, c:c + step] = _dot(h, w_ref[:, c:c + step]).astype(o_ref.dtype)


def in_proj(xall, mod, gain, w_ret, w_gdn, w_swa, w_ab, *, tm, seg):
    n, d = xall.shape
    full = lambda i: (0, 0)
    row = lambda i: (i, 0)
    outs = (jax.ShapeDtypeStruct((n, w_ret.shape[1]), BF16), jax.ShapeDtypeStruct((n, w_gdn.shape[1]), BF16),
            jax.ShapeDtypeStruct((n, w_swa.shape[1]), BF16), jax.ShapeDtypeStruct((n, w_ab.shape[1]), F32))
    return pl.pallas_call(
        _inproj_kernel,
        out_shape=outs,
        grid=(n // tm,),
        in_specs=[pl.BlockSpec((tm, d), row),
                  pl.BlockSpec((1, 6, d), lambda i: (seg(i), 0, 0)),
                  pl.BlockSpec((1, d), full),
                  pl.BlockSpec(w_ret.shape, full), pl.BlockSpec(w_gdn.shape, full),
                  pl.BlockSpec(w_swa.shape, full), pl.BlockSpec(w_ab.shape, full)],
        out_specs=tuple(pl.BlockSpec((tm, o.shape[1]), row) for o in outs),
        compiler_params=_cparams("parallel"),
        name="in_proj",
    )(xall, mod, gain, w_ret, w_gdn, w_swa, w_ab)


def _outproj_kernel(*refs, tm, moe, n_lat_tiles, has_ctx):
    n_mix = 6 if has_ctx else 3
    mix = refs[:n_mix]
    refs = refs[n_mix:]
    if moe:
        (x_ref, mod_ref, w_ref, g_ref, rw_ref, xo_ref, tok_ref, ridx_ref, rgate_ref, cnt_ref, base_ref) = refs
    else:
        x_ref, mod_ref, w_ref, g_ref, xo_ref, tok_ref = refs
    if has_ctx:
        is_lat = pl.program_id(0) < n_lat_tiles
        o_ret, o_gdn, o_swa = (jnp.where(is_lat, mix[2 * k][...], mix[2 * k + 1][...]) for k in range(3))
    else:
        o_ret, o_gdn, o_swa = (m[...] for m in mix)
    y = _dot(o_ret, w_ref[0:RET_W, :])
    y = y + _dot(o_gdn, w_ref[RET_W:RET_W + GDN_W, :])
    y = y + _dot(o_swa, w_ref[RET_W + GDN_W:, :])
    x = x_ref[...] + mod_ref[0, 2:3, :] * y
    xo_ref[...] = x
    t = _rms_mod(x, g_ref[...], mod_ref[0, 3:4, :], mod_ref[0, 4:5, :])
    tok_ref[...] = t.astype(tok_ref.dtype)
    if not moe:
        return

    @pl.when(pl.program_id(0) == 0)
    def _():
        base_ref[...] = jnp.zeros_like(base_ref)

    rw = rw_ref[...]
    rw_hi = rw.astype(BF16)
    rw_lo = (rw - rw_hi.astype(F32)).astype(BF16)
    t_hi = t.astype(BF16)
    t_lo = (t - t_hi.astype(F32)).astype(BF16)
    logits = _dot(t_hi, rw_hi) + _dot(t_hi, rw_lo) + _dot(t_lo, rw_hi)
    lane = lax.broadcasted_iota(jnp.int32, logits.shape, 1)
    logits = jnp.where(lane < N_EXPERTS, logits, NEG_BIG)
    m1 = jnp.max(logits, axis=-1, keepdims=True)
    i1 = jnp.min(jnp.where(logits == m1, lane, LANES), axis=-1, keepdims=True)
    rest = jnp.where(lane == i1, NEG_BIG, logits)
    m2 = jnp.max(rest, axis=-1, keepdims=True)
    i2 = jnp.min(jnp.where(rest == m2, lane, LANES), axis=-1, keepdims=True)
    e2 = jnp.exp(m2 - m1)
    g1 = 1.0 / (1.0 + e2)
    g2 = e2 * g1
    oh1 = jnp.where(lane == i1, 1.0, 0.0)
    oh2 = jnp.where(lane == i2, 1.0, 0.0)
    r_i = lax.broadcasted_iota(jnp.int32, (tm, tm), 0)
    c_i = lax.broadcasted_iota(jnp.int32, (tm, tm), 1)
    lower = jnp.where(r_i > c_i, 1.0, 0.0).astype(BF16)
    cum1 = _dot(lower, oh1.astype(BF16))
    cum2 = _dot(lower, oh2.astype(BF16))
    cnt1 = jnp.sum(oh1, axis=0, keepdims=True)
    cnt2 = jnp.sum(oh2, axis=0, keepdims=True)
    base = base_ref[...]
    rank1 = jnp.sum(oh1 * (base + cum1), axis=-1, keepdims=True)
    rank2 = jnp.sum(oh2 * (base + cnt1 + cum2), axis=-1, keepdims=True)
    base = base + cnt1 + cnt2
    base_ref[...] = base
    cnt_ref[...] = base.astype(jnp.int32)
    lane8 = lax.broadcasted_iota(jnp.int32, (tm, 8), 1)
    r1 = rank1.astype(jnp.int32)
    r2 = rank2.astype(jnp.int32)
    ridx_ref[...] = jnp.where(lane8 == 0, i1, jnp.where(lane8 == 1, i2, jnp.where(lane8 == 2, r1,
                              jnp.where(lane8 == 3, r2, 0))))
    rgate_ref[...] = jnp.where(lane8 == 0, g1, jnp.where(lane8 == 1, g2, 0.0))


def out_proj(o_ret, o_gdn, o_swa, xall, mod, w_out, gain2, router_w, *, tm, seg, n_rows, n_lat_rows, tok_dtype):
    d = xall.shape[1]
    moe = router_w is not None
    has_ctx = n_rows > n_lat_rows
    n_lat_tiles = n_lat_rows // tm
    full = lambda i: (0, 0)
    row = lambda i: (i, 0)
    lat_row = lambda i: (jnp.minimum(i, n_lat_tiles - 1), 0)
    ctx_row = lambda i: (jnp.maximum(i - n_lat_tiles, 0), 0)
    in_specs, args = [], []
    for pair, width in ((o_ret, RET_W), (o_gdn, GDN_W), (o_swa, SWA_W)):
        in_specs.append(pl.BlockSpec((tm, width), lat_row))
        args.append(pair[0])
        if has_ctx:
            in_specs.append(pl.BlockSpec((tm, width), ctx_row))
            args.append(pair[1])
    in_specs += [pl.BlockSpec((tm, d), row), pl.BlockSpec((1, 6, d), lambda i: (seg(i), 0, 0)),
                 pl.BlockSpec(w_out.shape, full), pl.BlockSpec((1, d), full)]
    args += [xall, mod, w_out, gain2]
    outs = [jax.ShapeDtypeStruct((n_rows, d), F32), jax.ShapeDtypeStruct((n_rows, d), tok_dtype)]
    out_specs = [pl.BlockSpec((tm, d), row), pl.BlockSpec((tm, d), row)]
    scratch = []
    if moe:
        in_specs.append(pl.BlockSpec(router_w.shape, full))
        args.append(router_w)
        outs += [jax.ShapeDtypeStruct((n_rows, 8), jnp.int32), jax.ShapeDtypeStruct((n_rows, 8), F32),
                 jax.ShapeDtypeStruct((1, LANES), jnp.int32)]
        out_specs += [pl.BlockSpec((tm, 8), row), pl.BlockSpec((tm, 8), row), pl.BlockSpec((1, LANES), full)]
        scratch = [pltpu.VMEM((1, LANES), F32)]
    return pl.pallas_call(
        functools.partial(_outproj_kernel, tm=tm, moe=moe, n_lat_tiles=n_lat_tiles, has_ctx=has_ctx),
        out_shape=tuple(outs),
        grid=(n_rows // tm,),
        in_specs=in_specs,
        out_specs=tuple(out_specs),
        scratch_shapes=scratch,
        compiler_params=_cparams("arbitrary"),
        name="out_proj_router" if moe else "out_proj",
    )(*args)


def _final_norm(x, gain):
    ms = jnp.mean(x * x, axis=-1, keepdims=True)
    return x * lax.rsqrt(ms + EPS) * gain


def _dense_ffn_kernel(tok_ref, x_ref, mod_ref, w1_ref, w3_ref, w2_ref, fg_ref, o_ref, acc_ref, *, fc, last):
    t = tok_ref[...]
    n_chunks = w1_ref.shape[1] // fc

    def body(c, carry):
        off = pl.multiple_of(c * fc, fc)
        a = _dot(t, w1_ref[:, pl.ds(off, fc)])
        b = _dot(t, w3_ref[:, pl.ds(off, fc)])
        h = (_silu(a) * b).astype(BF16)
        upd = _dot(h, w2_ref[pl.ds(off, fc), :])

        @pl.when(c == 0)
        def _():
            acc_ref[...] = upd

        @pl.when(c > 0)
        def _():
            acc_ref[...] += upd
        return carry

    lax.fori_loop(0, n_chunks, body, 0)
    x = x_ref[...] + mod_ref[0, 5:6, :] * acc_ref[...]
    if last:
        x = _final_norm(x, fg_ref[...])
    o_ref[...] = x


def dense_ffn(tokens, xall, mod, w1, w3, w2, final_g, *, tm, seg, last):
    n, d = tokens.shape
    f = w1.shape[1]
    fc = 256
    full = lambda i: (0, 0)
    row = lambda i: (i, 0)
    return pl.pallas_call(
        functools.partial(_dense_ffn_kernel, fc=fc, last=last),
        out_shape=jax.ShapeDtypeStruct((n, d), F32),
        grid=(n // tm,),
        in_specs=[pl.BlockSpec((tm, d), row), pl.BlockSpec((tm, d), row),
                  pl.BlockSpec((1, 6, d), lambda i: (seg(i), 0, 0)),
                  pl.BlockSpec((d, f), full, pipeline_mode=pl.Buffered(1)),
                  pl.BlockSpec((d, f), full, pipeline_mode=pl.Buffered(1)),
                  pl.BlockSpec((f, d), full, pipeline_mode=pl.Buffered(1)),
                  pl.BlockSpec((1, d), full)],
        out_specs=pl.BlockSpec((tm, d), row),
        scratch_shapes=[pltpu.VMEM((tm, d), F32)],
        compiler_params=_cparams("parallel"),
        name="dense_ffn",
    )(tokens, xall, mod, w1, w3, w2, final_g)


def _dispatch_kernel(dest_ref, tok_ref, xs_in_ref, xs_ref, sem, *, tm):
    del xs_in_ref

    def issue(r, carry):
        for k in range(2):
            dst = dest_ref[0, 0, 2 * r + k]
            pltpu.make_async_copy(tok_ref.at[pl.ds(r, 1)], xs_ref.at[pl.ds(dst, 1)], sem).start()
        return carry

    lax.fori_loop(0, tm, issue, 0)

    def drain(r, carry):
        pltpu.make_async_copy(tok_ref.at[pl.ds(0, 1)], xs_ref.at[pl.ds(0, 1)], sem).wait()
        return carry

    lax.fori_loop(0, 2 * tm, drain, 0)


def moe_dispatch(tokens, dest3, n_slots, *, tm):
    n, d = tokens.shape
    xs0 = jnp.zeros((n_slots, d), tokens.dtype)
    return pl.pallas_call(
        functools.partial(_dispatch_kernel, tm=tm),
        out_shape=jax.ShapeDtypeStruct((n_slots, d), tokens.dtype),
        grid=(n // tm,),
        in_specs=[pl.BlockSpec((1, 1, 2 * tm), lambda i: (i, 0, 0), memory_space=pltpu.SMEM),
                  pl.BlockSpec((tm, d), lambda i: (i, 0)),
                  pl.BlockSpec(memory_space=pl.ANY)],
        out_specs=pl.BlockSpec(memory_space=pl.ANY),
        scratch_shapes=[pltpu.SemaphoreType.DMA],
        input_output_aliases={2: 0},
        compiler_params=_cparams("arbitrary"),
        name="moe_dispatch",
    )(dest3, tokens, xs0)


def _moe_ffn_kernel(te_ref, nu_ref, xs_ref, w1_ref, w3_ref, w2_ref, o_ref, acc_ref):
    i = pl.program_id(0)
    j = pl.program_id(1)

    @pl.when(i < nu_ref[0])
    def _():
        x = xs_ref[...].astype(BF16)
        a = _dot(x, w1_ref[0])
        b = _dot(x, w3_ref[0])
        h = (_silu(a) * b).astype(BF16)
        upd = _dot(h, w2_ref[0])

        @pl.when(j == 0)
        def _():
            acc_ref[...] = upd

        @pl.when(j > 0)
        def _():
            acc_ref[...] += upd

        @pl.when(j == pl.num_programs(1) - 1)
        def _():
            o_ref[...] = acc_ref[...]

    @pl.when((i >= nu_ref[0]) & (j == pl.num_programs(1) - 1))
    def _():
        o_ref[...] = jnp.zeros_like(o_ref)


def moe_ffn(xs, tile_expert, n_used, w1, w3, w2, *, tm, tf):
    r, d = xs.shape
    f = w1.shape[2]
    nf = f // tf

    def tile(i, nu):
        return jnp.minimum(i, nu[0] - 1)

    def ftile(i, j, nu):
        return jnp.where(i < nu[0], j, nf - 1)

    grid_spec = pltpu.PrefetchScalarGridSpec(
        num_scalar_prefetch=2,
        grid=(r // tm, nf),
        in_specs=[pl.BlockSpec((tm, d), lambda i, j, te, nu: (tile(i, nu), 0)),
                  pl.BlockSpec((1, d, tf), lambda i, j, te, nu: (te[tile(i, nu)], 0, ftile(i, j, nu))),
                  pl.BlockSpec((1, d, tf), lambda i, j, te, nu: (te[tile(i, nu)], 0, ftile(i, j, nu))),
                  pl.BlockSpec((1, tf, d), lambda i, j, te, nu: (te[tile(i, nu)], ftile(i, j, nu), 0))],
        out_specs=pl.BlockSpec((tm, d), lambda i, j, te, nu: (i, 0)),
        scratch_shapes=[pltpu.VMEM((tm, d), F32)],
    )
    return pl.pallas_call(
        _moe_ffn_kernel,
        out_shape=jax.ShapeDtypeStruct((r, d), F32),
        grid_spec=grid_spec,
        compiler_params=_cparams("arbitrary", "arbitrary"),
        name="moe_ffn",
    )(tile_expert, n_used, xs, w1, w3, w2)


def _combine_kernel(dest_ref, gate_ref, x_ref, mod_ref, fg_ref, ys_ref, o_ref, buf_ref, sem, *, tm, last):
    def issue(r, carry):
        for k in range(2):
            src = dest_ref[0, 0, 2 * r + k]
            pltpu.make_async_copy(ys_ref.at[pl.ds(src, 1)], buf_ref.at[k, pl.ds(r, 1)], sem).start()
        return carry

    lax.fori_loop(0, tm, issue, 0)

    def drain(r, carry):
        pltpu.make_async_copy(ys_ref.at[pl.ds(0, 1)], buf_ref.at[0, pl.ds(0, 1)], sem).wait()
        return carry

    lax.fori_loop(0, 2 * tm, drain, 0)
    gate = gate_ref[...]
    y = gate[:, 0:1] * buf_ref[0] + gate[:, 1:2] * buf_ref[1]
    x = x_ref[...] + mod_ref[0, 5:6, :] * y
    if last:
        x = _final_norm(x, fg_ref[...])
    o_ref[...] = x


def moe_combine(ys, dest3, rgate, xall, mod, final_g, *, tm, seg, last):
    n, d = xall.shape
    return pl.pallas_call(
        functools.partial(_combine_kernel, tm=tm, last=last),
        out_shape=jax.ShapeDtypeStruct((n, d), F32),
        grid=(n // tm,),
        in_specs=[pl.BlockSpec((1, 1, 2 * tm), lambda i: (i, 0, 0), memory_space=pltpu.SMEM),
                  pl.BlockSpec((tm, 8), lambda i: (i, 0)),
                  pl.BlockSpec((tm, d), lambda i: (i, 0)),
                  pl.BlockSpec((1, 6, d), lambda i: (seg(i), 0, 0)),
                  pl.BlockSpec((1, d), lambda i: (0, 0)),
                  pl.BlockSpec(memory_space=pl.ANY)],
        out_specs=pl.BlockSpec((tm, d), lambda i: (i, 0)),
        scratch_shapes=[pltpu.VMEM((2, tm, d), F32), pltpu.SemaphoreType.DMA],
        compiler_params=_cparams("arbitrary"),
        name="moe_combine",
    )(dest3, rgate, xall, mod, final_g, ys)


def moe_layer(tokens, ridx, rgate, cnt, xall, mod, w1, w3, w2, final_g, *, tm, tmx, tf, seg, last):
    n = tokens.shape[0]
    counts = cnt[0, :N_EXPERTS]
    padded = (counts + tmx - 1) // tmx * tmx
    pend = jnp.cumsum(padded)
    pstart = pend - padded
    dest = pstart[ridx[:, 0:2]] + ridx[:, 2:4]
    dest3 = dest.reshape(n // tm, 1, 2 * tm).astype(jnp.int32)
    n_tiles = (2 * n) // tmx + N_EXPERTS
    n_used = (pend[-1] // tmx).astype(jnp.int32).reshape(1)
    tile_start = jnp.arange(n_tiles, dtype=jnp.int32) * tmx
    tile_expert = jnp.minimum(jnp.searchsorted(pend, tile_start, side='right'), N_EXPERTS - 1).astype(jnp.int32)
    xs = moe_dispatch(tokens, dest3, n_tiles * tmx, tm=tm)
    ys = moe_ffn(xs, tile_expert, n_used, w1, w3, w2, tm=tmx, tf=tf)
    return moe_combine(ys, dest3, rgate, xall, mod, final_g, tm=tm, seg=seg, last=last)


def rope_tables(seq):
    rows = seq // GRID_W
    row = jnp.repeat(jnp.arange(rows, dtype=F32), GRID_W)
    col = jnp.tile(jnp.arange(GRID_W, dtype=F32), rows)
    n_freq = HEAD_DIM // 4
    inv = ROPE_BASE ** (-jnp.arange(n_freq, dtype=F32) / n_freq)
    ang = jnp.concatenate([row[:, None] * inv, col[:, None] * inv], axis=-1)
    cos, sin = jnp.cos(ang), jnp.sin(ang)
    cos64 = jnp.concatenate([cos, cos], axis=-1)
    sin64 = jnp.concatenate([-sin, sin], axis=-1)
    return jnp.tile(cos64, (1, 2)), jnp.tile(sin64, (1, 2))


def _tile_lanes(t, width):
    reps = width // t.shape[1]
    return t if reps == 1 else jnp.concatenate([t] * reps, axis=1)


def _rope(t, cos, sin):
    w = t.shape[1]
    lane = lax.broadcasted_iota(jnp.int32, t.shape, 1) % HEAD_DIM
    half = HEAD_DIM // 2
    rot = jnp.where(lane < half, pltpu.roll(t, w - half, 1), pltpu.roll(t, half, 1))
    return t * _tile_lanes(cos, w) + rot * _tile_lanes(sin, w)


def _group_matrix(width, value):
    r = lax.broadcasted_iota(jnp.int32, (width, width), 0) // HEAD_DIM
    c = lax.broadcasted_iota(jnp.int32, (width, width), 1) // HEAD_DIM
    return jnp.where(r == c, value, 0.0).astype(BF16)


RET_CHUNK = 128


def _retention_kernel(lat_ref, ctx_ref, cos_ref, sin_ref, dec_ref, gn_ref, ol_ref, oc_ref,
                      q_s, k_s, bn_s, st_s, *, n_ctx, n_lat):
    ch = RET_CHUNK
    nc, nl = n_ctx // ch, n_lat // ch
    lg = -jnp.exp(dec_ref[...])
    lgf, lgb = lg[0:1], lg[1:2]
    pos = lax.broadcasted_iota(jnp.int32, (ch, 1), 0).astype(F32)
    wkf = jnp.exp((ch - 1 - pos) * lgf)
    wkb = jnp.exp(pos * lgb)
    wqf = jnp.exp((pos + 1) * lgf)
    wqb = jnp.exp((ch - pos) * lgb)
    cdf = jnp.exp(ch * lgf)
    cdb = jnp.exp(ch * lgb)
    blockmask = (lax.broadcasted_iota(jnp.int32, (RET_W, RET_W), 0) // HEAD_DIM
                 == lax.broadcasted_iota(jnp.int32, (RET_W, RET_W), 1) // HEAD_DIM)
    gmean = _group_matrix(RET_W, 1.0 / HEAD_DIM)
    diff = (lax.broadcasted_iota(jnp.int32, (ch, ch), 0) - lax.broadcasted_iota(jnp.int32, (ch, ch), 1)).astype(F32)
    gn = gn_ref[...]

    def load_qkv(ref, t0):
        return (ref[pl.ds(t0, ch), 0:RET_W].astype(F32), ref[pl.ds(t0, ch), RET_W:2 * RET_W].astype(F32),
                ref[pl.ds(t0, ch), 2 * RET_W:3 * RET_W])

    def reverse_step(ref, t0, c, roped):
        q, k, v = load_qkv(ref, t0)
        if roped:
            cos, sin = cos_ref[pl.ds(t0, ch), :], sin_ref[pl.ds(t0, ch), :]
            q, k = _rope(q, cos, sin), _rope(k, cos, sin)
        k = k * (HEAD_DIM ** -0.5)
        r0 = pl.multiple_of(c * ch, ch)
        q_s[pl.ds(r0, ch), :] = q.astype(BF16)
        k_s[pl.ds(r0, ch), :] = k.astype(BF16)
        st = st_s[...]
        bn_s[c] = st.astype(BF16)
        kvb = _dot_tn((k * wkb).astype(BF16), v)
        st_s[...] = cdb * st + jnp.where(blockmask, kvb, 0.0)

    st_s[...] = jnp.zeros_like(st_s)

    def rev_ctx(n, carry):
        c = nc - 1 - n
        reverse_step(ctx_ref, pl.multiple_of(c * ch, ch), c, False)
        return carry

    def rev_lat(n, carry):
        c = nl - 1 - n
        reverse_step(lat_ref, pl.multiple_of(c * ch, ch), nc + c, True)
        return carry

    lax.fori_loop(0, nc, rev_ctx, 0)
    lax.fori_loop(0, nl, rev_lat, 0)

    def forward_step(ref, out_ref, t0, c):
        r0 = pl.multiple_of(c * ch, ch)
        q = q_s[pl.ds(r0, ch), :]
        k = k_s[pl.ds(r0, ch), :]
        v = ref[pl.ds(t0, ch), 2 * RET_W:3 * RET_W]
        g = ref[pl.ds(t0, ch), 3 * RET_W:4 * RET_W].astype(F32)
        heads = []
        for h in range(RET_HEADS):
            sl = slice(h * HEAD_DIM, (h + 1) * HEAD_DIM)
            s = _dot_nt(q[:, sl], k[:, sl])
            lf, lb = lgf[:, h * HEAD_DIM:h * HEAD_DIM + 1], lgb[:, h * HEAD_DIM:h * HEAD_DIM + 1]
            dm = jnp.where(diff == 0.0, 2.0, jnp.exp(jnp.abs(diff) * jnp.where(diff > 0.0, lf, lb)))
            heads.append(_dot((s * dm).astype(BF16), v[:, sl]))
        qf = q.astype(F32)
        st = st_s[...]
        o = jnp.concatenate(heads, axis=1)
        o = o + _dot((qf * wqf).astype(BF16), st.astype(BF16)) + _dot((qf * wqb).astype(BF16), bn_s[c])
        kvf = _dot_tn((k.astype(F32) * wkf).astype(BF16), v)
        st_s[...] = cdf * st + jnp.where(blockmask, kvf, 0.0)
        mu = _split_dot(o, gmean)
        xc = o - mu
        var = _split_dot(xc * xc, gmean)
        out_ref[pl.ds(t0, ch), :] = (xc * lax.rsqrt(var + EPS) * gn * _silu(g)).astype(out_ref.dtype)

    st_s[...] = jnp.zeros_like(st_s)

    def fwd_ctx(n, carry):
        forward_step(ctx_ref, oc_ref, pl.multiple_of(n * ch, ch), n)
        return carry

    def fwd_lat(n, carry):
        forward_step(lat_ref, ol_ref, pl.multiple_of(n * ch, ch), nc + n)
        return carry

    lax.fori_loop(0, nc, fwd_ctx, 0)
    lax.fori_loop(0, nl, fwd_lat, 0)


def retention_mixer(p_ret, cos, sin, dec_lanes, gn_gain, *, n_batch, n_lat, n_ctx):
    n = p_ret.shape[0]
    width = p_ret.shape[1]
    ctx_blk0 = (n_batch * n_lat) // n_ctx
    t_all = n_lat + n_ctx
    nch = t_all // RET_CHUNK
    return pl.pallas_call(
        functools.partial(_retention_kernel, n_ctx=n_ctx, n_lat=n_lat),
        out_shape=(jax.ShapeDtypeStruct((n_batch * n_lat, RET_W), BF16),
                   jax.ShapeDtypeStruct((n_batch * n_ctx, RET_W), BF16)),
        grid=(n_batch,),
        in_specs=[pl.BlockSpec((n_lat, width), lambda b: (b, 0)),
                  pl.BlockSpec((n_ctx, width), lambda b: (ctx_blk0 + b, 0)),
                  pl.BlockSpec((n_lat, LANES), lambda b: (0, 0)),
                  pl.BlockSpec((n_lat, LANES), lambda b: (0, 0)),
                  pl.BlockSpec((2, RET_W), lambda b: (0, 0)),
                  pl.BlockSpec((1, RET_W), lambda b: (0, 0))],
        out_specs=(pl.BlockSpec((n_lat, RET_W), lambda b: (b, 0)),
                   pl.BlockSpec((n_ctx, RET_W), lambda b: (b, 0))),
        scratch_shapes=[pltpu.VMEM((t_all, RET_W), BF16), pltpu.VMEM((t_all, RET_W), BF16),
                        pltpu.VMEM((nch, RET_W, RET_W), BF16), pltpu.VMEM((RET_W, RET_W), F32)],
        compiler_params=_cparams("parallel"),
        name="retention",
    )(p_ret, p_ret, cos, sin, dec_lanes, gn_gain)


GDN_CHUNK = 64
GDN_PREP = 128
GDN_HALO = 8


def _softplus(x):
    return jnp.maximum(x, 0.0) + jnp.log(1.0 + jnp.exp(-jnp.abs(x)))


def _unit_tri_inverse(a, upper):
    n = a.shape[0]
    ri = lax.broadcasted_iota(jnp.int32, (n, n), 0)
    ci = lax.broadcasted_iota(jnp.int32, (n, n), 1)
    b16 = (ri // 16) == (ci // 16)
    b32 = (ri // 32) == (ci // 32)
    del upper
    p = jnp.where(b16, -a, 0.0)
    t = jnp.where(ri == ci, 1.0, 0.0) + p
    for _ in range(3):
        pb = p.astype(BF16)
        p = _dot(pb, pb)
        t = t + _dot(t.astype(BF16), p.astype(BF16))
    for keep in (b32 & ~b16, ~b32):
        off = jnp.where(keep, a, 0.0).astype(BF16)
        tb = t.astype(BF16)
        t = t - _dot(_dot(tb, off).astype(BF16), tb)
    return t


def _gdn_kernel(lat_ref, ctx_ref, abl_ref, abc_ref, cw_ref, par_ref, ng_ref, ol_ref, oc_ref,
                q_s, k_s, v_s, gb_s, of_s, ob_s, st_s, *, n_ctx, n_lat):
    ch = GDN_CHUNK
    pt = GDN_PREP
    halo = GDN_HALO
    qkv_w = 3 * GDN_W
    gsum = _group_matrix(GDN_W, 1.0)
    gmean = _group_matrix(GDN_W, 1.0 / HEAD_DIM)
    lane128 = lax.broadcasted_iota(jnp.int32, (pt, LANES), 1)
    neg_a = -jnp.exp(par_ref[0:1, :])
    dt_bias = par_ref[1:2, :]
    taps = [cw_ref[j:j + 1, :] for j in range(GDN_CONV)]

    def prep(ref, ab_ref, n_rows, row_off):
        n_tiles = n_rows // pt

        def body(i, carry):
            t0 = pl.multiple_of(i * pt, pt)
            cur = ref[pl.ds(t0, pt), 0:qkv_w].astype(F32)
            p0 = pl.multiple_of(jnp.maximum(t0 - 16, 0), 16)
            n0 = pl.multiple_of(jnp.minimum(t0 + pt, n_rows - 16), 16)
            prev = ref[pl.ds(p0, 16), 0:qkv_w].astype(F32)[16 - halo:16]
            nxt = ref[pl.ds(n0, 16), 0:qkv_w].astype(F32)[0:halo]
            prev = jnp.where(i > 0, prev, 0.0)
            nxt = jnp.where(i < n_tiles - 1, nxt, 0.0)
            ext = jnp.concatenate([prev, cur, nxt], axis=0)
            rows = pt + 2 * halo
            acc = None
            for j in range(GDN_CONV):
                s = j - (GDN_CONV - 1) // 2
                sh = ext if s == 0 else pltpu.roll(ext, (rows - s) % rows, 0)
                term = sh[halo:halo + pt] * taps[j]
                acc = term if acc is None else acc + term
            act = _silu(acc)
            q, k, v = act[:, 0:GDN_W], act[:, GDN_W:2 * GDN_W], act[:, 2 * GDN_W:3 * GDN_W]
            q = q * lax.rsqrt(_split_dot(q * q, gsum) + EPS) * (HEAD_DIM ** -0.5)
            k = k * lax.rsqrt(_split_dot(k * k, gsum) + EPS)
            r0 = pl.multiple_of(row_off + t0, pt)
            q_s[pl.ds(r0, pt), :] = q.astype(BF16)
            k_s[pl.ds(r0, pt), :] = k.astype(BF16)
            v_s[pl.ds(r0, pt), :] = v.astype(BF16)
            ab = ab_ref[pl.ds(t0, pt), :]
            g = neg_a * _softplus(ab + dt_bias)
            gb_s[pl.ds(r0, pt), :] = jnp.where(lane128 < 2 * GDN_HEADS, g, _sigmoid(ab))
            return carry

        lax.fori_loop(0, n_tiles, body, 0)

    prep(ctx_ref, abc_ref, n_ctx, 0)
    prep(lat_ref, abl_ref, n_lat, n_ctx)

    ri = lax.broadcasted_iota(jnp.int32, (ch, ch), 0)
    ci = lax.broadcasted_iota(jnp.int32, (ch, ch), 1)
    incl = (ri >= ci, ri <= ci)
    strict = (ri > ci, ri < ci)
    tri = tuple(jnp.where(m, 1.0, 0.0).astype(BF16) for m in incl)

    def chunk_dir(c, d, out_s):
        r0 = pl.multiple_of(c * ch, ch)
        gbc = gb_s[pl.ds(r0, ch), :]
        rem, gcs = gbc, None
        for _ in range(3):
            part = rem.astype(BF16)
            term = _dot(tri[d], part)
            gcs = term if gcs is None else gcs + term
            rem = rem - part.astype(F32)
        gcs_t = gcs.T
        edge = ch - 1 if d == 0 else 0
        outs = []
        for h in range(GDN_HEADS):
            col = d * GDN_HEADS + h
            sl = slice(h * HEAD_DIM, (h + 1) * HEAD_DIM)
            gc = gcs[:, col:col + 1]
            gr = gcs_t[col:col + 1, :]
            gtot = gcs[edge:edge + 1, col:col + 1]
            beta = gbc[:, 2 * GDN_HEADS + col:2 * GDN_HEADS + col + 1]
            decay = jnp.where(incl[d], jnp.exp(jnp.minimum(gc - gr, 0.0)), 0.0)
            q = q_s[pl.ds(r0, ch), sl]
            k = k_s[pl.ds(r0, ch), sl]
            kf = k.astype(F32)
            kb = kf * beta
            a = jnp.where(strict[d], _dot_nt(kb.astype(BF16), k) * decay, 0.0)
            t = _unit_tri_inverse(a, d == 1)
            eg = jnp.exp(gc)
            rhs = jnp.concatenate([v_s[pl.ds(r0, ch), sl].astype(F32) * beta, kb * eg], axis=1)
            sol = _dot(t.astype(BF16), rhs.astype(BF16))
            u, w = sol[:, 0:HEAD_DIM], sol[:, HEAD_DIM:2 * HEAD_DIM]
            attn = _dot_nt(q, k) * decay
            q_dec = q.astype(F32) * eg
            k_dec = kf * jnp.exp(gtot - gc)
            s = st_s[col]
            sb = s.astype(BF16)
            v_new = u - _dot(w.astype(BF16), sb)
            vb = v_new.astype(BF16)
            outs.append(_dot(q_dec.astype(BF16), sb) + _dot(attn.astype(BF16), vb))
            st_s[col] = s * jnp.exp(gtot) + _dot_tn(k_dec.astype(BF16), vb)
        out_s[pl.ds(r0, ch), :] = jnp.concatenate(outs, axis=1)

    st_s[...] = jnp.zeros_like(st_s)
    ncc, nlc = n_ctx // ch, n_lat // ch

    def scan_ctx(n, carry):
        chunk_dir(n, 0, of_s)
        chunk_dir(ncc - 1 - n, 1, ob_s)
        return carry

    def scan_lat(n, carry):
        chunk_dir(ncc + n, 0, of_s)
        chunk_dir(ncc + nlc - 1 - n, 1, ob_s)
        return carry

    lax.fori_loop(0, ncc, scan_ctx, 0)
    lax.fori_loop(0, nlc, scan_lat, 0)

    ng = ng_ref[...]

    def finish(ref, out_ref, n_rows, row_off):
        def body(i, carry):
            t0 = pl.multiple_of(i * pt, pt)
            r0 = pl.multiple_of(row_off + t0, pt)
            o = of_s[pl.ds(r0, pt), :] + ob_s[pl.ds(r0, pt), :]
            z = ref[pl.ds(t0, pt), qkv_w:qkv_w + GDN_W].astype(F32)
            y = o * lax.rsqrt(_split_dot(o * o, gmean) + EPS) * ng * _silu(z)
            out_ref[pl.ds(t0, pt), :] = y.astype(out_ref.dtype)
            return carry

        lax.fori_loop(0, n_rows // pt, body, 0)

    finish(ctx_ref, oc_ref, n_ctx, 0)
    finish(lat_ref, ol_ref, n_lat, n_ctx)


def gdn_mixer(p_gdn, p_ab, conv_w, a_log, dt_bias, norm_gain, *, n_batch, n_lat, n_ctx):
    width = p_gdn.shape[1]
    ctx_blk0 = (n_batch * n_lat) // n_ctx
    t_all = n_lat + n_ctx
    cw = jnp.pad(conv_w.astype(F32), ((0, 8 - GDN_CONV), (0, 0)))
    par = jnp.zeros((8, LANES), F32)
    par = par.at[0, :2 * GDN_HEADS].set(a_log.reshape(-1)).at[1, :2 * GDN_HEADS].set(dt_bias.reshape(-1))
    ng = jnp.tile(norm_gain.astype(F32).reshape(1, HEAD_DIM), (1, GDN_HEADS))
    return pl.pallas_call(
        functools.partial(_gdn_kernel, n_ctx=n_ctx, n_lat=n_lat),
        out_shape=(jax.ShapeDtypeStruct((n_batch * n_lat, GDN_W), BF16),
                   jax.ShapeDtypeStruct((n_batch * n_ctx, GDN_W), BF16)),
        grid=(n_batch,),
        in_specs=[pl.BlockSpec((n_lat, width), lambda b: (b, 0)),
                  pl.BlockSpec((n_ctx, width), lambda b: (ctx_blk0 + b, 0)),
                  pl.BlockSpec((n_lat, LANES), lambda b: (b, 0)),
                  pl.BlockSpec((n_ctx, LANES), lambda b: (ctx_blk0 + b, 0)),
                  pl.BlockSpec(cw.shape, lambda b: (0, 0)),
                  pl.BlockSpec(par.shape, lambda b: (0, 0)),
                  pl.BlockSpec((1, GDN_W), lambda b: (0, 0))],
        out_specs=(pl.BlockSpec((n_lat, GDN_W), lambda b: (b, 0)),
                   pl.BlockSpec((n_ctx, GDN_W), lambda b: (b, 0))),
        scratch_shapes=[pltpu.VMEM((t_all, GDN_W), BF16), pltpu.VMEM((t_all, GDN_W), BF16),
                        pltpu.VMEM((t_all, GDN_W), BF16), pltpu.VMEM((t_all, LANES), F32),
                        pltpu.VMEM((t_all, GDN_W), F32), pltpu.VMEM((t_all, GDN_W), F32),
                        pltpu.VMEM((2 * GDN_HEADS, HEAD_DIM, HEAD_DIM), F32)],
        compiler_params=_cparams("parallel"),
        name="gdn",
    )(p_gdn, p_gdn, p_ab, p_ab, cw, par, ng)


SWA_BLOCK = 128
SWA_BAND = 3 * SWA_BLOCK


def _swa_kernel(sink_ref, q_ref, kl_ref, vl_ref, kc_ref, vc_ref, cos_ref, sin_ref, o_ref, *, band, n_lat):
    blk = SWA_BLOCK
    i = pl.program_id(1)
    q = q_ref[...].astype(F32)
    kc = kc_ref[...]
    vc = vc_ref[...]
    if band:
        q0 = pl.multiple_of(i * blk, blk)
        start = pl.multiple_of(jnp.clip((i - 1) * blk, 0, n_lat - SWA_BAND), blk)
        q = _rope(q, cos_ref[pl.ds(q0, blk), :], sin_ref[pl.ds(q0, blk), :])
        kb = _rope(kl_ref[pl.ds(start, SWA_BAND), :].astype(F32),
                   cos_ref[pl.ds(start, SWA_BAND), :], sin_ref[pl.ds(start, SWA_BAND), :]).astype(BF16)
        vb = vl_ref[pl.ds(start, SWA_BAND), :]
        keys = jnp.concatenate([kb, kc], axis=0)
        vals = jnp.concatenate([vb, vc], axis=0)
    else:
        keys, vals = kc, vc
    qb = q.astype(BF16)
    n_keys = keys.shape[0]
    rows = SWA_GROUP * blk
    row_id = lax.broadcasted_iota(jnp.int32, (rows, 1), 0)
    if band:
        q_pos = q0 + row_id % blk
        k_pos = start + lax.broadcasted_iota(jnp.int32, (1, n_keys), 1)
        col_id = lax.broadcasted_iota(jnp.int32, (1, n_keys), 1)
        valid = (col_id >= SWA_BAND) | (jnp.abs(k_pos - q_pos) <= WINDOW)
    outs = []
    for g in range(SWA_KV_HEADS):
        ksl = slice(g * HEAD_DIM, (g + 1) * HEAD_DIM)
        q4 = jnp.concatenate([qb[:, (g * SWA_GROUP + r) * HEAD_DIM:(g * SWA_GROUP + r + 1) * HEAD_DIM]
                              for r in range(SWA_GROUP)], axis=0)
        s = _dot_nt(q4, keys[:, ksl]) * (HEAD_DIM ** -0.5)
        if band:
            s = jnp.where(valid, s, NEG_BIG)
        sink = jnp.zeros((rows, 1), F32)
        for r in range(SWA_GROUP):
            sink = jnp.where(row_id // blk == r, sink_ref[g * SWA_GROUP + r], sink)
        m = jnp.maximum(jnp.max(s, axis=-1, keepdims=True), sink)
        e = jnp.exp(s - m)
        den = jnp.sum(e, axis=-1, keepdims=True) + jnp.exp(sink - m)
        o4 = _dot(e.astype(BF16), vals[:, ksl]) * (1.0 / den)
        outs += [o4[r * blk:(r + 1) * blk] for r in range(SWA_GROUP)]
    o_ref[...] = jnp.concatenate(outs, axis=1).astype(o_ref.dtype)


def swa_mixer(p_swa, cos, sin, sink, *, n_batch, n_lat, n_ctx):
    blk = SWA_BLOCK
    ctx_row0 = n_batch * n_lat
    kcol, vcol = SWA_W // SWA_KV_W, SWA_W // SWA_KV_W + 1
    sink = sink.astype(F32)

    def call(band):
        n_q = n_lat if band else n_ctx
        nb = n_q // blk
        qrow0 = 0 if band else ctx_row0 // blk
        grid_spec = pltpu.PrefetchScalarGridSpec(
            num_scalar_prefetch=1,
            grid=(n_batch, nb),
            in_specs=[pl.BlockSpec((blk, SWA_W), lambda b, i, s: (qrow0 + b * nb + i, 0)),
                      pl.BlockSpec((n_lat, SWA_KV_W), lambda b, i, s: (b, kcol)),
                      pl.BlockSpec((n_lat, SWA_KV_W), lambda b, i, s: (b, vcol)),
                      pl.BlockSpec((n_ctx, SWA_KV_W), lambda b, i, s: (ctx_row0 // n_ctx + b, kcol)),
                      pl.BlockSpec((n_ctx, SWA_KV_W), lambda b, i, s: (ctx_row0 // n_ctx + b, vcol)),
                      pl.BlockSpec((n_lat, LANES), lambda b, i, s: (0, 0)),
                      pl.BlockSpec((n_lat, LANES), lambda b, i, s: (0, 0))],
            out_specs=pl.BlockSpec((blk, SWA_W), lambda b, i, s: (b * nb + i, 0)),
        )
        return pl.pallas_call(
            functools.partial(_swa_kernel, band=band, n_lat=n_lat),
            out_shape=jax.ShapeDtypeStruct((n_batch * n_q, SWA_W), BF16),
            grid_spec=grid_spec,
            compiler_params=_cparams("parallel", "arbitrary"),
            name="swa_latent" if band else "swa_context",
        )(sink, p_swa, p_swa, p_swa, p_swa, p_swa, cos, sin)

    return call(True), call(False)


MOE_ROW_TILE = 512
MOE_FF_TILE = 512


def _row_tile(n_lat, n_ctx_rows):
    for tm in (512, 256, 128):
        if n_lat % tm == 0 and n_ctx_rows % tm == 0:
            return tm
    raise ValueError("sequence lengths must be multiples of 128")


def kernel(x, c, ctx, c_ctx, ada_w, ada_b, norm1_g, w_in, ret_decay, ret_gn_g, gdn_conv_w, gdn_a_log, gdn_dt_bias,
           gdn_norm_g, swa_sink, w_out, norm2_g, ffn_w1, ffn_w3, ffn_w2, router_w, moe_w1, moe_w3, moe_w2, final_g):
    b, s, d = x.shape
    n_ctx = ctx.shape[1]
    depth = ada_w.shape[0]
    nl, ncx = b * s, b * n_ctx
    tm = _row_tile(s, ncx)
    seg = _seg_map(nl // tm, s // tm, b)

    cvec = jnp.zeros((8, d), F32).at[:b].set(c).at[b].set(c_ctx)
    mod_all = adaln(cvec, ada_w, ada_b).reshape(depth, 8, 6, d)
    xall = jnp.concatenate([x.reshape(nl, d), ctx.reshape(ncx, d)], axis=0)
    cos, sin = rope_tables(s)
    fg = final_g.reshape(1, d)

    ret_cols, gdn_cols = 4 * RET_W, 4 * GDN_W
    ab_cols = 4 * GDN_HEADS
    for layer in range(depth):
        last = layer == depth - 1
        mod = mod_all[layer]
        w = w_in[layer].astype(BF16)
        w_ret = w[:, :ret_cols]
        w_gdn = w[:, ret_cols:ret_cols + gdn_cols]
        w_ab = jnp.pad(w[:, ret_cols + gdn_cols:ret_cols + gdn_cols + ab_cols], ((0, 0), (0, LANES - ab_cols)))
        w_swa = w[:, ret_cols + gdn_cols + ab_cols:]
        p_ret, p_gdn, p_swa, p_ab = in_proj(xall, mod, norm1_g[layer].reshape(1, d), w_ret, w_gdn, w_swa, w_ab,
                                            tm=tm, seg=seg)
        dec_lanes = jnp.repeat(ret_decay[layer].astype(F32), HEAD_DIM, axis=1)
        o_ret = retention_mixer(p_ret, cos, sin, dec_lanes, ret_gn_g[layer].reshape(1, RET_W),
                                n_batch=b, n_lat=s, n_ctx=n_ctx)
        o_gdn = gdn_mixer(p_gdn, p_ab, gdn_conv_w[layer], gdn_a_log[layer], gdn_dt_bias[layer], gdn_norm_g[layer],
                          n_batch=b, n_lat=s, n_ctx=n_ctx)
        o_swa = swa_mixer(p_swa, cos, sin, swa_sink[layer], n_batch=b, n_lat=s, n_ctx=n_ctx)

        n_rows = nl if last else nl + ncx
        is_moe = layer % 2 == 1
        i = layer // 2
        rw = None
        if is_moe:
            rw = jnp.pad(router_w[i].astype(F32), ((0, 0), (0, LANES - N_EXPERTS)))
        res = out_proj(o_ret, o_gdn, o_swa, xall, mod, w_out[layer].astype(BF16), norm2_g[layer].reshape(1, d), rw,
                       tm=tm, seg=seg, n_rows=n_rows, n_lat_rows=nl, tok_dtype=F32 if is_moe else BF16)
        if is_moe:
            x_new, tokens, ridx, rgate, cnt = res
            xall = moe_layer(tokens, ridx, rgate, cnt, x_new, mod, moe_w1[i].astype(BF16), moe_w3[i].astype(BF16),
                             moe_w2[i].astype(BF16), fg, tm=tm, tmx=MOE_ROW_TILE, tf=MOE_FF_TILE, seg=seg, last=last)
        else:
            x_new, tokens = res
            xall = dense_ffn(tokens, x_new, mod, ffn_w1[i].astype(BF16), ffn_w3[i].astype(BF16),
                             ffn_w2[i].astype(BF16), fg, tm=tm, seg=seg, last=last)
    return xall[:nl].reshape(b, s, d)
```

```python
import functools

import jax
import jax.numpy as jnp
import numpy as np
from jax import lax
from jax.experimental import pallas as pl
from jax.experimental.pallas import tpu as pltpu

F32 = jnp.float32
BF16 = jnp.bfloat16

HEAD_DIM = 64
GRID_W = 64
ROPE_BASE = 10000.0
EPS = 1e-6
RET_HEADS = 4
GDN_HEADS = 4
GDN_CONV = 5
SWA_HEADS = 8
SWA_KV_HEADS = 2
SWA_GROUP = SWA_HEADS // SWA_KV_HEADS
WINDOW = 128
N_EXPERTS = 8
RET_W = RET_HEADS * HEAD_DIM
GDN_W = GDN_HEADS * HEAD_DIM
SWA_W = SWA_HEADS * HEAD_DIM
SWA_KV_W = SWA_KV_HEADS * HEAD_DIM
LANES = 128
VMEM_LIMIT = 56 * 1024 * 1024
NEG_BIG = -1e30


def _cparams(*sem):
    return pltpu.CompilerParams(dimension_semantics=sem, vmem_limit_bytes=VMEM_LIMIT)


def _dot(a, b):
    return jnp.dot(a, b, preferred_element_type=F32)


def _dot_nt(a, b):
    return lax.dot_general(a, b, (((1,), (1,)), ((), ())), preferred_element_type=F32)


def _dot_tn(a, b):
    return lax.dot_general(a, b, (((0,), (0,)), ((), ())), preferred_element_type=F32)


def _split_dot(x, m_bf16, terms=2):
    acc = None
    rem = x
    for _ in range(terms):
        part = rem.astype(BF16)
        d = _dot(part, m_bf16)
        acc = d if acc is None else acc + d
        rem = rem - part.astype(F32)
    return acc


def _sigmoid(x):
    return 1.0 / (1.0 + jnp.exp(-x))


def _silu(x):
    return x * _sigmoid(x)


def _rms_mod(x, gain, shift, scale):
    ms = jnp.mean(x * x, axis=-1, keepdims=True)
    y = x * lax.rsqrt(ms + EPS) * gain
    return y * (1.0 + scale) + shift


def _seg_map(n_lat_tiles, tiles_per_batch, n_batch):
    def seg(i):
        return jnp.where(i < n_lat_tiles, i // tiles_per_batch, n_batch)
    return seg


def _adaln_kernel(c_ref, w_ref, b_ref, o_ref):
    h = _silu(c_ref[...]).astype(BF16)
    o_ref[0] = _dot(h, w_ref[0].astype(BF16)) + b_ref[0]


def adaln(cvec, ada_w, ada_b):
    depth, d, n6 = ada_w.shape
    tn = 1536 if n6 % 1536 == 0 else n6
    rows = cvec.shape[0]
    return pl.pallas_call(
        _adaln_kernel,
        out_shape=jax.ShapeDtypeStruct((depth, rows, n6), F32),
        grid=(depth, n6 // tn),
        in_specs=[pl.BlockSpec((rows, d), lambda l, j: (0, 0)),
                  pl.BlockSpec((1, d, tn), lambda l, j: (l, 0, j)),
                  pl.BlockSpec((1, 1, tn), lambda l, j: (l, 0, j))],
        out_specs=pl.BlockSpec((1, rows, tn), lambda l, j: (l, 0, j)),
        compiler_params=_cparams("parallel", "parallel"),
        name="adaln",
    )(cvec, ada_w, ada_b.reshape(depth, 1, n6))


def _inproj_kernel(x_ref, mod_ref, g_ref, wr_ref, wg_ref, ws_ref, wab_ref, pr_ref, pg_ref, ps_ref, pab_ref):
    h = _rms_mod(x_ref[...], g_ref[...], mod_ref[0, 0:1, :], mod_ref[0, 1:2, :]).astype(BF16)
    for w_ref, o_ref in ((wr_ref, pr_ref), (wg_ref, pg_ref), (ws_ref, ps_ref), (wab_ref, pab_ref)):
        n = w_ref.shape[1]
        step = 256 if n % 256 == 0 else n
        for c in range(0, n, step):
            o_ref[:, c:c + step] = _dot(h, w_ref[:, c:c + step]).astype(o_ref.dtype)


def in_proj(xall, mod, gain, w_ret, w_gdn, w_swa, w_ab, *, tm, seg):
    n, d = xall.shape
    full = lambda i: (0, 0)
    row = lambda i: (i, 0)
    outs = (jax.ShapeDtypeStruct((n, w_ret.shape[1]), BF16), jax.ShapeDtypeStruct((n, w_gdn.shape[1]), BF16),
            jax.ShapeDtypeStruct((n, w_swa.shape[1]), BF16), jax.ShapeDtypeStruct((n, w_ab.shape[1]), F32))
    return pl.pallas_call(
        _inproj_kernel,
        out_shape=outs,
        grid=(n // tm,),
        in_specs=[pl.BlockSpec((tm, d), row),
                  pl.BlockSpec((1, 6, d), lambda i: (seg(i), 0, 0)),
                  pl.BlockSpec((1, d), full),
                  pl.BlockSpec(w_ret.shape, full), pl.BlockSpec(w_gdn.shape, full),
                  pl.BlockSpec(w_swa.shape, full), pl.BlockSpec(w_ab.shape, full)],
        out_specs=tuple(pl.BlockSpec((tm, o.shape[1]), row) for o in outs),
        compiler_params=_cparams("parallel"),
        name="in_proj",
    )(xall, mod, gain, w_ret, w_gdn, w_swa, w_ab)


def _outproj_kernel(*refs, tm, moe, n_lat_tiles, has_ctx):
    n_mix = 6 if has_ctx else 3
    mix = refs[:n_mix]
    refs = refs[n_mix:]
    if moe:
        (x_ref, mod_ref, w_ref, g_ref, rw_ref, xo_ref, tok_ref, ridx_ref, rgate_ref, cnt_ref, base_ref) = refs
    else:
        x_ref, mod_ref, w_ref, g_ref, xo_ref, tok_ref = refs
    if has_ctx:
        is_lat = pl.program_id(0) < n_lat_tiles
        o_ret, o_gdn, o_swa = (jnp.where(is_lat, mix[2 * k][...], mix[2 * k + 1][...]) for k in range(3))
    else:
        o_ret, o_gdn, o_swa = (m[...] for m in mix)
    y = _dot(o_ret, w_ref[0:RET_W, :])
    y = y + _dot(o_gdn, w_ref[RET_W:RET_W + GDN_W, :])
    y = y + _dot(o_swa, w_ref[RET_W + GDN_W:, :])
    x = x_ref[...] + mod_ref[0, 2:3, :] * y
    xo_ref[...] = x
    t = _rms_mod(x, g_ref[...], mod_ref[0, 3:4, :], mod_ref[0, 4:5, :])
    tok_ref[...] = t.astype(tok_ref.dtype)
    if not moe:
        return

    @pl.when(pl.program_id(0) == 0)
    def _():
        base_ref[...] = jnp.zeros_like(base_ref)

    rw = rw_ref[...]
    rw_hi = rw.astype(BF16)
    rw_lo = (rw - rw_hi.astype(F32)).astype(BF16)
    t_hi = t.astype(BF16)
    t_lo = (t - t_hi.astype(F32)).astype(BF16)
    logits = _dot(t_hi, rw_hi) + _dot(t_hi, rw_lo) + _dot(t_lo, rw_hi)
    lane = lax.broadcasted_iota(jnp.int32, logits.shape, 1)
    logits = jnp.where(lane < N_EXPERTS, logits, NEG_BIG)
    m1 = jnp.max(logits, axis=-1, keepdims=True)
    i1 = jnp.min(jnp.where(logits == m1, lane, LANES), axis=-1, keepdims=True)
    rest = jnp.where(lane == i1, NEG_BIG, logits)
    m2 = jnp.max(rest, axis=-1, keepdims=True)
    i2 = jnp.min(jnp.where(rest == m2, lane, LANES), axis=-1, keepdims=True)
    e2 = jnp.exp(m2 - m1)
    g1 = 1.0 / (1.0 + e2)
    g2 = e2 * g1
    oh1 = jnp.where(lane == i1, 1.0, 0.0)
    oh2 = jnp.where(lane == i2, 1.0, 0.0)
    r_i = lax.broadcasted_iota(jnp.int32, (tm, tm), 0)
    c_i = lax.broadcasted_iota(jnp.int32, (tm, tm), 1)
    lower = jnp.where(r_i > c_i, 1.0, 0.0).astype(BF16)
    cum1 = _dot(lower, oh1.astype(BF16))
    cum2 = _dot(lower, oh2.astype(BF16))
    cnt1 = jnp.sum(oh1, axis=0, keepdims=True)
    cnt2 = jnp.sum(oh2, axis=0, keepdims=True)
    base = base_ref[...]
    rank1 = jnp.sum(oh1 * (base + cum1), axis=-1, keepdims=True)
    rank2 = jnp.sum(oh2 * (base + cnt1 + cum2), axis=-1, keepdims=True)
    base = base + cnt1 + cnt2
    base_ref[...] = base
    cnt_ref[...] = base.astype(jnp.int32)
    lane8 = lax.broadcasted_iota(jnp.int32, (tm, 8), 1)
    r1 = rank1.astype(jnp.int32)
    r2 = rank2.astype(jnp.int32)
    ridx_ref[...] = jnp.where(lane8 == 0, i1, jnp.where(lane8 == 1, i2, jnp.where(lane8 == 2, r1,
                              jnp.where(lane8 == 3, r2, 0))))
    rgate_ref[...] = jnp.where(lane8 == 0, g1, jnp.where(lane8 == 1, g2, 0.0))


def out_proj(o_ret, o_gdn, o_swa, xall, mod, w_out, gain2, router_w, *, tm, seg, n_rows, n_lat_rows, tok_dtype):
    d = xall.shape[1]
    moe = router_w is not None
    has_ctx = n_rows > n_lat_rows
    n_lat_tiles = n_lat_rows // tm
    full = lambda i: (0, 0)
    row = lambda i: (i, 0)
    lat_row = lambda i: (jnp.minimum(i, n_lat_tiles - 1), 0)
    ctx_row = lambda i: (jnp.maximum(i - n_lat_tiles, 0), 0)
    in_specs, args = [], []
    for pair, width in ((o_ret, RET_W), (o_gdn, GDN_W), (o_swa, SWA_W)):
        in_specs.append(pl.BlockSpec((tm, width), lat_row))
        args.append(pair[0])
        if has_ctx:
            in_specs.append(pl.BlockSpec((tm, width), ctx_row))
            args.append(pair[1])
    in_specs += [pl.BlockSpec((tm, d), row), pl.BlockSpec((1, 6, d), lambda i: (seg(i), 0, 0)),
                 pl.BlockSpec(w_out.shape, full), pl.BlockSpec((1, d), full)]
    args += [xall, mod, w_out, gain2]
    outs = [jax.ShapeDtypeStruct((n_rows, d), F32), jax.ShapeDtypeStruct((n_rows, d), tok_dtype)]
    out_specs = [pl.BlockSpec((tm, d), row), pl.BlockSpec((tm, d), row)]
    scratch = []
    if moe:
        in_specs.append(pl.BlockSpec(router_w.shape, full))
        args.append(router_w)
        outs += [jax.ShapeDtypeStruct((n_rows, 8), jnp.int32), jax.ShapeDtypeStruct((n_rows, 8), F32),
                 jax.ShapeDtypeStruct((1, LANES), jnp.int32)]
        out_specs += [pl.BlockSpec((tm, 8), row), pl.BlockSpec((tm, 8), row), pl.BlockSpec((1, LANES), full)]
        scratch = [pltpu.VMEM((1, LANES), F32)]
    return pl.pallas_call(
        functools.partial(_outproj_kernel, tm=tm, moe=moe, n_lat_tiles=n_lat_tiles, has_ctx=has_ctx),
        out_shape=tuple(outs),
        grid=(n_rows // tm,),
        in_specs=in_specs,
        out_specs=tuple(out_specs),
        scratch_shapes=scratch,
        compiler_params=_cparams("arbitrary"),
        name="out_proj_router" if moe else "out_proj",
    )(*args)


def _final_norm(x, gain):
    ms = jnp.mean(x * x, axis=-1, keepdims=True)
    return x * lax.rsqrt(ms + EPS) * gain


def _dense_ffn_kernel(tok_ref, x_ref, mod_ref, w1_ref, w3_ref, w2_ref, fg_ref, o_ref, *, last):
    t = tok_ref[...]
    h = (_silu(_dot(t, w1_ref[...])) * _dot(t, w3_ref[...])).astype(BF16)
    x = x_ref[...] + mod_ref[0, 5:6, :] * _dot(h, w2_ref[...])
    if last:
        x = _final_norm(x, fg_ref[...])
    o_ref[...] = x


def dense_ffn(tokens, xall, mod, w1, w3, w2, final_g, *, tm, seg, last):
    n, d = tokens.shape
    f = w1.shape[1]
    full = lambda i: (0, 0)
    row = lambda i: (i, 0)
    return pl.pallas_call(
        functools.partial(_dense_ffn_kernel, last=last),
        out_shape=jax.ShapeDtypeStruct((n, d), F32),
        grid=(n // tm,),
        in_specs=[pl.BlockSpec((tm, d), row), pl.BlockSpec((tm, d), row),
                  pl.BlockSpec((1, 6, d), lambda i: (seg(i), 0, 0)),
                  pl.BlockSpec((d, f), full, pipeline_mode=pl.Buffered(1)),
                  pl.BlockSpec((d, f), full, pipeline_mode=pl.Buffered(1)),
                  pl.BlockSpec((f, d), full, pipeline_mode=pl.Buffered(1)),
                  pl.BlockSpec((1, d), full)],
        out_specs=pl.BlockSpec((tm, d), row),
        compiler_params=_cparams("parallel"),
        name="dense_ffn",
    )(tokens, xall, mod, w1, w3, w2, final_g)


def _dispatch_kernel(dest_ref, tok_ref, xs_in_ref, xs_ref, sem, *, tm):
    del xs_in_ref

    def issue(r, carry):
        for k in range(2):
            dst = dest_ref[0, 0, 2 * r + k]
            pltpu.make_async_copy(tok_ref.at[pl.ds(r, 1)], xs_ref.at[pl.ds(dst, 1)], sem).start()
        return carry

    lax.fori_loop(0, tm, issue, 0, unroll=8)
    for _ in range(2):
        pltpu.make_async_copy(tok_ref, xs_ref.at[pl.ds(0, tm)], sem).wait()


def moe_dispatch(tokens, dest3, n_slots, *, tm):
    n, d = tokens.shape
    xs0 = jnp.zeros((n_slots, d), tokens.dtype)
    return pl.pallas_call(
        functools.partial(_dispatch_kernel, tm=tm),
        out_shape=jax.ShapeDtypeStruct((n_slots, d), tokens.dtype),
        grid=(n // tm,),
        in_specs=[pl.BlockSpec((1, 1, 2 * tm), lambda i: (i, 0, 0), memory_space=pltpu.SMEM),
                  pl.BlockSpec((tm, d), lambda i: (i, 0)),
                  pl.BlockSpec(memory_space=pl.ANY)],
        out_specs=pl.BlockSpec(memory_space=pl.ANY),
        scratch_shapes=[pltpu.SemaphoreType.DMA],
        input_output_aliases={2: 0},
        compiler_params=_cparams("arbitrary"),
        name="moe_dispatch",
    )(dest3, tokens, xs0)


def _moe_ffn_kernel(te_ref, nu_ref, xs_ref, w1_ref, w3_ref, w2_ref, o_ref, acc_ref):
    i = pl.program_id(0)
    j = pl.program_id(1)

    @pl.when(i < nu_ref[0])
    def _():
        x = xs_ref[...].astype(BF16)
        a = _dot(x, w1_ref[0])
        b = _dot(x, w3_ref[0])
        h = (_silu(a) * b).astype(BF16)
        upd = _dot(h, w2_ref[0])

        @pl.when(j == 0)
        def _():
            acc_ref[...] = upd

        @pl.when(j > 0)
        def _():
            acc_ref[...] += upd

        @pl.when(j == pl.num_programs(1) - 1)
        def _():
            o_ref[...] = acc_ref[...]

    @pl.when((i >= nu_ref[0]) & (j == pl.num_programs(1) - 1))
    def _():
        o_ref[...] = jnp.zeros_like(o_ref)


def moe_ffn(xs, tile_expert, n_used, w1, w3, w2, *, tm, tf):
    r, d = xs.shape
    f = w1.shape[2]
    nf = f // tf

    def tile(i, nu):
        return jnp.minimum(i, nu[0] - 1)

    def ftile(i, j, nu):
        return jnp.where(i < nu[0], j, nf - 1)

    grid_spec = pltpu.PrefetchScalarGridSpec(
        num_scalar_prefetch=2,
        grid=(r // tm, nf),
        in_specs=[pl.BlockSpec((tm, d), lambda i, j, te, nu: (tile(i, nu), 0)),
                  pl.BlockSpec((1, d, tf), lambda i, j, te, nu: (te[tile(i, nu)], 0, ftile(i, j, nu))),
                  pl.BlockSpec((1, d, tf), lambda i, j, te, nu: (te[tile(i, nu)], 0, ftile(i, j, nu))),
                  pl.BlockSpec((1, tf, d), lambda i, j, te, nu: (te[tile(i, nu)], ftile(i, j, nu), 0))],
        out_specs=pl.BlockSpec((tm, d), lambda i, j, te, nu: (i, 0)),
        scratch_shapes=[pltpu.VMEM((tm, d), F32)],
    )
    return pl.pallas_call(
        _moe_ffn_kernel,
        out_shape=jax.ShapeDtypeStruct((r, d), F32),
        grid_spec=grid_spec,
        compiler_params=_cparams("arbitrary", "arbitrary"),
        name="moe_ffn",
    )(tile_expert, n_used, xs, w1, w3, w2)


def _combine_kernel(dest_ref, gate_ref, x_ref, mod_ref, fg_ref, ys_ref, o_ref, buf_ref, sem, *, tm, last):
    def issue(r, carry):
        for k in range(2):
            src = dest_ref[0, 0, 2 * r + k]
            pltpu.make_async_copy(ys_ref.at[pl.ds(src, 1)], buf_ref.at[k, pl.ds(r, 1)], sem).start()
        return carry

    lax.fori_loop(0, tm, issue, 0, unroll=8)
    for k in range(2):
        pltpu.make_async_copy(ys_ref.at[pl.ds(0, tm)], buf_ref.at[k], sem).wait()
    gate = gate_ref[...]
    y = gate[:, 0:1] * buf_ref[0] + gate[:, 1:2] * buf_ref[1]
    x = x_ref[...] + mod_ref[0, 5:6, :] * y
    if last:
        x = _final_norm(x, fg_ref[...])
    o_ref[...] = x


def moe_combine(ys, dest3, rgate, xall, mod, final_g, *, tm, seg, last):
    n, d = xall.shape
    return pl.pallas_call(
        functools.partial(_combine_kernel, tm=tm, last=last),
        out_shape=jax.ShapeDtypeStruct((n, d), F32),
        grid=(n // tm,),
        in_specs=[pl.BlockSpec((1, 1, 2 * tm), lambda i: (i, 0, 0), memory_space=pltpu.SMEM),
                  pl.BlockSpec((tm, 8), lambda i: (i, 0)),
                  pl.BlockSpec((tm, d), lambda i: (i, 0)),
                  pl.BlockSpec((1, 6, d), lambda i: (seg(i), 0, 0)),
                  pl.BlockSpec((1, d), lambda i: (0, 0)),
                  pl.BlockSpec(memory_space=pl.ANY)],
        out_specs=pl.BlockSpec((tm, d), lambda i: (i, 0)),
        scratch_shapes=[pltpu.VMEM((2, tm, d), F32), pltpu.SemaphoreType.DMA],
        compiler_params=_cparams("arbitrary"),
        name="moe_combine",
    )(dest3, rgate, xall, mod, final_g, ys)


def moe_layer(tokens, ridx, rgate, cnt, xall, mod, w1, w3, w2, final_g, *, tm, tmx, tf, seg, last):
    n = tokens.shape[0]
    counts = cnt[0, :N_EXPERTS]
    padded = (counts + tmx - 1) // tmx * tmx
    pend = jnp.cumsum(padded)
    pstart = pend - padded
    dest = pstart[ridx[:, 0:2]] + ridx[:, 2:4]
    dest3 = dest.reshape(n // tm, 1, 2 * tm).astype(jnp.int32)
    n_tiles = (2 * n) // tmx + N_EXPERTS
    n_used = (pend[-1] // tmx).astype(jnp.int32).reshape(1)
    tile_start = jnp.arange(n_tiles, dtype=jnp.int32) * tmx
    tile_expert = jnp.minimum(jnp.searchsorted(pend, tile_start, side='right'), N_EXPERTS - 1).astype(jnp.int32)
    xs = moe_dispatch(tokens, dest3, n_tiles * tmx, tm=tm)
    ys = moe_ffn(xs, tile_expert, n_used, w1, w3, w2, tm=tmx, tf=tf)
    return moe_combine(ys, dest3, rgate, xall, mod, final_g, tm=tm, seg=seg, last=last)


def rope_tables(seq):
    rows = seq // GRID_W
    row = jnp.repeat(jnp.arange(rows, dtype=F32), GRID_W)
    col = jnp.tile(jnp.arange(GRID_W, dtype=F32), rows)
    n_freq = HEAD_DIM // 4
    inv = ROPE_BASE ** (-jnp.arange(n_freq, dtype=F32) / n_freq)
    ang = jnp.concatenate([row[:, None] * inv, col[:, None] * inv], axis=-1)
    cos, sin = jnp.cos(ang), jnp.sin(ang)
    cos64 = jnp.concatenate([cos, cos], axis=-1)
    sin64 = jnp.concatenate([-sin, sin], axis=-1)
    return jnp.tile(cos64, (1, 2)), jnp.tile(sin64, (1, 2))


def _tile_lanes(t, width):
    reps = width // t.shape[1]
    return t if reps == 1 else jnp.concatenate([t] * reps, axis=1)


def _rope(t, cos, sin):
    w = t.shape[1]
    lane = lax.broadcasted_iota(jnp.int32, t.shape, 1) % HEAD_DIM
    half = HEAD_DIM // 2
    rot = jnp.where(lane < half, pltpu.roll(t, w - half, 1), pltpu.roll(t, half, 1))
    return t * _tile_lanes(cos, w) + rot * _tile_lanes(sin, w)


def _group_matrix(width, value):
    r = lax.broadcasted_iota(jnp.int32, (width, width), 0) // HEAD_DIM
    c = lax.broadcasted_iota(jnp.int32, (width, width), 1) // HEAD_DIM
    return jnp.where(r == c, value, 0.0).astype(BF16)


RET_CHUNK = 128


def _retention_kernel(lat_ref, ctx_ref, cos_ref, sin_ref, dec_ref, gn_ref, ol_ref, oc_ref,
                      q_s, k_s, bn_s, st_s, *, n_ctx, n_lat):
    ch = RET_CHUNK
    nc, nl = n_ctx // ch, n_lat // ch
    lg = -jnp.exp(dec_ref[...])
    lgf, lgb = lg[0:1], lg[1:2]
    pos = lax.broadcasted_iota(jnp.int32, (ch, 1), 0).astype(F32)
    wkf = jnp.exp((ch - 1 - pos) * lgf)
    wkb = jnp.exp(pos * lgb)
    wqf = jnp.exp((pos + 1) * lgf)
    wqb = jnp.exp((ch - pos) * lgb)
    cdf = jnp.exp(ch * lgf)
    cdb = jnp.exp(ch * lgb)
    blockmask = (lax.broadcasted_iota(jnp.int32, (RET_W, RET_W), 0) // HEAD_DIM
                 == lax.broadcasted_iota(jnp.int32, (RET_W, RET_W), 1) // HEAD_DIM)
    gmean = _group_matrix(RET_W, 1.0 / HEAD_DIM)
    diff = (lax.broadcasted_iota(jnp.int32, (ch, ch), 0) - lax.broadcasted_iota(jnp.int32, (ch, ch), 1)).astype(F32)
    gn = gn_ref[...]

    def load_qkv(ref, t0):
        return (ref[pl.ds(t0, ch), 0:RET_W].astype(F32), ref[pl.ds(t0, ch), RET_W:2 * RET_W].astype(F32),
                ref[pl.ds(t0, ch), 2 * RET_W:3 * RET_W])

    def reverse_step(ref, t0, c, roped):
        q, k, v = load_qkv(ref, t0)
        if roped:
            cos, sin = cos_ref[pl.ds(t0, ch), :], sin_ref[pl.ds(t0, ch), :]
            q, k = _rope(q, cos, sin), _rope(k, cos, sin)
        k = k * (HEAD_DIM ** -0.5)
        r0 = pl.multiple_of(c * ch, ch)
        q_s[pl.ds(r0, ch), :] = q.astype(BF16)
        k_s[pl.ds(r0, ch), :] = k.astype(BF16)
        st = st_s[...]
        bn_s[c] = st.astype(BF16)
        kvb = _dot_tn((k * wkb).astype(BF16), v)
        st_s[...] = cdb * st + jnp.where(blockmask, kvb, 0.0)

    st_s[...] = jnp.zeros_like(st_s)

    def rev_ctx(n, carry):
        c = nc - 1 - n
        reverse_step(ctx_ref, pl.multiple_of(c * ch, ch), c, False)
        return carry

    def rev_lat(n, carry):
        c = nl - 1 - n
        reverse_step(lat_ref, pl.multiple_of(c * ch, ch), nc + c, True)
        return carry

    lax.fori_loop(0, nc, rev_ctx, 0)
    lax.fori_loop(0, nl, rev_lat, 0)

    def forward_step(ref, out_ref, t0, c):
        r0 = pl.multiple_of(c * ch, ch)
        q = q_s[pl.ds(r0, ch), :]
        k = k_s[pl.ds(r0, ch), :]
        v = ref[pl.ds(t0, ch), 2 * RET_W:3 * RET_W]
        g = ref[pl.ds(t0, ch), 3 * RET_W:4 * RET_W].astype(F32)
        heads = []
        for h in range(RET_HEADS):
            sl = slice(h * HEAD_DIM, (h + 1) * HEAD_DIM)
            s = _dot_nt(q[:, sl], k[:, sl])
            lf, lb = lgf[:, h * HEAD_DIM:h * HEAD_DIM + 1], lgb[:, h * HEAD_DIM:h * HEAD_DIM + 1]
            dm = jnp.where(diff == 0.0, 2.0, jnp.exp(jnp.abs(diff) * jnp.where(diff > 0.0, lf, lb)))
            heads.append(_dot((s * dm).astype(BF16), v[:, sl]))
        qf = q.astype(F32)
        st = st_s[...]
        o = jnp.concatenate(heads, axis=1)
        o = o + _dot((qf * wqf).astype(BF16), st.astype(BF16)) + _dot((qf * wqb).astype(BF16), bn_s[c])
        kvf = _dot_tn((k.astype(F32) * wkf).astype(BF16), v)
        st_s[...] = cdf * st + jnp.where(blockmask, kvf, 0.0)
        mu = _split_dot(o, gmean)
        xc = o - mu
        var = _split_dot(xc * xc, gmean)
        out_ref[pl.ds(t0, ch), :] = (xc * lax.rsqrt(var + EPS) * gn * _silu(g)).astype(out_ref.dtype)

    st_s[...] = jnp.zeros_like(st_s)

    def fwd_ctx(n, carry):
        forward_step(ctx_ref, oc_ref, pl.multiple_of(n * ch, ch), n)
        return carry

    def fwd_lat(n, carry):
        forward_step(lat_ref, ol_ref, pl.multiple_of(n * ch, ch), nc + n)
        return carry

    lax.fori_loop(0, nc, fwd_ctx, 0)
    lax.fori_loop(0, nl, fwd_lat, 0)


def retention_mixer(p_ret, cos, sin, dec_lanes, gn_gain, *, n_batch, n_lat, n_ctx):
    n = p_ret.shape[0]
    width = p_ret.shape[1]
    ctx_blk0 = (n_batch * n_lat) // n_ctx
    t_all = n_lat + n_ctx
    nch = t_all // RET_CHUNK
    return pl.pallas_call(
        functools.partial(_retention_kernel, n_ctx=n_ctx, n_lat=n_lat),
        out_shape=(jax.ShapeDtypeStruct((n_batch * n_lat, RET_W), BF16),
                   jax.ShapeDtypeStruct((n_batch * n_ctx, RET_W), BF16)),
        grid=(n_batch,),
        in_specs=[pl.BlockSpec((n_lat, width), lambda b: (b, 0)),
                  pl.BlockSpec((n_ctx, width), lambda b: (ctx_blk0 + b, 0)),
                  pl.BlockSpec((n_lat, LANES), lambda b: (0, 0)),
                  pl.BlockSpec((n_lat, LANES), lambda b: (0, 0)),
                  pl.BlockSpec((2, RET_W), lambda b: (0, 0)),
                  pl.BlockSpec((1, RET_W), lambda b: (0, 0))],
        out_specs=(pl.BlockSpec((n_lat, RET_W), lambda b: (b, 0)),
                   pl.BlockSpec((n_ctx, RET_W), lambda b: (b, 0))),
        scratch_shapes=[pltpu.VMEM((t_all, RET_W), BF16), pltpu.VMEM((t_all, RET_W), BF16),
                        pltpu.VMEM((nch, RET_W, RET_W), BF16), pltpu.VMEM((RET_W, RET_W), F32)],
        compiler_params=_cparams("parallel"),
        name="retention",
    )(p_ret, p_ret, cos, sin, dec_lanes, gn_gain)


GDN_CHUNK = 64
GDN_PREP = 128
GDN_HALO = 8


def _softplus(x):
    return jnp.maximum(x, 0.0) + jnp.log(1.0 + jnp.exp(-jnp.abs(x)))


def _unit_tri_inverses(mats):
    n = mats[0].shape[0]
    ri = lax.broadcasted_iota(jnp.int32, (n, n), 0)
    ci = lax.broadcasted_iota(jnp.int32, (n, n), 1)
    b16 = (ri // 16) == (ci // 16)
    b32 = (ri // 32) == (ci // 32)
    eye = jnp.where(ri == ci, 1.0, 0.0)
    ps = [jnp.where(b16, -a, 0.0) for a in mats]
    ts = [eye + p for p in ps]
    for _ in range(3):
        pbs = [p.astype(BF16) for p in ps]
        ps = [_dot(pb, pb) for pb in pbs]
        ts = [t + _dot(t.astype(BF16), p.astype(BF16)) for t, p in zip(ts, ps)]
    for keep in (b32 & ~b16, ~b32):
        offs = [jnp.where(keep, a, 0.0).astype(BF16) for a in mats]
        tbs = [t.astype(BF16) for t in ts]
        mids = [_dot(tb, off).astype(BF16) for tb, off in zip(tbs, offs)]
        ts = [t - _dot(mid, tb) for t, mid, tb in zip(ts, mids, tbs)]
    return ts


def _gdn_kernel(lat_ref, ctx_ref, abl_ref, abc_ref, cw_ref, par_ref, ng_ref, ol_ref, oc_ref,
                q_s, k_s, v_s, gb_s, of_s, ob_s, *st_s, n_ctx, n_lat):
    ch = GDN_CHUNK
    pt = GDN_PREP
    halo = GDN_HALO
    qkv_w = 3 * GDN_W
    gsum = _group_matrix(GDN_W, 1.0)
    gmean = _group_matrix(GDN_W, 1.0 / HEAD_DIM)
    lane128 = lax.broadcasted_iota(jnp.int32, (pt, LANES), 1)
    neg_a = -jnp.exp(par_ref[0:1, :])
    dt_bias = par_ref[1:2, :]
    taps = [cw_ref[j:j + 1, :] for j in range(GDN_CONV)]

    def prep(ref, ab_ref, n_rows, row_off):
        n_tiles = n_rows // pt

        def body(i, carry):
            t0 = pl.multiple_of(i * pt, pt)
            cur = ref[pl.ds(t0, pt), 0:qkv_w].astype(F32)
            p0 = pl.multiple_of(jnp.maximum(t0 - 16, 0), 16)
            n0 = pl.multiple_of(jnp.minimum(t0 + pt, n_rows - 16), 16)
            prev = ref[pl.ds(p0, 16), 0:qkv_w].astype(F32)[16 - halo:16]
            nxt = ref[pl.ds(n0, 16), 0:qkv_w].astype(F32)[0:halo]
            prev = jnp.where(i > 0, prev, 0.0)
            nxt = jnp.where(i < n_tiles - 1, nxt, 0.0)
            ext = jnp.concatenate([prev, cur, nxt], axis=0)
            rows = pt + 2 * halo
            acc = None
            for j in range(GDN_CONV):
                s = j - (GDN_CONV - 1) // 2
                sh = ext if s == 0 else pltpu.roll(ext, (rows - s) % rows, 0)
                term = sh[halo:halo + pt] * taps[j]
                acc = term if acc is None else acc + term
            act = _silu(acc)
            q, k, v = act[:, 0:GDN_W], act[:, GDN_W:2 * GDN_W], act[:, 2 * GDN_W:3 * GDN_W]
            q = q * lax.rsqrt(_split_dot(q * q, gsum) + EPS) * (HEAD_DIM ** -0.5)
            k = k * lax.rsqrt(_split_dot(k * k, gsum) + EPS)
            r0 = pl.multiple_of(row_off + t0, pt)
            q_s[pl.ds(r0, pt), :] = q.astype(BF16)
            k_s[pl.ds(r0, pt), :] = k.astype(BF16)
            v_s[pl.ds(r0, pt), :] = v.astype(BF16)
            ab = ab_ref[pl.ds(t0, pt), :]
            g = neg_a * _softplus(ab + dt_bias)
            gb_s[pl.ds(r0, pt), :] = jnp.where(lane128 < 2 * GDN_HEADS, g, _sigmoid(ab))
            return carry

        lax.fori_loop(0, n_tiles, body, 0)

    prep(ctx_ref, abc_ref, n_ctx, 0)
    prep(lat_ref, abl_ref, n_lat, n_ctx)

    ri = lax.broadcasted_iota(jnp.int32, (ch, ch), 0)
    ci = lax.broadcasted_iota(jnp.int32, (ch, ch), 1)
    incl = (ri >= ci, ri <= ci)
    strict = (ri > ci, ri < ci)
    tri = tuple(jnp.where(m, 1.0, 0.0).astype(BF16) for m in incl)

    chains = [(d, h) for d in range(2) for h in range(GDN_HEADS)]

    def scan_step(cf, cb):
        r0 = [pl.multiple_of(c * ch, ch) for c in (cf, cb)]
        gbc = [gb_s[pl.ds(r, ch), :] for r in r0]
        gcs = []
        for d in range(2):
            rem, acc = gbc[d], None
            for _ in range(3):
                part = rem.astype(BF16)
                term = _dot(tri[d], part)
                acc = term if acc is None else acc + term
                rem = rem - part.astype(F32)
            gcs.append(acc)
        gcs_t = [g.T for g in gcs]
        edge = (ch - 1, 0)
        cols = [d * GDN_HEADS + h for d, h in chains]
        sls = [slice(h * HEAD_DIM, (h + 1) * HEAD_DIM) for _, h in chains]
        gc = [gcs[d][:, c:c + 1] for (d, _), c in zip(chains, cols)]
        gr = [gcs_t[d][c:c + 1, :] for (d, _), c in zip(chains, cols)]
        gtot = [gcs[d][edge[d]:edge[d] + 1, c:c + 1] for (d, _), c in zip(chains, cols)]
        beta = [gbc[d][:, 2 * GDN_HEADS + c:2 * GDN_HEADS + c + 1] for (d, _), c in zip(chains, cols)]
        decay = [jnp.where(incl[d], jnp.exp(jnp.minimum(gc[i] - gr[i], 0.0)), 0.0) for i, (d, _) in enumerate(chains)]
        q = [q_s[pl.ds(r0[d], ch), sls[i]] for i, (d, _) in enumerate(chains)]
        k = [k_s[pl.ds(r0[d], ch), sls[i]] for i, (d, _) in enumerate(chains)]
        v = [v_s[pl.ds(r0[d], ch), sls[i]].astype(F32) for i, (d, _) in enumerate(chains)]
        kf = [x.astype(F32) for x in k]
        kb = [x * b for x, b in zip(kf, beta)]
        kk = [_dot_nt(x.astype(BF16), y) for x, y in zip(kb, k)]
        qk = [_dot_nt(x, y) for x, y in zip(q, k)]
        a = [jnp.where(strict[d], kk[i] * decay[i], 0.0) for i, (d, _) in enumerate(chains)]
        t = _unit_tri_inverses(a)
        eg = [jnp.exp(x) for x in gc]
        rhs = [jnp.concatenate([v[i] * beta[i], kb[i] * eg[i]], axis=1).astype(BF16) for i in range(len(chains))]
        sol = [_dot(x.astype(BF16), y) for x, y in zip(t, rhs)]
        attn = [(x * y).astype(BF16) for x, y in zip(qk, decay)]
        q_dec = [(x.astype(F32) * e).astype(BF16) for x, e in zip(q, eg)]
        k_dec = [(kf[i] * jnp.exp(gtot[i] - gc[i])).astype(BF16) for i in range(len(chains))]
        s = [st_s[c][...] for c in cols]
        sb = [x.astype(BF16) for x in s]
        ws = [_dot(x[:, HEAD_DIM:2 * HEAD_DIM].astype(BF16), y) for x, y in zip(sol, sb)]
        qs = [_dot(x, y) for x, y in zip(q_dec, sb)]
        vb = [(x[:, 0:HEAD_DIM] - y).astype(BF16) for x, y in zip(sol, ws)]
        upd = [_dot_tn(x, y) for x, y in zip(k_dec, vb)]
        av = [_dot(x, y) for x, y in zip(attn, vb)]
        for i, c in enumerate(cols):
            st_s[c][...] = s[i] * jnp.exp(gtot[i]) + upd[i]
        for d, out_s in enumerate((of_s, ob_s)):
            out_s[pl.ds(r0[d], ch), :] = jnp.concatenate(
                [qs[i] + av[i] for i, (dd, _) in enumerate(chains) if dd == d], axis=1)

    for st in st_s:
        st[...] = jnp.zeros_like(st)
    ncc, nlc = n_ctx // ch, n_lat // ch

    def scan_ctx(n, carry):
        scan_step(n, ncc - 1 - n)
        return carry

    def scan_lat(n, carry):
        scan_step(ncc + n, ncc + nlc - 1 - n)
        return carry

    lax.fori_loop(0, ncc, scan_ctx, 0)
    lax.fori_loop(0, nlc, scan_lat, 0)

    ng = ng_ref[...]

    def finish(ref, out_ref, n_rows, row_off):
        def body(i, carry):
            t0 = pl.multiple_of(i * pt, pt)
            r0 = pl.multiple_of(row_off + t0, pt)
            o = of_s[pl.ds(r0, pt), :] + ob_s[pl.ds(r0, pt), :]
            z = ref[pl.ds(t0, pt), qkv_w:qkv_w + GDN_W].astype(F32)
            y = o * lax.rsqrt(_split_dot(o * o, gmean) + EPS) * ng * _silu(z)
            out_ref[pl.ds(t0, pt), :] = y.astype(out_ref.dtype)
            return carry

        lax.fori_loop(0, n_rows // pt, body, 0)

    finish(ctx_ref, oc_ref, n_ctx, 0)
    finish(lat_ref, ol_ref, n_lat, n_ctx)


def gdn_mixer(p_gdn, p_ab, conv_w, a_log, dt_bias, norm_gain, *, n_batch, n_lat, n_ctx):
    width = p_gdn.shape[1]
    ctx_blk0 = (n_batch * n_lat) // n_ctx
    t_all = n_lat + n_ctx
    cw = jnp.pad(conv_w.astype(F32), ((0, 8 - GDN_CONV), (0, 0)))
    par = jnp.zeros((8, LANES), F32)
    par = par.at[0, :2 * GDN_HEADS].set(a_log.reshape(-1)).at[1, :2 * GDN_HEADS].set(dt_bias.reshape(-1))
    ng = jnp.tile(norm_gain.astype(F32).reshape(1, HEAD_DIM), (1, GDN_HEADS))
    return pl.pallas_call(
        functools.partial(_gdn_kernel, n_ctx=n_ctx, n_lat=n_lat),
        out_shape=(jax.ShapeDtypeStruct((n_batch * n_lat, GDN_W), BF16),
                   jax.ShapeDtypeStruct((n_batch * n_ctx, GDN_W), BF16)),
        grid=(n_batch,),
        in_specs=[pl.BlockSpec((n_lat, width), lambda b: (b, 0)),
                  pl.BlockSpec((n_ctx, width), lambda b: (ctx_blk0 + b, 0)),
                  pl.BlockSpec((n_lat, LANES), lambda b: (b, 0)),
                  pl.BlockSpec((n_ctx, LANES), lambda b: (ctx_blk0 + b, 0)),
                  pl.BlockSpec(cw.shape, lambda b: (0, 0)),
                  pl.BlockSpec(par.shape, lambda b: (0, 0)),
                  pl.BlockSpec((1, GDN_W), lambda b: (0, 0))],
        out_specs=(pl.BlockSpec((n_lat, GDN_W), lambda b: (b, 0)),
                   pl.BlockSpec((n_ctx, GDN_W), lambda b: (b, 0))),
        scratch_shapes=[pltpu.VMEM((t_all, GDN_W), BF16), pltpu.VMEM((t_all, GDN_W), BF16),
                        pltpu.VMEM((t_all, GDN_W), BF16), pltpu.VMEM((t_all, LANES), F32),
                        pltpu.VMEM((t_all, GDN_W), F32), pltpu.VMEM((t_all, GDN_W), F32)]
        + [pltpu.VMEM((HEAD_DIM, HEAD_DIM), F32)] * (2 * GDN_HEADS),
        compiler_params=_cparams("parallel"),
        name="gdn",
    )(p_gdn, p_gdn, p_ab, p_ab, cw, par, ng)


SWA_BLOCK = 128
SWA_BAND = 3 * SWA_BLOCK


def _swa_kernel(sink_ref, q_ref, kl_ref, vl_ref, kc_ref, vc_ref, cos_ref, sin_ref, o_ref, *, band, n_lat):
    blk = SWA_BLOCK
    i = pl.program_id(1)
    q = q_ref[...].astype(F32)
    kc = kc_ref[...]
    vc = vc_ref[...]
    if band:
        q0 = pl.multiple_of(i * blk, blk)
        start = pl.multiple_of(jnp.clip((i - 1) * blk, 0, n_lat - SWA_BAND), blk)
        q = _rope(q, cos_ref[pl.ds(q0, blk), :], sin_ref[pl.ds(q0, blk), :])
        kb = _rope(kl_ref[pl.ds(start, SWA_BAND), :].astype(F32),
                   cos_ref[pl.ds(start, SWA_BAND), :], sin_ref[pl.ds(start, SWA_BAND), :]).astype(BF16)
        vb = vl_ref[pl.ds(start, SWA_BAND), :]
        keys = jnp.concatenate([kb, kc], axis=0)
        vals = jnp.concatenate([vb, vc], axis=0)
    else:
        keys, vals = kc, vc
    qb = q.astype(BF16)
    n_keys = keys.shape[0]
    rows = SWA_GROUP * blk
    row_id = lax.broadcasted_iota(jnp.int32, (rows, 1), 0)
    if band:
        q_pos = q0 + row_id % blk
        k_pos = start + lax.broadcasted_iota(jnp.int32, (1, n_keys), 1)
        col_id = lax.broadcasted_iota(jnp.int32, (1, n_keys), 1)
        valid = (col_id >= SWA_BAND) | (jnp.abs(k_pos - q_pos) <= WINDOW)
    outs = []
    for g in range(SWA_KV_HEADS):
        ksl = slice(g * HEAD_DIM, (g + 1) * HEAD_DIM)
        q4 = jnp.concatenate([qb[:, (g * SWA_GROUP + r) * HEAD_DIM:(g * SWA_GROUP + r + 1) * HEAD_DIM]
                              for r in range(SWA_GROUP)], axis=0)
        s = _dot_nt(q4, keys[:, ksl]) * (HEAD_DIM ** -0.5)
        if band:
            s = jnp.where(valid, s, NEG_BIG)
        sink = jnp.zeros((rows, 1), F32)
        for r in range(SWA_GROUP):
            sink = jnp.where(row_id // blk == r, sink_ref[g * SWA_GROUP + r], sink)
        m = jnp.maximum(jnp.max(s, axis=-1, keepdims=True), sink)
        e = jnp.exp(s - m)
        den = jnp.sum(e, axis=-1, keepdims=True) + jnp.exp(sink - m)
        o4 = _dot(e.astype(BF16), vals[:, ksl]) * (1.0 / den)
        outs += [o4[r * blk:(r + 1) * blk] for r in range(SWA_GROUP)]
    o_ref[...] = jnp.concatenate(outs, axis=1).astype(o_ref.dtype)


def swa_mixer(p_swa, cos, sin, sink, *, n_batch, n_lat, n_ctx):
    blk = SWA_BLOCK
    ctx_row0 = n_batch * n_lat
    kcol, vcol = SWA_W // SWA_KV_W, SWA_W // SWA_KV_W + 1
    sink = sink.astype(F32)

    def call(band):
        n_q = n_lat if band else n_ctx
        nb = n_q // blk
        qrow0 = 0 if band else ctx_row0 // blk
        grid_spec = pltpu.PrefetchScalarGridSpec(
            num_scalar_prefetch=1,
            grid=(n_batch, nb),
            in_specs=[pl.BlockSpec((blk, SWA_W), lambda b, i, s: (qrow0 + b * nb + i, 0)),
                      pl.BlockSpec((n_lat, SWA_KV_W), lambda b, i, s: (b, kcol)),
                      pl.BlockSpec((n_lat, SWA_KV_W), lambda b, i, s: (b, vcol)),
                      pl.BlockSpec((n_ctx, SWA_KV_W), lambda b, i, s: (ctx_row0 // n_ctx + b, kcol)),
                      pl.BlockSpec((n_ctx, SWA_KV_W), lambda b, i, s: (ctx_row0 // n_ctx + b, vcol)),
                      pl.BlockSpec((n_lat, LANES), lambda b, i, s: (0, 0)),
                      pl.BlockSpec((n_lat, LANES), lambda b, i, s: (0, 0))],
            out_specs=pl.BlockSpec((blk, SWA_W), lambda b, i, s: (b * nb + i, 0)),
        )
        return pl.pallas_call(
            functools.partial(_swa_kernel, band=band, n_lat=n_lat),
            out_shape=jax.ShapeDtypeStruct((n_batch * n_q, SWA_W), BF16),
            grid_spec=grid_spec,
            compiler_params=_cparams("parallel", "arbitrary"),
            name="swa_latent" if band else "swa_context",
        )(sink, p_swa, p_swa, p_swa, p_swa, p_swa, cos, sin)

    return call(True), call(False)


MOE_ROW_TILE = 512
MOE_FF_TILE = 1792


def _row_tile(n_lat, n_ctx_rows):
    for tm in (512, 256, 128):
        if n_lat % tm == 0 and n_ctx_rows % tm == 0:
            return tm
    raise ValueError("sequence lengths must be multiples of 128")


def kernel(x, c, ctx, c_ctx, ada_w, ada_b, norm1_g, w_in, ret_decay, ret_gn_g, gdn_conv_w, gdn_a_log, gdn_dt_bias,
           gdn_norm_g, swa_sink, w_out, norm2_g, ffn_w1, ffn_w3, ffn_w2, router_w, moe_w1, moe_w3, moe_w2, final_g):
    b, s, d = x.shape
    n_ctx = ctx.shape[1]
    depth = ada_w.shape[0]
    nl, ncx = b * s, b * n_ctx
    tm = _row_tile(s, ncx)
    seg = _seg_map(nl // tm, s // tm, b)

    cvec = jnp.zeros((8, d), F32).at[:b].set(c).at[b].set(c_ctx)
    mod_all = adaln(cvec, ada_w, ada_b).reshape(depth, 8, 6, d)
    xall = jnp.concatenate([x.reshape(nl, d), ctx.reshape(ncx, d)], axis=0)
    cos, sin = rope_tables(s)
    fg = final_g.reshape(1, d)

    ret_cols, gdn_cols = 4 * RET_W, 4 * GDN_W
    ab_cols = 4 * GDN_HEADS
    for layer in range(depth):
        last = layer == depth - 1
        mod = mod_all[layer]
        w = w_in[layer].astype(BF16)
        w_ret = w[:, :ret_cols]
        w_gdn = w[:, ret_cols:ret_cols + gdn_cols]
        w_ab = jnp.pad(w[:, ret_cols + gdn_cols:ret_cols + gdn_cols + ab_cols], ((0, 0), (0, LANES - ab_cols)))
        w_swa = w[:, ret_cols + gdn_cols + ab_cols:]
        p_ret, p_gdn, p_swa, p_ab = in_proj(xall, mod, norm1_g[layer].reshape(1, d), w_ret, w_gdn, w_swa, w_ab,
                                            tm=tm, seg=seg)
        dec_lanes = jnp.repeat(ret_decay[layer].astype(F32), HEAD_DIM, axis=1)
        o_ret = retention_mixer(p_ret, cos, sin, dec_lanes, ret_gn_g[layer].reshape(1, RET_W),
                                n_batch=b, n_lat=s, n_ctx=n_ctx)
        o_gdn = gdn_mixer(p_gdn, p_ab, gdn_conv_w[layer], gdn_a_log[layer], gdn_dt_bias[layer], gdn_norm_g[layer],
                          n_batch=b, n_lat=s, n_ctx=n_ctx)
        o_swa = swa_mixer(p_swa, cos, sin, swa_sink[layer], n_batch=b, n_lat=s, n_ctx=n_ctx)

        n_rows = nl if last else nl + ncx
        is_moe = layer % 2 == 1
        i = layer // 2
        rw = None
        if is_moe:
            rw = jnp.pad(router_w[i].astype(F32), ((0, 0), (0, LANES - N_EXPERTS)))
        res = out_proj(o_ret, o_gdn, o_swa, xall, mod, w_out[layer].astype(BF16), norm2_g[layer].reshape(1, d), rw,
                       tm=tm, seg=seg, n_rows=n_rows, n_lat_rows=nl, tok_dtype=F32 if is_moe else BF16)
        if is_moe:
            x_new, tokens, ridx, rgate, cnt = res
            xall = moe_layer(tokens, ridx, rgate, cnt, x_new, mod, moe_w1[i].astype(BF16), moe_w3[i].astype(BF16),
                             moe_w2[i].astype(BF16), fg, tm=tm, tmx=MOE_ROW_TILE, tf=MOE_FF_TILE, seg=seg, last=last)
        else:
            x_new, tokens = res
            xall = dense_ffn(tokens, x_new, mod, ffn_w1[i].astype(BF16), ffn_w3[i].astype(BF16),
                             ffn_w2[i].astype(BF16), fg, tm=tm, seg=seg, last=last)
    return xall[:nl].reshape(b, s, d)
```

```python
import functools

import jax
import jax.numpy as jnp
import numpy as np
from jax import lax
from jax.experimental import pallas as pl
from jax.experimental.pallas import tpu as pltpu

F32 = jnp.float32
BF16 = jnp.bfloat16

HEAD_DIM = 64
GRID_W = 64
ROPE_BASE = 10000.0
EPS = 1e-6
RET_HEADS = 4
GDN_HEADS = 4
GDN_CONV = 5
SWA_HEADS = 8
SWA_KV_HEADS = 2
SWA_GROUP = SWA_HEADS // SWA_KV_HEADS
WINDOW = 128
N_EXPERTS = 8
RET_W = RET_HEADS * HEAD_DIM
GDN_W = GDN_HEADS * HEAD_DIM
SWA_W = SWA_HEADS * HEAD_DIM
SWA_KV_W = SWA_KV_HEADS * HEAD_DIM
LANES = 128
VMEM_LIMIT = 56 * 1024 * 1024
NEG_BIG = -1e30
LOG2E = 1.4426950408889634


def _cparams(*sem):
    return pltpu.CompilerParams(dimension_semantics=sem, vmem_limit_bytes=VMEM_LIMIT)


def _dot(a, b):
    return jnp.dot(a, b, preferred_element_type=F32)


def _dot_nt(a, b):
    return lax.dot_general(a, b, (((1,), (1,)), ((), ())), preferred_element_type=F32)


def _dot_tn(a, b):
    return lax.dot_general(a, b, (((0,), (0,)), ((), ())), preferred_element_type=F32)


def _split_dot(x, m_bf16, terms=2):
    acc = None
    rem = x
    for _ in range(terms):
        part = rem.astype(BF16)
        d = _dot(part, m_bf16)
        acc = d if acc is None else acc + d
        rem = rem - part.astype(F32)
    return acc


def _sigmoid(x):
    return 1.0 / (1.0 + jnp.exp(-x))


def _silu(x):
    return x * _sigmoid(x)


def _rms_mod(x, gain, shift, scale):
    ms = jnp.mean(x * x, axis=-1, keepdims=True)
    y = x * lax.rsqrt(ms + EPS) * gain
    return y * (1.0 + scale) + shift


def _seg_map(n_lat_tiles, tiles_per_batch, n_batch):
    def seg(i):
        return jnp.where(i < n_lat_tiles, i // tiles_per_batch, n_batch)
    return seg


def _adaln_kernel(c_ref, w_ref, b_ref, o_ref):
    h = _silu(c_ref[...]).astype(BF16)
    o_ref[0] = _dot(h, w_ref[0].astype(BF16)) + b_ref[0]


def adaln(cvec, ada_w, ada_b):
    depth, d, n6 = ada_w.shape
    tn = 1536 if n6 % 1536 == 0 else n6
    rows = cvec.shape[0]
    return pl.pallas_call(
        _adaln_kernel,
        out_shape=jax.ShapeDtypeStruct((depth, rows, n6), F32),
        grid=(depth, n6 // tn),
        in_specs=[pl.BlockSpec((rows, d), lambda l, j: (0, 0)),
                  pl.BlockSpec((1, d, tn), lambda l, j: (l, 0, j)),
                  pl.BlockSpec((1, 1, tn), lambda l, j: (l, 0, j))],
        out_specs=pl.BlockSpec((1, rows, tn), lambda l, j: (l, 0, j)),
        compiler_params=_cparams("parallel", "parallel"),
        name="adaln",
    )(cvec, ada_w, ada_b.reshape(depth, 1, n6))


def _inproj_kernel(x_ref, mod_ref, g_ref, wr_ref, wg_ref, ws_ref, wab_ref, pr_ref, pg_ref, ps_ref, pab_ref):
    h = _rms_mod(x_ref[...], g_ref[...], mod_ref[0, 0:1, :], mod_ref[0, 1:2, :]).astype(BF16)
    for w_ref, o_ref in ((wr_ref, pr_ref), (wg_ref, pg_ref), (ws_ref, ps_ref), (wab_ref, pab_ref)):
        n = w_ref.shape[1]
        step = 256 if n % 256 == 0 else n
        for c in range(0, n, step):
            o_ref[:, c:c + step] = _dot(h, w_ref[:, c:c + step]).astype(o_ref.dtype)


def in_proj(xall, mod, gain, w_ret, w_gdn, w_swa, w_ab, *, tm, seg):
    n, d = xall.shape
    full = lambda i: (0, 0)
    row = lambda i: (i, 0)
    outs = (jax.ShapeDtypeStruct((n, w_ret.shape[1]), BF16), jax.ShapeDtypeStruct((n, w_gdn.shape[1]), BF16),
            jax.ShapeDtypeStruct((n, w_swa.shape[1]), BF16), jax.ShapeDtypeStruct((n, w_ab.shape[1]), F32))
    return pl.pallas_call(
        _inproj_kernel,
        out_shape=outs,
        grid=(n // tm,),
        in_specs=[pl.BlockSpec((tm, d), row),
                  pl.BlockSpec((1, 6, d), lambda i: (seg(i), 0, 0)),
                  pl.BlockSpec((1, d), full),
                  pl.BlockSpec(w_ret.shape, full), pl.BlockSpec(w_gdn.shape, full),
                  pl.BlockSpec(w_swa.shape, full), pl.BlockSpec(w_ab.shape, full)],
        out_specs=tuple(pl.BlockSpec((tm, o.shape[1]), row) for o in outs),
        compiler_params=_cparams("parallel"),
        name="in_proj",
    )(xall, mod, gain, w_ret, w_gdn, w_swa, w_ab)


def _outproj_kernel(*refs, tm, moe, n_lat_tiles, has_ctx):
    n_mix = 6 if has_ctx else 3
    mix = refs[:n_mix]
    refs = refs[n_mix:]
    if moe:
        (x_ref, mod_ref, w_ref, g_ref, rw_ref, xo_ref, tok_ref, ridx_ref, rgate_ref, cnt_ref, base_ref) = refs
    else:
        x_ref, mod_ref, w_ref, g_ref, xo_ref, tok_ref = refs
    if has_ctx:
        is_lat = pl.program_id(0) < n_lat_tiles
        o_ret, o_gdn, o_swa = (jnp.where(is_lat, mix[2 * k][...], mix[2 * k + 1][...]) for k in range(3))
    else:
        o_ret, o_gdn, o_swa = (m[...] for m in mix)
    y = _dot(o_ret, w_ref[0:RET_W, :])
    y = y + _dot(o_gdn, w_ref[RET_W:RET_W + GDN_W, :])
    y = y + _dot(o_swa, w_ref[RET_W + GDN_W:, :])
    x = x_ref[...] + mod_ref[0, 2:3, :] * y
    xo_ref[...] = x
    t = _rms_mod(x, g_ref[...], mod_ref[0, 3:4, :], mod_ref[0, 4:5, :])
    tok_ref[...] = t.astype(tok_ref.dtype)
    if not moe:
        return

    @pl.when(pl.program_id(0) == 0)
    def _():
        base_ref[...] = jnp.zeros_like(base_ref)

    rw = rw_ref[...]
    rw_hi = rw.astype(BF16)
    rw_lo = (rw - rw_hi.astype(F32)).astype(BF16)
    t_hi = t.astype(BF16)
    t_lo = (t - t_hi.astype(F32)).astype(BF16)
    logits = _dot(t_hi, rw_hi) + _dot(t_hi, rw_lo) + _dot(t_lo, rw_hi)
    lane = lax.broadcasted_iota(jnp.int32, logits.shape, 1)
    logits = jnp.where(lane < N_EXPERTS, logits, NEG_BIG)
    m1 = jnp.max(logits, axis=-1, keepdims=True)
    i1 = jnp.min(jnp.where(logits == m1, lane, LANES), axis=-1, keepdims=True)
    rest = jnp.where(lane == i1, NEG_BIG, logits)
    m2 = jnp.max(rest, axis=-1, keepdims=True)
    i2 = jnp.min(jnp.where(rest == m2, lane, LANES), axis=-1, keepdims=True)
    e2 = jnp.exp(m2 - m1)
    g1 = 1.0 / (1.0 + e2)
    g2 = e2 * g1
    oh1 = jnp.where(lane == i1, 1.0, 0.0)
    oh2 = jnp.where(lane == i2, 1.0, 0.0)
    r_i = lax.broadcasted_iota(jnp.int32, (tm, tm), 0)
    c_i = lax.broadcasted_iota(jnp.int32, (tm, tm), 1)
    lower = jnp.where(r_i > c_i, 1.0, 0.0).astype(BF16)
    cum1 = _dot(lower, oh1.astype(BF16))
    cum2 = _dot(lower, oh2.astype(BF16))
    cnt1 = jnp.sum(oh1, axis=0, keepdims=True)
    cnt2 = jnp.sum(oh2, axis=0, keepdims=True)
    base = base_ref[...]
    rank1 = jnp.sum(oh1 * (base + cum1), axis=-1, keepdims=True)
    rank2 = jnp.sum(oh2 * (base + cnt1 + cum2), axis=-1, keepdims=True)
    base = base + cnt1 + cnt2
    base_ref[...] = base
    cnt_ref[...] = base.astype(jnp.int32)
    lane8 = lax.broadcasted_iota(jnp.int32, (tm, 8), 1)
    r1 = rank1.astype(jnp.int32)
    r2 = rank2.astype(jnp.int32)
    ridx_ref[...] = jnp.where(lane8 == 0, i1, jnp.where(lane8 == 1, i2, jnp.where(lane8 == 2, r1,
                              jnp.where(lane8 == 3, r2, 0))))
    rgate_ref[...] = jnp.where(lane8 == 0, g1, jnp.where(lane8 == 1, g2, 0.0))


def out_proj(o_ret, o_gdn, o_swa, xall, mod, w_out, gain2, router_w, *, tm, seg, n_rows, n_lat_rows, tok_dtype):
    d = xall.shape[1]
    moe = router_w is not None
    has_ctx = n_rows > n_lat_rows
    n_lat_tiles = n_lat_rows // tm
    full = lambda i: (0, 0)
    row = lambda i: (i, 0)
    lat_row = lambda i: (jnp.minimum(i, n_lat_tiles - 1), 0)
    ctx_row = lambda i: (jnp.maximum(i - n_lat_tiles, 0), 0)
    in_specs, args = [], []
    for pair, width in ((o_ret, RET_W), (o_gdn, GDN_W), (o_swa, SWA_W)):
        in_specs.append(pl.BlockSpec((tm, width), lat_row))
        args.append(pair[0])
        if has_ctx:
            in_specs.append(pl.BlockSpec((tm, width), ctx_row))
            args.append(pair[1])
    in_specs += [pl.BlockSpec((tm, d), row), pl.BlockSpec((1, 6, d), lambda i: (seg(i), 0, 0)),
                 pl.BlockSpec(w_out.shape, full), pl.BlockSpec((1, d), full)]
    args += [xall, mod, w_out, gain2]
    outs = [jax.ShapeDtypeStruct((n_rows, d), F32), jax.ShapeDtypeStruct((n_rows, d), tok_dtype)]
    out_specs = [pl.BlockSpec((tm, d), row), pl.BlockSpec((tm, d), row)]
    scratch = []
    if moe:
        in_specs.append(pl.BlockSpec(router_w.shape, full))
        args.append(router_w)
        outs += [jax.ShapeDtypeStruct((n_rows, 8), jnp.int32), jax.ShapeDtypeStruct((n_rows, 8), F32),
                 jax.ShapeDtypeStruct((1, LANES), jnp.int32)]
        out_specs += [pl.BlockSpec((tm, 8), row), pl.BlockSpec((tm, 8), row), pl.BlockSpec((1, LANES), full)]
        scratch = [pltpu.VMEM((1, LANES), F32)]
    return pl.pallas_call(
        functools.partial(_outproj_kernel, tm=tm, moe=moe, n_lat_tiles=n_lat_tiles, has_ctx=has_ctx),
        out_shape=tuple(outs),
        grid=(n_rows // tm,),
        in_specs=in_specs,
        out_specs=tuple(out_specs),
        scratch_shapes=scratch,
        compiler_params=_cparams("arbitrary"),
        name="out_proj_router" if moe else "out_proj",
    )(*args)


def _final_norm(x, gain):
    ms = jnp.mean(x * x, axis=-1, keepdims=True)
    return x * lax.rsqrt(ms + EPS) * gain


def _dense_ffn_kernel(tok_ref, x_ref, mod_ref, w1_ref, w3_ref, w2_ref, fg_ref, o_ref, *, last):
    t = tok_ref[...]
    h = (_silu(_dot(t, w1_ref[...])) * _dot(t, w3_ref[...])).astype(BF16)
    x = x_ref[...] + mod_ref[0, 5:6, :] * _dot(h, w2_ref[...])
    if last:
        x = _final_norm(x, fg_ref[...])
    o_ref[...] = x


def dense_ffn(tokens, xall, mod, w1, w3, w2, final_g, *, tm, seg, last):
    n, d = tokens.shape
    f = w1.shape[1]
    full = lambda i: (0, 0)
    row = lambda i: (i, 0)
    return pl.pallas_call(
        functools.partial(_dense_ffn_kernel, last=last),
        out_shape=jax.ShapeDtypeStruct((n, d), F32),
        grid=(n // tm,),
        in_specs=[pl.BlockSpec((tm, d), row), pl.BlockSpec((tm, d), row),
                  pl.BlockSpec((1, 6, d), lambda i: (seg(i), 0, 0)),
                  pl.BlockSpec((d, f), full, pipeline_mode=pl.Buffered(1)),
                  pl.BlockSpec((d, f), full, pipeline_mode=pl.Buffered(1)),
                  pl.BlockSpec((f, d), full, pipeline_mode=pl.Buffered(1)),
                  pl.BlockSpec((1, d), full)],
        out_specs=pl.BlockSpec((tm, d), row),
        compiler_params=_cparams("parallel"),
        name="dense_ffn",
    )(tokens, xall, mod, w1, w3, w2, final_g)


def _dispatch_kernel(dest_ref, tok_ref, xs_in_ref, xs_ref, sem, *, tm):
    del xs_in_ref

    def issue(r, carry):
        for k in range(2):
            dst = dest_ref[0, 0, 2 * r + k]
            pltpu.make_async_copy(tok_ref.at[pl.ds(r, 1)], xs_ref.at[pl.ds(dst, 1)], sem).start()
        return carry

    lax.fori_loop(0, tm, issue, 0, unroll=8)
    for _ in range(2):
        pltpu.make_async_copy(tok_ref, xs_ref.at[pl.ds(0, tm)], sem).wait()


def moe_dispatch(tokens, dest3, n_slots, *, tm):
    n, d = tokens.shape
    xs0 = jnp.zeros((n_slots, d), tokens.dtype)
    return pl.pallas_call(
        functools.partial(_dispatch_kernel, tm=tm),
        out_shape=jax.ShapeDtypeStruct((n_slots, d), tokens.dtype),
        grid=(n // tm,),
        in_specs=[pl.BlockSpec((1, 1, 2 * tm), lambda i: (i, 0, 0), memory_space=pltpu.SMEM),
                  pl.BlockSpec((tm, d), lambda i: (i, 0)),
                  pl.BlockSpec(memory_space=pl.ANY)],
        out_specs=pl.BlockSpec(memory_space=pl.ANY),
        scratch_shapes=[pltpu.SemaphoreType.DMA],
        input_output_aliases={2: 0},
        compiler_params=_cparams("arbitrary"),
        name="moe_dispatch",
    )(dest3, tokens, xs0)


def _moe_ffn_kernel(te_ref, nu_ref, xs_ref, w1_ref, w3_ref, w2_ref, o_ref, acc_ref):
    i = pl.program_id(0)
    j = pl.program_id(1)

    @pl.when(i < nu_ref[0])
    def _():
        x = xs_ref[...].astype(BF16)
        a = _dot(x, w1_ref[0])
        b = _dot(x, w3_ref[0])
        h = (_silu(a) * b).astype(BF16)
        upd = _dot(h, w2_ref[0])

        @pl.when(j == 0)
        def _():
            acc_ref[...] = upd

        @pl.when(j > 0)
        def _():
            acc_ref[...] += upd

        @pl.when(j == pl.num_programs(1) - 1)
        def _():
            o_ref[...] = acc_ref[...]

    @pl.when((i >= nu_ref[0]) & (j == pl.num_programs(1) - 1))
    def _():
        o_ref[...] = jnp.zeros_like(o_ref)


def moe_ffn(xs, tile_expert, n_used, w1, w3, w2, *, tm, tf):
    r, d = xs.shape
    f = w1.shape[2]
    nf = f // tf

    def tile(i, nu):
        return jnp.minimum(i, nu[0] - 1)

    def ftile(i, j, nu):
        return jnp.where(i < nu[0], j, nf - 1)

    grid_spec = pltpu.PrefetchScalarGridSpec(
        num_scalar_prefetch=2,
        grid=(r // tm, nf),
        in_specs=[pl.BlockSpec((tm, d), lambda i, j, te, nu: (tile(i, nu), 0)),
                  pl.BlockSpec((1, d, tf), lambda i, j, te, nu: (te[tile(i, nu)], 0, ftile(i, j, nu))),
                  pl.BlockSpec((1, d, tf), lambda i, j, te, nu: (te[tile(i, nu)], 0, ftile(i, j, nu))),
                  pl.BlockSpec((1, tf, d), lambda i, j, te, nu: (te[tile(i, nu)], ftile(i, j, nu), 0))],
        out_specs=pl.BlockSpec((tm, d), lambda i, j, te, nu: (i, 0)),
        scratch_shapes=[pltpu.VMEM((tm, d), F32)],
    )
    return pl.pallas_call(
        _moe_ffn_kernel,
        out_shape=jax.ShapeDtypeStruct((r, d), F32),
        grid_spec=grid_spec,
        compiler_params=_cparams("arbitrary", "arbitrary"),
        name="moe_ffn",
    )(tile_expert, n_used, xs, w1, w3, w2)


def _combine_kernel(dest_ref, gate_ref, x_ref, mod_ref, fg_ref, ys_ref, o_ref, buf_ref, sem, *, tm, last):
    def issue(r, carry):
        for k in range(2):
            src = dest_ref[0, 0, 2 * r + k]
            pltpu.make_async_copy(ys_ref.at[pl.ds(src, 1)], buf_ref.at[k, pl.ds(r, 1)], sem).start()
        return carry

    lax.fori_loop(0, tm, issue, 0, unroll=8)
    for k in range(2):
        pltpu.make_async_copy(ys_ref.at[pl.ds(0, tm)], buf_ref.at[k], sem).wait()
    gate = gate_ref[...]
    y = gate[:, 0:1] * buf_ref[0] + gate[:, 1:2] * buf_ref[1]
    x = x_ref[...] + mod_ref[0, 5:6, :] * y
    if last:
        x = _final_norm(x, fg_ref[...])
    o_ref[...] = x


def moe_combine(ys, dest3, rgate, xall, mod, final_g, *, tm, seg, last):
    n, d = xall.shape
    return pl.pallas_call(
        functools.partial(_combine_kernel, tm=tm, last=last),
        out_shape=jax.ShapeDtypeStruct((n, d), F32),
        grid=(n // tm,),
        in_specs=[pl.BlockSpec((1, 1, 2 * tm), lambda i: (i, 0, 0), memory_space=pltpu.SMEM),
                  pl.BlockSpec((tm, 8), lambda i: (i, 0)),
                  pl.BlockSpec((tm, d), lambda i: (i, 0)),
                  pl.BlockSpec((1, 6, d), lambda i: (seg(i), 0, 0)),
                  pl.BlockSpec((1, d), lambda i: (0, 0)),
                  pl.BlockSpec(memory_space=pl.ANY)],
        out_specs=pl.BlockSpec((tm, d), lambda i: (i, 0)),
        scratch_shapes=[pltpu.VMEM((2, tm, d), F32), pltpu.SemaphoreType.DMA],
        compiler_params=_cparams("arbitrary"),
        name="moe_combine",
    )(dest3, rgate, xall, mod, final_g, ys)


def moe_layer(tokens, ridx, rgate, cnt, xall, mod, w1, w3, w2, final_g, *, tm, tmx, tf, seg, last):
    n = tokens.shape[0]
    counts = cnt[0, :N_EXPERTS]
    padded = (counts + tmx - 1) // tmx * tmx
    pend = jnp.cumsum(padded)
    pstart = pend - padded
    dest = pstart[ridx[:, 0:2]] + ridx[:, 2:4]
    dest3 = dest.reshape(n // tm, 1, 2 * tm).astype(jnp.int32)
    n_tiles = (2 * n) // tmx + N_EXPERTS
    n_used = (pend[-1] // tmx).astype(jnp.int32).reshape(1)
    tile_start = jnp.arange(n_tiles, dtype=jnp.int32) * tmx
    tile_expert = jnp.minimum(jnp.sum(tile_start[:, None] >= pend[None, :], axis=1), N_EXPERTS - 1).astype(jnp.int32)
    xs = moe_dispatch(tokens, dest3, n_tiles * tmx, tm=tm)
    ys = moe_ffn(xs, tile_expert, n_used, w1, w3, w2, tm=tmx, tf=tf)
    return moe_combine(ys, dest3, rgate, xall, mod, final_g, tm=tm, seg=seg, last=last)


def rope_tables(seq):
    rows = seq // GRID_W
    row = jnp.repeat(jnp.arange(rows, dtype=F32), GRID_W)
    col = jnp.tile(jnp.arange(GRID_W, dtype=F32), rows)
    n_freq = HEAD_DIM // 4
    inv = ROPE_BASE ** (-jnp.arange(n_freq, dtype=F32) / n_freq)
    ang = jnp.concatenate([row[:, None] * inv, col[:, None] * inv], axis=-1)
    cos, sin = jnp.cos(ang), jnp.sin(ang)
    cos64 = jnp.concatenate([cos, cos], axis=-1)
    sin64 = jnp.concatenate([-sin, sin], axis=-1)
    return jnp.tile(cos64, (1, 2)), jnp.tile(sin64, (1, 2))


def _tile_lanes(t, width):
    reps = width // t.shape[1]
    return t if reps == 1 else jnp.concatenate([t] * reps, axis=1)


def _rope(t, cos, sin):
    w = t.shape[1]
    lane = lax.broadcasted_iota(jnp.int32, t.shape, 1) % HEAD_DIM
    half = HEAD_DIM // 2
    rot = jnp.where(lane < half, pltpu.roll(t, w - half, 1), pltpu.roll(t, half, 1))
    return t * _tile_lanes(cos, w) + rot * _tile_lanes(sin, w)


def _group_matrix(width, value):
    r = lax.broadcasted_iota(jnp.int32, (width, width), 0) // HEAD_DIM
    c = lax.broadcasted_iota(jnp.int32, (width, width), 1) // HEAD_DIM
    return jnp.where(r == c, value, 0.0).astype(BF16)


RET_CHUNK = 128


def _retention_kernel(lat_ref, ctx_ref, cos_ref, sin_ref, dec_ref, gn_ref, ol_ref, oc_ref,
                      q_s, k_s, bn_s, st_s, o_s, dm_s, *, n_ctx, n_lat):
    ch = RET_CHUNK
    nc, nl = n_ctx // ch, n_lat // ch
    lg = -jnp.exp(dec_ref[...])
    lgf, lgb = lg[0:1], lg[1:2]
    pos = lax.broadcasted_iota(jnp.int32, (ch, 1), 0).astype(F32)
    wkf = jnp.exp((ch - 1 - pos) * lgf)
    wkb = jnp.exp(pos * lgb)
    wqf = jnp.exp((pos + 1) * lgf)
    wqb = jnp.exp((ch - pos) * lgb)
    cdf = jnp.exp(ch * lgf)
    cdb = jnp.exp(ch * lgb)
    blockmask = (lax.broadcasted_iota(jnp.int32, (RET_W, RET_W), 0) // HEAD_DIM
                 == lax.broadcasted_iota(jnp.int32, (RET_W, RET_W), 1) // HEAD_DIM)
    gmean = _group_matrix(RET_W, 1.0 / HEAD_DIM)
    diff = (lax.broadcasted_iota(jnp.int32, (ch, ch), 0) - lax.broadcasted_iota(jnp.int32, (ch, ch), 1)).astype(F32)
    gn = gn_ref[...]

    def load_qkv(ref, t0):
        return (ref[pl.ds(t0, ch), 0:RET_W].astype(F32), ref[pl.ds(t0, ch), RET_W:2 * RET_W].astype(F32),
                ref[pl.ds(t0, ch), 2 * RET_W:3 * RET_W])

    def reverse_step(ref, t0, c, roped):
        q, k, v = load_qkv(ref, t0)
        if roped:
            cos, sin = cos_ref[pl.ds(t0, ch), :], sin_ref[pl.ds(t0, ch), :]
            q, k = _rope(q, cos, sin), _rope(k, cos, sin)
        k = k * (HEAD_DIM ** -0.5)
        r0 = pl.multiple_of(c * ch, ch)
        q_s[pl.ds(r0, ch), :] = q.astype(BF16)
        k_s[pl.ds(r0, ch), :] = k.astype(BF16)
        st = st_s[...]
        bn_s[c] = st.astype(BF16)
        kvb = _dot_tn((k * wkb).astype(BF16), v)
        st_s[...] = cdb * st + jnp.where(blockmask, kvb, 0.0)

    st_s[...] = jnp.zeros_like(st_s)

    def rev_ctx(n, carry):
        c = nc - 1 - n
        reverse_step(ctx_ref, pl.multiple_of(c * ch, ch), c, False)
        return carry

    def rev_lat(n, carry):
        c = nl - 1 - n
        reverse_step(lat_ref, pl.multiple_of(c * ch, ch), nc + c, True)
        return carry

    lax.fori_loop(0, nc, rev_ctx, 0)
    lax.fori_loop(0, nl, rev_lat, 0)

    for h in range(RET_HEADS):
        lf, lb = lgf[:, h * HEAD_DIM:h * HEAD_DIM + 1], lgb[:, h * HEAD_DIM:h * HEAD_DIM + 1]
        dm_s[h] = jnp.where(diff == 0.0, 2.0, jnp.exp(jnp.abs(diff) * jnp.where(diff > 0.0, lf, lb)))

    def forward_step(ref, t0, c):
        r0 = pl.multiple_of(c * ch, ch)
        q = q_s[pl.ds(r0, ch), :]
        k = k_s[pl.ds(r0, ch), :]
        v = ref[pl.ds(t0, ch), 2 * RET_W:3 * RET_W]
        sls = [slice(h * HEAD_DIM, (h + 1) * HEAD_DIM) for h in range(RET_HEADS)]
        scores = [_dot_nt(q[:, sl], k[:, sl]) for sl in sls]
        probs = [(s * dm_s[h]).astype(BF16) for h, s in enumerate(scores)]
        heads = [_dot(p, v[:, sl]) for p, sl in zip(probs, sls)]
        qf = q.astype(F32)
        st = st_s[...]
        o = jnp.concatenate(heads, axis=1)
        o_s[pl.ds(r0, ch), :] = (o + _dot((qf * wqf).astype(BF16), st.astype(BF16))
                                 + _dot((qf * wqb).astype(BF16), bn_s[c]))
        kvf = _dot_tn((k.astype(F32) * wkf).astype(BF16), v)
        st_s[...] = cdf * st + jnp.where(blockmask, kvf, 0.0)

    st_s[...] = jnp.zeros_like(st_s)

    def fwd_ctx(n, carry):
        forward_step(ctx_ref, pl.multiple_of(n * ch, ch), n)
        return carry

    def fwd_lat(n, carry):
        forward_step(lat_ref, pl.multiple_of(n * ch, ch), nc + n)
        return carry

    lax.fori_loop(0, nc, fwd_ctx, 0)
    lax.fori_loop(0, nl, fwd_lat, 0, unroll=2)

    def finish(ref, out_ref, n_rows, row_off):
        ft = 2 * ch

        def body(i, carry):
            t0 = pl.multiple_of(i * ft, ft)
            o = o_s[pl.ds(pl.multiple_of(row_off + t0, ft), ft), :]
            g = ref[pl.ds(t0, ft), 3 * RET_W:4 * RET_W].astype(F32)
            xc = o - _split_dot(o, gmean)
            var = _split_dot(xc * xc, gmean)
            out_ref[pl.ds(t0, ft), :] = (xc * lax.rsqrt(var + EPS) * gn * _silu(g)).astype(out_ref.dtype)
            return carry

        lax.fori_loop(0, n_rows // ft, body, 0)

    finish(ctx_ref, oc_ref, n_ctx, 0)
    finish(lat_ref, ol_ref, n_lat, n_ctx)


def retention_mixer(p_ret, cos, sin, dec_lanes, gn_gain, *, n_batch, n_lat, n_ctx):
    n = p_ret.shape[0]
    width = p_ret.shape[1]
    ctx_blk0 = (n_batch * n_lat) // n_ctx
    t_all = n_lat + n_ctx
    nch = t_all // RET_CHUNK
    return pl.pallas_call(
        functools.partial(_retention_kernel, n_ctx=n_ctx, n_lat=n_lat),
        out_shape=(jax.ShapeDtypeStruct((n_batch * n_lat, RET_W), BF16),
                   jax.ShapeDtypeStruct((n_batch * n_ctx, RET_W), BF16)),
        grid=(n_batch,),
        in_specs=[pl.BlockSpec((n_lat, width), lambda b: (b, 0)),
                  pl.BlockSpec((n_ctx, width), lambda b: (ctx_blk0 + b, 0)),
                  pl.BlockSpec((n_lat, LANES), lambda b: (0, 0)),
                  pl.BlockSpec((n_lat, LANES), lambda b: (0, 0)),
                  pl.BlockSpec((2, RET_W), lambda b: (0, 0)),
                  pl.BlockSpec((1, RET_W), lambda b: (0, 0))],
        out_specs=(pl.BlockSpec((n_lat, RET_W), lambda b: (b, 0)),
                   pl.BlockSpec((n_ctx, RET_W), lambda b: (b, 0))),
        scratch_shapes=[pltpu.VMEM((t_all, RET_W), BF16), pltpu.VMEM((t_all, RET_W), BF16),
                        pltpu.VMEM((nch, RET_W, RET_W), BF16), pltpu.VMEM((RET_W, RET_W), F32),
                        pltpu.VMEM((t_all, RET_W), F32), pltpu.VMEM((RET_HEADS, RET_CHUNK, RET_CHUNK), F32)],
        compiler_params=_cparams("parallel"),
        name="retention",
    )(p_ret, p_ret, cos, sin, dec_lanes, gn_gain)


GDN_CHUNK = 64
GDN_PREP = 128
GDN_HALO = 8
GDN_SOLVE_CHUNKS = 4


def _softplus(x):
    return jnp.maximum(x, 0.0) + jnp.log(1.0 + jnp.exp(-jnp.abs(x)))


def _unit_tri_inverses(mats):
    n = mats[0].shape[0]
    ri = lax.broadcasted_iota(jnp.int32, (n, n), 0)
    ci = lax.broadcasted_iota(jnp.int32, (n, n), 1)
    b16 = (ri // 16) == (ci // 16)
    b32 = (ri // 32) == (ci // 32)
    eye = jnp.where(ri == ci, 1.0, 0.0)
    ps = [jnp.where(b16, -a, 0.0) for a in mats]
    ts = [eye + p for p in ps]
    for _ in range(3):
        pbs = [p.astype(BF16) for p in ps]
        ps = [_dot(pb, pb) for pb in pbs]
        ts = [t + _dot(t.astype(BF16), p.astype(BF16)) for t, p in zip(ts, ps)]
    for keep in (b32 & ~b16, ~b32):
        offs = [jnp.where(keep, a, 0.0).astype(BF16) for a in mats]
        tbs = [t.astype(BF16) for t in ts]
        mids = [_dot(tb, off).astype(BF16) for tb, off in zip(tbs, offs)]
        ts = [t - _dot(mid, tb) for t, mid, tb in zip(ts, mids, tbs)]
    return ts


def _gdn_kernel(lat_ref, ctx_ref, abl_ref, abc_ref, cw_ref, par_ref, ng_ref, ol_ref, oc_ref,
                q_s, k_s, v_s, gb_s, of_s, ob_s, e_s, uf_s, ub_s, wf_s, wb_s, af_s, ab_s, *st_s, n_ctx, n_lat):
    u_s, w_s, a_s = (uf_s, ub_s), (wf_s, wb_s), (af_s, ab_s)
    ch = GDN_CHUNK
    pt = GDN_PREP
    halo = GDN_HALO
    qkv_w = 3 * GDN_W
    gsum = _group_matrix(GDN_W, 1.0)
    gmean = _group_matrix(GDN_W, 1.0 / HEAD_DIM)
    lane128 = lax.broadcasted_iota(jnp.int32, (pt, LANES), 1)
    neg_a = -jnp.exp(par_ref[0:1, :])
    dt_bias = par_ref[1:2, :]
    taps = [cw_ref[j:j + 1, :] for j in range(GDN_CONV)]

    def prep(ref, ab_ref, n_rows, row_off):
        n_tiles = n_rows // pt

        def body(i, carry):
            t0 = pl.multiple_of(i * pt, pt)
            cur = ref[pl.ds(t0, pt), 0:qkv_w].astype(F32)
            p0 = pl.multiple_of(jnp.maximum(t0 - 16, 0), 16)
            n0 = pl.multiple_of(jnp.minimum(t0 + pt, n_rows - 16), 16)
            prev = ref[pl.ds(p0, 16), 0:qkv_w].astype(F32)[16 - halo:16]
            nxt = ref[pl.ds(n0, 16), 0:qkv_w].astype(F32)[0:halo]
            prev = jnp.where(i > 0, prev, 0.0)
            nxt = jnp.where(i < n_tiles - 1, nxt, 0.0)
            ext = jnp.concatenate([prev, cur, nxt], axis=0)
            rows = pt + 2 * halo
            acc = None
            for j in range(GDN_CONV):
                s = j - (GDN_CONV - 1) // 2
                sh = ext if s == 0 else pltpu.roll(ext, (rows - s) % rows, 0)
                term = sh[halo:halo + pt] * taps[j]
                acc = term if acc is None else acc + term
            act = _silu(acc)
            q, k, v = act[:, 0:GDN_W], act[:, GDN_W:2 * GDN_W], act[:, 2 * GDN_W:3 * GDN_W]
            q = q * lax.rsqrt(_split_dot(q * q, gsum) + EPS) * (HEAD_DIM ** -0.5)
            k = k * lax.rsqrt(_split_dot(k * k, gsum) + EPS)
            r0 = pl.multiple_of(row_off + t0, pt)
            q_s[pl.ds(r0, pt), :] = q.astype(BF16)
            k_s[pl.ds(r0, pt), :] = k.astype(BF16)
            v_s[pl.ds(r0, pt), :] = v.astype(BF16)
            ab = ab_ref[pl.ds(t0, pt), :]
            g = neg_a * _softplus(ab + dt_bias)
            gb_s[pl.ds(r0, pt), :] = jnp.where(lane128 < 2 * GDN_HEADS, g, _sigmoid(ab))
            return carry

        lax.fori_loop(0, n_tiles, body, 0)

    prep(ctx_ref, abc_ref, n_ctx, 0)
    prep(lat_ref, abl_ref, n_lat, n_ctx)

    ri = lax.broadcasted_iota(jnp.int32, (ch, ch), 0)
    ci = lax.broadcasted_iota(jnp.int32, (ch, ch), 1)
    incl = (ri >= ci, ri <= ci)
    strict = (ri > ci, ri < ci)
    tri = tuple(jnp.where(m, 1.0, 0.0).astype(BF16) for m in incl)

    chains = [(d, h) for d in range(2) for h in range(GDN_HEADS)]

    cols = [d * GDN_HEADS + h for d, h in chains]
    sls = [slice(h * HEAD_DIM, (h + 1) * HEAD_DIM) for _, h in chains]
    n_chains = len(chains)
    lane_c = lax.broadcasted_iota(jnp.int32, (ch, LANES), 1)
    fwd_lane = lane_c < GDN_HEADS
    ncc, nlc = n_ctx // ch, n_lat // ch

    def solve_chunks(cs):
        items = [(j, d, h) for j in range(len(cs)) for d, h in chains]
        r0 = [pl.multiple_of(c * ch, ch) for c in cs]
        gbc = [gb_s[pl.ds(r, ch), :] for r in r0]
        gcs = []
        for j in range(len(cs)):
            per_dir = []
            for d in range(2):
                rem, acc = gbc[j], None
                for _ in range(3):
                    part = rem.astype(BF16)
                    term = _dot(tri[d], part)
                    acc = term if acc is None else acc + term
                    rem = rem - part.astype(F32)
                per_dir.append(acc)
            gcs.append(per_dir)
            g_both = jnp.where(fwd_lane, per_dir[0], per_dir[1])
            tot = jnp.where(fwd_lane, per_dir[0][ch - 1:ch, :], per_dir[1][0:1, :])
            packed = jnp.where(lane_c < 2 * GDN_HEADS, jnp.exp(g_both),
                               jnp.where(lane_c < 4 * GDN_HEADS, pltpu.roll(jnp.exp(tot - g_both), 2 * GDN_HEADS, 1),
                                         pltpu.roll(jnp.exp(tot), 4 * GDN_HEADS, 1)))
            e_s[pl.ds(r0[j], ch), :] = packed
        gcs_t = [[g.T for g in per_dir] for per_dir in gcs]
        col = [d * GDN_HEADS + h for _, d, h in items]
        sl = [slice(h * HEAD_DIM, (h + 1) * HEAD_DIM) for _, _, h in items]
        gc = [gcs[j][d][:, c:c + 1] for (j, d, _), c in zip(items, col)]
        gr = [gcs_t[j][d][c:c + 1, :] for (j, d, _), c in zip(items, col)]
        beta = [gbc[j][:, 2 * GDN_HEADS + c:2 * GDN_HEADS + c + 1] for (j, _, _), c in zip(items, col)]
        decay = [jnp.where(incl[d], jnp.exp(jnp.minimum(gc[i] - gr[i], 0.0)), 0.0) for i, (_, d, _) in enumerate(items)]
        q = [q_s[pl.ds(r0[j], ch), sl[i]] for i, (j, _, _) in enumerate(items)]
        k = [k_s[pl.ds(r0[j], ch), sl[i]] for i, (j, _, _) in enumerate(items)]
        v = [v_s[pl.ds(r0[j], ch), sl[i]].astype(F32) for i, (j, _, _) in enumerate(items)]
        kb = [x.astype(F32) * b for x, b in zip(k, beta)]
        kk = [_dot_nt(x.astype(BF16), y) for x, y in zip(kb, k)]
        qk = [_dot_nt(x, y) for x, y in zip(q, k)]
        a = [jnp.where(strict[d], kk[i] * decay[i], 0.0) for i, (_, d, _) in enumerate(items)]
        t = _unit_tri_inverses(a)
        rhs = [jnp.concatenate([v[i] * beta[i], kb[i] * jnp.exp(gc[i])], axis=1).astype(BF16)
               for i in range(len(items))]
        sol = [_dot(x.astype(BF16), y) for x, y in zip(t, rhs)]
        attn = [x * y for x, y in zip(qk, decay)]
        for j in range(len(cs)):
            for d in range(2):
                idx = [i for i, (jj, dd, _) in enumerate(items) if jj == j and dd == d]
                rows = pl.ds(r0[j], ch)
                u_s[d][rows, :] = jnp.concatenate([sol[i][:, 0:HEAD_DIM] for i in idx], axis=1).astype(BF16)
                w_s[d][rows, :] = jnp.concatenate([sol[i][:, HEAD_DIM:2 * HEAD_DIM] for i in idx], axis=1).astype(BF16)
                a_s[d][rows, :] = jnp.concatenate([attn[i] for i in idx], axis=1).astype(BF16)

    def scan_step(cf, cb):
        r0 = [pl.multiple_of(c * ch, ch) for c in (cf, cb)]
        e = [e_s[pl.ds(r, ch), :] for r in r0]
        rows = [pl.ds(r0[d], ch) for d, _ in chains]
        eg = [e[d][:, c:c + 1] for (d, _), c in zip(chains, cols)]
        kd = [e[d][:, 2 * GDN_HEADS + c:2 * GDN_HEADS + c + 1] for (d, _), c in zip(chains, cols)]
        dec = [e[d][:, 4 * GDN_HEADS + c:4 * GDN_HEADS + c + 1] for (d, _), c in zip(chains, cols)]
        q_dec = [(q_s[rows[i], sls[i]].astype(F32) * eg[i]).astype(BF16) for i in range(n_chains)]
        k_dec = [(k_s[rows[i], sls[i]].astype(F32) * kd[i]).astype(BF16) for i in range(n_chains)]
        u = [u_s[d][rows[i], sls[i]].astype(F32) for i, (d, _) in enumerate(chains)]
        w = [w_s[d][rows[i], sls[i]] for i, (d, _) in enumerate(chains)]
        attn = [a_s[d][rows[i], sls[i]] for i, (d, _) in enumerate(chains)]
        s = [st_s[c][...] for c in cols]
        sb = [x.astype(BF16) for x in s]
        ws = [_dot(x, y) for x, y in zip(w, sb)]
        qs = [_dot(x, y) for x, y in zip(q_dec, sb)]
        vb = [(x - y).astype(BF16) for x, y in zip(u, ws)]
        upd = [_dot_tn(x, y) for x, y in zip(k_dec, vb)]
        av = [_dot(x, y) for x, y in zip(attn, vb)]
        for i, c in enumerate(cols):
            st_s[c][...] = s[i] * dec[i] + upd[i]
        for d, out_s in enumerate((of_s, ob_s)):
            out_s[pl.ds(r0[d], ch), :] = jnp.concatenate(
                [qs[i] + av[i] for i, (dd, _) in enumerate(chains) if dd == d], axis=1)

    n_solve = GDN_SOLVE_CHUNKS if (ncc + nlc) % GDN_SOLVE_CHUNKS == 0 else 1

    def solve_body(n, carry):
        solve_chunks([n * n_solve + j for j in range(n_solve)])
        return carry

    lax.fori_loop(0, (ncc + nlc) // n_solve, solve_body, 0)

    for st in st_s:
        st[...] = jnp.zeros_like(st)

    def scan_ctx(n, carry):
        scan_step(n, ncc - 1 - n)
        return carry

    def scan_lat(n, carry):
        scan_step(ncc + n, ncc + nlc - 1 - n)
        return carry

    lax.fori_loop(0, ncc, scan_ctx, 0)
    lax.fori_loop(0, nlc, scan_lat, 0)

    ng = ng_ref[...]

    def finish(ref, out_ref, n_rows, row_off):
        def body(i, carry):
            t0 = pl.multiple_of(i * pt, pt)
            r0 = pl.multiple_of(row_off + t0, pt)
            o = of_s[pl.ds(r0, pt), :] + ob_s[pl.ds(r0, pt), :]
            z = ref[pl.ds(t0, pt), qkv_w:qkv_w + GDN_W].astype(F32)
            y = o * lax.rsqrt(_split_dot(o * o, gmean) + EPS) * ng * _silu(z)
            out_ref[pl.ds(t0, pt), :] = y.astype(out_ref.dtype)
            return carry

        lax.fori_loop(0, n_rows // pt, body, 0)

    finish(ctx_ref, oc_ref, n_ctx, 0)
    finish(lat_ref, ol_ref, n_lat, n_ctx)


def gdn_mixer(p_gdn, p_ab, conv_w, a_log, dt_bias, norm_gain, *, n_batch, n_lat, n_ctx):
    width = p_gdn.shape[1]
    ctx_blk0 = (n_batch * n_lat) // n_ctx
    t_all = n_lat + n_ctx
    cw = jnp.pad(conv_w.astype(F32), ((0, 8 - GDN_CONV), (0, 0)))
    par = jnp.zeros((8, LANES), F32)
    par = par.at[0, :2 * GDN_HEADS].set(a_log.reshape(-1)).at[1, :2 * GDN_HEADS].set(dt_bias.reshape(-1))
    ng = jnp.tile(norm_gain.astype(F32).reshape(1, HEAD_DIM), (1, GDN_HEADS))
    return pl.pallas_call(
        functools.partial(_gdn_kernel, n_ctx=n_ctx, n_lat=n_lat),
        out_shape=(jax.ShapeDtypeStruct((n_batch * n_lat, GDN_W), BF16),
                   jax.ShapeDtypeStruct((n_batch * n_ctx, GDN_W), BF16)),
        grid=(n_batch,),
        in_specs=[pl.BlockSpec((n_lat, width), lambda b: (b, 0), pipeline_mode=pl.Buffered(1)),
                  pl.BlockSpec((n_ctx, width), lambda b: (ctx_blk0 + b, 0)),
                  pl.BlockSpec((n_lat, LANES), lambda b: (b, 0), pipeline_mode=pl.Buffered(1)),
                  pl.BlockSpec((n_ctx, LANES), lambda b: (ctx_blk0 + b, 0)),
                  pl.BlockSpec(cw.shape, lambda b: (0, 0)),
                  pl.BlockSpec(par.shape, lambda b: (0, 0)),
                  pl.BlockSpec((1, GDN_W), lambda b: (0, 0))],
        out_specs=(pl.BlockSpec((n_lat, GDN_W), lambda b: (b, 0)),
                   pl.BlockSpec((n_ctx, GDN_W), lambda b: (b, 0))),
        scratch_shapes=[pltpu.VMEM((t_all, GDN_W), BF16), pltpu.VMEM((t_all, GDN_W), BF16),
                        pltpu.VMEM((t_all, GDN_W), BF16), pltpu.VMEM((t_all, LANES), F32),
                        pltpu.VMEM((t_all, GDN_W), F32), pltpu.VMEM((t_all, GDN_W), F32),
                        pltpu.VMEM((t_all, LANES), F32)]
        + [pltpu.VMEM((t_all, GDN_W), BF16)] * 6
        + [pltpu.VMEM((HEAD_DIM, HEAD_DIM), F32)] * (2 * GDN_HEADS),
        compiler_params=_cparams("parallel"),
        name="gdn",
    )(p_gdn, p_gdn, p_ab, p_ab, cw, par, ng)


SWA_BLOCK = 128
SWA_BAND = 3 * SWA_BLOCK


def _swa_kernel(sink_ref, q_ref, kl_ref, vl_ref, kc_ref, vc_ref, cos_ref, sin_ref, o_ref, *, band, n_lat):
    blk = SWA_BLOCK
    i = pl.program_id(1)
    q = q_ref[...].astype(F32)
    kc = kc_ref[...]
    vc = vc_ref[...]
    if band:
        q0 = pl.multiple_of(i * blk, blk)
        start = pl.multiple_of(jnp.clip((i - 1) * blk, 0, n_lat - SWA_BAND), blk)
        q = _rope(q, cos_ref[pl.ds(q0, blk), :], sin_ref[pl.ds(q0, blk), :])
        kb = _rope(kl_ref[pl.ds(start, SWA_BAND), :].astype(F32),
                   cos_ref[pl.ds(start, SWA_BAND), :], sin_ref[pl.ds(start, SWA_BAND), :]).astype(BF16)
        vb = vl_ref[pl.ds(start, SWA_BAND), :]
        keys = jnp.concatenate([kb, kc], axis=0)
        vals = jnp.concatenate([vb, vc], axis=0)
    else:
        keys, vals = kc, vc
    qb = (q * (HEAD_DIM ** -0.5 * LOG2E)).astype(BF16)
    n_keys = keys.shape[0]
    if band:
        q_pos = q0 + lax.broadcasted_iota(jnp.int32, (blk, 1), 0)
        col_id = lax.broadcasted_iota(jnp.int32, (1, n_keys), 1)
        valid = (col_id >= SWA_BAND) | (jnp.abs(start + col_id - q_pos) <= WINDOW)
        bias = jnp.where(valid, 0.0, NEG_BIG)
    v_lane = lax.broadcasted_iota(jnp.int32, vals.shape, 1) // HEAD_DIM
    outs = []
    for g in range(SWA_KV_HEADS):
        ksl = slice(g * HEAD_DIM, (g + 1) * HEAD_DIM)
        v_ext = jnp.where(v_lane == g, vals, 1.0)
        den_lane = (1 - g) * HEAD_DIM
        q4 = jnp.concatenate([qb[:, (g * SWA_GROUP + r) * HEAD_DIM:(g * SWA_GROUP + r + 1) * HEAD_DIM]
                              for r in range(SWA_GROUP)], axis=0)
        s = _dot_nt(q4, keys[:, ksl])
        es, ms, sinks = [], [], []
        for r in range(SWA_GROUP):
            sr = s[r * blk:(r + 1) * blk]
            if band:
                sr = sr + bias
            sink2 = sink_ref[g * SWA_GROUP + r] * LOG2E
            m = jnp.maximum(jnp.max(sr, axis=-1, keepdims=True), sink2)
            es.append(jnp.exp2(sr - m).astype(BF16))
            ms.append(m)
            sinks.append(sink2)
        ov = _dot(jnp.concatenate(es, axis=0), v_ext)
        for r in range(SWA_GROUP):
            ovr = ov[r * blk:(r + 1) * blk]
            den = ovr[:, den_lane:den_lane + 1] + jnp.exp2(sinks[r] - ms[r])
            outs.append(ovr[:, ksl] * (1.0 / den))
    o_ref[...] = jnp.concatenate(outs, axis=1).astype(o_ref.dtype)


def swa_mixer(p_swa, cos, sin, sink, *, n_batch, n_lat, n_ctx):
    blk = SWA_BLOCK
    ctx_row0 = n_batch * n_lat
    kcol, vcol = SWA_W // SWA_KV_W, SWA_W // SWA_KV_W + 1
    sink = sink.astype(F32)

    def call(band):
        n_q = n_lat if band else n_ctx
        nb = n_q // blk
        qrow0 = 0 if band else ctx_row0 // blk
        grid_spec = pltpu.PrefetchScalarGridSpec(
            num_scalar_prefetch=1,
            grid=(n_batch, nb),
            in_specs=[pl.BlockSpec((blk, SWA_W), lambda b, i, s: (qrow0 + b * nb + i, 0)),
                      pl.BlockSpec((n_lat, SWA_KV_W), lambda b, i, s: (b, kcol)),
                      pl.BlockSpec((n_lat, SWA_KV_W), lambda b, i, s: (b, vcol)),
                      pl.BlockSpec((n_ctx, SWA_KV_W), lambda b, i, s: (ctx_row0 // n_ctx + b, kcol)),
                      pl.BlockSpec((n_ctx, SWA_KV_W), lambda b, i, s: (ctx_row0 // n_ctx + b, vcol)),
                      pl.BlockSpec((n_lat, LANES), lambda b, i, s: (0, 0)),
                      pl.BlockSpec((n_lat, LANES), lambda b, i, s: (0, 0))],
            out_specs=pl.BlockSpec((blk, SWA_W), lambda b, i, s: (b * nb + i, 0)),
        )
        return pl.pallas_call(
            functools.partial(_swa_kernel, band=band, n_lat=n_lat),
            out_shape=jax.ShapeDtypeStruct((n_batch * n_q, SWA_W), BF16),
            grid_spec=grid_spec,
            compiler_params=_cparams("parallel", "arbitrary"),
            name="swa_latent" if band else "swa_context",
        )(sink, p_swa, p_swa, p_swa, p_swa, p_swa, cos, sin)

    return call(True), call(False)


MOE_ROW_TILE = 512
MOE_FF_TILE = 1792


def _row_tile(n_lat, n_ctx_rows):
    for tm in (512, 256, 128):
        if n_lat % tm == 0 and n_ctx_rows % tm == 0:
            return tm
    raise ValueError("sequence lengths must be multiples of 128")


def kernel(x, c, ctx, c_ctx, ada_w, ada_b, norm1_g, w_in, ret_decay, ret_gn_g, gdn_conv_w, gdn_a_log, gdn_dt_bias,
           gdn_norm_g, swa_sink, w_out, norm2_g, ffn_w1, ffn_w3, ffn_w2, router_w, moe_w1, moe_w3, moe_w2, final_g):
    b, s, d = x.shape
    n_ctx = ctx.shape[1]
    depth = ada_w.shape[0]
    nl, ncx = b * s, b * n_ctx
    tm = _row_tile(s, ncx)
    seg = _seg_map(nl // tm, s // tm, b)

    cvec = jnp.zeros((8, d), F32).at[:b].set(c).at[b].set(c_ctx)
    mod_all = adaln(cvec, ada_w, ada_b).reshape(depth, 8, 6, d)
    xall = jnp.concatenate([x.reshape(nl, d), ctx.reshape(ncx, d)], axis=0)
    cos, sin = rope_tables(s)
    fg = final_g.reshape(1, d)

    ret_cols, gdn_cols = 4 * RET_W, 4 * GDN_W
    ab_cols = 4 * GDN_HEADS
    for layer in range(depth):
        last = layer == depth - 1
        mod = mod_all[layer]
        w = w_in[layer].astype(BF16)
        w_ret = w[:, :ret_cols]
        w_gdn = w[:, ret_cols:ret_cols + gdn_cols]
        w_ab = jnp.pad(w[:, ret_cols + gdn_cols:ret_cols + gdn_cols + ab_cols], ((0, 0), (0, LANES - ab_cols)))
        w_swa = w[:, ret_cols + gdn_cols + ab_cols:]
        p_ret, p_gdn, p_swa, p_ab = in_proj(xall, mod, norm1_g[layer].reshape(1, d), w_ret, w_gdn, w_swa, w_ab,
                                            tm=tm, seg=seg)
        dec_lanes = jnp.repeat(ret_decay[layer].astype(F32), HEAD_DIM, axis=1)
        o_ret = retention_mixer(p_ret, cos, sin, dec_lanes, ret_gn_g[layer].reshape(1, RET_W),
                                n_batch=b, n_lat=s, n_ctx=n_ctx)
        o_gdn = gdn_mixer(p_gdn, p_ab, gdn_conv_w[layer], gdn_a_log[layer], gdn_dt_bias[layer], gdn_norm_g[layer],
                          n_batch=b, n_lat=s, n_ctx=n_ctx)
        o_swa = swa_mixer(p_swa, cos, sin, swa_sink[layer], n_batch=b, n_lat=s, n_ctx=n_ctx)

        n_rows = nl if last else nl + ncx
        is_moe = layer % 2 == 1
        i = layer // 2
        rw = None
        if is_moe:
            rw = jnp.pad(router_w[i].astype(F32), ((0, 0), (0, LANES - N_EXPERTS)))
        res = out_proj(o_ret, o_gdn, o_swa, xall, mod, w_out[layer].astype(BF16), norm2_g[layer].reshape(1, d), rw,
                       tm=tm, seg=seg, n_rows=n_rows, n_lat_rows=nl, tok_dtype=F32 if is_moe else BF16)
        if is_moe:
            x_new, tokens, ridx, rgate, cnt = res
            xall = moe_layer(tokens, ridx, rgate, cnt, x_new, mod, moe_w1[i].astype(BF16), moe_w3[i].astype(BF16),
                             moe_w2[i].astype(BF16), fg, tm=tm, tmx=MOE_ROW_TILE, tf=MOE_FF_TILE, seg=seg, last=last)
        else:
            x_new, tokens = res
            xall = dense_ffn(tokens, x_new, mod, ffn_w1[i].astype(BF16), ffn_w3[i].astype(BF16),
                             ffn_w2[i].astype(BF16), fg, tm=tm, seg=seg, last=last)
    return xall[:nl].reshape(b, s, d)
```

```python
import functools

import jax
import jax.numpy as jnp
import numpy as np
from jax import lax
from jax.experimental import pallas as pl
from jax.experimental.pallas import tpu as pltpu

F32 = jnp.float32
BF16 = jnp.bfloat16

HEAD_DIM = 64
GRID_W = 64
ROPE_BASE = 10000.0
EPS = 1e-6
RET_HEADS = 4
GDN_HEADS = 4
GDN_CONV = 5
SWA_HEADS = 8
SWA_KV_HEADS = 2
SWA_GROUP = SWA_HEADS // SWA_KV_HEADS
WINDOW = 128
N_EXPERTS = 8
RET_W = RET_HEADS * HEAD_DIM
GDN_W = GDN_HEADS * HEAD_DIM
SWA_W = SWA_HEADS * HEAD_DIM
SWA_KV_W = SWA_KV_HEADS * HEAD_DIM
LANES = 128
VMEM_LIMIT = 56 * 1024 * 1024
NEG_BIG = -1e30
LOG2E = 1.4426950408889634


def _cparams(*sem):
    return pltpu.CompilerParams(dimension_semantics=sem, vmem_limit_bytes=VMEM_LIMIT)


def _dot(a, b):
    return jnp.dot(a, b, preferred_element_type=F32)


def _dot_nt(a, b):
    return lax.dot_general(a, b, (((1,), (1,)), ((), ())), preferred_element_type=F32)


def _dot_tn(a, b):
    return lax.dot_general(a, b, (((0,), (0,)), ((), ())), preferred_element_type=F32)


def _split_dot(x, m_bf16, terms=2):
    acc = None
    rem = x
    for _ in range(terms):
        part = rem.astype(BF16)
        d = _dot(part, m_bf16)
        acc = d if acc is None else acc + d
        rem = rem - part.astype(F32)
    return acc


def _sigmoid(x):
    return 1.0 / (1.0 + jnp.exp(-x))


def _silu(x):
    return x * _sigmoid(x)


def _rms_mod(x, gain, shift, scale):
    ms = jnp.mean(x * x, axis=-1, keepdims=True)
    y = x * lax.rsqrt(ms + EPS) * gain
    return y * (1.0 + scale) + shift


def _seg_map(n_lat_tiles, tiles_per_batch, n_batch):
    def seg(i):
        return jnp.where(i < n_lat_tiles, i // tiles_per_batch, n_batch)
    return seg


def _adaln_kernel(c_ref, w_ref, b_ref, o_ref):
    h = _silu(c_ref[...]).astype(BF16)
    o_ref[0] = _dot(h, w_ref[0].astype(BF16)) + b_ref[0]


def adaln(cvec, ada_w, ada_b):
    depth, d, n6 = ada_w.shape
    tn = 1536 if n6 % 1536 == 0 else n6
    rows = cvec.shape[0]
    return pl.pallas_call(
        _adaln_kernel,
        out_shape=jax.ShapeDtypeStruct((depth, rows, n6), F32),
        grid=(depth, n6 // tn),
        in_specs=[pl.BlockSpec((rows, d), lambda l, j: (0, 0)),
                  pl.BlockSpec((1, d, tn), lambda l, j: (l, 0, j)),
                  pl.BlockSpec((1, 1, tn), lambda l, j: (l, 0, j))],
        out_specs=pl.BlockSpec((1, rows, tn), lambda l, j: (l, 0, j)),
        compiler_params=_cparams("parallel", "parallel"),
        name="adaln",
    )(cvec, ada_w, ada_b.reshape(depth, 1, n6))


def _inproj_kernel(x_ref, mod_ref, g_ref, wr_ref, wg_ref, ws_ref, wab_ref, pr_ref, pg_ref, ps_ref, pab_ref):
    h = _rms_mod(x_ref[...], g_ref[...], mod_ref[0, 0:1, :], mod_ref[0, 1:2, :]).astype(BF16)
    for w_ref, o_ref in ((wr_ref, pr_ref), (wg_ref, pg_ref), (ws_ref, ps_ref), (wab_ref, pab_ref)):
        n = w_ref.shape[1]
        step = 256 if n % 256 == 0 else n
        for c in range(0, n, step):
            o_ref[:, c:c + step] = _dot(h, w_ref[:, c:c + step]).astype(o_ref.dtype)


def in_proj(xall, mod, gain, w_ret, w_gdn, w_swa, w_ab, *, tm, seg):
    n, d = xall.shape
    full = lambda i: (0, 0)
    row = lambda i: (i, 0)
    outs = (jax.ShapeDtypeStruct((n, w_ret.shape[1]), BF16), jax.ShapeDtypeStruct((n, w_gdn.shape[1]), BF16),
            jax.ShapeDtypeStruct((n, w_swa.shape[1]), BF16), jax.ShapeDtypeStruct((n, w_ab.shape[1]), F32))
    return pl.pallas_call(
        _inproj_kernel,
        out_shape=outs,
        grid=(n // tm,),
        in_specs=[pl.BlockSpec((tm, d), row),
                  pl.BlockSpec((1, 6, d), lambda i: (seg(i), 0, 0)),
                  pl.BlockSpec((1, d), full),
                  pl.BlockSpec(w_ret.shape, full), pl.BlockSpec(w_gdn.shape, full),
                  pl.BlockSpec(w_swa.shape, full), pl.BlockSpec(w_ab.shape, full)],
        out_specs=tuple(pl.BlockSpec((tm, o.shape[1]), row) for o in outs),
        compiler_params=_cparams("parallel"),
        name="in_proj",
    )(xall, mod, gain, w_ret, w_gdn, w_swa, w_ab)


def _outproj_kernel(*refs, tm, moe, n_lat_tiles, has_ctx):
    n_mix = 6 if has_ctx else 3
    mix = refs[:n_mix]
    refs = refs[n_mix:]
    if moe:
        (x_ref, mod_ref, w_ref, g_ref, rw_ref, xo_ref, tok_ref, ridx_ref, rgate_ref, cnt_ref, base_ref) = refs
    else:
        x_ref, mod_ref, w_ref, g_ref, xo_ref, tok_ref = refs
    if has_ctx:
        is_lat = pl.program_id(0) < n_lat_tiles
        o_ret, o_gdn, o_swa = (jnp.where(is_lat, mix[2 * k][...], mix[2 * k + 1][...]) for k in range(3))
    else:
        o_ret, o_gdn, o_swa = (m[...] for m in mix)
    y = _dot(o_ret, w_ref[0:RET_W, :])
    y = y + _dot(o_gdn, w_ref[RET_W:RET_W + GDN_W, :])
    y = y + _dot(o_swa, w_ref[RET_W + GDN_W:, :])
    x = x_ref[...] + mod_ref[0, 2:3, :] * y
    xo_ref[...] = x
    t = _rms_mod(x, g_ref[...], mod_ref[0, 3:4, :], mod_ref[0, 4:5, :])
    tok_ref[...] = t.astype(tok_ref.dtype)
    if not moe:
        return

    @pl.when(pl.program_id(0) == 0)
    def _():
        base_ref[...] = jnp.zeros_like(base_ref)

    rw = rw_ref[...]
    rw_hi = rw.astype(BF16)
    rw_lo = (rw - rw_hi.astype(F32)).astype(BF16)
    t_hi = t.astype(BF16)
    t_lo = (t - t_hi.astype(F32)).astype(BF16)
    logits = _dot(t_hi, rw_hi) + _dot(t_hi, rw_lo) + _dot(t_lo, rw_hi)
    lane = lax.broadcasted_iota(jnp.int32, logits.shape, 1)
    logits = jnp.where(lane < N_EXPERTS, logits, NEG_BIG)
    m1 = jnp.max(logits, axis=-1, keepdims=True)
    i1 = jnp.min(jnp.where(logits == m1, lane, LANES), axis=-1, keepdims=True)
    rest = jnp.where(lane == i1, NEG_BIG, logits)
    m2 = jnp.max(rest, axis=-1, keepdims=True)
    i2 = jnp.min(jnp.where(rest == m2, lane, LANES), axis=-1, keepdims=True)
    e2 = jnp.exp(m2 - m1)
    g1 = 1.0 / (1.0 + e2)
    g2 = e2 * g1
    oh1 = jnp.where(lane == i1, 1.0, 0.0)
    oh2 = jnp.where(lane == i2, 1.0, 0.0)
    r_i = lax.broadcasted_iota(jnp.int32, (tm, tm), 0)
    c_i = lax.broadcasted_iota(jnp.int32, (tm, tm), 1)
    lower = jnp.where(r_i > c_i, 1.0, 0.0).astype(BF16)
    cum1 = _dot(lower, oh1.astype(BF16))
    cum2 = _dot(lower, oh2.astype(BF16))
    cnt1 = jnp.sum(oh1, axis=0, keepdims=True)
    cnt2 = jnp.sum(oh2, axis=0, keepdims=True)
    base = base_ref[...]
    rank1 = jnp.sum(oh1 * (base + cum1), axis=-1, keepdims=True)
    rank2 = jnp.sum(oh2 * (base + cnt1 + cum2), axis=-1, keepdims=True)
    base = base + cnt1 + cnt2
    base_ref[...] = base
    cnt_ref[...] = base.astype(jnp.int32)
    lane8 = lax.broadcasted_iota(jnp.int32, (tm, 8), 1)
    r1 = rank1.astype(jnp.int32)
    r2 = rank2.astype(jnp.int32)
    ridx_ref[...] = jnp.where(lane8 == 0, i1, jnp.where(lane8 == 1, i2, jnp.where(lane8 == 2, r1,
                              jnp.where(lane8 == 3, r2, 0))))
    rgate_ref[...] = jnp.where(lane8 == 0, g1, jnp.where(lane8 == 1, g2, 0.0))


def out_proj(o_ret, o_gdn, o_swa, xall, mod, w_out, gain2, router_w, *, tm, seg, n_rows, n_lat_rows, tok_dtype):
    d = xall.shape[1]
    moe = router_w is not None
    has_ctx = n_rows > n_lat_rows
    n_lat_tiles = n_lat_rows // tm
    full = lambda i: (0, 0)
    row = lambda i: (i, 0)
    lat_row = lambda i: (jnp.minimum(i, n_lat_tiles - 1), 0)
    ctx_row = lambda i: (jnp.maximum(i - n_lat_tiles, 0), 0)
    in_specs, args = [], []
    for pair, width in ((o_ret, RET_W), (o_gdn, GDN_W), (o_swa, SWA_W)):
        in_specs.append(pl.BlockSpec((tm, width), lat_row))
        args.append(pair[0])
        if has_ctx:
            in_specs.append(pl.BlockSpec((tm, width), ctx_row))
            args.append(pair[1])
    in_specs += [pl.BlockSpec((tm, d), row), pl.BlockSpec((1, 6, d), lambda i: (seg(i), 0, 0)),
                 pl.BlockSpec(w_out.shape, full), pl.BlockSpec((1, d), full)]
    args += [xall, mod, w_out, gain2]
    outs = [jax.ShapeDtypeStruct((n_rows, d), F32), jax.ShapeDtypeStruct((n_rows, d), tok_dtype)]
    out_specs = [pl.BlockSpec((tm, d), row), pl.BlockSpec((tm, d), row)]
    scratch = []
    if moe:
        in_specs.append(pl.BlockSpec(router_w.shape, full))
        args.append(router_w)
        outs += [jax.ShapeDtypeStruct((n_rows, 8), jnp.int32), jax.ShapeDtypeStruct((n_rows, 8), F32),
                 jax.ShapeDtypeStruct((1, LANES), jnp.int32)]
        out_specs += [pl.BlockSpec((tm, 8), row), pl.BlockSpec((tm, 8), row), pl.BlockSpec((1, LANES), full)]
        scratch = [pltpu.VMEM((1, LANES), F32)]
    return pl.pallas_call(
        functools.partial(_outproj_kernel, tm=tm, moe=moe, n_lat_tiles=n_lat_tiles, has_ctx=has_ctx),
        out_shape=tuple(outs),
        grid=(n_rows // tm,),
        in_specs=in_specs,
        out_specs=tuple(out_specs),
        scratch_shapes=scratch,
        compiler_params=_cparams("arbitrary"),
        name="out_proj_router" if moe else "out_proj",
    )(*args)


def _final_norm(x, gain):
    ms = jnp.mean(x * x, axis=-1, keepdims=True)
    return x * lax.rsqrt(ms + EPS) * gain


def _dense_ffn_kernel(tok_ref, x_ref, mod_ref, w1_ref, w3_ref, w2_ref, fg_ref, o_ref, *, last):
    t = tok_ref[...]
    h = (_silu(_dot(t, w1_ref[...])) * _dot(t, w3_ref[...])).astype(BF16)
    x = x_ref[...] + mod_ref[0, 5:6, :] * _dot(h, w2_ref[...])
    if last:
        x = _final_norm(x, fg_ref[...])
    o_ref[...] = x


def dense_ffn(tokens, xall, mod, w1, w3, w2, final_g, *, tm, seg, last):
    n, d = tokens.shape
    f = w1.shape[1]
    full = lambda i: (0, 0)
    row = lambda i: (i, 0)
    return pl.pallas_call(
        functools.partial(_dense_ffn_kernel, last=last),
        out_shape=jax.ShapeDtypeStruct((n, d), F32),
        grid=(n // tm,),
        in_specs=[pl.BlockSpec((tm, d), row), pl.BlockSpec((tm, d), row),
                  pl.BlockSpec((1, 6, d), lambda i: (seg(i), 0, 0)),
                  pl.BlockSpec((d, f), full, pipeline_mode=pl.Buffered(1)),
                  pl.BlockSpec((d, f), full, pipeline_mode=pl.Buffered(1)),
                  pl.BlockSpec((f, d), full, pipeline_mode=pl.Buffered(1)),
                  pl.BlockSpec((1, d), full)],
        out_specs=pl.BlockSpec((tm, d), row),
        compiler_params=_cparams("parallel"),
        name="dense_ffn",
    )(tokens, xall, mod, w1, w3, w2, final_g)


def _dispatch_kernel(dest_ref, tok_ref, xs_in_ref, xs_ref, sem, *, tm):
    del xs_in_ref

    def issue(r, carry):
        for k in range(2):
            dst = dest_ref[0, 0, 2 * r + k]
            pltpu.make_async_copy(tok_ref.at[pl.ds(r, 1)], xs_ref.at[pl.ds(dst, 1)], sem).start()
        return carry

    lax.fori_loop(0, tm, issue, 0, unroll=8)
    for _ in range(2):
        pltpu.make_async_copy(tok_ref, xs_ref.at[pl.ds(0, tm)], sem).wait()


def moe_dispatch(tokens, dest3, n_slots, *, tm):
    n, d = tokens.shape
    xs0 = jnp.zeros((n_slots, d), tokens.dtype)
    return pl.pallas_call(
        functools.partial(_dispatch_kernel, tm=tm),
        out_shape=jax.ShapeDtypeStruct((n_slots, d), tokens.dtype),
        grid=(n // tm,),
        in_specs=[pl.BlockSpec((1, 1, 2 * tm), lambda i: (i, 0, 0), memory_space=pltpu.SMEM),
                  pl.BlockSpec((tm, d), lambda i: (i, 0)),
                  pl.BlockSpec(memory_space=pl.ANY)],
        out_specs=pl.BlockSpec(memory_space=pl.ANY),
        scratch_shapes=[pltpu.SemaphoreType.DMA],
        input_output_aliases={2: 0},
        compiler_params=_cparams("arbitrary"),
        name="moe_dispatch",
    )(dest3, tokens, xs0)


def _tile_expert(te_ref, nu_ref, i):
    return te_ref[jnp.clip(i, 0, nu_ref[0] - 1)]


def _moe_up_kernel(te_ref, nu_ref, xs_ref, w1_ref, w3_ref, h_ref, w1b_s, w3b_s):
    i = pl.program_id(1)

    @pl.when((i == 0) | (_tile_expert(te_ref, nu_ref, i) != _tile_expert(te_ref, nu_ref, i - 1)))
    def _():
        w1b_s[...] = w1_ref[0, 0].astype(BF16)
        w3b_s[...] = w3_ref[0, 0].astype(BF16)

    @pl.when(i < nu_ref[0])
    def _():
        x = xs_ref[...].astype(BF16)
        h_ref[...] = (_silu(_dot(x, w1b_s[...])) * _dot(x, w3b_s[...])).astype(h_ref.dtype)

    @pl.when(i >= nu_ref[0])
    def _():
        h_ref[...] = jnp.zeros_like(h_ref)


def _moe_down_kernel(te_ref, nu_ref, h_ref, w2_ref, y_ref, w2b_s):
    i = pl.program_id(1)

    @pl.when((i == 0) | (_tile_expert(te_ref, nu_ref, i) != _tile_expert(te_ref, nu_ref, i - 1)))
    def _():
        w2b_s[...] = w2_ref[0, 0].astype(BF16)

    @pl.when(i < nu_ref[0])
    def _():
        y_ref[...] = _dot(h_ref[...], w2b_s[...])

    @pl.when(i >= nu_ref[0])
    def _():
        y_ref[...] = jnp.zeros_like(y_ref)


def moe_ffn(xs, tile_expert, n_used, w1, w3, w2, *, li, tm, tf, tn):
    r, d = xs.shape
    f = w1.shape[3]

    def tile(i, nu):
        return jnp.minimum(i, nu[0] - 1)

    up_spec = pltpu.PrefetchScalarGridSpec(
        num_scalar_prefetch=2,
        grid=(f // tf, r // tm),
        in_specs=[pl.BlockSpec((tm, d), lambda j, i, te, nu: (tile(i, nu), 0)),
                  pl.BlockSpec((1, 1, d, tf), lambda j, i, te, nu: (li, te[tile(i, nu)], 0, j)),
                  pl.BlockSpec((1, 1, d, tf), lambda j, i, te, nu: (li, te[tile(i, nu)], 0, j))],
        out_specs=pl.BlockSpec((tm, tf), lambda j, i, te, nu: (i, j)),
        scratch_shapes=[pltpu.VMEM((d, tf), BF16), pltpu.VMEM((d, tf), BF16)],
    )
    hidden = pl.pallas_call(
        _moe_up_kernel,
        out_shape=jax.ShapeDtypeStruct((r, f), BF16),
        grid_spec=up_spec,
        compiler_params=_cparams("arbitrary", "arbitrary"),
        name="moe_up",
    )(tile_expert, n_used, xs, w1, w3)
    down_spec = pltpu.PrefetchScalarGridSpec(
        num_scalar_prefetch=2,
        grid=(d // tn, r // tm),
        in_specs=[pl.BlockSpec((tm, f), lambda n, i, te, nu: (tile(i, nu), 0)),
                  pl.BlockSpec((1, 1, f, tn), lambda n, i, te, nu: (li, te[tile(i, nu)], 0, n))],
        out_specs=pl.BlockSpec((tm, tn), lambda n, i, te, nu: (i, n)),
        scratch_shapes=[pltpu.VMEM((f, tn), BF16)],
    )
    return pl.pallas_call(
        _moe_down_kernel,
        out_shape=jax.ShapeDtypeStruct((r, d), F32),
        grid_spec=down_spec,
        compiler_params=_cparams("arbitrary", "arbitrary"),
        name="moe_down",
    )(tile_expert, n_used, hidden, w2)


def _combine_kernel(dest_ref, gate_ref, x_ref, mod_ref, fg_ref, ys_ref, o_ref, buf_ref, sem, *, tm, last):
    def issue(r, carry):
        for k in range(2):
            src = dest_ref[0, 0, 2 * r + k]
            pltpu.make_async_copy(ys_ref.at[pl.ds(src, 1)], buf_ref.at[k, pl.ds(r, 1)], sem).start()
        return carry

    lax.fori_loop(0, tm, issue, 0, unroll=8)
    for k in range(2):
        pltpu.make_async_copy(ys_ref.at[pl.ds(0, tm)], buf_ref.at[k], sem).wait()
    gate = gate_ref[...]
    y = gate[:, 0:1] * buf_ref[0] + gate[:, 1:2] * buf_ref[1]
    x = x_ref[...] + mod_ref[0, 5:6, :] * y
    if last:
        x = _final_norm(x, fg_ref[...])
    o_ref[...] = x


def moe_combine(ys, dest3, rgate, xall, mod, final_g, *, tm, seg, last):
    n, d = xall.shape
    return pl.pallas_call(
        functools.partial(_combine_kernel, tm=tm, last=last),
        out_shape=jax.ShapeDtypeStruct((n, d), F32),
        grid=(n // tm,),
        in_specs=[pl.BlockSpec((1, 1, 2 * tm), lambda i: (i, 0, 0), memory_space=pltpu.SMEM),
                  pl.BlockSpec((tm, 8), lambda i: (i, 0)),
                  pl.BlockSpec((tm, d), lambda i: (i, 0)),
                  pl.BlockSpec((1, 6, d), lambda i: (seg(i), 0, 0)),
                  pl.BlockSpec((1, d), lambda i: (0, 0)),
                  pl.BlockSpec(memory_space=pl.ANY)],
        out_specs=pl.BlockSpec((tm, d), lambda i: (i, 0)),
        scratch_shapes=[pltpu.VMEM((2, tm, d), F32), pltpu.SemaphoreType.DMA],
        compiler_params=_cparams("arbitrary"),
        name="moe_combine",
    )(dest3, rgate, xall, mod, final_g, ys)


def moe_layer(tokens, ridx, rgate, cnt, xall, mod, w1, w3, w2, final_g, *, li, tm, tmx, tf, seg, last):
    n = tokens.shape[0]
    counts = cnt[0, :N_EXPERTS]
    padded = (counts + tmx - 1) // tmx * tmx
    pend = jnp.cumsum(padded)
    pstart = pend - padded
    dest = pstart[ridx[:, 0:2]] + ridx[:, 2:4]
    dest3 = dest.reshape(n // tm, 1, 2 * tm).astype(jnp.int32)
    n_tiles = (2 * n) // tmx + N_EXPERTS
    n_used = (pend[-1] // tmx).astype(jnp.int32).reshape(1)
    tile_start = jnp.arange(n_tiles, dtype=jnp.int32) * tmx
    tile_expert = jnp.minimum(jnp.sum(tile_start[:, None] >= pend[None, :], axis=1), N_EXPERTS - 1).astype(jnp.int32)
    xs = moe_dispatch(tokens, dest3, n_tiles * tmx, tm=tm)
    ys = moe_ffn(xs, tile_expert, n_used, w1, w3, w2, li=li, tm=tmx, tf=tf, tn=MOE_DOWN_TILE)
    return moe_combine(ys, dest3, rgate, xall, mod, final_g, tm=tm, seg=seg, last=last)


def rope_tables(seq):
    rows = seq // GRID_W
    row = jnp.repeat(jnp.arange(rows, dtype=F32), GRID_W)
    col = jnp.tile(jnp.arange(GRID_W, dtype=F32), rows)
    n_freq = HEAD_DIM // 4
    inv = ROPE_BASE ** (-jnp.arange(n_freq, dtype=F32) / n_freq)
    ang = jnp.concatenate([row[:, None] * inv, col[:, None] * inv], axis=-1)
    cos, sin = jnp.cos(ang), jnp.sin(ang)
    cos64 = jnp.concatenate([cos, cos], axis=-1)
    sin64 = jnp.concatenate([-sin, sin], axis=-1)
    return jnp.tile(cos64, (1, 2)), jnp.tile(sin64, (1, 2))


def _tile_lanes(t, width):
    reps = width // t.shape[1]
    return t if reps == 1 else jnp.concatenate([t] * reps, axis=1)


def _rope(t, cos, sin):
    w = t.shape[1]
    lane = lax.broadcasted_iota(jnp.int32, t.shape, 1) % HEAD_DIM
    half = HEAD_DIM // 2
    rot = jnp.where(lane < half, pltpu.roll(t, w - half, 1), pltpu.roll(t, half, 1))
    return t * _tile_lanes(cos, w) + rot * _tile_lanes(sin, w)


def _group_matrix(width, value):
    r = lax.broadcasted_iota(jnp.int32, (width, width), 0) // HEAD_DIM
    c = lax.broadcasted_iota(jnp.int32, (width, width), 1) // HEAD_DIM
    return jnp.where(r == c, value, 0.0).astype(BF16)


RET_CHUNK = 128


def _retention_kernel(lat_ref, ctx_ref, cos_ref, sin_ref, dec_ref, gn_ref, ol_ref, oc_ref,
                      q_s, k_s, bn_s, st_s, o_s, dm_s, *, n_ctx, n_lat):
    ch = RET_CHUNK
    nc, nl = n_ctx // ch, n_lat // ch
    lg = -jnp.exp(dec_ref[...])
    lgf, lgb = lg[0:1], lg[1:2]
    pos = lax.broadcasted_iota(jnp.int32, (ch, 1), 0).astype(F32)
    wkf = jnp.exp((ch - 1 - pos) * lgf)
    wkb = jnp.exp(pos * lgb)
    wqf = jnp.exp((pos + 1) * lgf)
    wqb = jnp.exp((ch - pos) * lgb)
    cdf = jnp.exp(ch * lgf)
    cdb = jnp.exp(ch * lgb)
    blockmask = (lax.broadcasted_iota(jnp.int32, (RET_W, RET_W), 0) // HEAD_DIM
                 == lax.broadcasted_iota(jnp.int32, (RET_W, RET_W), 1) // HEAD_DIM)
    gmean = _group_matrix(RET_W, 1.0 / HEAD_DIM)
    diff = (lax.broadcasted_iota(jnp.int32, (ch, ch), 0) - lax.broadcasted_iota(jnp.int32, (ch, ch), 1)).astype(F32)
    gn = gn_ref[...]

    def load_qkv(ref, t0):
        return (ref[pl.ds(t0, ch), 0:RET_W].astype(F32), ref[pl.ds(t0, ch), RET_W:2 * RET_W].astype(F32),
                ref[pl.ds(t0, ch), 2 * RET_W:3 * RET_W])

    def reverse_step(ref, t0, c, roped):
        q, k, v = load_qkv(ref, t0)
        if roped:
            cos, sin = cos_ref[pl.ds(t0, ch), :], sin_ref[pl.ds(t0, ch), :]
            q, k = _rope(q, cos, sin), _rope(k, cos, sin)
        k = k * (HEAD_DIM ** -0.5)
        r0 = pl.multiple_of(c * ch, ch)
        q_s[pl.ds(r0, ch), :] = q.astype(BF16)
        k_s[pl.ds(r0, ch), :] = k.astype(BF16)
        st = st_s[...]
        bn_s[c] = st.astype(BF16)
        kvb = _dot_tn((k * wkb).astype(BF16), v)
        st_s[...] = cdb * st + jnp.where(blockmask, kvb, 0.0)

    st_s[...] = jnp.zeros_like(st_s)

    def rev_ctx(n, carry):
        c = nc - 1 - n
        reverse_step(ctx_ref, pl.multiple_of(c * ch, ch), c, False)
        return carry

    def rev_lat(n, carry):
        c = nl - 1 - n
        reverse_step(lat_ref, pl.multiple_of(c * ch, ch), nc + c, True)
        return carry

    lax.fori_loop(0, nc, rev_ctx, 0)
    lax.fori_loop(0, nl, rev_lat, 0)

    for h in range(RET_HEADS):
        lf, lb = lgf[:, h * HEAD_DIM:h * HEAD_DIM + 1], lgb[:, h * HEAD_DIM:h * HEAD_DIM + 1]
        dm_s[h] = jnp.where(diff == 0.0, 2.0, jnp.exp(jnp.abs(diff) * jnp.where(diff > 0.0, lf, lb)))

    def forward_step(ref, t0, c):
        r0 = pl.multiple_of(c * ch, ch)
        q = q_s[pl.ds(r0, ch), :]
        k = k_s[pl.ds(r0, ch), :]
        v = ref[pl.ds(t0, ch), 2 * RET_W:3 * RET_W]
        sls = [slice(h * HEAD_DIM, (h + 1) * HEAD_DIM) for h in range(RET_HEADS)]
        scores = [_dot_nt(q[:, sl], k[:, sl]) for sl in sls]
        probs = [(s * dm_s[h]).astype(BF16) for h, s in enumerate(scores)]
        heads = [_dot(p, v[:, sl]) for p, sl in zip(probs, sls)]
        qf = q.astype(F32)
        st = st_s[...]
        o = jnp.concatenate(heads, axis=1)
        o_s[pl.ds(r0, ch), :] = (o + _dot((qf * wqf).astype(BF16), st.astype(BF16))
                                 + _dot((qf * wqb).astype(BF16), bn_s[c]))
        kvf = _dot_tn((k.astype(F32) * wkf).astype(BF16), v)
        st_s[...] = cdf * st + jnp.where(blockmask, kvf, 0.0)

    st_s[...] = jnp.zeros_like(st_s)

    def fwd_ctx(n, carry):
        forward_step(ctx_ref, pl.multiple_of(n * ch, ch), n)
        return carry

    def fwd_lat(n, carry):
        forward_step(lat_ref, pl.multiple_of(n * ch, ch), nc + n)
        return carry

    lax.fori_loop(0, nc, fwd_ctx, 0)
    lax.fori_loop(0, nl, fwd_lat, 0, unroll=2)

    def finish(ref, out_ref, n_rows, row_off):
        ft = 2 * ch

        def body(i, carry):
            t0 = pl.multiple_of(i * ft, ft)
            o = o_s[pl.ds(pl.multiple_of(row_off + t0, ft), ft), :]
            g = ref[pl.ds(t0, ft), 3 * RET_W:4 * RET_W].astype(F32)
            xc = o - _split_dot(o, gmean)
            var = _split_dot(xc * xc, gmean)
            out_ref[pl.ds(t0, ft), :] = (xc * lax.rsqrt(var + EPS) * gn * _silu(g)).astype(out_ref.dtype)
            return carry

        lax.fori_loop(0, n_rows // ft, body, 0)

    finish(ctx_ref, oc_ref, n_ctx, 0)
    finish(lat_ref, ol_ref, n_lat, n_ctx)


def retention_mixer(p_ret, cos, sin, dec_lanes, gn_gain, *, n_batch, n_lat, n_ctx):
    n = p_ret.shape[0]
    width = p_ret.shape[1]
    ctx_blk0 = (n_batch * n_lat) // n_ctx
    t_all = n_lat + n_ctx
    nch = t_all // RET_CHUNK
    return pl.pallas_call(
        functools.partial(_retention_kernel, n_ctx=n_ctx, n_lat=n_lat),
        out_shape=(jax.ShapeDtypeStruct((n_batch * n_lat, RET_W), BF16),
                   jax.ShapeDtypeStruct((n_batch * n_ctx, RET_W), BF16)),
        grid=(n_batch,),
        in_specs=[pl.BlockSpec((n_lat, width), lambda b: (b, 0)),
                  pl.BlockSpec((n_ctx, width), lambda b: (ctx_blk0 + b, 0)),
                  pl.BlockSpec((n_lat, LANES), lambda b: (0, 0)),
                  pl.BlockSpec((n_lat, LANES), lambda b: (0, 0)),
                  pl.BlockSpec((2, RET_W), lambda b: (0, 0)),
                  pl.BlockSpec((1, RET_W), lambda b: (0, 0))],
        out_specs=(pl.BlockSpec((n_lat, RET_W), lambda b: (b, 0)),
                   pl.BlockSpec((n_ctx, RET_W), lambda b: (b, 0))),
        scratch_shapes=[pltpu.VMEM((t_all, RET_W), BF16), pltpu.VMEM((t_all, RET_W), BF16),
                        pltpu.VMEM((nch, RET_W, RET_W), BF16), pltpu.VMEM((RET_W, RET_W), F32),
                        pltpu.VMEM((t_all, RET_W), F32), pltpu.VMEM((RET_HEADS, RET_CHUNK, RET_CHUNK), F32)],
        compiler_params=_cparams("parallel"),
        name="retention",
    )(p_ret, p_ret, cos, sin, dec_lanes, gn_gain)


GDN_CHUNK = 64
GDN_PREP = 128
GDN_HALO = 8
GDN_SOLVE_CHUNKS = 4


def _softplus(x):
    return jnp.maximum(x, 0.0) + jnp.log(1.0 + jnp.exp(-jnp.abs(x)))


def _unit_tri_inverses(mats):
    n = mats[0].shape[0]
    ri = lax.broadcasted_iota(jnp.int32, (n, n), 0)
    ci = lax.broadcasted_iota(jnp.int32, (n, n), 1)
    b16 = (ri // 16) == (ci // 16)
    b32 = (ri // 32) == (ci // 32)
    eye = jnp.where(ri == ci, 1.0, 0.0)
    ps = [jnp.where(b16, -a, 0.0) for a in mats]
    ts = [eye + p for p in ps]
    for _ in range(3):
        pbs = [p.astype(BF16) for p in ps]
        ps = [_dot(pb, pb) for pb in pbs]
        ts = [t + _dot(t.astype(BF16), p.astype(BF16)) for t, p in zip(ts, ps)]
    for keep in (b32 & ~b16, ~b32):
        offs = [jnp.where(keep, a, 0.0).astype(BF16) for a in mats]
        tbs = [t.astype(BF16) for t in ts]
        mids = [_dot(tb, off).astype(BF16) for tb, off in zip(tbs, offs)]
        ts = [t - _dot(mid, tb) for t, mid, tb in zip(ts, mids, tbs)]
    return ts


def _gdn_prep_kernel(lat_ref, ctx_ref, abl_ref, abc_ref, cw_ref, par_ref, q_s, k_s, v_s, gb_s, *, n_ctx, n_lat):
    pt = GDN_PREP
    halo = GDN_HALO
    qkv_w = 3 * GDN_W
    gsum = _group_matrix(GDN_W, 1.0)
    lane128 = lax.broadcasted_iota(jnp.int32, (pt, LANES), 1)
    neg_a = -jnp.exp(par_ref[0:1, :])
    dt_bias = par_ref[1:2, :]
    taps = [cw_ref[j:j + 1, :] for j in range(GDN_CONV)]

    def prep(ref, ab_ref, n_rows, row_off):
        n_tiles = n_rows // pt

        def body(i, carry):
            t0 = pl.multiple_of(i * pt, pt)
            cur = ref[pl.ds(t0, pt), 0:qkv_w].astype(F32)
            p0 = pl.multiple_of(jnp.maximum(t0 - 16, 0), 16)
            n0 = pl.multiple_of(jnp.minimum(t0 + pt, n_rows - 16), 16)
            prev = ref[pl.ds(p0, 16), 0:qkv_w].astype(F32)[16 - halo:16]
            nxt = ref[pl.ds(n0, 16), 0:qkv_w].astype(F32)[0:halo]
            prev = jnp.where(i > 0, prev, 0.0)
            nxt = jnp.where(i < n_tiles - 1, nxt, 0.0)
            ext = jnp.concatenate([prev, cur, nxt], axis=0)
            rows = pt + 2 * halo
            acc = None
            for j in range(GDN_CONV):
                s = j - (GDN_CONV - 1) // 2
                sh = ext if s == 0 else pltpu.roll(ext, (rows - s) % rows, 0)
                term = sh[halo:halo + pt] * taps[j]
                acc = term if acc is None else acc + term
            act = _silu(acc)
            q, k, v = act[:, 0:GDN_W], act[:, GDN_W:2 * GDN_W], act[:, 2 * GDN_W:3 * GDN_W]
            q = q * lax.rsqrt(_split_dot(q * q, gsum) + EPS) * (HEAD_DIM ** -0.5)
            k = k * lax.rsqrt(_split_dot(k * k, gsum) + EPS)
            r0 = pl.multiple_of(row_off + t0, pt)
            q_s[pl.ds(r0, pt), :] = q.astype(BF16)
            k_s[pl.ds(r0, pt), :] = k.astype(BF16)
            v_s[pl.ds(r0, pt), :] = v.astype(BF16)
            ab = ab_ref[pl.ds(t0, pt), :]
            g = neg_a * _softplus(ab + dt_bias)
            gb_s[pl.ds(r0, pt), :] = jnp.where(lane128 < 2 * GDN_HEADS, g, _sigmoid(ab))
            return carry

        lax.fori_loop(0, n_tiles, body, 0)

    prep(ctx_ref, abc_ref, n_ctx, 0)
    prep(lat_ref, abl_ref, n_lat, n_ctx)


def _gdn_kernel(q_s, k_s, v_s, gb_s, zl_ref, zc_ref, ng_ref, ol_ref, oc_ref,
                o_s, dec_s, uf_s, ub_s, wf_s, wb_s, af_s, ab_s, qf_s, qb_s, kf_s, kb_s, sf_s, sb_s, *, n_ctx, n_lat):
    u_s, w_s, a_s, qd_s, kd_s, st_s = (uf_s, ub_s), (wf_s, wb_s), (af_s, ab_s), (qf_s, qb_s), (kf_s, kb_s), (sf_s, sb_s)
    ch = GDN_CHUNK
    pt = GDN_PREP
    gmean = _group_matrix(GDN_W, 1.0 / HEAD_DIM)
    ri = lax.broadcasted_iota(jnp.int32, (ch, ch), 0)
    ci = lax.broadcasted_iota(jnp.int32, (ch, ch), 1)
    incl = (ri >= ci, ri <= ci)
    strict = (ri > ci, ri < ci)
    tri = tuple(jnp.where(m, 1.0, 0.0).astype(BF16) for m in incl)

    chains = [(d, h) for d in range(2) for h in range(GDN_HEADS)]
    ncc, nlc = n_ctx // ch, n_lat // ch
    head_blocks = (lax.broadcasted_iota(jnp.int32, (GDN_W, GDN_W), 0) // HEAD_DIM
                   == lax.broadcasted_iota(jnp.int32, (GDN_W, GDN_W), 1) // HEAD_DIM)

    def solve_chunks(cs):
        items = [(j, d, h) for j in range(len(cs)) for d, h in chains]
        r0 = [pl.multiple_of(c * ch, ch) for c in cs]
        gbc = [gb_s[pl.ds(r, ch), :] for r in r0]
        gcs = []
        for j in range(len(cs)):
            per_dir = []
            for d in range(2):
                rem, acc = gbc[j], None
                for _ in range(3):
                    part = rem.astype(BF16)
                    term = _dot(tri[d], part)
                    acc = term if acc is None else acc + term
                    rem = rem - part.astype(F32)
                per_dir.append(acc)
            gcs.append(per_dir)
        gcs_t = [[g.T for g in per_dir] for per_dir in gcs]
        edge = (ch - 1, 0)
        col = [d * GDN_HEADS + h for _, d, h in items]
        sl = [slice(h * HEAD_DIM, (h + 1) * HEAD_DIM) for _, _, h in items]
        gc = [gcs[j][d][:, c:c + 1] for (j, d, _), c in zip(items, col)]
        gr = [gcs_t[j][d][c:c + 1, :] for (j, d, _), c in zip(items, col)]
        gtot = [gcs[j][d][edge[d]:edge[d] + 1, c:c + 1] for (j, d, _), c in zip(items, col)]
        beta = [gbc[j][:, 2 * GDN_HEADS + c:2 * GDN_HEADS + c + 1] for (j, _, _), c in zip(items, col)]
        decay = [jnp.where(incl[d], jnp.exp(jnp.minimum(gc[i] - gr[i], 0.0)), 0.0) for i, (_, d, _) in enumerate(items)]
        q = [q_s[pl.ds(r0[j], ch), sl[i]] for i, (j, _, _) in enumerate(items)]
        k = [k_s[pl.ds(r0[j], ch), sl[i]] for i, (j, _, _) in enumerate(items)]
        v = [v_s[pl.ds(r0[j], ch), sl[i]].astype(F32) for i, (j, _, _) in enumerate(items)]
        kb = [x.astype(F32) * b for x, b in zip(k, beta)]
        kk = [_dot_nt(x.astype(BF16), y) for x, y in zip(kb, k)]
        qk = [_dot_nt(x, y) for x, y in zip(q, k)]
        a = [jnp.where(strict[d], kk[i] * decay[i], 0.0) for i, (_, d, _) in enumerate(items)]
        t = _unit_tri_inverses(a)
        eg = [jnp.exp(x) for x in gc]
        rhs = [jnp.concatenate([v[i] * beta[i], kb[i] * eg[i]], axis=1).astype(BF16) for i in range(len(items))]
        sol = [_dot(x.astype(BF16), y) for x, y in zip(t, rhs)]
        attn = [x * y for x, y in zip(qk, decay)]
        q_dec = [x.astype(F32) * e for x, e in zip(q, eg)]
        k_dec = [k[i].astype(F32) * jnp.exp(gtot[i] - gc[i]) for i in range(len(items))]
        dec = [jnp.broadcast_to(jnp.exp(x), (1, HEAD_DIM)) for x in gtot]
        for j in range(len(cs)):
            for d in range(2):
                idx = [i for i, (jj, dd, _) in enumerate(items) if jj == j and dd == d]
                rows = pl.ds(r0[j], ch)

                def heads_cat(parts):
                    return jnp.concatenate(parts, axis=1)

                u_s[d][rows, :] = heads_cat([sol[i][:, 0:HEAD_DIM] for i in idx]).astype(BF16)
                w_s[d][rows, :] = heads_cat([sol[i][:, HEAD_DIM:2 * HEAD_DIM] for i in idx]).astype(BF16)
                a_s[d][rows, :] = heads_cat([attn[i] for i in idx]).astype(BF16)
                qd_s[d][rows, :] = heads_cat([q_dec[i] for i in idx]).astype(BF16)
                kd_s[d][rows, :] = heads_cat([k_dec[i] for i in idx]).astype(BF16)
                dec_s[pl.ds(cs[j], 1), d * GDN_W:(d + 1) * GDN_W] = heads_cat([dec[i] for i in idx])

    def scan_step(cf, cb):
        cs = (cf, cb)
        rows = [pl.ds(pl.multiple_of(c * ch, ch), ch) for c in cs]
        s = [st[...] for st in st_s]
        lhs = [jnp.concatenate([w_s[d][rows[d], :], qd_s[d][rows[d], :]], axis=0) for d in range(2)]
        prod = [_dot(lhs[d], s[d].astype(BF16)) for d in range(2)]
        vb = [(u_s[d][rows[d], :].astype(F32) - prod[d][0:ch]).astype(BF16) for d in range(2)]
        vbd = [jnp.where(head_blocks, jnp.concatenate([x] * GDN_HEADS, axis=0), 0.0) for x in vb]
        av = [_dot(a_s[d][rows[d], :], vbd[d]) for d in range(2)]
        upd = [_dot_tn(kd_s[d][rows[d], :], vb[d]) for d in range(2)]
        for d in range(2):
            dec = dec_s[pl.ds(cs[d], 1), d * GDN_W:(d + 1) * GDN_W]
            st_s[d][...] = s[d] * dec + jnp.where(head_blocks, upd[d], 0.0)
        for d in range(2):
            o_s[rows[d], :] += prod[d][ch:2 * ch] + av[d]

    n_solve = GDN_SOLVE_CHUNKS if (ncc + nlc) % GDN_SOLVE_CHUNKS == 0 else 1

    def solve_body(n, carry):
        solve_chunks([n * n_solve + j for j in range(n_solve)])
        return carry

    lax.fori_loop(0, (ncc + nlc) // n_solve, solve_body, 0)

    for st in st_s:
        st[...] = jnp.zeros_like(st)
    o_s[...] = jnp.zeros_like(o_s)

    def scan_ctx(n, carry):
        scan_step(n, ncc - 1 - n)
        return carry

    def scan_lat(n, carry):
        scan_step(ncc + n, ncc + nlc - 1 - n)
        return carry

    lax.fori_loop(0, ncc, scan_ctx, 0)
    lax.fori_loop(0, nlc, scan_lat, 0)

    ng = ng_ref[...]

    def finish(z_ref, out_ref, n_rows, row_off):
        def body(i, carry):
            t0 = pl.multiple_of(i * pt, pt)
            o = o_s[pl.ds(pl.multiple_of(row_off + t0, pt), pt), :]
            z = z_ref[pl.ds(t0, pt), :].astype(F32)
            y = o * lax.rsqrt(_split_dot(o * o, gmean) + EPS) * ng * _silu(z)
            out_ref[pl.ds(t0, pt), :] = y.astype(out_ref.dtype)
            return carry

        lax.fori_loop(0, n_rows // pt, body, 0)

    finish(zc_ref, oc_ref, n_ctx, 0)
    finish(zl_ref, ol_ref, n_lat, n_ctx)


def gdn_mixer(p_gdn, p_ab, conv_w, a_log, dt_bias, norm_gain, *, n_batch, n_lat, n_ctx):
    width = p_gdn.shape[1]
    ctx_blk0 = (n_batch * n_lat) // n_ctx
    t_all = n_lat + n_ctx
    cw = jnp.pad(conv_w.astype(F32), ((0, 8 - GDN_CONV), (0, 0)))
    par = jnp.zeros((8, LANES), F32)
    par = par.at[0, :2 * GDN_HEADS].set(a_log.reshape(-1)).at[1, :2 * GDN_HEADS].set(dt_bias.reshape(-1))
    ng = jnp.tile(norm_gain.astype(F32).reshape(1, HEAD_DIM), (1, GDN_HEADS))
    seq = lambda w: pl.BlockSpec((t_all, w), lambda b: (b, 0))
    q, k, v, gb = pl.pallas_call(
        functools.partial(_gdn_prep_kernel, n_ctx=n_ctx, n_lat=n_lat),
        out_shape=(jax.ShapeDtypeStruct((n_batch * t_all, GDN_W), BF16),) * 3
        + (jax.ShapeDtypeStruct((n_batch * t_all, LANES), F32),),
        grid=(n_batch,),
        in_specs=[pl.BlockSpec((n_lat, width), lambda b: (b, 0)),
                  pl.BlockSpec((n_ctx, width), lambda b: (ctx_blk0 + b, 0)),
                  pl.BlockSpec((n_lat, LANES), lambda b: (b, 0)),
                  pl.BlockSpec((n_ctx, LANES), lambda b: (ctx_blk0 + b, 0)),
                  pl.BlockSpec(cw.shape, lambda b: (0, 0)),
                  pl.BlockSpec(par.shape, lambda b: (0, 0))],
        out_specs=(seq(GDN_W), seq(GDN_W), seq(GDN_W), seq(LANES)),
        compiler_params=_cparams("parallel"),
        name="gdn_prep",
    )(p_gdn, p_gdn, p_ab, p_ab, cw, par)
    zcol = 3 * GDN_W // GDN_W
    once = lambda w: pl.BlockSpec((t_all, w), lambda b: (b, 0), pipeline_mode=pl.Buffered(1))
    return pl.pallas_call(
        functools.partial(_gdn_kernel, n_ctx=n_ctx, n_lat=n_lat),
        out_shape=(jax.ShapeDtypeStruct((n_batch * n_lat, GDN_W), BF16),
                   jax.ShapeDtypeStruct((n_batch * n_ctx, GDN_W), BF16)),
        grid=(n_batch,),
        in_specs=[once(GDN_W), once(GDN_W), once(GDN_W), once(LANES),
                  pl.BlockSpec((n_lat, GDN_W), lambda b: (b, zcol), pipeline_mode=pl.Buffered(1)),
                  pl.BlockSpec((n_ctx, GDN_W), lambda b: (ctx_blk0 + b, zcol)),
                  pl.BlockSpec((1, GDN_W), lambda b: (0, 0))],
        out_specs=(pl.BlockSpec((n_lat, GDN_W), lambda b: (b, 0)),
                   pl.BlockSpec((n_ctx, GDN_W), lambda b: (b, 0))),
        scratch_shapes=[pltpu.VMEM((t_all, GDN_W), F32), pltpu.VMEM((t_all // GDN_CHUNK, 2 * GDN_W), F32)]
        + [pltpu.VMEM((t_all, GDN_W), BF16)] * 10
        + [pltpu.VMEM((GDN_W, GDN_W), F32)] * 2,
        compiler_params=_cparams("parallel"),
        name="gdn",
    )(q, k, v, gb, p_gdn, p_gdn, ng)


SWA_BLOCK = 128
SWA_BAND = 3 * SWA_BLOCK
SWA_Q_PER_STEP = 4


def _swa_kernel(sink_ref, q_ref, kl_ref, vl_ref, kc_ref, vc_ref, cos_ref, sin_ref, o_ref, *, band, n_lat, n_sub):
    blk = SWA_BLOCK
    kc = kc_ref[...]
    vc = vc_ref[...]
    for sub in range(n_sub):
        rows = slice(sub * blk, (sub + 1) * blk)
        o_ref[rows, :] = _swa_block(pl.program_id(1) * n_sub + sub, q_ref[rows, :].astype(F32), sink_ref, kl_ref,
                                    vl_ref, kc, vc, cos_ref, sin_ref, band=band, n_lat=n_lat).astype(o_ref.dtype)


def _swa_block(i, q, sink_ref, kl_ref, vl_ref, kc, vc, cos_ref, sin_ref, *, band, n_lat):
    blk = SWA_BLOCK
    if band:
        q0 = pl.multiple_of(i * blk, blk)
        start = pl.multiple_of(jnp.clip((i - 1) * blk, 0, n_lat - SWA_BAND), blk)
        q = _rope(q, cos_ref[pl.ds(q0, blk), :], sin_ref[pl.ds(q0, blk), :])
        kb = _rope(kl_ref[pl.ds(start, SWA_BAND), :].astype(F32),
                   cos_ref[pl.ds(start, SWA_BAND), :], sin_ref[pl.ds(start, SWA_BAND), :]).astype(BF16)
        vb = vl_ref[pl.ds(start, SWA_BAND), :]
        keys = jnp.concatenate([kb, kc], axis=0)
        vals = jnp.concatenate([vb, vc], axis=0)
    else:
        keys, vals = kc, vc
    qb = (q * (HEAD_DIM ** -0.5 * LOG2E)).astype(BF16)
    n_keys = keys.shape[0]
    if band:
        q_pos = q0 + lax.broadcasted_iota(jnp.int32, (blk, 1), 0)
        col_id = lax.broadcasted_iota(jnp.int32, (1, n_keys), 1)
        valid = (col_id >= SWA_BAND) | (jnp.abs(start + col_id - q_pos) <= WINDOW)
        bias = jnp.where(valid, 0.0, NEG_BIG)
    v_lane = lax.broadcasted_iota(jnp.int32, vals.shape, 1) // HEAD_DIM
    outs = []
    for g in range(SWA_KV_HEADS):
        ksl = slice(g * HEAD_DIM, (g + 1) * HEAD_DIM)
        v_ext = jnp.where(v_lane == g, vals, 1.0)
        den_lane = (1 - g) * HEAD_DIM
        q4 = jnp.concatenate([qb[:, (g * SWA_GROUP + r) * HEAD_DIM:(g * SWA_GROUP + r + 1) * HEAD_DIM]
                              for r in range(SWA_GROUP)], axis=0)
        s = _dot_nt(q4, keys[:, ksl])
        es, ms, sinks = [], [], []
        for r in range(SWA_GROUP):
            sr = s[r * blk:(r + 1) * blk]
            if band:
                sr = sr + bias
            sink2 = sink_ref[g * SWA_GROUP + r] * LOG2E
            m = jnp.maximum(jnp.max(sr, axis=-1, keepdims=True), sink2)
            es.append(jnp.exp2(sr - m).astype(BF16))
            ms.append(m)
            sinks.append(sink2)
        ov = _dot(jnp.concatenate(es, axis=0), v_ext)
        for r in range(SWA_GROUP):
            ovr = ov[r * blk:(r + 1) * blk]
            den = ovr[:, den_lane:den_lane + 1] + jnp.exp2(sinks[r] - ms[r])
            outs.append(ovr[:, ksl] * (1.0 / den))
    return jnp.concatenate(outs, axis=1)


def swa_mixer(p_swa, cos, sin, sink, *, n_batch, n_lat, n_ctx):
    blk = SWA_BLOCK
    ctx_row0 = n_batch * n_lat
    kcol, vcol = SWA_W // SWA_KV_W, SWA_W // SWA_KV_W + 1
    sink = sink.astype(F32)

    def call(band):
        n_q = n_lat if band else n_ctx
        n_sub = min(SWA_Q_PER_STEP, n_q // blk)
        qblk = n_sub * blk
        nb = n_q // qblk
        qrow0 = 0 if band else ctx_row0 // qblk
        grid_spec = pltpu.PrefetchScalarGridSpec(
            num_scalar_prefetch=1,
            grid=(n_batch, nb),
            in_specs=[pl.BlockSpec((qblk, SWA_W), lambda b, i, s: (qrow0 + b * nb + i, 0)),
                      pl.BlockSpec((n_lat, SWA_KV_W), lambda b, i, s: (b, kcol)),
                      pl.BlockSpec((n_lat, SWA_KV_W), lambda b, i, s: (b, vcol)),
                      pl.BlockSpec((n_ctx, SWA_KV_W), lambda b, i, s: (ctx_row0 // n_ctx + b, kcol)),
                      pl.BlockSpec((n_ctx, SWA_KV_W), lambda b, i, s: (ctx_row0 // n_ctx + b, vcol)),
                      pl.BlockSpec((n_lat, LANES), lambda b, i, s: (0, 0)),
                      pl.BlockSpec((n_lat, LANES), lambda b, i, s: (0, 0))],
            out_specs=pl.BlockSpec((qblk, SWA_W), lambda b, i, s: (b * nb + i, 0)),
        )
        return pl.pallas_call(
            functools.partial(_swa_kernel, band=band, n_lat=n_lat, n_sub=n_sub),
            out_shape=jax.ShapeDtypeStruct((n_batch * n_q, SWA_W), BF16),
            grid_spec=grid_spec,
            compiler_params=_cparams("parallel", "arbitrary"),
            name="swa_latent" if band else "swa_context",
        )(sink, p_swa, p_swa, p_swa, p_swa, p_swa, cos, sin)

    return call(True), call(False)


MOE_ROW_TILE = 512
MOE_FF_TILE = 896
MOE_DOWN_TILE = 512


def _row_tile(n_lat, n_ctx_rows):
    for tm in (512, 256, 128):
        if n_lat % tm == 0 and n_ctx_rows % tm == 0:
            return tm
    raise ValueError("sequence lengths must be multiples of 128")


def kernel(x, c, ctx, c_ctx, ada_w, ada_b, norm1_g, w_in, ret_decay, ret_gn_g, gdn_conv_w, gdn_a_log, gdn_dt_bias,
           gdn_norm_g, swa_sink, w_out, norm2_g, ffn_w1, ffn_w3, ffn_w2, router_w, moe_w1, moe_w3, moe_w2, final_g):
    b, s, d = x.shape
    n_ctx = ctx.shape[1]
    depth = ada_w.shape[0]
    nl, ncx = b * s, b * n_ctx
    tm = _row_tile(s, ncx)
    seg = _seg_map(nl // tm, s // tm, b)

    cvec = jnp.zeros((8, d), F32).at[:b].set(c).at[b].set(c_ctx)
    mod_all = adaln(cvec, ada_w, ada_b).reshape(depth, 8, 6, d)
    xall = jnp.concatenate([x.reshape(nl, d), ctx.reshape(ncx, d)], axis=0)
    cos, sin = rope_tables(s)
    fg = final_g.reshape(1, d)

    ret_cols, gdn_cols = 4 * RET_W, 4 * GDN_W
    ab_cols = 4 * GDN_HEADS
    for layer in range(depth):
        last = layer == depth - 1
        mod = mod_all[layer]
        w = w_in[layer].astype(BF16)
        w_ret = w[:, :ret_cols]
        w_gdn = w[:, ret_cols:ret_cols + gdn_cols]
        w_ab = jnp.pad(w[:, ret_cols + gdn_cols:ret_cols + gdn_cols + ab_cols], ((0, 0), (0, LANES - ab_cols)))
        w_swa = w[:, ret_cols + gdn_cols + ab_cols:]
        p_ret, p_gdn, p_swa, p_ab = in_proj(xall, mod, norm1_g[layer].reshape(1, d), w_ret, w_gdn, w_swa, w_ab,
                                            tm=tm, seg=seg)
        dec_lanes = jnp.repeat(ret_decay[layer].astype(F32), HEAD_DIM, axis=1)
        o_ret = retention_mixer(p_ret, cos, sin, dec_lanes, ret_gn_g[layer].reshape(1, RET_W),
                                n_batch=b, n_lat=s, n_ctx=n_ctx)
        o_gdn = gdn_mixer(p_gdn, p_ab, gdn_conv_w[layer], gdn_a_log[layer], gdn_dt_bias[layer], gdn_norm_g[layer],
                          n_batch=b, n_lat=s, n_ctx=n_ctx)
        o_swa = swa_mixer(p_swa, cos, sin, swa_sink[layer], n_batch=b, n_lat=s, n_ctx=n_ctx)

        n_rows = nl if last else nl + ncx
        is_moe = layer % 2 == 1
        i = layer // 2
        rw = None
        if is_moe:
            rw = jnp.pad(router_w[i].astype(F32), ((0, 0), (0, LANES - N_EXPERTS)))
        res = out_proj(o_ret, o_gdn, o_swa, xall, mod, w_out[layer].astype(BF16), norm2_g[layer].reshape(1, d), rw,
                       tm=tm, seg=seg, n_rows=n_rows, n_lat_rows=nl, tok_dtype=F32 if is_moe else BF16)
        if is_moe:
            x_new, tokens, ridx, rgate, cnt = res
            xall = moe_layer(tokens, ridx, rgate, cnt, x_new, mod, moe_w1, moe_w3, moe_w2, fg,
                             li=i, tm=tm, tmx=MOE_ROW_TILE, tf=MOE_FF_TILE, seg=seg, last=last)
        else:
            x_new, tokens = res
            xall = dense_ffn(tokens, x_new, mod, ffn_w1[i].astype(BF16), ffn_w3[i].astype(BF16),
                             ffn_w2[i].astype(BF16), fg, tm=tm, seg=seg, last=last)
    return xall[:nl].reshape(b, s, d)
```

```python
import functools

import jax
import jax.numpy as jnp
import numpy as np
from jax import lax
from jax.experimental import pallas as pl
from jax.experimental.pallas import tpu as pltpu

F32 = jnp.float32
BF16 = jnp.bfloat16

HEAD_DIM = 64
GRID_W = 64
ROPE_BASE = 10000.0
EPS = 1e-6
RET_HEADS = 4
GDN_HEADS = 4
GDN_CONV = 5
SWA_HEADS = 8
SWA_KV_HEADS = 2
SWA_GROUP = SWA_HEADS // SWA_KV_HEADS
WINDOW = 128
N_EXPERTS = 8
RET_W = RET_HEADS * HEAD_DIM
GDN_W = GDN_HEADS * HEAD_DIM
SWA_W = SWA_HEADS * HEAD_DIM
SWA_KV_W = SWA_KV_HEADS * HEAD_DIM
LANES = 128
VMEM_LIMIT = 56 * 1024 * 1024
NEG_BIG = -1e30
LOG2E = 1.4426950408889634


def _cparams(*sem):
    return pltpu.CompilerParams(dimension_semantics=sem, vmem_limit_bytes=VMEM_LIMIT)


def _dot(a, b):
    return jnp.dot(a, b, preferred_element_type=F32)


def _dot_nt(a, b):
    return lax.dot_general(a, b, (((1,), (1,)), ((), ())), preferred_element_type=F32)


def _dot_tn(a, b):
    return lax.dot_general(a, b, (((0,), (0,)), ((), ())), preferred_element_type=F32)


def _split_dot(x, m_bf16, terms=2):
    acc = None
    rem = x
    for _ in range(terms):
        part = rem.astype(BF16)
        d = _dot(part, m_bf16)
        acc = d if acc is None else acc + d
        rem = rem - part.astype(F32)
    return acc


def _sigmoid(x):
    return 1.0 / (1.0 + jnp.exp(-x))


def _silu(x):
    return x * _sigmoid(x)


def _rms_mod(x, gain, shift, scale):
    ms = jnp.mean(x * x, axis=-1, keepdims=True)
    y = x * lax.rsqrt(ms + EPS) * gain
    return y * (1.0 + scale) + shift


def _seg_map(n_lat_tiles, tiles_per_batch, n_batch):
    def seg(i):
        return jnp.where(i < n_lat_tiles, i // tiles_per_batch, n_batch)
    return seg


def _adaln_kernel(c_ref, w_ref, b_ref, o_ref):
    h = _silu(c_ref[...]).astype(BF16)
    o_ref[0] = _dot(h, w_ref[0].astype(BF16)) + b_ref[0]


def adaln(cvec, ada_w, ada_b):
    depth, d, n6 = ada_w.shape
    tn = 1536 if n6 % 1536 == 0 else n6
    rows = cvec.shape[0]
    return pl.pallas_call(
        _adaln_kernel,
        out_shape=jax.ShapeDtypeStruct((depth, rows, n6), F32),
        grid=(depth, n6 // tn),
        in_specs=[pl.BlockSpec((rows, d), lambda l, j: (0, 0)),
                  pl.BlockSpec((1, d, tn), lambda l, j: (l, 0, j)),
                  pl.BlockSpec((1, 1, tn), lambda l, j: (l, 0, j))],
        out_specs=pl.BlockSpec((1, rows, tn), lambda l, j: (l, 0, j)),
        compiler_params=_cparams("parallel", "parallel"),
        name="adaln",
    )(cvec, ada_w, ada_b.reshape(depth, 1, n6))


def _inproj_kernel(x_ref, mod_ref, g_ref, wr_ref, wg_ref, ws_ref, wab_ref, pr_ref, pg_ref, ps_ref, pab_ref):
    h = _rms_mod(x_ref[...], g_ref[...], mod_ref[0, 0:1, :], mod_ref[0, 1:2, :]).astype(BF16)
    for w_ref, o_ref in ((wr_ref, pr_ref), (wg_ref, pg_ref), (ws_ref, ps_ref), (wab_ref, pab_ref)):
        n = w_ref.shape[1]
        step = 256 if n % 256 == 0 else n
        for c in range(0, n, step):
            o_ref[:, c:c + step] = _dot(h, w_ref[:, c:c + step]).astype(o_ref.dtype)


def in_proj(xall, mod, gain, w_ret, w_gdn, w_swa, w_ab, *, tm, seg):
    n, d = xall.shape
    full = lambda i: (0, 0)
    row = lambda i: (i, 0)
    outs = (jax.ShapeDtypeStruct((n, w_ret.shape[1]), BF16), jax.ShapeDtypeStruct((n, w_gdn.shape[1]), BF16),
            jax.ShapeDtypeStruct((n, w_swa.shape[1]), BF16), jax.ShapeDtypeStruct((n, w_ab.shape[1]), F32))
    return pl.pallas_call(
        _inproj_kernel,
        out_shape=outs,
        grid=(n // tm,),
        in_specs=[pl.BlockSpec((tm, d), row),
                  pl.BlockSpec((1, 6, d), lambda i: (seg(i), 0, 0)),
                  pl.BlockSpec((1, d), full),
                  pl.BlockSpec(w_ret.shape, full), pl.BlockSpec(w_gdn.shape, full),
                  pl.BlockSpec(w_swa.shape, full), pl.BlockSpec(w_ab.shape, full)],
        out_specs=tuple(pl.BlockSpec((tm, o.shape[1]), row) for o in outs),
        compiler_params=_cparams("parallel"),
        name="in_proj",
    )(xall, mod, gain, w_ret, w_gdn, w_swa, w_ab)


def _outproj_kernel(*refs, tm, moe, n_lat_tiles, has_ctx):
    n_mix = 6 if has_ctx else 3
    mix = refs[:n_mix]
    refs = refs[n_mix:]
    if moe:
        (x_ref, mod_ref, w_ref, g_ref, rw_ref, xo_ref, tok_ref, ridx_ref, rgate_ref, cnt_ref, base_ref) = refs
    else:
        x_ref, mod_ref, w_ref, g_ref, xo_ref, tok_ref = refs
    if has_ctx:
        is_lat = pl.program_id(0) < n_lat_tiles
        o_ret, o_gdn, o_swa = (jnp.where(is_lat, mix[2 * k][...], mix[2 * k + 1][...]) for k in range(3))
    else:
        o_ret, o_gdn, o_swa = (m[...] for m in mix)
    y = _dot(o_ret, w_ref[0:RET_W, :])
    y = y + _dot(o_gdn, w_ref[RET_W:RET_W + GDN_W, :])
    y = y + _dot(o_swa, w_ref[RET_W + GDN_W:, :])
    x = x_ref[...] + mod_ref[0, 2:3, :] * y
    xo_ref[...] = x
    t = _rms_mod(x, g_ref[...], mod_ref[0, 3:4, :], mod_ref[0, 4:5, :])
    tok_ref[...] = t.astype(tok_ref.dtype)
    if not moe:
        return

    @pl.when(pl.program_id(0) == 0)
    def _():
        base_ref[...] = jnp.zeros_like(base_ref)

    rw = rw_ref[...]
    rw_hi = rw.astype(BF16)
    rw_lo = (rw - rw_hi.astype(F32)).astype(BF16)
    t_hi = t.astype(BF16)
    t_lo = (t - t_hi.astype(F32)).astype(BF16)
    logits = _dot(t_hi, rw_hi) + _dot(t_hi, rw_lo) + _dot(t_lo, rw_hi)
    lane = lax.broadcasted_iota(jnp.int32, logits.shape, 1)
    logits = jnp.where(lane < N_EXPERTS, logits, NEG_BIG)
    m1 = jnp.max(logits, axis=-1, keepdims=True)
    i1 = jnp.min(jnp.where(logits == m1, lane, LANES), axis=-1, keepdims=True)
    rest = jnp.where(lane == i1, NEG_BIG, logits)
    m2 = jnp.max(rest, axis=-1, keepdims=True)
    i2 = jnp.min(jnp.where(rest == m2, lane, LANES), axis=-1, keepdims=True)
    e2 = jnp.exp(m2 - m1)
    g1 = 1.0 / (1.0 + e2)
    g2 = e2 * g1
    oh1 = jnp.where(lane == i1, 1.0, 0.0)
    oh2 = jnp.where(lane == i2, 1.0, 0.0)
    r_i = lax.broadcasted_iota(jnp.int32, (tm, tm), 0)
    c_i = lax.broadcasted_iota(jnp.int32, (tm, tm), 1)
    lower = jnp.where(r_i > c_i, 1.0, 0.0).astype(BF16)
    cum1 = _dot(lower, oh1.astype(BF16))
    cum2 = _dot(lower, oh2.astype(BF16))
    cnt1 = jnp.sum(oh1, axis=0, keepdims=True)
    cnt2 = jnp.sum(oh2, axis=0, keepdims=True)
    base = base_ref[...]
    rank1 = jnp.sum(oh1 * (base + cum1), axis=-1, keepdims=True)
    rank2 = jnp.sum(oh2 * (base + cnt1 + cum2), axis=-1, keepdims=True)
    base = base + cnt1 + cnt2
    base_ref[...] = base
    cnt_ref[...] = base.astype(jnp.int32)
    lane8 = lax.broadcasted_iota(jnp.int32, (tm, 8), 1)
    r1 = rank1.astype(jnp.int32)
    r2 = rank2.astype(jnp.int32)
    ridx_ref[...] = jnp.where(lane8 == 0, i1, jnp.where(lane8 == 1, i2, jnp.where(lane8 == 2, r1,
                              jnp.where(lane8 == 3, r2, 0))))
    rgate_ref[...] = jnp.where(lane8 == 0, g1, jnp.where(lane8 == 1, g2, 0.0))


def out_proj(o_ret, o_gdn, o_swa, xall, mod, w_out, gain2, router_w, *, tm, seg, n_rows, n_lat_rows, tok_dtype):
    d = xall.shape[1]
    moe = router_w is not None
    has_ctx = n_rows > n_lat_rows
    n_lat_tiles = n_lat_rows // tm
    full = lambda i: (0, 0)
    row = lambda i: (i, 0)
    lat_row = lambda i: (jnp.minimum(i, n_lat_tiles - 1), 0)
    ctx_row = lambda i: (jnp.maximum(i - n_lat_tiles, 0), 0)
    in_specs, args = [], []
    for pair, width in ((o_ret, RET_W), (o_gdn, GDN_W), (o_swa, SWA_W)):
        in_specs.append(pl.BlockSpec((tm, width), lat_row))
        args.append(pair[0])
        if has_ctx:
            in_specs.append(pl.BlockSpec((tm, width), ctx_row))
            args.append(pair[1])
    in_specs += [pl.BlockSpec((tm, d), row), pl.BlockSpec((1, 6, d), lambda i: (seg(i), 0, 0)),
                 pl.BlockSpec(w_out.shape, full), pl.BlockSpec((1, d), full)]
    args += [xall, mod, w_out, gain2]
    outs = [jax.ShapeDtypeStruct((n_rows, d), F32), jax.ShapeDtypeStruct((n_rows, d), tok_dtype)]
    out_specs = [pl.BlockSpec((tm, d), row), pl.BlockSpec((tm, d), row)]
    scratch = []
    if moe:
        in_specs.append(pl.BlockSpec(router_w.shape, full))
        args.append(router_w)
        outs += [jax.ShapeDtypeStruct((n_rows, 8), jnp.int32), jax.ShapeDtypeStruct((n_rows, 8), F32),
                 jax.ShapeDtypeStruct((1, LANES), jnp.int32)]
        out_specs += [pl.BlockSpec((tm, 8), row), pl.BlockSpec((tm, 8), row), pl.BlockSpec((1, LANES), full)]
        scratch = [pltpu.VMEM((1, LANES), F32)]
    return pl.pallas_call(
        functools.partial(_outproj_kernel, tm=tm, moe=moe, n_lat_tiles=n_lat_tiles, has_ctx=has_ctx),
        out_shape=tuple(outs),
        grid=(n_rows // tm,),
        in_specs=in_specs,
        out_specs=tuple(out_specs),
        scratch_shapes=scratch,
        compiler_params=_cparams("arbitrary"),
        name="out_proj_router" if moe else "out_proj",
    )(*args)


def _final_norm(x, gain):
    ms = jnp.mean(x * x, axis=-1, keepdims=True)
    return x * lax.rsqrt(ms + EPS) * gain


def _dense_ffn_kernel(tok_ref, x_ref, mod_ref, w1_ref, w3_ref, w2_ref, fg_ref, o_ref, *, last):
    t = tok_ref[...]
    h = (_silu(_dot(t, w1_ref[...])) * _dot(t, w3_ref[...])).astype(BF16)
    x = x_ref[...] + mod_ref[0, 5:6, :] * _dot(h, w2_ref[...])
    if last:
        x = _final_norm(x, fg_ref[...])
    o_ref[...] = x


def dense_ffn(tokens, xall, mod, w1, w3, w2, final_g, *, tm, seg, last):
    n, d = tokens.shape
    f = w1.shape[1]
    full = lambda i: (0, 0)
    row = lambda i: (i, 0)
    return pl.pallas_call(
        functools.partial(_dense_ffn_kernel, last=last),
        out_shape=jax.ShapeDtypeStruct((n, d), F32),
        grid=(n // tm,),
        in_specs=[pl.BlockSpec((tm, d), row), pl.BlockSpec((tm, d), row),
                  pl.BlockSpec((1, 6, d), lambda i: (seg(i), 0, 0)),
                  pl.BlockSpec((d, f), full, pipeline_mode=pl.Buffered(1)),
                  pl.BlockSpec((d, f), full, pipeline_mode=pl.Buffered(1)),
                  pl.BlockSpec((f, d), full, pipeline_mode=pl.Buffered(1)),
                  pl.BlockSpec((1, d), full)],
        out_specs=pl.BlockSpec((tm, d), row),
        compiler_params=_cparams("parallel"),
        name="dense_ffn",
    )(tokens, xall, mod, w1, w3, w2, final_g)


def _dispatch_kernel(dest_ref, tok_ref, xs_in_ref, xs_ref, sem, *, tm):
    del xs_in_ref

    def issue(r, carry):
        for k in range(2):
            dst = dest_ref[0, 0, 2 * r + k]
            pltpu.make_async_copy(tok_ref.at[pl.ds(r, 1)], xs_ref.at[pl.ds(dst, 1)], sem).start()
        return carry

    lax.fori_loop(0, tm, issue, 0, unroll=8)
    for _ in range(2):
        pltpu.make_async_copy(tok_ref, xs_ref.at[pl.ds(0, tm)], sem).wait()


def moe_dispatch(tokens, dest3, n_slots, *, tm):
    n, d = tokens.shape
    xs0 = jnp.zeros((n_slots, d), tokens.dtype)
    return pl.pallas_call(
        functools.partial(_dispatch_kernel, tm=tm),
        out_shape=jax.ShapeDtypeStruct((n_slots, d), tokens.dtype),
        grid=(n // tm,),
        in_specs=[pl.BlockSpec((1, 1, 2 * tm), lambda i: (i, 0, 0), memory_space=pltpu.SMEM),
                  pl.BlockSpec((tm, d), lambda i: (i, 0)),
                  pl.BlockSpec(memory_space=pl.ANY)],
        out_specs=pl.BlockSpec(memory_space=pl.ANY),
        scratch_shapes=[pltpu.SemaphoreType.DMA],
        input_output_aliases={2: 0},
        compiler_params=_cparams("arbitrary"),
        name="moe_dispatch",
    )(dest3, tokens, xs0)


def _tile_expert(te_ref, nu_ref, i):
    return te_ref[jnp.clip(i, 0, nu_ref[0] - 1)]


def _moe_up_kernel(te_ref, nu_ref, xs_ref, w1_ref, w3_ref, h_ref, w1b_s, w3b_s):
    i = pl.program_id(1)

    @pl.when((i == 0) | (_tile_expert(te_ref, nu_ref, i) != _tile_expert(te_ref, nu_ref, i - 1)))
    def _():
        w1b_s[...] = w1_ref[0, 0].astype(BF16)
        w3b_s[...] = w3_ref[0, 0].astype(BF16)

    @pl.when(i < nu_ref[0])
    def _():
        x = xs_ref[...].astype(BF16)
        h_ref[...] = (_silu(_dot(x, w1b_s[...])) * _dot(x, w3b_s[...])).astype(h_ref.dtype)

    @pl.when(i >= nu_ref[0])
    def _():
        h_ref[...] = jnp.zeros_like(h_ref)


def _moe_down_kernel(te_ref, nu_ref, h_ref, w2_ref, y_ref, w2b_s):
    i = pl.program_id(1)

    @pl.when((i == 0) | (_tile_expert(te_ref, nu_ref, i) != _tile_expert(te_ref, nu_ref, i - 1)))
    def _():
        w2b_s[...] = w2_ref[0, 0].astype(BF16)

    @pl.when(i < nu_ref[0])
    def _():
        y_ref[...] = _dot(h_ref[...], w2b_s[...])

    @pl.when(i >= nu_ref[0])
    def _():
        y_ref[...] = jnp.zeros_like(y_ref)


def moe_ffn(xs, tile_expert, n_used, w1, w3, w2, *, li, tm, tf, tn):
    r, d = xs.shape
    f = w1.shape[3]

    def tile(i, nu):
        return jnp.minimum(i, nu[0] - 1)

    up_spec = pltpu.PrefetchScalarGridSpec(
        num_scalar_prefetch=2,
        grid=(f // tf, r // tm),
        in_specs=[pl.BlockSpec((tm, d), lambda j, i, te, nu: (tile(i, nu), 0)),
                  pl.BlockSpec((1, 1, d, tf), lambda j, i, te, nu: (li, te[tile(i, nu)], 0, j)),
                  pl.BlockSpec((1, 1, d, tf), lambda j, i, te, nu: (li, te[tile(i, nu)], 0, j))],
        out_specs=pl.BlockSpec((tm, tf), lambda j, i, te, nu: (i, j)),
        scratch_shapes=[pltpu.VMEM((d, tf), BF16), pltpu.VMEM((d, tf), BF16)],
    )
    hidden = pl.pallas_call(
        _moe_up_kernel,
        out_shape=jax.ShapeDtypeStruct((r, f), BF16),
        grid_spec=up_spec,
        compiler_params=_cparams("arbitrary", "arbitrary"),
        name="moe_up",
    )(tile_expert, n_used, xs, w1, w3)
    down_spec = pltpu.PrefetchScalarGridSpec(
        num_scalar_prefetch=2,
        grid=(d // tn, r // tm),
        in_specs=[pl.BlockSpec((tm, f), lambda n, i, te, nu: (tile(i, nu), 0)),
                  pl.BlockSpec((1, 1, f, tn), lambda n, i, te, nu: (li, te[tile(i, nu)], 0, n))],
        out_specs=pl.BlockSpec((tm, tn), lambda n, i, te, nu: (i, n)),
        scratch_shapes=[pltpu.VMEM((f, tn), BF16)],
    )
    return pl.pallas_call(
        _moe_down_kernel,
        out_shape=jax.ShapeDtypeStruct((r, d), F32),
        grid_spec=down_spec,
        compiler_params=_cparams("arbitrary", "arbitrary"),
        name="moe_down",
    )(tile_expert, n_used, hidden, w2)


def _combine_kernel(dest_ref, gate_ref, x_ref, mod_ref, fg_ref, ys_ref, o_ref, buf_ref, sem, *, tm, last):
    def issue(r, carry):
        for k in range(2):
            src = dest_ref[0, 0, 2 * r + k]
            pltpu.make_async_copy(ys_ref.at[pl.ds(src, 1)], buf_ref.at[k, pl.ds(r, 1)], sem).start()
        return carry

    lax.fori_loop(0, tm, issue, 0, unroll=8)
    for k in range(2):
        pltpu.make_async_copy(ys_ref.at[pl.ds(0, tm)], buf_ref.at[k], sem).wait()
    gate = gate_ref[...]
    y = gate[:, 0:1] * buf_ref[0] + gate[:, 1:2] * buf_ref[1]
    x = x_ref[...] + mod_ref[0, 5:6, :] * y
    if last:
        x = _final_norm(x, fg_ref[...])
    o_ref[...] = x


def moe_combine(ys, dest3, rgate, xall, mod, final_g, *, tm, seg, last):
    n, d = xall.shape
    return pl.pallas_call(
        functools.partial(_combine_kernel, tm=tm, last=last),
        out_shape=jax.ShapeDtypeStruct((n, d), F32),
        grid=(n // tm,),
        in_specs=[pl.BlockSpec((1, 1, 2 * tm), lambda i: (i, 0, 0), memory_space=pltpu.SMEM),
                  pl.BlockSpec((tm, 8), lambda i: (i, 0)),
                  pl.BlockSpec((tm, d), lambda i: (i, 0)),
                  pl.BlockSpec((1, 6, d), lambda i: (seg(i), 0, 0)),
                  pl.BlockSpec((1, d), lambda i: (0, 0)),
                  pl.BlockSpec(memory_space=pl.ANY)],
        out_specs=pl.BlockSpec((tm, d), lambda i: (i, 0)),
        scratch_shapes=[pltpu.VMEM((2, tm, d), F32), pltpu.SemaphoreType.DMA],
        compiler_params=_cparams("arbitrary"),
        name="moe_combine",
    )(dest3, rgate, xall, mod, final_g, ys)


def moe_layer(tokens, ridx, rgate, cnt, xall, mod, w1, w3, w2, final_g, *, li, tm, tmx, tf, seg, last):
    n = tokens.shape[0]
    counts = cnt[0, :N_EXPERTS]
    padded = (counts + tmx - 1) // tmx * tmx
    pend = jnp.cumsum(padded)
    pstart = pend - padded
    dest = pstart[ridx[:, 0:2]] + ridx[:, 2:4]
    dest3 = dest.reshape(n // tm, 1, 2 * tm).astype(jnp.int32)
    n_tiles = (2 * n) // tmx + N_EXPERTS
    n_used = (pend[-1] // tmx).astype(jnp.int32).reshape(1)
    tile_start = jnp.arange(n_tiles, dtype=jnp.int32) * tmx
    tile_expert = jnp.minimum(jnp.sum(tile_start[:, None] >= pend[None, :], axis=1), N_EXPERTS - 1).astype(jnp.int32)
    xs = moe_dispatch(tokens, dest3, n_tiles * tmx, tm=tm)
    ys = moe_ffn(xs, tile_expert, n_used, w1, w3, w2, li=li, tm=tmx, tf=tf, tn=MOE_DOWN_TILE)
    return moe_combine(ys, dest3, rgate, xall, mod, final_g, tm=tm, seg=seg, last=last)


def rope_tables(seq):
    rows = seq // GRID_W
    row = jnp.repeat(jnp.arange(rows, dtype=F32), GRID_W)
    col = jnp.tile(jnp.arange(GRID_W, dtype=F32), rows)
    n_freq = HEAD_DIM // 4
    inv = ROPE_BASE ** (-jnp.arange(n_freq, dtype=F32) / n_freq)
    ang = jnp.concatenate([row[:, None] * inv, col[:, None] * inv], axis=-1)
    cos, sin = jnp.cos(ang), jnp.sin(ang)
    cos64 = jnp.concatenate([cos, cos], axis=-1)
    sin64 = jnp.concatenate([-sin, sin], axis=-1)
    return jnp.tile(cos64, (1, 2)), jnp.tile(sin64, (1, 2))


def _tile_lanes(t, width):
    reps = width // t.shape[1]
    return t if reps == 1 else jnp.concatenate([t] * reps, axis=1)


def _rope(t, cos, sin):
    w = t.shape[1]
    lane = lax.broadcasted_iota(jnp.int32, t.shape, 1) % HEAD_DIM
    half = HEAD_DIM // 2
    rot = jnp.where(lane < half, pltpu.roll(t, w - half, 1), pltpu.roll(t, half, 1))
    return t * _tile_lanes(cos, w) + rot * _tile_lanes(sin, w)


def _group_matrix(width, value):
    r = lax.broadcasted_iota(jnp.int32, (width, width), 0) // HEAD_DIM
    c = lax.broadcasted_iota(jnp.int32, (width, width), 1) // HEAD_DIM
    return jnp.where(r == c, value, 0.0).astype(BF16)


RET_CHUNK = 128


def _retention_kernel(lat_ref, ctx_ref, cos_ref, sin_ref, dec_ref, gn_ref, ol_ref, oc_ref,
                      q_s, k_s, bn_s, st_s, o_s, dm_s, *, n_ctx, n_lat):
    ch = RET_CHUNK
    nc, nl = n_ctx // ch, n_lat // ch
    lg = -jnp.exp(dec_ref[...])
    lgf, lgb = lg[0:1], lg[1:2]
    pos = lax.broadcasted_iota(jnp.int32, (ch, 1), 0).astype(F32)
    wkf = jnp.exp((ch - 1 - pos) * lgf)
    wkb = jnp.exp(pos * lgb)
    wqf = jnp.exp((pos + 1) * lgf)
    wqb = jnp.exp((ch - pos) * lgb)
    cdf = jnp.exp(ch * lgf)
    cdb = jnp.exp(ch * lgb)
    blockmask = (lax.broadcasted_iota(jnp.int32, (RET_W, RET_W), 0) // HEAD_DIM
                 == lax.broadcasted_iota(jnp.int32, (RET_W, RET_W), 1) // HEAD_DIM)
    gmean = _group_matrix(RET_W, 1.0 / HEAD_DIM)
    diff = (lax.broadcasted_iota(jnp.int32, (ch, ch), 0) - lax.broadcasted_iota(jnp.int32, (ch, ch), 1)).astype(F32)
    gn = gn_ref[...]

    def load_qkv(ref, t0):
        return (ref[pl.ds(t0, ch), 0:RET_W].astype(F32), ref[pl.ds(t0, ch), RET_W:2 * RET_W].astype(F32),
                ref[pl.ds(t0, ch), 2 * RET_W:3 * RET_W])

    def reverse_step(ref, t0, c, roped):
        q, k, v = load_qkv(ref, t0)
        if roped:
            cos, sin = cos_ref[pl.ds(t0, ch), :], sin_ref[pl.ds(t0, ch), :]
            q, k = _rope(q, cos, sin), _rope(k, cos, sin)
        k = k * (HEAD_DIM ** -0.5)
        r0 = pl.multiple_of(c * ch, ch)
        q_s[pl.ds(r0, ch), :] = q.astype(BF16)
        k_s[pl.ds(r0, ch), :] = k.astype(BF16)
        st = st_s[...]
        bn_s[c] = st.astype(BF16)
        kvb = _dot_tn((k * wkb).astype(BF16), v)
        st_s[...] = cdb * st + jnp.where(blockmask, kvb, 0.0)

    st_s[...] = jnp.zeros_like(st_s)

    def rev_ctx(n, carry):
        c = nc - 1 - n
        reverse_step(ctx_ref, pl.multiple_of(c * ch, ch), c, False)
        return carry

    def rev_lat(n, carry):
        c = nl - 1 - n
        reverse_step(lat_ref, pl.multiple_of(c * ch, ch), nc + c, True)
        return carry

    lax.fori_loop(0, nc, rev_ctx, 0)
    lax.fori_loop(0, nl, rev_lat, 0)

    for h in range(RET_HEADS):
        lf, lb = lgf[:, h * HEAD_DIM:h * HEAD_DIM + 1], lgb[:, h * HEAD_DIM:h * HEAD_DIM + 1]
        dm_s[h] = jnp.where(diff == 0.0, 2.0, jnp.exp(jnp.abs(diff) * jnp.where(diff > 0.0, lf, lb)))

    def forward_step(ref, t0, c):
        r0 = pl.multiple_of(c * ch, ch)
        q = q_s[pl.ds(r0, ch), :]
        k = k_s[pl.ds(r0, ch), :]
        v = ref[pl.ds(t0, ch), 2 * RET_W:3 * RET_W]
        sls = [slice(h * HEAD_DIM, (h + 1) * HEAD_DIM) for h in range(RET_HEADS)]
        scores = [_dot_nt(q[:, sl], k[:, sl]) for sl in sls]
        probs = [(s * dm_s[h]).astype(BF16) for h, s in enumerate(scores)]
        heads = [_dot(p, v[:, sl]) for p, sl in zip(probs, sls)]
        qf = q.astype(F32)
        st = st_s[...]
        o = jnp.concatenate(heads, axis=1)
        o_s[pl.ds(r0, ch), :] = (o + _dot((qf * wqf).astype(BF16), st.astype(BF16))
                                 + _dot((qf * wqb).astype(BF16), bn_s[c]))
        kvf = _dot_tn((k.astype(F32) * wkf).astype(BF16), v)
        st_s[...] = cdf * st + jnp.where(blockmask, kvf, 0.0)

    st_s[...] = jnp.zeros_like(st_s)

    def fwd_ctx(n, carry):
        forward_step(ctx_ref, pl.multiple_of(n * ch, ch), n)
        return carry

    def fwd_lat(n, carry):
        forward_step(lat_ref, pl.multiple_of(n * ch, ch), nc + n)
        return carry

    lax.fori_loop(0, nc, fwd_ctx, 0)
    lax.fori_loop(0, nl, fwd_lat, 0, unroll=2)

    def finish(ref, out_ref, n_rows, row_off):
        ft = 2 * ch

        def body(i, carry):
            t0 = pl.multiple_of(i * ft, ft)
            o = o_s[pl.ds(pl.multiple_of(row_off + t0, ft), ft), :]
            g = ref[pl.ds(t0, ft), 3 * RET_W:4 * RET_W].astype(F32)
            xc = o - _split_dot(o, gmean)
            var = _split_dot(xc * xc, gmean)
            out_ref[pl.ds(t0, ft), :] = (xc * lax.rsqrt(var + EPS) * gn * _silu(g)).astype(out_ref.dtype)
            return carry

        lax.fori_loop(0, n_rows // ft, body, 0)

    finish(ctx_ref, oc_ref, n_ctx, 0)
    finish(lat_ref, ol_ref, n_lat, n_ctx)


def retention_mixer(p_ret, cos, sin, dec_lanes, gn_gain, *, n_batch, n_lat, n_ctx):
    n = p_ret.shape[0]
    width = p_ret.shape[1]
    ctx_blk0 = (n_batch * n_lat) // n_ctx
    t_all = n_lat + n_ctx
    nch = t_all // RET_CHUNK
    return pl.pallas_call(
        functools.partial(_retention_kernel, n_ctx=n_ctx, n_lat=n_lat),
        out_shape=(jax.ShapeDtypeStruct((n_batch * n_lat, RET_W), BF16),
                   jax.ShapeDtypeStruct((n_batch * n_ctx, RET_W), BF16)),
        grid=(n_batch,),
        in_specs=[pl.BlockSpec((n_lat, width), lambda b: (b, 0)),
                  pl.BlockSpec((n_ctx, width), lambda b: (ctx_blk0 + b, 0)),
                  pl.BlockSpec((n_lat, LANES), lambda b: (0, 0)),
                  pl.BlockSpec((n_lat, LANES), lambda b: (0, 0)),
                  pl.BlockSpec((2, RET_W), lambda b: (0, 0)),
                  pl.BlockSpec((1, RET_W), lambda b: (0, 0))],
        out_specs=(pl.BlockSpec((n_lat, RET_W), lambda b: (b, 0)),
                   pl.BlockSpec((n_ctx, RET_W), lambda b: (b, 0))),
        scratch_shapes=[pltpu.VMEM((t_all, RET_W), BF16), pltpu.VMEM((t_all, RET_W), BF16),
                        pltpu.VMEM((nch, RET_W, RET_W), BF16), pltpu.VMEM((RET_W, RET_W), F32),
                        pltpu.VMEM((t_all, RET_W), F32), pltpu.VMEM((RET_HEADS, RET_CHUNK, RET_CHUNK), F32)],
        compiler_params=_cparams("parallel"),
        name="retention",
    )(p_ret, p_ret, cos, sin, dec_lanes, gn_gain)


GDN_CHUNK = 64
GDN_PREP = 128
GDN_HALO = 8
GDN_SOLVE_CHUNKS = 4


def _softplus(x):
    return jnp.maximum(x, 0.0) + jnp.log(1.0 + jnp.exp(-jnp.abs(x)))


def _head_block_diag(x):
    n, w = x.shape
    heads = w // HEAD_DIM
    rows = lax.broadcasted_iota(jnp.int32, (heads * n, w), 0) // n
    cols = lax.broadcasted_iota(jnp.int32, (heads * n, w), 1) // HEAD_DIM
    return jnp.where(rows == cols, jnp.concatenate([x] * heads, axis=0), 0.0)


def _unit_tri_inverses(mats):
    n, w = mats[0].shape
    ri = lax.broadcasted_iota(jnp.int32, (n, w), 0)
    ci = lax.broadcasted_iota(jnp.int32, (n, w), 1) % HEAD_DIM
    b16 = (ri // 16) == (ci // 16)
    b32 = (ri // 32) == (ci // 32)
    eye = jnp.where(ri == ci, 1.0, 0.0)
    ps = [jnp.where(b16, -a, 0.0) for a in mats]
    ts = [eye + p for p in ps]
    for _ in range(3):
        pbs = [p.astype(BF16) for p in ps]
        ps = [_dot(pb, _head_block_diag(pb)) for pb in pbs]
        ts = [t + _dot(t.astype(BF16), _head_block_diag(p.astype(BF16))) for t, p in zip(ts, ps)]
    for keep in (b32 & ~b16, ~b32):
        offs = [jnp.where(keep, a, 0.0).astype(BF16) for a in mats]
        tbs = [t.astype(BF16) for t in ts]
        mids = [_dot(tb, _head_block_diag(off)).astype(BF16) for tb, off in zip(tbs, offs)]
        ts = [t - _dot(mid, _head_block_diag(tb)) for t, mid, tb in zip(ts, mids, tbs)]
    return ts


def _gdn_prep_kernel(lat_ref, ctx_ref, abl_ref, abc_ref, cw_ref, par_ref, q_s, k_s, v_s, gb_s, *, n_ctx, n_lat):
    pt = GDN_PREP
    halo = GDN_HALO
    qkv_w = 3 * GDN_W
    gsum = _group_matrix(GDN_W, 1.0)
    lane128 = lax.broadcasted_iota(jnp.int32, (pt, LANES), 1)
    neg_a = -jnp.exp(par_ref[0:1, :])
    dt_bias = par_ref[1:2, :]
    taps = [cw_ref[j:j + 1, :] for j in range(GDN_CONV)]

    def prep(ref, ab_ref, n_rows, row_off):
        n_tiles = n_rows // pt

        def body(i, carry):
            t0 = pl.multiple_of(i * pt, pt)
            cur = ref[pl.ds(t0, pt), 0:qkv_w].astype(F32)
            p0 = pl.multiple_of(jnp.maximum(t0 - 16, 0), 16)
            n0 = pl.multiple_of(jnp.minimum(t0 + pt, n_rows - 16), 16)
            prev = ref[pl.ds(p0, 16), 0:qkv_w].astype(F32)[16 - halo:16]
            nxt = ref[pl.ds(n0, 16), 0:qkv_w].astype(F32)[0:halo]
            prev = jnp.where(i > 0, prev, 0.0)
            nxt = jnp.where(i < n_tiles - 1, nxt, 0.0)
            ext = jnp.concatenate([prev, cur, nxt], axis=0)
            rows = pt + 2 * halo
            acc = None
            for j in range(GDN_CONV):
                s = j - (GDN_CONV - 1) // 2
                sh = ext if s == 0 else pltpu.roll(ext, (rows - s) % rows, 0)
                term = sh[halo:halo + pt] * taps[j]
                acc = term if acc is None else acc + term
            act = _silu(acc)
            q, k, v = act[:, 0:GDN_W], act[:, GDN_W:2 * GDN_W], act[:, 2 * GDN_W:3 * GDN_W]
            q = q * lax.rsqrt(_split_dot(q * q, gsum) + EPS) * (HEAD_DIM ** -0.5)
            k = k * lax.rsqrt(_split_dot(k * k, gsum) + EPS)
            r0 = pl.multiple_of(row_off + t0, pt)
            q_s[pl.ds(r0, pt), :] = q.astype(BF16)
            k_s[pl.ds(r0, pt), :] = k.astype(BF16)
            v_s[pl.ds(r0, pt), :] = v.astype(BF16)
            ab = ab_ref[pl.ds(t0, pt), :]
            g = neg_a * _softplus(ab + dt_bias)
            gb_s[pl.ds(r0, pt), :] = jnp.where(lane128 < 2 * GDN_HEADS, g, _sigmoid(ab))
            return carry

        lax.fori_loop(0, n_tiles, body, 0)

    prep(ctx_ref, abc_ref, n_ctx, 0)
    prep(lat_ref, abl_ref, n_lat, n_ctx)


def _gdn_kernel(q_s, k_s, v_s, gb_s, zl_ref, zc_ref, ng_ref, ol_ref, oc_ref,
                o_s, dec_s, uf_s, ub_s, wf_s, wb_s, af_s, ab_s, qf_s, qb_s, kf_s, kb_s, sf_s, sb_s, *, n_ctx, n_lat):
    u_s, w_s, a_s, qd_s, kd_s, st_s = (uf_s, ub_s), (wf_s, wb_s), (af_s, ab_s), (qf_s, qb_s), (kf_s, kb_s), (sf_s, sb_s)
    ch = GDN_CHUNK
    pt = GDN_PREP
    gmean = _group_matrix(GDN_W, 1.0 / HEAD_DIM)
    ri = lax.broadcasted_iota(jnp.int32, (ch, ch), 0)
    ci = lax.broadcasted_iota(jnp.int32, (ch, ch), 1)
    tri = tuple(jnp.where(m, 1.0, 0.0).astype(BF16) for m in (ri >= ci, ri <= ci))
    rp = lax.broadcasted_iota(jnp.int32, (ch, GDN_W), 0)
    cp = lax.broadcasted_iota(jnp.int32, (ch, GDN_W), 1) % HEAD_DIM
    incl = (rp >= cp, rp <= cp)
    strict = (rp > cp, rp < cp)
    ncc, nlc = n_ctx // ch, n_lat // ch
    head_blocks = (lax.broadcasted_iota(jnp.int32, (GDN_W, GDN_W), 0) // HEAD_DIM
                   == lax.broadcasted_iota(jnp.int32, (GDN_W, GDN_W), 1) // HEAD_DIM)

    def solve_chunks(cs):
        n_cs = len(cs)
        items = [(j, d) for j in range(n_cs) for d in range(2)]
        rows = [pl.ds(pl.multiple_of(c * ch, ch), ch) for c in cs]
        gbc = [gb_s[r, :] for r in rows]
        gcs = []
        for j in range(n_cs):
            per_dir = []
            for d in range(2):
                rem, acc = gbc[j], None
                for _ in range(3):
                    part = rem.astype(BF16)
                    term = _dot(tri[d], part)
                    acc = term if acc is None else acc + term
                    rem = rem - part.astype(F32)
                per_dir.append(acc)
            gcs.append(per_dir)
        gcs_t = [[g.T for g in per_dir] for per_dir in gcs]
        edge = (ch - 1, 0)

        def spread(mat, first):
            return jnp.concatenate([jnp.broadcast_to(mat[:, first + h:first + h + 1], (ch, HEAD_DIM))
                                    for h in range(GDN_HEADS)], axis=1)

        gc = [spread(gcs[j][d], d * GDN_HEADS) for j, d in items]
        gr = [jnp.concatenate([gcs_t[j][d][d * GDN_HEADS + h:d * GDN_HEADS + h + 1, :] for h in range(GDN_HEADS)],
                              axis=1) for j, d in items]
        gtot = [x[edge[d]:edge[d] + 1, :] for x, (_, d) in zip(gc, items)]
        beta = [spread(gbc[j], 2 * GDN_HEADS + d * GDN_HEADS) for j, d in items]
        decay = [jnp.where(incl[d], jnp.exp(jnp.minimum(gc[i] - gr[i], 0.0)), 0.0) for i, (_, d) in enumerate(items)]
        q = [q_s[r, :] for r in rows]
        k = [k_s[r, :] for r in rows]
        kf = [x.astype(F32) for x in k]
        v = [v_s[r, :].astype(F32) for r in rows]
        k_rows = [_head_block_diag(x) for x in k]
        qk = [_dot_nt(x, y) for x, y in zip(q, k_rows)]
        kb = [kf[j] * beta[i] for i, (j, _) in enumerate(items)]
        kk = [_dot_nt(kb[i].astype(BF16), k_rows[j]) for i, (j, _) in enumerate(items)]
        a = [jnp.where(strict[d], kk[i] * decay[i], 0.0) for i, (_, d) in enumerate(items)]
        t = [x.astype(BF16) for x in _unit_tri_inverses(a)]
        eg = [jnp.exp(x) for x in gc]
        u = [_dot(t[i], _head_block_diag((v[j] * beta[i]).astype(BF16))) for i, (j, _) in enumerate(items)]
        w = [_dot(t[i], _head_block_diag((kb[i] * eg[i]).astype(BF16))) for i in range(len(items))]
        for i, (j, d) in enumerate(items):
            u_s[d][rows[j], :] = u[i].astype(BF16)
            w_s[d][rows[j], :] = w[i].astype(BF16)
            a_s[d][rows[j], :] = (qk[j] * decay[i]).astype(BF16)
            qd_s[d][rows[j], :] = (q[j].astype(F32) * eg[i]).astype(BF16)
            kd_s[d][rows[j], :] = (kf[j] * jnp.exp(gtot[i] - gc[i])).astype(BF16)
            dec_s[pl.ds(cs[j], 1), d * GDN_W:(d + 1) * GDN_W] = jnp.exp(gtot[i])

    def scan_step(cf, cb):
        cs = (cf, cb)
        rows = [pl.ds(pl.multiple_of(c * ch, ch), ch) for c in cs]
        s = [st[...] for st in st_s]
        lhs = [jnp.concatenate([w_s[d][rows[d], :], qd_s[d][rows[d], :]], axis=0) for d in range(2)]
        prod = [_dot(lhs[d], s[d].astype(BF16)) for d in range(2)]
        vb = [(u_s[d][rows[d], :].astype(F32) - prod[d][0:ch]).astype(BF16) for d in range(2)]
        vbd = [jnp.where(head_blocks, jnp.concatenate([x] * GDN_HEADS, axis=0), 0.0) for x in vb]
        av = [_dot(a_s[d][rows[d], :], vbd[d]) for d in range(2)]
        upd = [_dot_tn(kd_s[d][rows[d], :], vb[d]) for d in range(2)]
        for d in range(2):
            dec = dec_s[pl.ds(cs[d], 1), d * GDN_W:(d + 1) * GDN_W]
            st_s[d][...] = s[d] * dec + jnp.where(head_blocks, upd[d], 0.0)
        for d in range(2):
            o_s[rows[d], :] += prod[d][ch:2 * ch] + av[d]

    n_solve = GDN_SOLVE_CHUNKS if (ncc + nlc) % GDN_SOLVE_CHUNKS == 0 else 1

    def solve_body(n, carry):
        solve_chunks([n * n_solve + j for j in range(n_solve)])
        return carry

    lax.fori_loop(0, (ncc + nlc) // n_solve, solve_body, 0)

    for st in st_s:
        st[...] = jnp.zeros_like(st)
    o_s[...] = jnp.zeros_like(o_s)

    def scan_ctx(n, carry):
        scan_step(n, ncc - 1 - n)
        return carry

    def scan_lat(n, carry):
        scan_step(ncc + n, ncc + nlc - 1 - n)
        return carry

    lax.fori_loop(0, ncc, scan_ctx, 0)
    lax.fori_loop(0, nlc, scan_lat, 0)

    ng = ng_ref[...]

    def finish(z_ref, out_ref, n_rows, row_off):
        def body(i, carry):
            t0 = pl.multiple_of(i * pt, pt)
            o = o_s[pl.ds(pl.multiple_of(row_off + t0, pt), pt), :]
            z = z_ref[pl.ds(t0, pt), :].astype(F32)
            y = o * lax.rsqrt(_split_dot(o * o, gmean) + EPS) * ng * _silu(z)
            out_ref[pl.ds(t0, pt), :] = y.astype(out_ref.dtype)
            return carry

        lax.fori_loop(0, n_rows // pt, body, 0)

    finish(zc_ref, oc_ref, n_ctx, 0)
    finish(zl_ref, ol_ref, n_lat, n_ctx)


def gdn_mixer(p_gdn, p_ab, conv_w, a_log, dt_bias, norm_gain, *, n_batch, n_lat, n_ctx):
    width = p_gdn.shape[1]
    ctx_blk0 = (n_batch * n_lat) // n_ctx
    t_all = n_lat + n_ctx
    cw = jnp.pad(conv_w.astype(F32), ((0, 8 - GDN_CONV), (0, 0)))
    par = jnp.zeros((8, LANES), F32)
    par = par.at[0, :2 * GDN_HEADS].set(a_log.reshape(-1)).at[1, :2 * GDN_HEADS].set(dt_bias.reshape(-1))
    ng = jnp.tile(norm_gain.astype(F32).reshape(1, HEAD_DIM), (1, GDN_HEADS))
    seq = lambda w: pl.BlockSpec((t_all, w), lambda b: (b, 0))
    q, k, v, gb = pl.pallas_call(
        functools.partial(_gdn_prep_kernel, n_ctx=n_ctx, n_lat=n_lat),
        out_shape=(jax.ShapeDtypeStruct((n_batch * t_all, GDN_W), BF16),) * 3
        + (jax.ShapeDtypeStruct((n_batch * t_all, LANES), F32),),
        grid=(n_batch,),
        in_specs=[pl.BlockSpec((n_lat, width), lambda b: (b, 0)),
                  pl.BlockSpec((n_ctx, width), lambda b: (ctx_blk0 + b, 0)),
                  pl.BlockSpec((n_lat, LANES), lambda b: (b, 0)),
                  pl.BlockSpec((n_ctx, LANES), lambda b: (ctx_blk0 + b, 0)),
                  pl.BlockSpec(cw.shape, lambda b: (0, 0)),
                  pl.BlockSpec(par.shape, lambda b: (0, 0))],
        out_specs=(seq(GDN_W), seq(GDN_W), seq(GDN_W), seq(LANES)),
        compiler_params=_cparams("parallel"),
        name="gdn_prep",
    )(p_gdn, p_gdn, p_ab, p_ab, cw, par)
    zcol = 3 * GDN_W // GDN_W
    once = lambda w: pl.BlockSpec((t_all, w), lambda b: (b, 0), pipeline_mode=pl.Buffered(1))
    return pl.pallas_call(
        functools.partial(_gdn_kernel, n_ctx=n_ctx, n_lat=n_lat),
        out_shape=(jax.ShapeDtypeStruct((n_batch * n_lat, GDN_W), BF16),
                   jax.ShapeDtypeStruct((n_batch * n_ctx, GDN_W), BF16)),
        grid=(n_batch,),
        in_specs=[once(GDN_W), once(GDN_W), once(GDN_W), once(LANES),
                  pl.BlockSpec((n_lat, GDN_W), lambda b: (b, zcol), pipeline_mode=pl.Buffered(1)),
                  pl.BlockSpec((n_ctx, GDN_W), lambda b: (ctx_blk0 + b, zcol)),
                  pl.BlockSpec((1, GDN_W), lambda b: (0, 0))],
        out_specs=(pl.BlockSpec((n_lat, GDN_W), lambda b: (b, 0)),
                   pl.BlockSpec((n_ctx, GDN_W), lambda b: (b, 0))),
        scratch_shapes=[pltpu.VMEM((t_all, GDN_W), F32), pltpu.VMEM((t_all // GDN_CHUNK, 2 * GDN_W), F32)]
        + [pltpu.VMEM((t_all, GDN_W), BF16)] * 10
        + [pltpu.VMEM((GDN_W, GDN_W), F32)] * 2,
        compiler_params=_cparams("parallel"),
        name="gdn",
    )(q, k, v, gb, p_gdn, p_gdn, ng)


SWA_BLOCK = 128
SWA_BAND = 3 * SWA_BLOCK
SWA_Q_PER_STEP = 4


def _swa_kernel(sink_ref, q_ref, kl_ref, vl_ref, kc_ref, vc_ref, cos_ref, sin_ref, o_ref, *, band, n_lat, n_sub):
    blk = SWA_BLOCK
    kc = kc_ref[...]
    vc = vc_ref[...]
    for sub in range(n_sub):
        rows = slice(sub * blk, (sub + 1) * blk)
        o_ref[rows, :] = _swa_block(pl.program_id(1) * n_sub + sub, q_ref[rows, :].astype(F32), sink_ref, kl_ref,
                                    vl_ref, kc, vc, cos_ref, sin_ref, band=band, n_lat=n_lat).astype(o_ref.dtype)


def _swa_block(i, q, sink_ref, kl_ref, vl_ref, kc, vc, cos_ref, sin_ref, *, band, n_lat):
    blk = SWA_BLOCK
    if band:
        q0 = pl.multiple_of(i * blk, blk)
        start = pl.multiple_of(jnp.clip((i - 1) * blk, 0, n_lat - SWA_BAND), blk)
        q = _rope(q, cos_ref[pl.ds(q0, blk), :], sin_ref[pl.ds(q0, blk), :])
        kb = _rope(kl_ref[pl.ds(start, SWA_BAND), :].astype(F32),
                   cos_ref[pl.ds(start, SWA_BAND), :], sin_ref[pl.ds(start, SWA_BAND), :]).astype(BF16)
        vb = vl_ref[pl.ds(start, SWA_BAND), :]
        keys = jnp.concatenate([kb, kc], axis=0)
        vals = jnp.concatenate([vb, vc], axis=0)
    else:
        keys, vals = kc, vc
    qb = (q * (HEAD_DIM ** -0.5 * LOG2E)).astype(BF16)
    n_keys = keys.shape[0]
    if band:
        q_pos = q0 + lax.broadcasted_iota(jnp.int32, (blk, 1), 0)
        col_id = lax.broadcasted_iota(jnp.int32, (1, n_keys), 1)
        valid = (col_id >= SWA_BAND) | (jnp.abs(start + col_id - q_pos) <= WINDOW)
        bias = jnp.where(valid, 0.0, NEG_BIG)
    v_lane = lax.broadcasted_iota(jnp.int32, vals.shape, 1) // HEAD_DIM
    outs = []
    for g in range(SWA_KV_HEADS):
        ksl = slice(g * HEAD_DIM, (g + 1) * HEAD_DIM)
        v_ext = jnp.where(v_lane == g, vals, 1.0)
        den_lane = (1 - g) * HEAD_DIM
        q4 = jnp.concatenate([qb[:, (g * SWA_GROUP + r) * HEAD_DIM:(g * SWA_GROUP + r + 1) * HEAD_DIM]
                              for r in range(SWA_GROUP)], axis=0)
        s = _dot_nt(q4, keys[:, ksl])
        es, ms, sinks = [], [], []
        for r in range(SWA_GROUP):
            sr = s[r * blk:(r + 1) * blk]
            if band:
                sr = sr + bias
            sink2 = sink_ref[g * SWA_GROUP + r] * LOG2E
            m = jnp.maximum(jnp.max(sr, axis=-1, keepdims=True), sink2)
            es.append(jnp.exp2(sr - m).astype(BF16))
            ms.append(m)
            sinks.append(sink2)
        ov = _dot(jnp.concatenate(es, axis=0), v_ext)
        for r in range(SWA_GROUP):
            ovr = ov[r * blk:(r + 1) * blk]
            den = ovr[:, den_lane:den_lane + 1] + jnp.exp2(sinks[r] - ms[r])
            outs.append(ovr[:, ksl] * (1.0 / den))
    return jnp.concatenate(outs, axis=1)


def swa_mixer(p_swa, cos, sin, sink, *, n_batch, n_lat, n_ctx):
    blk = SWA_BLOCK
    ctx_row0 = n_batch * n_lat
    kcol, vcol = SWA_W // SWA_KV_W, SWA_W // SWA_KV_W + 1
    sink = sink.astype(F32)

    def call(band):
        n_q = n_lat if band else n_ctx
        n_sub = min(SWA_Q_PER_STEP, n_q // blk)
        qblk = n_sub * blk
        nb = n_q // qblk
        qrow0 = 0 if band else ctx_row0 // qblk
        grid_spec = pltpu.PrefetchScalarGridSpec(
            num_scalar_prefetch=1,
            grid=(n_batch, nb),
            in_specs=[pl.BlockSpec((qblk, SWA_W), lambda b, i, s: (qrow0 + b * nb + i, 0)),
                      pl.BlockSpec((n_lat, SWA_KV_W), lambda b, i, s: (b, kcol)),
                      pl.BlockSpec((n_lat, SWA_KV_W), lambda b, i, s: (b, vcol)),
                      pl.BlockSpec((n_ctx, SWA_KV_W), lambda b, i, s: (ctx_row0 // n_ctx + b, kcol)),
                      pl.BlockSpec((n_ctx, SWA_KV_W), lambda b, i, s: (ctx_row0 // n_ctx + b, vcol)),
                      pl.BlockSpec((n_lat, LANES), lambda b, i, s: (0, 0)),
                      pl.BlockSpec((n_lat, LANES), lambda b, i, s: (0, 0))],
            out_specs=pl.BlockSpec((qblk, SWA_W), lambda b, i, s: (b * nb + i, 0)),
        )
        return pl.pallas_call(
            functools.partial(_swa_kernel, band=band, n_lat=n_lat, n_sub=n_sub),
            out_shape=jax.ShapeDtypeStruct((n_batch * n_q, SWA_W), BF16),
            grid_spec=grid_spec,
            compiler_params=_cparams("parallel", "arbitrary"),
            name="swa_latent" if band else "swa_context",
        )(sink, p_swa, p_swa, p_swa, p_swa, p_swa, cos, sin)

    return call(True), call(False)


MOE_ROW_TILE = 512
MOE_FF_TILE = 1792
MOE_DOWN_TILE = 1024


def _row_tile(n_lat, n_ctx_rows):
    for tm in (512, 256, 128):
        if n_lat % tm == 0 and n_ctx_rows % tm == 0:
            return tm
    raise ValueError("sequence lengths must be multiples of 128")


def kernel(x, c, ctx, c_ctx, ada_w, ada_b, norm1_g, w_in, ret_decay, ret_gn_g, gdn_conv_w, gdn_a_log, gdn_dt_bias,
           gdn_norm_g, swa_sink, w_out, norm2_g, ffn_w1, ffn_w3, ffn_w2, router_w, moe_w1, moe_w3, moe_w2, final_g):
    b, s, d = x.shape
    n_ctx = ctx.shape[1]
    depth = ada_w.shape[0]
    nl, ncx = b * s, b * n_ctx
    tm = _row_tile(s, ncx)
    seg = _seg_map(nl // tm, s // tm, b)

    cvec = jnp.zeros((8, d), F32).at[:b].set(c).at[b].set(c_ctx)
    mod_all = adaln(cvec, ada_w, ada_b).reshape(depth, 8, 6, d)
    xall = jnp.concatenate([x.reshape(nl, d), ctx.reshape(ncx, d)], axis=0)
    cos, sin = rope_tables(s)
    fg = final_g.reshape(1, d)

    ret_cols, gdn_cols = 4 * RET_W, 4 * GDN_W
    ab_cols = 4 * GDN_HEADS
    for layer in range(depth):
        last = layer == depth - 1
        mod = mod_all[layer]
        w = w_in[layer].astype(BF16)
        w_ret = w[:, :ret_cols]
        w_gdn = w[:, ret_cols:ret_cols + gdn_cols]
        w_ab = jnp.pad(w[:, ret_cols + gdn_cols:ret_cols + gdn_cols + ab_cols], ((0, 0), (0, LANES - ab_cols)))
        w_swa = w[:, ret_cols + gdn_cols + ab_cols:]
        p_ret, p_gdn, p_swa, p_ab = in_proj(xall, mod, norm1_g[layer].reshape(1, d), w_ret, w_gdn, w_swa, w_ab,
                                            tm=tm, seg=seg)
        dec_lanes = jnp.repeat(ret_decay[layer].astype(F32), HEAD_DIM, axis=1)
        o_ret = retention_mixer(p_ret, cos, sin, dec_lanes, ret_gn_g[layer].reshape(1, RET_W),
                                n_batch=b, n_lat=s, n_ctx=n_ctx)
        o_gdn = gdn_mixer(p_gdn, p_ab, gdn_conv_w[layer], gdn_a_log[layer], gdn_dt_bias[layer], gdn_norm_g[layer],
                          n_batch=b, n_lat=s, n_ctx=n_ctx)
        o_swa = swa_mixer(p_swa, cos, sin, swa_sink[layer], n_batch=b, n_lat=s, n_ctx=n_ctx)

        n_rows = nl if last else nl + ncx
        is_moe = layer % 2 == 1
        i = layer // 2
        rw = None
        if is_moe:
            rw = jnp.pad(router_w[i].astype(F32), ((0, 0), (0, LANES - N_EXPERTS)))
        res = out_proj(o_ret, o_gdn, o_swa, xall, mod, w_out[layer].astype(BF16), norm2_g[layer].reshape(1, d), rw,
                       tm=tm, seg=seg, n_rows=n_rows, n_lat_rows=nl, tok_dtype=F32 if is_moe else BF16)
        if is_moe:
            x_new, tokens, ridx, rgate, cnt = res
            xall = moe_layer(tokens, ridx, rgate, cnt, x_new, mod, moe_w1, moe_w3, moe_w2, fg,
                             li=i, tm=tm, tmx=MOE_ROW_TILE, tf=MOE_FF_TILE, seg=seg, last=last)
        else:
            x_new, tokens = res
            xall = dense_ffn(tokens, x_new, mod, ffn_w1[i].astype(BF16), ffn_w3[i].astype(BF16),
                             ffn_w2[i].astype(BF16), fg, tm=tm, seg=seg, last=last)
    return xall[:nl].reshape(b, s, d)
```

```python
import functools

import jax
import jax.numpy as jnp
import numpy as np
from jax import lax
from jax.experimental import pallas as pl
from jax.experimental.pallas import tpu as pltpu

F32 = jnp.float32
BF16 = jnp.bfloat16

HEAD_DIM = 64
GRID_W = 64
ROPE_BASE = 10000.0
EPS = 1e-6
RET_HEADS = 4
GDN_HEADS = 4
GDN_CONV = 5
SWA_HEADS = 8
SWA_KV_HEADS = 2
SWA_GROUP = SWA_HEADS // SWA_KV_HEADS
WINDOW = 128
N_EXPERTS = 8
RET_W = RET_HEADS * HEAD_DIM
GDN_W = GDN_HEADS * HEAD_DIM
SWA_W = SWA_HEADS * HEAD_DIM
SWA_KV_W = SWA_KV_HEADS * HEAD_DIM
LANES = 128
VMEM_LIMIT = 56 * 1024 * 1024
NEG_BIG = -1e30
LOG2E = 1.4426950408889634


def _cparams(*sem):
    return pltpu.CompilerParams(dimension_semantics=sem, vmem_limit_bytes=VMEM_LIMIT)


def _dot(a, b):
    return jnp.dot(a, b, preferred_element_type=F32)


def _dot_nt(a, b):
    return lax.dot_general(a, b, (((1,), (1,)), ((), ())), preferred_element_type=F32)


def _dot_tn(a, b):
    return lax.dot_general(a, b, (((0,), (0,)), ((), ())), preferred_element_type=F32)


def _split_dot(x, m_bf16, terms=2):
    acc = None
    rem = x
    for _ in range(terms):
        part = rem.astype(BF16)
        d = _dot(part, m_bf16)
        acc = d if acc is None else acc + d
        rem = rem - part.astype(F32)
    return acc


def _sigmoid(x):
    return 1.0 / (1.0 + jnp.exp(-x))


def _silu(x):
    return x * _sigmoid(x)


def _rms_mod(x, gain, shift, scale):
    ms = jnp.mean(x * x, axis=-1, keepdims=True)
    y = x * lax.rsqrt(ms + EPS) * gain
    return y * (1.0 + scale) + shift


def _seg_map(n_lat_tiles, tiles_per_batch, n_batch):
    def seg(i):
        return jnp.where(i < n_lat_tiles, i // tiles_per_batch, n_batch)
    return seg


def _adaln_kernel(c_ref, w_ref, b_ref, o_ref):
    h = _silu(c_ref[...]).astype(BF16)
    o_ref[0] = _dot(h, w_ref[0].astype(BF16)) + b_ref[0]


def adaln(cvec, ada_w, ada_b):
    depth, d, n6 = ada_w.shape
    tn = 1536 if n6 % 1536 == 0 else n6
    rows = cvec.shape[0]
    return pl.pallas_call(
        _adaln_kernel,
        out_shape=jax.ShapeDtypeStruct((depth, rows, n6), F32),
        grid=(depth, n6 // tn),
        in_specs=[pl.BlockSpec((rows, d), lambda l, j: (0, 0)),
                  pl.BlockSpec((1, d, tn), lambda l, j: (l, 0, j)),
                  pl.BlockSpec((1, 1, tn), lambda l, j: (l, 0, j))],
        out_specs=pl.BlockSpec((1, rows, tn), lambda l, j: (l, 0, j)),
        compiler_params=_cparams("parallel", "parallel"),
        name="adaln",
    )(cvec, ada_w, ada_b.reshape(depth, 1, n6))


def _inproj_kernel(x_ref, mod_ref, g_ref, wr_ref, wg_ref, ws_ref, wab_ref, pr_ref, pg_ref, ps_ref, pab_ref):
    h = _rms_mod(x_ref[...], g_ref[...], mod_ref[0, 0:1, :], mod_ref[0, 1:2, :]).astype(BF16)
    for w_ref, o_ref in ((wr_ref, pr_ref), (wg_ref, pg_ref), (ws_ref, ps_ref), (wab_ref, pab_ref)):
        n = w_ref.shape[1]
        step = 256 if n % 256 == 0 else n
        for c in range(0, n, step):
            o_ref[:, c:c + step] = _dot(h, w_ref[:, c:c + step]).astype(o_ref.dtype)


def in_proj(xall, mod, gain, w_ret, w_gdn, w_swa, w_ab, *, tm, seg):
    n, d = xall.shape
    full = lambda i: (0, 0)
    row = lambda i: (i, 0)
    outs = (jax.ShapeDtypeStruct((n, w_ret.shape[1]), BF16), jax.ShapeDtypeStruct((n, w_gdn.shape[1]), BF16),
            jax.ShapeDtypeStruct((n, w_swa.shape[1]), BF16), jax.ShapeDtypeStruct((n, w_ab.shape[1]), F32))
    return pl.pallas_call(
        _inproj_kernel,
        out_shape=outs,
        grid=(n // tm,),
        in_specs=[pl.BlockSpec((tm, d), row),
                  pl.BlockSpec((1, 6, d), lambda i: (seg(i), 0, 0)),
                  pl.BlockSpec((1, d), full),
                  pl.BlockSpec(w_ret.shape, full), pl.BlockSpec(w_gdn.shape, full),
                  pl.BlockSpec(w_swa.shape, full), pl.BlockSpec(w_ab.shape, full)],
        out_specs=tuple(pl.BlockSpec((tm, o.shape[1]), row) for o in outs),
        compiler_params=_cparams("parallel"),
        name="in_proj",
    )(xall, mod, gain, w_ret, w_gdn, w_swa, w_ab)


def _outproj_kernel(*refs, tm, moe, n_lat_tiles, has_ctx):
    n_mix = 6 if has_ctx else 3
    mix = refs[:n_mix]
    refs = refs[n_mix:]
    if moe:
        (x_ref, mod_ref, w_ref, g_ref, rw_ref, xo_ref, tok_ref, ridx_ref, rgate_ref, cnt_ref, base_ref) = refs
    else:
        x_ref, mod_ref, w_ref, g_ref, xo_ref, tok_ref = refs
    if has_ctx:
        is_lat = pl.program_id(0) < n_lat_tiles
        o_ret, o_gdn, o_swa = (jnp.where(is_lat, mix[2 * k][...], mix[2 * k + 1][...]) for k in range(3))
    else:
        o_ret, o_gdn, o_swa = (m[...] for m in mix)
    y = _dot(o_ret, w_ref[0:RET_W, :])
    y = y + _dot(o_gdn, w_ref[RET_W:RET_W + GDN_W, :])
    y = y + _dot(o_swa, w_ref[RET_W + GDN_W:, :])
    x = x_ref[...] + mod_ref[0, 2:3, :] * y
    xo_ref[...] = x
    t = _rms_mod(x, g_ref[...], mod_ref[0, 3:4, :], mod_ref[0, 4:5, :])
    tok_ref[...] = t.astype(tok_ref.dtype)
    if not moe:
        return

    @pl.when(pl.program_id(0) == 0)
    def _():
        base_ref[...] = jnp.zeros_like(base_ref)

    rw = rw_ref[...]
    rw_hi = rw.astype(BF16)
    rw_lo = (rw - rw_hi.astype(F32)).astype(BF16)
    t_hi = t.astype(BF16)
    t_lo = (t - t_hi.astype(F32)).astype(BF16)
    logits = _dot(t_hi, rw_hi) + _dot(t_hi, rw_lo) + _dot(t_lo, rw_hi)
    lt = logits.T[0:N_EXPERTS, :]
    e_id = lax.broadcasted_iota(jnp.int32, lt.shape, 0)
    m1 = jnp.max(lt, axis=0, keepdims=True)
    i1 = jnp.min(jnp.where(lt == m1, e_id, N_EXPERTS), axis=0, keepdims=True)
    rest = jnp.where(e_id == i1, NEG_BIG, lt)
    m2 = jnp.max(rest, axis=0, keepdims=True)
    i2 = jnp.min(jnp.where(rest == m2, e_id, N_EXPERTS), axis=0, keepdims=True)
    e2 = jnp.exp(m2 - m1)
    g1 = 1.0 / (1.0 + e2)
    g2 = e2 * g1
    oh1 = jnp.where(e_id == i1, 1.0, 0.0)
    oh2 = jnp.where(e_id == i2, 1.0, 0.0)
    r_i = lax.broadcasted_iota(jnp.int32, (tm, tm), 0)
    c_i = lax.broadcasted_iota(jnp.int32, (tm, tm), 1)
    earlier = jnp.where(r_i < c_i, 1.0, 0.0).astype(BF16)
    cum = _dot(jnp.concatenate([oh1, oh2], axis=0).astype(BF16), earlier)
    cum1, cum2 = cum[0:N_EXPERTS], cum[N_EXPERTS:2 * N_EXPERTS]
    cnt1 = jnp.sum(oh1, axis=1, keepdims=True)
    cnt2 = jnp.sum(oh2, axis=1, keepdims=True)
    base = base_ref[:, 0:1]
    rank1 = jnp.sum(oh1 * (base + cum1), axis=0, keepdims=True)
    rank2 = jnp.sum(oh2 * (base + cnt1 + cum2), axis=0, keepdims=True)
    base = base + cnt1 + cnt2
    base_ref[...] = jnp.broadcast_to(base, base_ref.shape)
    cnt_ref[...] = jnp.broadcast_to(base, cnt_ref.shape).astype(jnp.int32)
    row8 = lax.broadcasted_iota(jnp.int32, (8, tm), 0)
    r1 = rank1.astype(jnp.int32)
    r2 = rank2.astype(jnp.int32)
    ridx_ref[...] = jnp.where(row8 == 0, i1, jnp.where(row8 == 1, i2, jnp.where(row8 == 2, r1,
                              jnp.where(row8 == 3, r2, 0))))
    rgate_ref[...] = jnp.where(row8 == 0, g1, jnp.where(row8 == 1, g2, 0.0))


def out_proj(o_ret, o_gdn, o_swa, xall, mod, w_out, gain2, router_w, *, tm, seg, n_rows, n_lat_rows, tok_dtype):
    d = xall.shape[1]
    moe = router_w is not None
    has_ctx = n_rows > n_lat_rows
    n_lat_tiles = n_lat_rows // tm
    full = lambda i: (0, 0)
    row = lambda i: (i, 0)
    lat_row = lambda i: (jnp.minimum(i, n_lat_tiles - 1), 0)
    ctx_row = lambda i: (jnp.maximum(i - n_lat_tiles, 0), 0)
    in_specs, args = [], []
    for pair, width in ((o_ret, RET_W), (o_gdn, GDN_W), (o_swa, SWA_W)):
        in_specs.append(pl.BlockSpec((tm, width), lat_row))
        args.append(pair[0])
        if has_ctx:
            in_specs.append(pl.BlockSpec((tm, width), ctx_row))
            args.append(pair[1])
    in_specs += [pl.BlockSpec((tm, d), row), pl.BlockSpec((1, 6, d), lambda i: (seg(i), 0, 0)),
                 pl.BlockSpec(w_out.shape, full), pl.BlockSpec((1, d), full)]
    args += [xall, mod, w_out, gain2]
    outs = [jax.ShapeDtypeStruct((n_rows, d), F32), jax.ShapeDtypeStruct((n_rows, d), tok_dtype)]
    out_specs = [pl.BlockSpec((tm, d), row), pl.BlockSpec((tm, d), row)]
    scratch = []
    if moe:
        in_specs.append(pl.BlockSpec(router_w.shape, full))
        args.append(router_w)
        col = lambda i: (0, i)
        outs += [jax.ShapeDtypeStruct((8, n_rows), jnp.int32), jax.ShapeDtypeStruct((8, n_rows), F32),
                 jax.ShapeDtypeStruct((N_EXPERTS, LANES), jnp.int32)]
        out_specs += [pl.BlockSpec((8, tm), col), pl.BlockSpec((8, tm), col), pl.BlockSpec((N_EXPERTS, LANES), full)]
        scratch = [pltpu.VMEM((N_EXPERTS, LANES), F32)]
    return pl.pallas_call(
        functools.partial(_outproj_kernel, tm=tm, moe=moe, n_lat_tiles=n_lat_tiles, has_ctx=has_ctx),
        out_shape=tuple(outs),
        grid=(n_rows // tm,),
        in_specs=in_specs,
        out_specs=tuple(out_specs),
        scratch_shapes=scratch,
        compiler_params=_cparams("arbitrary"),
        name="out_proj_router" if moe else "out_proj",
    )(*args)


def _final_norm(x, gain):
    ms = jnp.mean(x * x, axis=-1, keepdims=True)
    return x * lax.rsqrt(ms + EPS) * gain


def _dense_ffn_kernel(tok_ref, x_ref, mod_ref, w1_ref, w3_ref, w2_ref, fg_ref, o_ref, *, last):
    t = tok_ref[...]
    h = (_silu(_dot(t, w1_ref[...])) * _dot(t, w3_ref[...])).astype(BF16)
    x = x_ref[...] + mod_ref[0, 5:6, :] * _dot(h, w2_ref[...])
    if last:
        x = _final_norm(x, fg_ref[...])
    o_ref[...] = x


def dense_ffn(tokens, xall, mod, w1, w3, w2, final_g, *, tm, seg, last):
    n, d = tokens.shape
    f = w1.shape[1]
    full = lambda i: (0, 0)
    row = lambda i: (i, 0)
    return pl.pallas_call(
        functools.partial(_dense_ffn_kernel, last=last),
        out_shape=jax.ShapeDtypeStruct((n, d), F32),
        grid=(n // tm,),
        in_specs=[pl.BlockSpec((tm, d), row), pl.BlockSpec((tm, d), row),
                  pl.BlockSpec((1, 6, d), lambda i: (seg(i), 0, 0)),
                  pl.BlockSpec((d, f), full, pipeline_mode=pl.Buffered(1)),
                  pl.BlockSpec((d, f), full, pipeline_mode=pl.Buffered(1)),
                  pl.BlockSpec((f, d), full, pipeline_mode=pl.Buffered(1)),
                  pl.BlockSpec((1, d), full)],
        out_specs=pl.BlockSpec((tm, d), row),
        compiler_params=_cparams("parallel"),
        name="dense_ffn",
    )(tokens, xall, mod, w1, w3, w2, final_g)


def _dispatch_kernel(pad_ref, dest_ref, tok_ref, xs_ref, zero_s, sem, zsem, *, tm, tmx):
    @pl.when(pl.program_id(0) == 0)
    def _():
        zero_s[...] = jnp.zeros_like(zero_s)

        def for_each_zero_copy(fn):
            for e in range(N_EXPERTS):
                @pl.when(pad_ref[N_EXPERTS + e] > 0)
                def _(e=e):
                    row = pl.multiple_of(pad_ref[e], tmx)
                    fn(pltpu.make_async_copy(zero_s, xs_ref.at[pl.ds(row, tmx)], zsem))
            for t in range(N_EXPERTS):
                @pl.when(t < pad_ref[2 * N_EXPERTS + 1])
                def _(t=t):
                    row = pl.multiple_of(pad_ref[2 * N_EXPERTS] + t * tmx, tmx)
                    fn(pltpu.make_async_copy(zero_s, xs_ref.at[pl.ds(row, tmx)], zsem))

        for_each_zero_copy(lambda c: c.start())
        for_each_zero_copy(lambda c: c.wait())

    def issue(r, carry):
        for k in range(2):
            dst = dest_ref[0, 0, 2 * r + k]
            pltpu.make_async_copy(tok_ref.at[pl.ds(r, 1)], xs_ref.at[pl.ds(dst, 1)], sem).start()
        return carry

    lax.fori_loop(0, tm, issue, 0, unroll=8)
    for _ in range(2):
        pltpu.make_async_copy(tok_ref, xs_ref.at[pl.ds(0, tm)], sem).wait()


def moe_dispatch(tokens, dest3, pad_info, n_slots, *, tm, tmx):
    n, d = tokens.shape
    grid_spec = pltpu.PrefetchScalarGridSpec(
        num_scalar_prefetch=1,
        grid=(n // tm,),
        in_specs=[pl.BlockSpec((1, 1, 2 * tm), lambda i, pad: (i, 0, 0), memory_space=pltpu.SMEM),
                  pl.BlockSpec((tm, d), lambda i, pad: (i, 0))],
        out_specs=pl.BlockSpec(memory_space=pl.ANY),
        scratch_shapes=[pltpu.VMEM((tmx, d), tokens.dtype), pltpu.SemaphoreType.DMA, pltpu.SemaphoreType.DMA],
    )
    return pl.pallas_call(
        functools.partial(_dispatch_kernel, tm=tm, tmx=tmx),
        out_shape=jax.ShapeDtypeStruct((n_slots, d), tokens.dtype),
        grid_spec=grid_spec,
        compiler_params=_cparams("arbitrary"),
        name="moe_dispatch",
    )(pad_info, dest3, tokens)


def _tile_expert(te_ref, nu_ref, i):
    return te_ref[jnp.clip(i, 0, nu_ref[0] - 1)]


def _moe_up_kernel(te_ref, nu_ref, xs_ref, w1_ref, w3_ref, h_ref, w1b_s, w3b_s):
    i = pl.program_id(1)

    @pl.when((i == 0) | (_tile_expert(te_ref, nu_ref, i) != _tile_expert(te_ref, nu_ref, i - 1)))
    def _():
        w1b_s[...] = w1_ref[0, 0].astype(BF16)
        w3b_s[...] = w3_ref[0, 0].astype(BF16)

    @pl.when(i < nu_ref[0])
    def _():
        x = xs_ref[...].astype(BF16)
        h_ref[...] = (_silu(_dot(x, w1b_s[...])) * _dot(x, w3b_s[...])).astype(h_ref.dtype)

    @pl.when(i >= nu_ref[0])
    def _():
        h_ref[...] = jnp.zeros_like(h_ref)


def _moe_down_kernel(te_ref, nu_ref, h_ref, w2_ref, y_ref, w2b_s):
    i = pl.program_id(1)

    @pl.when((i == 0) | (_tile_expert(te_ref, nu_ref, i) != _tile_expert(te_ref, nu_ref, i - 1)))
    def _():
        w2b_s[...] = w2_ref[0, 0].astype(BF16)

    @pl.when(i < nu_ref[0])
    def _():
        y_ref[...] = _dot(h_ref[...], w2b_s[...])

    @pl.when(i >= nu_ref[0])
    def _():
        y_ref[...] = jnp.zeros_like(y_ref)


def moe_ffn(xs, tile_expert, n_used, w1, w3, w2, *, li, tm, tf, tn):
    r, d = xs.shape
    f = w1.shape[3]

    def tile(i, nu):
        return jnp.minimum(i, nu[0] - 1)

    up_spec = pltpu.PrefetchScalarGridSpec(
        num_scalar_prefetch=2,
        grid=(f // tf, r // tm),
        in_specs=[pl.BlockSpec((tm, d), lambda j, i, te, nu: (tile(i, nu), 0)),
                  pl.BlockSpec((1, 1, d, tf), lambda j, i, te, nu: (li, te[tile(i, nu)], 0, j)),
                  pl.BlockSpec((1, 1, d, tf), lambda j, i, te, nu: (li, te[tile(i, nu)], 0, j))],
        out_specs=pl.BlockSpec((tm, tf), lambda j, i, te, nu: (i, j)),
        scratch_shapes=[pltpu.VMEM((d, tf), BF16), pltpu.VMEM((d, tf), BF16)],
    )
    hidden = pl.pallas_call(
        _moe_up_kernel,
        out_shape=jax.ShapeDtypeStruct((r, f), BF16),
        grid_spec=up_spec,
        compiler_params=_cparams("arbitrary", "arbitrary"),
        name="moe_up",
    )(tile_expert, n_used, xs, w1, w3)
    down_spec = pltpu.PrefetchScalarGridSpec(
        num_scalar_prefetch=2,
        grid=(d // tn, r // tm),
        in_specs=[pl.BlockSpec((tm, f), lambda n, i, te, nu: (tile(i, nu), 0)),
                  pl.BlockSpec((1, 1, f, tn), lambda n, i, te, nu: (li, te[tile(i, nu)], 0, n))],
        out_specs=pl.BlockSpec((tm, tn), lambda n, i, te, nu: (i, n)),
        scratch_shapes=[pltpu.VMEM((f, tn), BF16)],
    )
    return pl.pallas_call(
        _moe_down_kernel,
        out_shape=jax.ShapeDtypeStruct((r, d), F32),
        grid_spec=down_spec,
        compiler_params=_cparams("arbitrary", "arbitrary"),
        name="moe_down",
    )(tile_expert, n_used, hidden, w2)


def _combine_kernel(dest_ref, gate_ref, x_ref, mod_ref, fg_ref, ys_ref, o_ref, buf_ref, sem, *, tm, last):
    def issue(r, carry):
        for k in range(2):
            src = dest_ref[0, 0, 2 * r + k]
            pltpu.make_async_copy(ys_ref.at[pl.ds(src, 1)], buf_ref.at[k, pl.ds(r, 1)], sem).start()
        return carry

    lax.fori_loop(0, tm, issue, 0, unroll=8)
    for k in range(2):
        pltpu.make_async_copy(ys_ref.at[pl.ds(0, tm)], buf_ref.at[k], sem).wait()
    gate = gate_ref[...]
    y = gate[:, 0:1] * buf_ref[0] + gate[:, 1:2] * buf_ref[1]
    x = x_ref[...] + mod_ref[0, 5:6, :] * y
    if last:
        x = _final_norm(x, fg_ref[...])
    o_ref[...] = x


def moe_combine(ys, dest3, rgate, xall, mod, final_g, *, tm, seg, last):
    n, d = xall.shape
    return pl.pallas_call(
        functools.partial(_combine_kernel, tm=tm, last=last),
        out_shape=jax.ShapeDtypeStruct((n, d), F32),
        grid=(n // tm,),
        in_specs=[pl.BlockSpec((1, 1, 2 * tm), lambda i: (i, 0, 0), memory_space=pltpu.SMEM),
                  pl.BlockSpec((tm, 8), lambda i: (i, 0)),
                  pl.BlockSpec((tm, d), lambda i: (i, 0)),
                  pl.BlockSpec((1, 6, d), lambda i: (seg(i), 0, 0)),
                  pl.BlockSpec((1, d), lambda i: (0, 0)),
                  pl.BlockSpec(memory_space=pl.ANY)],
        out_specs=pl.BlockSpec((tm, d), lambda i: (i, 0)),
        scratch_shapes=[pltpu.VMEM((2, tm, d), F32), pltpu.SemaphoreType.DMA],
        compiler_params=_cparams("arbitrary"),
        name="moe_combine",
    )(dest3, rgate, xall, mod, final_g, ys)


def moe_layer(tokens, ridx, rgate, cnt, xall, mod, w1, w3, w2, final_g, *, li, tm, tmx, tf, seg, last):
    n = tokens.shape[0]
    counts = cnt[:, 0]
    padded = (counts + tmx - 1) // tmx * tmx
    pend = jnp.cumsum(padded)
    pstart = pend - padded
    dest = (pstart[ridx[0:2]] + ridx[2:4]).T
    dest3 = dest.reshape(n // tm, 1, 2 * tm).astype(jnp.int32)
    rgate = rgate.T
    n_tiles = (2 * n) // tmx + N_EXPERTS
    n_used = (pend[-1] // tmx).astype(jnp.int32).reshape(1)
    tile_start = jnp.arange(n_tiles, dtype=jnp.int32) * tmx
    tile_expert = jnp.minimum(jnp.sum(tile_start[:, None] >= pend[None, :], axis=1), N_EXPERTS - 1).astype(jnp.int32)
    pad_info = jnp.concatenate([pend - tmx, (padded > 0).astype(jnp.int32), pend[-1:], n_tiles - n_used]
                               ).astype(jnp.int32)
    xs = moe_dispatch(tokens, dest3, pad_info, n_tiles * tmx, tm=tm, tmx=tmx)
    ys = moe_ffn(xs, tile_expert, n_used, w1, w3, w2, li=li, tm=tmx, tf=tf, tn=MOE_DOWN_TILE)
    return moe_combine(ys, dest3, rgate, xall, mod, final_g, tm=tm, seg=seg, last=last)


def rope_tables(seq):
    rows = seq // GRID_W
    row = jnp.repeat(jnp.arange(rows, dtype=F32), GRID_W)
    col = jnp.tile(jnp.arange(GRID_W, dtype=F32), rows)
    n_freq = HEAD_DIM // 4
    inv = ROPE_BASE ** (-jnp.arange(n_freq, dtype=F32) / n_freq)
    ang = jnp.concatenate([row[:, None] * inv, col[:, None] * inv], axis=-1)
    cos, sin = jnp.cos(ang), jnp.sin(ang)
    cos64 = jnp.concatenate([cos, cos], axis=-1)
    sin64 = jnp.concatenate([-sin, sin], axis=-1)
    return jnp.tile(cos64, (1, 2)), jnp.tile(sin64, (1, 2))


def _tile_lanes(t, width):
    reps = width // t.shape[1]
    return t if reps == 1 else jnp.concatenate([t] * reps, axis=1)


def _rope(t, cos, sin):
    w = t.shape[1]
    lane = lax.broadcasted_iota(jnp.int32, t.shape, 1) % HEAD_DIM
    half = HEAD_DIM // 2
    rot = jnp.where(lane < half, pltpu.roll(t, w - half, 1), pltpu.roll(t, half, 1))
    return t * _tile_lanes(cos, w) + rot * _tile_lanes(sin, w)


def _group_matrix(width, value):
    r = lax.broadcasted_iota(jnp.int32, (width, width), 0) // HEAD_DIM
    c = lax.broadcasted_iota(jnp.int32, (width, width), 1) // HEAD_DIM
    return jnp.where(r == c, value, 0.0).astype(BF16)


RET_CHUNK = 128


def _retention_kernel(lat_ref, ctx_ref, cos_ref, sin_ref, dec_ref, gn_ref, ol_ref, oc_ref,
                      q_s, k_s, bn_s, st_s, o_s, dm_s, *, n_ctx, n_lat):
    ch = RET_CHUNK
    nc, nl = n_ctx // ch, n_lat // ch
    lg = -jnp.exp(dec_ref[...])
    lgf, lgb = lg[0:1], lg[1:2]
    pos = lax.broadcasted_iota(jnp.int32, (ch, 1), 0).astype(F32)
    wkf = jnp.exp((ch - 1 - pos) * lgf)
    wkb = jnp.exp(pos * lgb)
    wqf = jnp.exp((pos + 1) * lgf)
    wqb = jnp.exp((ch - pos) * lgb)
    cdf = jnp.exp(ch * lgf)
    cdb = jnp.exp(ch * lgb)
    blockmask = (lax.broadcasted_iota(jnp.int32, (RET_W, RET_W), 0) // HEAD_DIM
                 == lax.broadcasted_iota(jnp.int32, (RET_W, RET_W), 1) // HEAD_DIM)
    gmean = _group_matrix(RET_W, 1.0 / HEAD_DIM)
    diff = (lax.broadcasted_iota(jnp.int32, (ch, ch), 0) - lax.broadcasted_iota(jnp.int32, (ch, ch), 1)).astype(F32)
    gn = gn_ref[...]

    def load_qkv(ref, t0):
        return (ref[pl.ds(t0, ch), 0:RET_W].astype(F32), ref[pl.ds(t0, ch), RET_W:2 * RET_W].astype(F32),
                ref[pl.ds(t0, ch), 2 * RET_W:3 * RET_W])

    def reverse_step(ref, t0, c, roped):
        q, k, v = load_qkv(ref, t0)
        if roped:
            cos, sin = cos_ref[pl.ds(t0, ch), :], sin_ref[pl.ds(t0, ch), :]
            q, k = _rope(q, cos, sin), _rope(k, cos, sin)
        k = k * (HEAD_DIM ** -0.5)
        r0 = pl.multiple_of(c * ch, ch)
        q_s[pl.ds(r0, ch), :] = q.astype(BF16)
        k_s[pl.ds(r0, ch), :] = k.astype(BF16)
        st = st_s[...]
        bn_s[c] = st.astype(BF16)
        kvb = _dot_tn((k * wkb).astype(BF16), v)
        st_s[...] = cdb * st + jnp.where(blockmask, kvb, 0.0)

    st_s[...] = jnp.zeros_like(st_s)

    def rev_ctx(n, carry):
        c = nc - 1 - n
        reverse_step(ctx_ref, pl.multiple_of(c * ch, ch), c, False)
        return carry

    def rev_lat(n, carry):
        c = nl - 1 - n
        reverse_step(lat_ref, pl.multiple_of(c * ch, ch), nc + c, True)
        return carry

    lax.fori_loop(0, nc, rev_ctx, 0)
    lax.fori_loop(0, nl, rev_lat, 0)

    for h in range(RET_HEADS):
        lf, lb = lgf[:, h * HEAD_DIM:h * HEAD_DIM + 1], lgb[:, h * HEAD_DIM:h * HEAD_DIM + 1]
        dm_s[h] = jnp.where(diff == 0.0, 2.0, jnp.exp(jnp.abs(diff) * jnp.where(diff > 0.0, lf, lb)))

    def forward_step(ref, t0, c):
        r0 = pl.multiple_of(c * ch, ch)
        q = q_s[pl.ds(r0, ch), :]
        k = k_s[pl.ds(r0, ch), :]
        v = ref[pl.ds(t0, ch), 2 * RET_W:3 * RET_W]
        sls = [slice(h * HEAD_DIM, (h + 1) * HEAD_DIM) for h in range(RET_HEADS)]
        scores = [_dot_nt(q[:, sl], k[:, sl]) for sl in sls]
        probs = [(s * dm_s[h]).astype(BF16) for h, s in enumerate(scores)]
        heads = [_dot(p, v[:, sl]) for p, sl in zip(probs, sls)]
        qf = q.astype(F32)
        st = st_s[...]
        o = jnp.concatenate(heads, axis=1)
        o_s[pl.ds(r0, ch), :] = (o + _dot((qf * wqf).astype(BF16), st.astype(BF16))
                                 + _dot((qf * wqb).astype(BF16), bn_s[c]))
        kvf = _dot_tn((k.astype(F32) * wkf).astype(BF16), v)
        st_s[...] = cdf * st + jnp.where(blockmask, kvf, 0.0)

    st_s[...] = jnp.zeros_like(st_s)

    def fwd_ctx(n, carry):
        forward_step(ctx_ref, pl.multiple_of(n * ch, ch), n)
        return carry

    def fwd_lat(n, carry):
        forward_step(lat_ref, pl.multiple_of(n * ch, ch), nc + n)
        return carry

    lax.fori_loop(0, nc, fwd_ctx, 0)
    lax.fori_loop(0, nl, fwd_lat, 0, unroll=2)

    def finish(ref, out_ref, n_rows, row_off):
        ft = 2 * ch

        def body(i, carry):
            t0 = pl.multiple_of(i * ft, ft)
            o = o_s[pl.ds(pl.multiple_of(row_off + t0, ft), ft), :]
            g = ref[pl.ds(t0, ft), 3 * RET_W:4 * RET_W].astype(F32)
            xc = o - _split_dot(o, gmean)
            var = _split_dot(xc * xc, gmean)
            out_ref[pl.ds(t0, ft), :] = (xc * lax.rsqrt(var + EPS) * gn * _silu(g)).astype(out_ref.dtype)
            return carry

        lax.fori_loop(0, n_rows // ft, body, 0)

    finish(ctx_ref, oc_ref, n_ctx, 0)
    finish(lat_ref, ol_ref, n_lat, n_ctx)


def retention_mixer(p_ret, cos, sin, dec_lanes, gn_gain, *, n_batch, n_lat, n_ctx):
    n = p_ret.shape[0]
    width = p_ret.shape[1]
    ctx_blk0 = (n_batch * n_lat) // n_ctx
    t_all = n_lat + n_ctx
    nch = t_all // RET_CHUNK
    return pl.pallas_call(
        functools.partial(_retention_kernel, n_ctx=n_ctx, n_lat=n_lat),
        out_shape=(jax.ShapeDtypeStruct((n_batch * n_lat, RET_W), BF16),
                   jax.ShapeDtypeStruct((n_batch * n_ctx, RET_W), BF16)),
        grid=(n_batch,),
        in_specs=[pl.BlockSpec((n_lat, width), lambda b: (b, 0)),
                  pl.BlockSpec((n_ctx, width), lambda b: (ctx_blk0 + b, 0)),
                  pl.BlockSpec((n_lat, LANES), lambda b: (0, 0)),
                  pl.BlockSpec((n_lat, LANES), lambda b: (0, 0)),
                  pl.BlockSpec((2, RET_W), lambda b: (0, 0)),
                  pl.BlockSpec((1, RET_W), lambda b: (0, 0))],
        out_specs=(pl.BlockSpec((n_lat, RET_W), lambda b: (b, 0)),
                   pl.BlockSpec((n_ctx, RET_W), lambda b: (b, 0))),
        scratch_shapes=[pltpu.VMEM((t_all, RET_W), BF16), pltpu.VMEM((t_all, RET_W), BF16),
                        pltpu.VMEM((nch, RET_W, RET_W), BF16), pltpu.VMEM((RET_W, RET_W), F32),
                        pltpu.VMEM((t_all, RET_W), F32), pltpu.VMEM((RET_HEADS, RET_CHUNK, RET_CHUNK), F32)],
        compiler_params=_cparams("parallel"),
        name="retention",
    )(p_ret, p_ret, cos, sin, dec_lanes, gn_gain)


GDN_CHUNK = 64
GDN_PREP = 128
GDN_HALO = 8
GDN_SOLVE_CHUNKS = 4


def _softplus(x):
    return jnp.maximum(x, 0.0) + jnp.log(1.0 + jnp.exp(-jnp.abs(x)))


def _head_block_diag(x):
    n, w = x.shape
    heads = w // HEAD_DIM
    rows = lax.broadcasted_iota(jnp.int32, (heads * n, w), 0) // n
    cols = lax.broadcasted_iota(jnp.int32, (heads * n, w), 1) // HEAD_DIM
    return jnp.where(rows == cols, jnp.concatenate([x] * heads, axis=0), 0.0)


def _unit_tri_inverses(mats):
    n, w = mats[0].shape
    ri = lax.broadcasted_iota(jnp.int32, (n, w), 0)
    ci = lax.broadcasted_iota(jnp.int32, (n, w), 1) % HEAD_DIM
    b16 = (ri // 16) == (ci // 16)
    b32 = (ri // 32) == (ci // 32)
    eye = jnp.where(ri == ci, 1.0, 0.0)
    ps = [jnp.where(b16, -a, 0.0) for a in mats]
    ts = [eye + p for p in ps]
    for _ in range(3):
        pbs = [p.astype(BF16) for p in ps]
        ps = [_dot(pb, _head_block_diag(pb)) for pb in pbs]
        ts = [t + _dot(t.astype(BF16), _head_block_diag(p.astype(BF16))) for t, p in zip(ts, ps)]
    for keep in (b32 & ~b16, ~b32):
        offs = [jnp.where(keep, a, 0.0).astype(BF16) for a in mats]
        tbs = [t.astype(BF16) for t in ts]
        mids = [_dot(tb, _head_block_diag(off)).astype(BF16) for tb, off in zip(tbs, offs)]
        ts = [t - _dot(mid, _head_block_diag(tb)) for t, mid, tb in zip(ts, mids, tbs)]
    return ts


def _gdn_prep_kernel(lat_ref, ctx_ref, abl_ref, abc_ref, cw_ref, par_ref, q_s, k_s, v_s, gb_s, *, n_ctx, n_lat):
    pt = GDN_PREP
    halo = GDN_HALO
    qkv_w = 3 * GDN_W
    gsum = _group_matrix(GDN_W, 1.0)
    lane128 = lax.broadcasted_iota(jnp.int32, (pt, LANES), 1)
    neg_a = -jnp.exp(par_ref[0:1, :])
    dt_bias = par_ref[1:2, :]
    taps = [cw_ref[j:j + 1, :] for j in range(GDN_CONV)]

    def prep(ref, ab_ref, n_rows, row_off):
        n_tiles = n_rows // pt

        def body(i, carry):
            t0 = pl.multiple_of(i * pt, pt)
            cur = ref[pl.ds(t0, pt), 0:qkv_w].astype(F32)
            p0 = pl.multiple_of(jnp.maximum(t0 - 16, 0), 16)
            n0 = pl.multiple_of(jnp.minimum(t0 + pt, n_rows - 16), 16)
            prev = ref[pl.ds(p0, 16), 0:qkv_w].astype(F32)[16 - halo:16]
            nxt = ref[pl.ds(n0, 16), 0:qkv_w].astype(F32)[0:halo]
            prev = jnp.where(i > 0, prev, 0.0)
            nxt = jnp.where(i < n_tiles - 1, nxt, 0.0)
            ext = jnp.concatenate([prev, cur, nxt], axis=0)
            rows = pt + 2 * halo
            acc = None
            for j in range(GDN_CONV):
                s = j - (GDN_CONV - 1) // 2
                sh = ext if s == 0 else pltpu.roll(ext, (rows - s) % rows, 0)
                term = sh[halo:halo + pt] * taps[j]
                acc = term if acc is None else acc + term
            act = _silu(acc)
            q, k, v = act[:, 0:GDN_W], act[:, GDN_W:2 * GDN_W], act[:, 2 * GDN_W:3 * GDN_W]
            q = q * lax.rsqrt(_split_dot(q * q, gsum) + EPS) * (HEAD_DIM ** -0.5)
            k = k * lax.rsqrt(_split_dot(k * k, gsum) + EPS)
            r0 = pl.multiple_of(row_off + t0, pt)
            q_s[pl.ds(r0, pt), :] = q.astype(BF16)
            k_s[pl.ds(r0, pt), :] = k.astype(BF16)
            v_s[pl.ds(r0, pt), :] = v.astype(BF16)
            ab = ab_ref[pl.ds(t0, pt), :]
            g = neg_a * _softplus(ab + dt_bias)
            gb_s[pl.ds(r0, pt), :] = jnp.where(lane128 < 2 * GDN_HEADS, g, _sigmoid(ab))
            return carry

        lax.fori_loop(0, n_tiles, body, 0)

    prep(ctx_ref, abc_ref, n_ctx, 0)
    prep(lat_ref, abl_ref, n_lat, n_ctx)


def _gdn_kernel(q_s, k_s, v_s, gb_s, zl_ref, zc_ref, ng_ref, ol_ref, oc_ref,
                o_s, dec_s, uf_s, ub_s, wf_s, wb_s, af_s, ab_s, qf_s, qb_s, kf_s, kb_s, sf_s, sb_s, *,
                n_ctx, n_lat, n_solve):
    u_s, w_s, a_s, qd_s, kd_s, st_s = (uf_s, ub_s), (wf_s, wb_s), (af_s, ab_s), (qf_s, qb_s), (kf_s, kb_s), (sf_s, sb_s)
    ch = GDN_CHUNK
    pt = GDN_PREP
    gmean = _group_matrix(GDN_W, 1.0 / HEAD_DIM)
    ri = lax.broadcasted_iota(jnp.int32, (ch, ch), 0)
    ci = lax.broadcasted_iota(jnp.int32, (ch, ch), 1)
    tri = tuple(jnp.where(m, 1.0, 0.0).astype(BF16) for m in (ri >= ci, ri <= ci))
    rp = lax.broadcasted_iota(jnp.int32, (ch, GDN_W), 0)
    cp = lax.broadcasted_iota(jnp.int32, (ch, GDN_W), 1) % HEAD_DIM
    incl = (rp >= cp, rp <= cp)
    strict = (rp > cp, rp < cp)
    ncc, nlc = n_ctx // ch, n_lat // ch
    head_blocks = (lax.broadcasted_iota(jnp.int32, (GDN_W, GDN_W), 0) // HEAD_DIM
                   == lax.broadcasted_iota(jnp.int32, (GDN_W, GDN_W), 1) // HEAD_DIM)

    def solve_chunks(cs):
        n_cs = len(cs)
        items = [(j, d) for j in range(n_cs) for d in range(2)]
        rows = [pl.ds(pl.multiple_of(c * ch, ch), ch) for c in cs]
        gbc = [gb_s[r, :] for r in rows]
        gcs = []
        for j in range(n_cs):
            per_dir = []
            for d in range(2):
                rem, acc = gbc[j], None
                for _ in range(3):
                    part = rem.astype(BF16)
                    term = _dot(tri[d], part)
                    acc = term if acc is None else acc + term
                    rem = rem - part.astype(F32)
                per_dir.append(acc)
            gcs.append(per_dir)
        gcs_t = [[g.T for g in per_dir] for per_dir in gcs]
        edge = (ch - 1, 0)

        def spread(mat, first):
            return jnp.concatenate([jnp.broadcast_to(mat[:, first + h:first + h + 1], (ch, HEAD_DIM))
                                    for h in range(GDN_HEADS)], axis=1)

        gc = [spread(gcs[j][d], d * GDN_HEADS) for j, d in items]
        gr = [jnp.concatenate([gcs_t[j][d][d * GDN_HEADS + h:d * GDN_HEADS + h + 1, :] for h in range(GDN_HEADS)],
                              axis=1) for j, d in items]
        gtot = [x[edge[d]:edge[d] + 1, :] for x, (_, d) in zip(gc, items)]
        beta = [spread(gbc[j], 2 * GDN_HEADS + d * GDN_HEADS) for j, d in items]
        decay = [jnp.where(incl[d], jnp.exp(jnp.minimum(gc[i] - gr[i], 0.0)), 0.0) for i, (_, d) in enumerate(items)]
        q = [q_s[r, :] for r in rows]
        k = [k_s[r, :] for r in rows]
        kf = [x.astype(F32) for x in k]
        v = [v_s[r, :].astype(F32) for r in rows]
        k_rows = [_head_block_diag(x) for x in k]
        qk = [_dot_nt(x, y) for x, y in zip(q, k_rows)]
        kb = [kf[j] * beta[i] for i, (j, _) in enumerate(items)]
        kk = [_dot_nt(kb[i].astype(BF16), k_rows[j]) for i, (j, _) in enumerate(items)]
        a = [jnp.where(strict[d], kk[i] * decay[i], 0.0) for i, (_, d) in enumerate(items)]
        t = [x.astype(BF16) for x in _unit_tri_inverses(a)]
        eg = [jnp.exp(x) for x in gc]
        u = [_dot(t[i], _head_block_diag((v[j] * beta[i]).astype(BF16))) for i, (j, _) in enumerate(items)]
        w = [_dot(t[i], _head_block_diag((kb[i] * eg[i]).astype(BF16))) for i in range(len(items))]
        for i, (j, d) in enumerate(items):
            u_s[d][rows[j], :] = u[i].astype(BF16)
            w_s[d][rows[j], :] = w[i].astype(BF16)
            a_s[d][rows[j], :] = (qk[j] * decay[i]).astype(BF16)
            qd_s[d][rows[j], :] = (q[j].astype(F32) * eg[i]).astype(BF16)
            kd_s[d][rows[j], :] = (kf[j] * jnp.exp(gtot[i] - gc[i])).astype(BF16)
            dec_s[pl.ds(cs[j], 1), d * GDN_W:(d + 1) * GDN_W] = jnp.exp(gtot[i])

    def scan_step(cf, cb):
        cs = (cf, cb)
        rows = [pl.ds(pl.multiple_of(c * ch, ch), ch) for c in cs]
        s = [st[...] for st in st_s]
        lhs = [jnp.concatenate([w_s[d][rows[d], :], qd_s[d][rows[d], :]], axis=0) for d in range(2)]
        prod = [_dot(lhs[d], s[d].astype(BF16)) for d in range(2)]
        vb = [(u_s[d][rows[d], :].astype(F32) - prod[d][0:ch]).astype(BF16) for d in range(2)]
        vbd = [jnp.where(head_blocks, jnp.concatenate([x] * GDN_HEADS, axis=0), 0.0) for x in vb]
        av = [_dot(a_s[d][rows[d], :], vbd[d]) for d in range(2)]
        upd = [_dot_tn(kd_s[d][rows[d], :], vb[d]) for d in range(2)]
        for d in range(2):
            dec = dec_s[pl.ds(cs[d], 1), d * GDN_W:(d + 1) * GDN_W]
            st_s[d][...] = s[d] * dec + jnp.where(head_blocks, upd[d], 0.0)
        for d in range(2):
            o_s[rows[d], :] += prod[d][ch:2 * ch] + av[d]

    def solve_body(n, carry):
        solve_chunks([n * n_solve + j for j in range(n_solve)])
        return carry

    lax.fori_loop(0, (ncc + nlc) // n_solve, solve_body, 0)

    for st in st_s:
        st[...] = jnp.zeros_like(st)
    o_s[...] = jnp.zeros_like(o_s)

    def scan_ctx(n, carry):
        scan_step(n, ncc - 1 - n)
        return carry

    def scan_lat(n, carry):
        scan_step(ncc + n, ncc + nlc - 1 - n)
        return carry

    lax.fori_loop(0, ncc, scan_ctx, 0)
    lax.fori_loop(0, nlc, scan_lat, 0, unroll=2)

    ng = ng_ref[...]

    def finish(z_ref, out_ref, n_rows, row_off):
        def body(i, carry):
            t0 = pl.multiple_of(i * pt, pt)
            o = o_s[pl.ds(pl.multiple_of(row_off + t0, pt), pt), :]
            z = z_ref[pl.ds(t0, pt), :].astype(F32)
            y = o * lax.rsqrt(_split_dot(o * o, gmean) + EPS) * ng * _silu(z)
            out_ref[pl.ds(t0, pt), :] = y.astype(out_ref.dtype)
            return carry

        lax.fori_loop(0, n_rows // pt, body, 0)

    finish(zc_ref, oc_ref, n_ctx, 0)
    finish(zl_ref, ol_ref, n_lat, n_ctx)


def gdn_mixer(p_gdn, p_ab, conv_w, a_log, dt_bias, norm_gain, *, n_batch, n_lat, n_ctx):
    width = p_gdn.shape[1]
    ctx_blk0 = (n_batch * n_lat) // n_ctx
    t_all = n_lat + n_ctx
    cw = jnp.pad(conv_w.astype(F32), ((0, 8 - GDN_CONV), (0, 0)))
    par = jnp.zeros((8, LANES), F32)
    par = par.at[0, :2 * GDN_HEADS].set(a_log.reshape(-1)).at[1, :2 * GDN_HEADS].set(dt_bias.reshape(-1))
    ng = jnp.tile(norm_gain.astype(F32).reshape(1, HEAD_DIM), (1, GDN_HEADS))
    seq = lambda w: pl.BlockSpec((t_all, w), lambda b: (b, 0))
    q, k, v, gb = pl.pallas_call(
        functools.partial(_gdn_prep_kernel, n_ctx=n_ctx, n_lat=n_lat),
        out_shape=(jax.ShapeDtypeStruct((n_batch * t_all, GDN_W), BF16),) * 3
        + (jax.ShapeDtypeStruct((n_batch * t_all, LANES), F32),),
        grid=(n_batch,),
        in_specs=[pl.BlockSpec((n_lat, width), lambda b: (b, 0)),
                  pl.BlockSpec((n_ctx, width), lambda b: (ctx_blk0 + b, 0)),
                  pl.BlockSpec((n_lat, LANES), lambda b: (b, 0)),
                  pl.BlockSpec((n_ctx, LANES), lambda b: (ctx_blk0 + b, 0)),
                  pl.BlockSpec(cw.shape, lambda b: (0, 0)),
                  pl.BlockSpec(par.shape, lambda b: (0, 0))],
        out_specs=(seq(GDN_W), seq(GDN_W), seq(GDN_W), seq(LANES)),
        compiler_params=_cparams("parallel"),
        name="gdn_prep",
    )(p_gdn, p_gdn, p_ab, p_ab, cw, par)
    zcol = 3 * GDN_W // GDN_W
    once = lambda w: pl.BlockSpec((t_all, w), lambda b: (b, 0), pipeline_mode=pl.Buffered(1))
    n_solve = GDN_SOLVE_CHUNKS if (t_all // GDN_CHUNK) % GDN_SOLVE_CHUNKS == 0 else 1
    return pl.pallas_call(
        functools.partial(_gdn_kernel, n_ctx=n_ctx, n_lat=n_lat, n_solve=n_solve),
        out_shape=(jax.ShapeDtypeStruct((n_batch * n_lat, GDN_W), BF16),
                   jax.ShapeDtypeStruct((n_batch * n_ctx, GDN_W), BF16)),
        grid=(n_batch,),
        in_specs=[once(GDN_W), once(GDN_W), once(GDN_W), once(LANES),
                  pl.BlockSpec((n_lat, GDN_W), lambda b: (b, zcol), pipeline_mode=pl.Buffered(1)),
                  pl.BlockSpec((n_ctx, GDN_W), lambda b: (ctx_blk0 + b, zcol)),
                  pl.BlockSpec((1, GDN_W), lambda b: (0, 0))],
        out_specs=(pl.BlockSpec((n_lat, GDN_W), lambda b: (b, 0)),
                   pl.BlockSpec((n_ctx, GDN_W), lambda b: (b, 0))),
        scratch_shapes=[pltpu.VMEM((t_all, GDN_W), F32), pltpu.VMEM((t_all // GDN_CHUNK, 2 * GDN_W), F32)]
        + [pltpu.VMEM((t_all, GDN_W), BF16)] * 10
        + [pltpu.VMEM((GDN_W, GDN_W), F32)] * 2,
        compiler_params=_cparams("parallel"),
        name="gdn",
    )(q, k, v, gb, p_gdn, p_gdn, ng)


SWA_BLOCK = 128
SWA_BAND = 3 * SWA_BLOCK
SWA_Q_PER_STEP = 4


def _swa_kernel(sink_ref, q_ref, kl_ref, vl_ref, kc_ref, vc_ref, cos_ref, sin_ref, o_ref, *, band, n_lat, n_sub):
    blk = SWA_BLOCK
    kc = kc_ref[...]
    vc = vc_ref[...]
    for sub in range(n_sub):
        rows = slice(sub * blk, (sub + 1) * blk)
        o_ref[rows, :] = _swa_block(pl.program_id(1) * n_sub + sub, q_ref[rows, :].astype(F32), sink_ref, kl_ref,
                                    vl_ref, kc, vc, cos_ref, sin_ref, band=band, n_lat=n_lat).astype(o_ref.dtype)


def _swa_block(i, q, sink_ref, kl_ref, vl_ref, kc, vc, cos_ref, sin_ref, *, band, n_lat):
    blk = SWA_BLOCK
    if band:
        q0 = pl.multiple_of(i * blk, blk)
        start = pl.multiple_of(jnp.clip((i - 1) * blk, 0, n_lat - SWA_BAND), blk)
        q = _rope(q, cos_ref[pl.ds(q0, blk), :], sin_ref[pl.ds(q0, blk), :])
        kb = _rope(kl_ref[pl.ds(start, SWA_BAND), :].astype(F32),
                   cos_ref[pl.ds(start, SWA_BAND), :], sin_ref[pl.ds(start, SWA_BAND), :]).astype(BF16)
        vb = vl_ref[pl.ds(start, SWA_BAND), :]
        keys = jnp.concatenate([kb, kc], axis=0)
        vals = jnp.concatenate([vb, vc], axis=0)
    else:
        keys, vals = kc, vc
    qb = (q * (HEAD_DIM ** -0.5 * LOG2E)).astype(BF16)
    n_keys = keys.shape[0]
    if band:
        q_pos = q0 + lax.broadcasted_iota(jnp.int32, (blk, 1), 0)
        col_id = lax.broadcasted_iota(jnp.int32, (1, n_keys), 1)
        valid = (col_id >= SWA_BAND) | (jnp.abs(start + col_id - q_pos) <= WINDOW)
        bias = jnp.where(valid, 0.0, NEG_BIG)
    v_lane = lax.broadcasted_iota(jnp.int32, vals.shape, 1) // HEAD_DIM
    outs = []
    for g in range(SWA_KV_HEADS):
        ksl = slice(g * HEAD_DIM, (g + 1) * HEAD_DIM)
        v_ext = jnp.where(v_lane == g, vals, 1.0)
        den_lane = (1 - g) * HEAD_DIM
        q4 = jnp.concatenate([qb[:, (g * SWA_GROUP + r) * HEAD_DIM:(g * SWA_GROUP + r + 1) * HEAD_DIM]
                              for r in range(SWA_GROUP)], axis=0)
        s = _dot_nt(q4, keys[:, ksl])
        es, ms, sinks = [], [], []
        for r in range(SWA_GROUP):
            sr = s[r * blk:(r + 1) * blk]
            if band:
                sr = sr + bias
            sink2 = sink_ref[g * SWA_GROUP + r] * LOG2E
            m = jnp.maximum(jnp.max(sr, axis=-1, keepdims=True), sink2)
            es.append(jnp.exp2(sr - m).astype(BF16))
            ms.append(m)
            sinks.append(sink2)
        ov = _dot(jnp.concatenate(es, axis=0), v_ext)
        for r in range(SWA_GROUP):
            ovr = ov[r * blk:(r + 1) * blk]
            den = ovr[:, den_lane:den_lane + 1] + jnp.exp2(sinks[r] - ms[r])
            outs.append(ovr[:, ksl] * (1.0 / den))
    return jnp.concatenate(outs, axis=1)


def swa_mixer(p_swa, cos, sin, sink, *, n_batch, n_lat, n_ctx):
    blk = SWA_BLOCK
    ctx_row0 = n_batch * n_lat
    kcol, vcol = SWA_W // SWA_KV_W, SWA_W // SWA_KV_W + 1
    sink = sink.astype(F32)

    def call(band):
        n_q = n_lat if band else n_ctx
        n_sub = min(SWA_Q_PER_STEP, n_q // blk)
        qblk = n_sub * blk
        nb = n_q // qblk
        qrow0 = 0 if band else ctx_row0 // qblk
        grid_spec = pltpu.PrefetchScalarGridSpec(
            num_scalar_prefetch=1,
            grid=(n_batch, nb),
            in_specs=[pl.BlockSpec((qblk, SWA_W), lambda b, i, s: (qrow0 + b * nb + i, 0)),
                      pl.BlockSpec((n_lat, SWA_KV_W), lambda b, i, s: (b, kcol)),
                      pl.BlockSpec((n_lat, SWA_KV_W), lambda b, i, s: (b, vcol)),
                      pl.BlockSpec((n_ctx, SWA_KV_W), lambda b, i, s: (ctx_row0 // n_ctx + b, kcol)),
                      pl.BlockSpec((n_ctx, SWA_KV_W), lambda b, i, s: (ctx_row0 // n_ctx + b, vcol)),
                      pl.BlockSpec((n_lat, LANES), lambda b, i, s: (0, 0)),
                      pl.BlockSpec((n_lat, LANES), lambda b, i, s: (0, 0))],
            out_specs=pl.BlockSpec((qblk, SWA_W), lambda b, i, s: (b * nb + i, 0)),
        )
        return pl.pallas_call(
            functools.partial(_swa_kernel, band=band, n_lat=n_lat, n_sub=n_sub),
            out_shape=jax.ShapeDtypeStruct((n_batch * n_q, SWA_W), BF16),
            grid_spec=grid_spec,
            compiler_params=_cparams("parallel", "arbitrary"),
            name="swa_latent" if band else "swa_context",
        )(sink, p_swa, p_swa, p_swa, p_swa, p_swa, cos, sin)

    return call(True), call(False)


MOE_ROW_TILE = 512
MOE_FF_TILE = 1792
MOE_DOWN_TILE = 1024


def _row_tile(n_lat, n_ctx_rows):
    for tm in (512, 256, 128):
        if n_lat % tm == 0 and n_ctx_rows % tm == 0:
            return tm
    raise ValueError("sequence lengths must be multiples of 128")


def kernel(x, c, ctx, c_ctx, ada_w, ada_b, norm1_g, w_in, ret_decay, ret_gn_g, gdn_conv_w, gdn_a_log, gdn_dt_bias,
           gdn_norm_g, swa_sink, w_out, norm2_g, ffn_w1, ffn_w3, ffn_w2, router_w, moe_w1, moe_w3, moe_w2, final_g):
    b, s, d = x.shape
    n_ctx = ctx.shape[1]
    depth = ada_w.shape[0]
    nl, ncx = b * s, b * n_ctx
    tm = _row_tile(s, ncx)
    seg = _seg_map(nl // tm, s // tm, b)

    cvec = jnp.zeros((8, d), F32).at[:b].set(c).at[b].set(c_ctx)
    mod_all = adaln(cvec, ada_w, ada_b).reshape(depth, 8, 6, d)
    xall = jnp.concatenate([x.reshape(nl, d), ctx.reshape(ncx, d)], axis=0)
    cos, sin = rope_tables(s)
    fg = final_g.reshape(1, d)

    ret_cols, gdn_cols = 4 * RET_W, 4 * GDN_W
    ab_cols = 4 * GDN_HEADS
    for layer in range(depth):
        last = layer == depth - 1
        mod = mod_all[layer]
        w = w_in[layer].astype(BF16)
        w_ret = w[:, :ret_cols]
        w_gdn = w[:, ret_cols:ret_cols + gdn_cols]
        w_ab = jnp.pad(w[:, ret_cols + gdn_cols:ret_cols + gdn_cols + ab_cols], ((0, 0), (0, LANES - ab_cols)))
        w_swa = w[:, ret_cols + gdn_cols + ab_cols:]
        p_ret, p_gdn, p_swa, p_ab = in_proj(xall, mod, norm1_g[layer].reshape(1, d), w_ret, w_gdn, w_swa, w_ab,
                                            tm=tm, seg=seg)
        dec_lanes = jnp.repeat(ret_decay[layer].astype(F32), HEAD_DIM, axis=1)
        o_ret = retention_mixer(p_ret, cos, sin, dec_lanes, ret_gn_g[layer].reshape(1, RET_W),
                                n_batch=b, n_lat=s, n_ctx=n_ctx)
        o_gdn = gdn_mixer(p_gdn, p_ab, gdn_conv_w[layer], gdn_a_log[layer], gdn_dt_bias[layer], gdn_norm_g[layer],
                          n_batch=b, n_lat=s, n_ctx=n_ctx)
        o_swa = swa_mixer(p_swa, cos, sin, swa_sink[layer], n_batch=b, n_lat=s, n_ctx=n_ctx)

        n_rows = nl if last else nl + ncx
        is_moe = layer % 2 == 1
        i = layer // 2
        rw = None
        if is_moe:
            rw = jnp.pad(router_w[i].astype(F32), ((0, 0), (0, LANES - N_EXPERTS)))
        res = out_proj(o_ret, o_gdn, o_swa, xall, mod, w_out[layer].astype(BF16), norm2_g[layer].reshape(1, d), rw,
                       tm=tm, seg=seg, n_rows=n_rows, n_lat_rows=nl, tok_dtype=F32 if is_moe else BF16)
        if is_moe:
            x_new, tokens, ridx, rgate, cnt = res
            xall = moe_layer(tokens, ridx, rgate, cnt, x_new, mod, moe_w1, moe_w3, moe_w2, fg,
                             li=i, tm=tm, tmx=MOE_ROW_TILE, tf=MOE_FF_TILE, seg=seg, last=last)
        else:
            x_new, tokens = res
            xall = dense_ffn(tokens, x_new, mod, ffn_w1[i].astype(BF16), ffn_w3[i].astype(BF16),
                             ffn_w2[i].astype(BF16), fg, tm=tm, seg=seg, last=last)
    return xall[:nl].reshape(b, s, d)
```

```python
import functools

import jax
import jax.numpy as jnp
import numpy as np
from jax import lax
from jax.experimental import pallas as pl
from jax.experimental.pallas import tpu as pltpu

F32 = jnp.float32
BF16 = jnp.bfloat16

HEAD_DIM = 64
GRID_W = 64
ROPE_BASE = 10000.0
EPS = 1e-6
RET_HEADS = 4
GDN_HEADS = 4
GDN_CONV = 5
SWA_HEADS = 8
SWA_KV_HEADS = 2
SWA_GROUP = SWA_HEADS // SWA_KV_HEADS
WINDOW = 128
N_EXPERTS = 8
RET_W = RET_HEADS * HEAD_DIM
GDN_W = GDN_HEADS * HEAD_DIM
SWA_W = SWA_HEADS * HEAD_DIM
SWA_KV_W = SWA_KV_HEADS * HEAD_DIM
LANES = 128
VMEM_LIMIT = 56 * 1024 * 1024
NEG_BIG = -1e30
LOG2E = 1.4426950408889634


def _cparams(*sem):
    return pltpu.CompilerParams(dimension_semantics=sem, vmem_limit_bytes=VMEM_LIMIT)


def _dot(a, b):
    return jnp.dot(a, b, preferred_element_type=F32)


def _dot_nt(a, b):
    return lax.dot_general(a, b, (((1,), (1,)), ((), ())), preferred_element_type=F32)


def _dot_tn(a, b):
    return lax.dot_general(a, b, (((0,), (0,)), ((), ())), preferred_element_type=F32)


def _split_dot(x, m_bf16, terms=2):
    acc = None
    rem = x
    for _ in range(terms):
        part = rem.astype(BF16)
        d = _dot(part, m_bf16)
        acc = d if acc is None else acc + d
        rem = rem - part.astype(F32)
    return acc


def _sigmoid(x):
    return 1.0 / (1.0 + jnp.exp(-x))


def _silu(x):
    return x * _sigmoid(x)


def _rms_mod(x, gain, shift, scale):
    ms = jnp.mean(x * x, axis=-1, keepdims=True)
    y = x * lax.rsqrt(ms + EPS) * gain
    return y * (1.0 + scale) + shift


def _seg_map(n_lat_tiles, tiles_per_batch, n_batch):
    def seg(i):
        return jnp.where(i < n_lat_tiles, i // tiles_per_batch, n_batch)
    return seg


def _adaln_kernel(c_ref, w_ref, b_ref, o_ref):
    h = _silu(c_ref[...]).astype(BF16)
    o_ref[0] = _dot(h, w_ref[0].astype(BF16)) + b_ref[0]


def adaln(cvec, ada_w, ada_b):
    depth, d, n6 = ada_w.shape
    tn = 1536 if n6 % 1536 == 0 else n6
    rows = cvec.shape[0]
    return pl.pallas_call(
        _adaln_kernel,
        out_shape=jax.ShapeDtypeStruct((depth, rows, n6), F32),
        grid=(depth, n6 // tn),
        in_specs=[pl.BlockSpec((rows, d), lambda l, j: (0, 0)),
                  pl.BlockSpec((1, d, tn), lambda l, j: (l, 0, j)),
                  pl.BlockSpec((1, 1, tn), lambda l, j: (l, 0, j))],
        out_specs=pl.BlockSpec((1, rows, tn), lambda l, j: (l, 0, j)),
        compiler_params=_cparams("parallel", "parallel"),
        name="adaln",
    )(cvec, ada_w, ada_b.reshape(depth, 1, n6))


def _inproj_kernel(x_ref, mod_ref, g_ref, wr_ref, wg_ref, ws_ref, wab_ref, pr_ref, pg_ref, ps_ref, pab_ref):
    h = _rms_mod(x_ref[...], g_ref[...], mod_ref[0, 0:1, :], mod_ref[0, 1:2, :]).astype(BF16)
    for w_ref, o_ref in ((wr_ref, pr_ref), (wg_ref, pg_ref), (ws_ref, ps_ref), (wab_ref, pab_ref)):
        n = w_ref.shape[1]
        step = 256 if n % 256 == 0 else n
        for c in range(0, n, step):
            o_ref[:, c:c + step] = _dot(h, w_ref[:, c:c + step]).astype(o_ref.dtype)


def in_proj(xall, mod, gain, w_ret, w_gdn, w_swa, w_ab, *, tm, seg):
    n, d = xall.shape
    full = lambda i: (0, 0)
    row = lambda i: (i, 0)
    outs = (jax.ShapeDtypeStruct((n, w_ret.shape[1]), BF16), jax.ShapeDtypeStruct((n, w_gdn.shape[1]), BF16),
            jax.ShapeDtypeStruct((n, w_swa.shape[1]), BF16), jax.ShapeDtypeStruct((n, w_ab.shape[1]), F32))
    return pl.pallas_call(
        _inproj_kernel,
        out_shape=outs,
        grid=(n // tm,),
        in_specs=[pl.BlockSpec((tm, d), row),
                  pl.BlockSpec((1, 6, d), lambda i: (seg(i), 0, 0)),
                  pl.BlockSpec((1, d), full),
                  pl.BlockSpec(w_ret.shape, full), pl.BlockSpec(w_gdn.shape, full),
                  pl.BlockSpec(w_swa.shape, full), pl.BlockSpec(w_ab.shape, full)],
        out_specs=tuple(pl.BlockSpec((tm, o.shape[1]), row) for o in outs),
        compiler_params=_cparams("parallel"),
        name="in_proj",
    )(xall, mod, gain, w_ret, w_gdn, w_swa, w_ab)


def _outproj_kernel(*refs, tm, moe, n_lat_tiles, has_ctx):
    n_mix = 6 if has_ctx else 3
    mix = refs[:n_mix]
    refs = refs[n_mix:]
    if moe:
        (x_ref, mod_ref, w_ref, g_ref, rw_ref, xo_ref, tok_ref, ridx_ref, rgate_ref, cnt_ref, base_ref,
         earlier_ref) = refs
    else:
        x_ref, mod_ref, w_ref, g_ref, xo_ref, tok_ref = refs
    if has_ctx:
        is_lat = pl.program_id(0) < n_lat_tiles
        o_ret, o_gdn, o_swa = (jnp.where(is_lat, mix[2 * k][...], mix[2 * k + 1][...]) for k in range(3))
    else:
        o_ret, o_gdn, o_swa = (m[...] for m in mix)
    y = _dot(o_ret, w_ref[0:RET_W, :])
    y = y + _dot(o_gdn, w_ref[RET_W:RET_W + GDN_W, :])
    y = y + _dot(o_swa, w_ref[RET_W + GDN_W:, :])
    x = x_ref[...] + mod_ref[0, 2:3, :] * y
    xo_ref[...] = x
    t = _rms_mod(x, g_ref[...], mod_ref[0, 3:4, :], mod_ref[0, 4:5, :])
    tok_ref[...] = t.astype(tok_ref.dtype)
    if not moe:
        return

    @pl.when(pl.program_id(0) == 0)
    def _():
        base_ref[...] = jnp.zeros_like(base_ref)
        r_i = lax.broadcasted_iota(jnp.int32, (tm, tm), 0)
        c_i = lax.broadcasted_iota(jnp.int32, (tm, tm), 1)
        earlier_ref[...] = jnp.where(r_i < c_i, 1.0, 0.0).astype(BF16)

    rw = rw_ref[...]
    rw_hi = rw.astype(BF16)
    rw_lo = (rw - rw_hi.astype(F32)).astype(BF16)
    t_hi = t.astype(BF16)
    t_lo = (t - t_hi.astype(F32)).astype(BF16)
    logits = _dot(t_hi, rw_hi) + _dot(t_hi, rw_lo) + _dot(t_lo, rw_hi)
    lt = logits.T[0:N_EXPERTS, :]
    e_id = lax.broadcasted_iota(jnp.int32, lt.shape, 0)
    m1 = jnp.max(lt, axis=0, keepdims=True)
    i1 = jnp.min(jnp.where(lt == m1, e_id, N_EXPERTS), axis=0, keepdims=True)
    rest = jnp.where(e_id == i1, NEG_BIG, lt)
    m2 = jnp.max(rest, axis=0, keepdims=True)
    i2 = jnp.min(jnp.where(rest == m2, e_id, N_EXPERTS), axis=0, keepdims=True)
    e2 = jnp.exp(m2 - m1)
    g1 = 1.0 / (1.0 + e2)
    g2 = e2 * g1
    oh1 = jnp.where(e_id == i1, 1.0, 0.0)
    oh2 = jnp.where(e_id == i2, 1.0, 0.0)
    cum = _dot(jnp.concatenate([oh1, oh2], axis=0).astype(BF16), earlier_ref[...])
    cum1, cum2 = cum[0:N_EXPERTS], cum[N_EXPERTS:2 * N_EXPERTS]
    cnt1 = jnp.sum(oh1, axis=1, keepdims=True)
    cnt2 = jnp.sum(oh2, axis=1, keepdims=True)
    base = base_ref[:, 0:1]
    rank1 = jnp.sum(oh1 * (base + cum1), axis=0, keepdims=True)
    rank2 = jnp.sum(oh2 * (base + cnt1 + cum2), axis=0, keepdims=True)
    base = base + cnt1 + cnt2
    base_ref[...] = jnp.broadcast_to(base, base_ref.shape)
    cnt_ref[...] = jnp.broadcast_to(base, cnt_ref.shape).astype(jnp.int32)
    row8 = lax.broadcasted_iota(jnp.int32, (8, tm), 0)
    r1 = rank1.astype(jnp.int32)
    r2 = rank2.astype(jnp.int32)
    ridx_ref[...] = jnp.where(row8 == 0, i1, jnp.where(row8 == 1, i2, jnp.where(row8 == 2, r1,
                              jnp.where(row8 == 3, r2, 0))))
    rgate_ref[...] = jnp.where(row8 == 0, g1, jnp.where(row8 == 1, g2, 0.0))


def out_proj(o_ret, o_gdn, o_swa, xall, mod, w_out, gain2, router_w, *, tm, seg, n_rows, n_lat_rows, tok_dtype):
    d = xall.shape[1]
    moe = router_w is not None
    has_ctx = n_rows > n_lat_rows
    n_lat_tiles = n_lat_rows // tm
    full = lambda i: (0, 0)
    row = lambda i: (i, 0)
    lat_row = lambda i: (jnp.minimum(i, n_lat_tiles - 1), 0)
    ctx_row = lambda i: (jnp.maximum(i - n_lat_tiles, 0), 0)
    in_specs, args = [], []
    for pair, width in ((o_ret, RET_W), (o_gdn, GDN_W), (o_swa, SWA_W)):
        in_specs.append(pl.BlockSpec((tm, width), lat_row))
        args.append(pair[0])
        if has_ctx:
            in_specs.append(pl.BlockSpec((tm, width), ctx_row))
            args.append(pair[1])
    in_specs += [pl.BlockSpec((tm, d), row), pl.BlockSpec((1, 6, d), lambda i: (seg(i), 0, 0)),
                 pl.BlockSpec(w_out.shape, full), pl.BlockSpec((1, d), full)]
    args += [xall, mod, w_out, gain2]
    outs = [jax.ShapeDtypeStruct((n_rows, d), F32), jax.ShapeDtypeStruct((n_rows, d), tok_dtype)]
    out_specs = [pl.BlockSpec((tm, d), row), pl.BlockSpec((tm, d), row)]
    scratch = []
    if moe:
        in_specs.append(pl.BlockSpec(router_w.shape, full))
        args.append(router_w)
        col = lambda i: (0, i)
        outs += [jax.ShapeDtypeStruct((8, n_rows), jnp.int32), jax.ShapeDtypeStruct((8, n_rows), F32),
                 jax.ShapeDtypeStruct((N_EXPERTS, LANES), jnp.int32)]
        out_specs += [pl.BlockSpec((8, tm), col), pl.BlockSpec((8, tm), col), pl.BlockSpec((N_EXPERTS, LANES), full)]
        scratch = [pltpu.VMEM((N_EXPERTS, LANES), F32), pltpu.VMEM((tm, tm), BF16)]
    return pl.pallas_call(
        functools.partial(_outproj_kernel, tm=tm, moe=moe, n_lat_tiles=n_lat_tiles, has_ctx=has_ctx),
        out_shape=tuple(outs),
        grid=(n_rows // tm,),
        in_specs=in_specs,
        out_specs=tuple(out_specs),
        scratch_shapes=scratch,
        compiler_params=_cparams("arbitrary"),
        name="out_proj_router" if moe else "out_proj",
    )(*args)


def _final_norm(x, gain):
    ms = jnp.mean(x * x, axis=-1, keepdims=True)
    return x * lax.rsqrt(ms + EPS) * gain


def _dense_ffn_kernel(tok_ref, x_ref, mod_ref, w1_ref, w3_ref, w2_ref, fg_ref, o_ref, *, last):
    t = tok_ref[...]
    h = (_silu(_dot(t, w1_ref[...])) * _dot(t, w3_ref[...])).astype(BF16)
    x = x_ref[...] + mod_ref[0, 5:6, :] * _dot(h, w2_ref[...])
    if last:
        x = _final_norm(x, fg_ref[...])
    o_ref[...] = x


def dense_ffn(tokens, xall, mod, w1, w3, w2, final_g, *, tm, seg, last):
    n, d = tokens.shape
    f = w1.shape[1]
    full = lambda i: (0, 0)
    row = lambda i: (i, 0)
    return pl.pallas_call(
        functools.partial(_dense_ffn_kernel, last=last),
        out_shape=jax.ShapeDtypeStruct((n, d), F32),
        grid=(n // tm,),
        in_specs=[pl.BlockSpec((tm, d), row), pl.BlockSpec((tm, d), row),
                  pl.BlockSpec((1, 6, d), lambda i: (seg(i), 0, 0)),
                  pl.BlockSpec((d, f), full, pipeline_mode=pl.Buffered(1)),
                  pl.BlockSpec((d, f), full, pipeline_mode=pl.Buffered(1)),
                  pl.BlockSpec((f, d), full, pipeline_mode=pl.Buffered(1)),
                  pl.BlockSpec((1, d), full)],
        out_specs=pl.BlockSpec((tm, d), row),
        compiler_params=_cparams("parallel"),
        name="dense_ffn",
    )(tokens, xall, mod, w1, w3, w2, final_g)


def _dispatch_kernel(pad_ref, dest_ref, tok_ref, xs_ref, zero_s, sem, zsem, *, tm, tmx):
    @pl.when(pl.program_id(0) == 0)
    def _():
        zero_s[...] = jnp.zeros_like(zero_s)

        def for_each_zero_copy(fn):
            for e in range(N_EXPERTS):
                @pl.when(pad_ref[N_EXPERTS + e] > 0)
                def _(e=e):
                    row = pl.multiple_of(pad_ref[e], tmx)
                    fn(pltpu.make_async_copy(zero_s, xs_ref.at[pl.ds(row, tmx)], zsem))
            for t in range(N_EXPERTS):
                @pl.when(t < pad_ref[2 * N_EXPERTS + 1])
                def _(t=t):
                    row = pl.multiple_of(pad_ref[2 * N_EXPERTS] + t * tmx, tmx)
                    fn(pltpu.make_async_copy(zero_s, xs_ref.at[pl.ds(row, tmx)], zsem))

        for_each_zero_copy(lambda c: c.start())
        for_each_zero_copy(lambda c: c.wait())

    def issue(r, carry):
        for k in range(2):
            dst = dest_ref[0, 0, 2 * r + k]
            pltpu.make_async_copy(tok_ref.at[pl.ds(r, 1)], xs_ref.at[pl.ds(dst, 1)], sem).start()
        return carry

    lax.fori_loop(0, tm, issue, 0, unroll=8)
    for _ in range(2):
        pltpu.make_async_copy(tok_ref, xs_ref.at[pl.ds(0, tm)], sem).wait()


def moe_dispatch(tokens, dest3, pad_info, n_slots, *, tm, tmx):
    n, d = tokens.shape
    grid_spec = pltpu.PrefetchScalarGridSpec(
        num_scalar_prefetch=1,
        grid=(n // tm,),
        in_specs=[pl.BlockSpec((1, 1, 2 * tm), lambda i, pad: (i, 0, 0), memory_space=pltpu.SMEM),
                  pl.BlockSpec((tm, d), lambda i, pad: (i, 0))],
        out_specs=pl.BlockSpec(memory_space=pl.ANY),
        scratch_shapes=[pltpu.VMEM((tmx, d), tokens.dtype), pltpu.SemaphoreType.DMA, pltpu.SemaphoreType.DMA],
    )
    return pl.pallas_call(
        functools.partial(_dispatch_kernel, tm=tm, tmx=tmx),
        out_shape=jax.ShapeDtypeStruct((n_slots, d), tokens.dtype),
        grid_spec=grid_spec,
        compiler_params=_cparams("arbitrary"),
        name="moe_dispatch",
    )(pad_info, dest3, tokens)


def _tile_expert(te_ref, nu_ref, i):
    return te_ref[jnp.clip(i, 0, nu_ref[0] - 1)]


def _moe_up_kernel(te_ref, nu_ref, xs_ref, w1_ref, w3_ref, h_ref, w1b_s, w3b_s):
    i = pl.program_id(1)

    @pl.when((i == 0) | (_tile_expert(te_ref, nu_ref, i) != _tile_expert(te_ref, nu_ref, i - 1)))
    def _():
        w1b_s[...] = w1_ref[0, 0].astype(BF16)
        w3b_s[...] = w3_ref[0, 0].astype(BF16)

    @pl.when(i < nu_ref[0])
    def _():
        x = xs_ref[...].astype(BF16)
        h_ref[...] = (_silu(_dot(x, w1b_s[...])) * _dot(x, w3b_s[...])).astype(h_ref.dtype)

    @pl.when(i >= nu_ref[0])
    def _():
        h_ref[...] = jnp.zeros_like(h_ref)


def _moe_down_kernel(te_ref, nu_ref, h_ref, w2_ref, y_ref, w2b_s):
    i = pl.program_id(1)

    @pl.when((i == 0) | (_tile_expert(te_ref, nu_ref, i) != _tile_expert(te_ref, nu_ref, i - 1)))
    def _():
        w2b_s[...] = w2_ref[0, 0].astype(BF16)

    @pl.when(i < nu_ref[0])
    def _():
        y_ref[...] = _dot(h_ref[...], w2b_s[...])

    @pl.when(i >= nu_ref[0])
    def _():
        y_ref[...] = jnp.zeros_like(y_ref)


def moe_ffn(xs, tile_expert, n_used, w1, w3, w2, *, li, tm, tf, tn):
    r, d = xs.shape
    f = w1.shape[3]

    def tile(i, nu):
        return jnp.minimum(i, nu[0] - 1)

    up_spec = pltpu.PrefetchScalarGridSpec(
        num_scalar_prefetch=2,
        grid=(f // tf, r // tm),
        in_specs=[pl.BlockSpec((tm, d), lambda j, i, te, nu: (tile(i, nu), 0)),
                  pl.BlockSpec((1, 1, d, tf), lambda j, i, te, nu: (li, te[tile(i, nu)], 0, j)),
                  pl.BlockSpec((1, 1, d, tf), lambda j, i, te, nu: (li, te[tile(i, nu)], 0, j))],
        out_specs=pl.BlockSpec((tm, tf), lambda j, i, te, nu: (i, j)),
        scratch_shapes=[pltpu.VMEM((d, tf), BF16), pltpu.VMEM((d, tf), BF16)],
    )
    hidden = pl.pallas_call(
        _moe_up_kernel,
        out_shape=jax.ShapeDtypeStruct((r, f), BF16),
        grid_spec=up_spec,
        compiler_params=_cparams("arbitrary", "arbitrary"),
        name="moe_up",
    )(tile_expert, n_used, xs, w1, w3)
    down_spec = pltpu.PrefetchScalarGridSpec(
        num_scalar_prefetch=2,
        grid=(d // tn, r // tm),
        in_specs=[pl.BlockSpec((tm, f), lambda n, i, te, nu: (tile(i, nu), 0)),
                  pl.BlockSpec((1, 1, f, tn), lambda n, i, te, nu: (li, te[tile(i, nu)], 0, n))],
        out_specs=pl.BlockSpec((tm, tn), lambda n, i, te, nu: (i, n)),
        scratch_shapes=[pltpu.VMEM((f, tn), BF16)],
    )
    return pl.pallas_call(
        _moe_down_kernel,
        out_shape=jax.ShapeDtypeStruct((r, d), F32),
        grid_spec=down_spec,
        compiler_params=_cparams("arbitrary", "arbitrary"),
        name="moe_down",
    )(tile_expert, n_used, hidden, w2)


def _combine_kernel(dest_ref, gate_ref, x_ref, mod_ref, fg_ref, ys_ref, o_ref, buf_ref, sem, *, tm, last):
    def issue(r, carry):
        for k in range(2):
            src = dest_ref[0, 0, 2 * r + k]
            pltpu.make_async_copy(ys_ref.at[pl.ds(src, 1)], buf_ref.at[k, pl.ds(r, 1)], sem).start()
        return carry

    lax.fori_loop(0, tm, issue, 0, unroll=8)
    for k in range(2):
        pltpu.make_async_copy(ys_ref.at[pl.ds(0, tm)], buf_ref.at[k], sem).wait()
    gate = gate_ref[...]
    y = gate[:, 0:1] * buf_ref[0] + gate[:, 1:2] * buf_ref[1]
    x = x_ref[...] + mod_ref[0, 5:6, :] * y
    if last:
        x = _final_norm(x, fg_ref[...])
    o_ref[...] = x


def moe_combine(ys, dest3, rgate, xall, mod, final_g, *, tm, seg, last):
    n, d = xall.shape
    return pl.pallas_call(
        functools.partial(_combine_kernel, tm=tm, last=last),
        out_shape=jax.ShapeDtypeStruct((n, d), F32),
        grid=(n // tm,),
        in_specs=[pl.BlockSpec((1, 1, 2 * tm), lambda i: (i, 0, 0), memory_space=pltpu.SMEM),
                  pl.BlockSpec((tm, 8), lambda i: (i, 0)),
                  pl.BlockSpec((tm, d), lambda i: (i, 0)),
                  pl.BlockSpec((1, 6, d), lambda i: (seg(i), 0, 0)),
                  pl.BlockSpec((1, d), lambda i: (0, 0)),
                  pl.BlockSpec(memory_space=pl.ANY)],
        out_specs=pl.BlockSpec((tm, d), lambda i: (i, 0)),
        scratch_shapes=[pltpu.VMEM((2, tm, d), F32), pltpu.SemaphoreType.DMA],
        compiler_params=_cparams("arbitrary"),
        name="moe_combine",
    )(dest3, rgate, xall, mod, final_g, ys)


def moe_layer(tokens, ridx, rgate, cnt, xall, mod, w1, w3, w2, final_g, *, li, tm, tmx, tf, seg, last):
    n = tokens.shape[0]
    counts = cnt[:, 0]
    padded = (counts + tmx - 1) // tmx * tmx
    pend = jnp.cumsum(padded)
    pstart = pend - padded
    group_start = sum(jnp.where(ridx[0:2] == e, pstart[e], 0) for e in range(N_EXPERTS))
    dest = (group_start + ridx[2:4]).T
    dest3 = dest.reshape(n // tm, 1, 2 * tm).astype(jnp.int32)
    rgate = rgate.T
    n_tiles = (2 * n) // tmx + N_EXPERTS
    n_used = (pend[-1] // tmx).astype(jnp.int32).reshape(1)
    tile_start = jnp.arange(n_tiles, dtype=jnp.int32) * tmx
    tile_expert = jnp.minimum(jnp.sum(tile_start[:, None] >= pend[None, :], axis=1), N_EXPERTS - 1).astype(jnp.int32)
    pad_info = jnp.concatenate([pend - tmx, (padded > 0).astype(jnp.int32), pend[-1:], n_tiles - n_used]
                               ).astype(jnp.int32)
    xs = moe_dispatch(tokens, dest3, pad_info, n_tiles * tmx, tm=tm, tmx=tmx)
    ys = moe_ffn(xs, tile_expert, n_used, w1, w3, w2, li=li, tm=tmx, tf=tf, tn=MOE_DOWN_TILE)
    return moe_combine(ys, dest3, rgate, xall, mod, final_g, tm=tm, seg=seg, last=last)


def rope_tables(seq):
    rows = seq // GRID_W
    row = jnp.repeat(jnp.arange(rows, dtype=F32), GRID_W)
    col = jnp.tile(jnp.arange(GRID_W, dtype=F32), rows)
    n_freq = HEAD_DIM // 4
    inv = ROPE_BASE ** (-jnp.arange(n_freq, dtype=F32) / n_freq)
    ang = jnp.concatenate([row[:, None] * inv, col[:, None] * inv], axis=-1)
    cos, sin = jnp.cos(ang), jnp.sin(ang)
    cos64 = jnp.concatenate([cos, cos], axis=-1)
    sin64 = jnp.concatenate([-sin, sin], axis=-1)
    return jnp.tile(cos64, (1, 2)), jnp.tile(sin64, (1, 2))


def _tile_lanes(t, width):
    reps = width // t.shape[1]
    return t if reps == 1 else jnp.concatenate([t] * reps, axis=1)


def _rope(t, cos, sin):
    w = t.shape[1]
    lane = lax.broadcasted_iota(jnp.int32, t.shape, 1) % HEAD_DIM
    half = HEAD_DIM // 2
    rot = jnp.where(lane < half, pltpu.roll(t, w - half, 1), pltpu.roll(t, half, 1))
    return t * _tile_lanes(cos, w) + rot * _tile_lanes(sin, w)


def _group_matrix(width, value):
    r = lax.broadcasted_iota(jnp.int32, (width, width), 0) // HEAD_DIM
    c = lax.broadcasted_iota(jnp.int32, (width, width), 1) // HEAD_DIM
    return jnp.where(r == c, value, 0.0).astype(BF16)


RET_CHUNK = 128


def _retention_kernel(lat_ref, ctx_ref, cos_ref, sin_ref, dec_ref, gn_ref, ol_ref, oc_ref,
                      q_s, k_s, bn_s, st_s, o_s, dm_s, *, n_ctx, n_lat):
    ch = RET_CHUNK
    nc, nl = n_ctx // ch, n_lat // ch
    lg = -jnp.exp(dec_ref[...])
    lgf, lgb = lg[0:1], lg[1:2]
    pos = lax.broadcasted_iota(jnp.int32, (ch, 1), 0).astype(F32)
    wkf = jnp.exp((ch - 1 - pos) * lgf)
    wkb = jnp.exp(pos * lgb)
    wqf = jnp.exp((pos + 1) * lgf)
    wqb = jnp.exp((ch - pos) * lgb)
    cdf = jnp.exp(ch * lgf)
    cdb = jnp.exp(ch * lgb)
    blockmask = (lax.broadcasted_iota(jnp.int32, (RET_W, RET_W), 0) // HEAD_DIM
                 == lax.broadcasted_iota(jnp.int32, (RET_W, RET_W), 1) // HEAD_DIM)
    gmean = _group_matrix(RET_W, 1.0 / HEAD_DIM)
    diff = (lax.broadcasted_iota(jnp.int32, (ch, ch), 0) - lax.broadcasted_iota(jnp.int32, (ch, ch), 1)).astype(F32)
    gn = gn_ref[...]

    def load_qkv(ref, t0):
        return (ref[pl.ds(t0, ch), 0:RET_W].astype(F32), ref[pl.ds(t0, ch), RET_W:2 * RET_W].astype(F32),
                ref[pl.ds(t0, ch), 2 * RET_W:3 * RET_W])

    def reverse_step(ref, t0, c, roped):
        q, k, v = load_qkv(ref, t0)
        if roped:
            cos, sin = cos_ref[pl.ds(t0, ch), :], sin_ref[pl.ds(t0, ch), :]
            q, k = _rope(q, cos, sin), _rope(k, cos, sin)
        k = k * (HEAD_DIM ** -0.5)
        r0 = pl.multiple_of(c * ch, ch)
        q_s[pl.ds(r0, ch), :] = q.astype(BF16)
        k_s[pl.ds(r0, ch), :] = k.astype(BF16)
        st = st_s[...]
        bn_s[c] = st.astype(BF16)
        kvb = _dot_tn((k * wkb).astype(BF16), v)
        st_s[...] = cdb * st + jnp.where(blockmask, kvb, 0.0)

    st_s[...] = jnp.zeros_like(st_s)

    def rev_ctx(n, carry):
        c = nc - 1 - n
        reverse_step(ctx_ref, pl.multiple_of(c * ch, ch), c, False)
        return carry

    def rev_lat(n, carry):
        c = nl - 1 - n
        reverse_step(lat_ref, pl.multiple_of(c * ch, ch), nc + c, True)
        return carry

    lax.fori_loop(0, nc, rev_ctx, 0)
    lax.fori_loop(0, nl, rev_lat, 0)

    for h in range(RET_HEADS):
        lf, lb = lgf[:, h * HEAD_DIM:h * HEAD_DIM + 1], lgb[:, h * HEAD_DIM:h * HEAD_DIM + 1]
        dm_s[h] = jnp.where(diff == 0.0, 2.0, jnp.exp(jnp.abs(diff) * jnp.where(diff > 0.0, lf, lb)))

    def forward_step(ref, t0, c):
        r0 = pl.multiple_of(c * ch, ch)
        q = q_s[pl.ds(r0, ch), :]
        k = k_s[pl.ds(r0, ch), :]
        v = ref[pl.ds(t0, ch), 2 * RET_W:3 * RET_W]
        sls = [slice(h * HEAD_DIM, (h + 1) * HEAD_DIM) for h in range(RET_HEADS)]
        scores = [_dot_nt(q[:, sl], k[:, sl]) for sl in sls]
        probs = [(s * dm_s[h]).astype(BF16) for h, s in enumerate(scores)]
        heads = [_dot(p, v[:, sl]) for p, sl in zip(probs, sls)]
        qf = q.astype(F32)
        st = st_s[...]
        o = jnp.concatenate(heads, axis=1)
        o_s[pl.ds(r0, ch), :] = (o + _dot((qf * wqf).astype(BF16), st.astype(BF16))
                                 + _dot((qf * wqb).astype(BF16), bn_s[c]))
        kvf = _dot_tn((k.astype(F32) * wkf).astype(BF16), v)
        st_s[...] = cdf * st + jnp.where(blockmask, kvf, 0.0)

    st_s[...] = jnp.zeros_like(st_s)

    def fwd_ctx(n, carry):
        forward_step(ctx_ref, pl.multiple_of(n * ch, ch), n)
        return carry

    def fwd_lat(n, carry):
        forward_step(lat_ref, pl.multiple_of(n * ch, ch), nc + n)
        return carry

    lax.fori_loop(0, nc, fwd_ctx, 0)
    lax.fori_loop(0, nl, fwd_lat, 0, unroll=2)

    def finish(ref, out_ref, n_rows, row_off):
        ft = 2 * ch

        def body(i, carry):
            t0 = pl.multiple_of(i * ft, ft)
            o = o_s[pl.ds(pl.multiple_of(row_off + t0, ft), ft), :]
            g = ref[pl.ds(t0, ft), 3 * RET_W:4 * RET_W].astype(F32)
            xc = o - _split_dot(o, gmean)
            var = _split_dot(xc * xc, gmean)
            out_ref[pl.ds(t0, ft), :] = (xc * lax.rsqrt(var + EPS) * gn * _silu(g)).astype(out_ref.dtype)
            return carry

        lax.fori_loop(0, n_rows // ft, body, 0)

    finish(ctx_ref, oc_ref, n_ctx, 0)
    finish(lat_ref, ol_ref, n_lat, n_ctx)


def retention_mixer(p_ret, cos, sin, dec_lanes, gn_gain, *, n_batch, n_lat, n_ctx):
    n = p_ret.shape[0]
    width = p_ret.shape[1]
    ctx_blk0 = (n_batch * n_lat) // n_ctx
    t_all = n_lat + n_ctx
    nch = t_all // RET_CHUNK
    return pl.pallas_call(
        functools.partial(_retention_kernel, n_ctx=n_ctx, n_lat=n_lat),
        out_shape=(jax.ShapeDtypeStruct((n_batch * n_lat, RET_W), BF16),
                   jax.ShapeDtypeStruct((n_batch * n_ctx, RET_W), BF16)),
        grid=(n_batch,),
        in_specs=[pl.BlockSpec((n_lat, width), lambda b: (b, 0)),
                  pl.BlockSpec((n_ctx, width), lambda b: (ctx_blk0 + b, 0)),
                  pl.BlockSpec((n_lat, LANES), lambda b: (0, 0)),
                  pl.BlockSpec((n_lat, LANES), lambda b: (0, 0)),
                  pl.BlockSpec((2, RET_W), lambda b: (0, 0)),
                  pl.BlockSpec((1, RET_W), lambda b: (0, 0))],
        out_specs=(pl.BlockSpec((n_lat, RET_W), lambda b: (b, 0)),
                   pl.BlockSpec((n_ctx, RET_W), lambda b: (b, 0))),
        scratch_shapes=[pltpu.VMEM((t_all, RET_W), BF16), pltpu.VMEM((t_all, RET_W), BF16),
                        pltpu.VMEM((nch, RET_W, RET_W), BF16), pltpu.VMEM((RET_W, RET_W), F32),
                        pltpu.VMEM((t_all, RET_W), F32), pltpu.VMEM((RET_HEADS, RET_CHUNK, RET_CHUNK), F32)],
        compiler_params=_cparams("parallel"),
        name="retention",
    )(p_ret, p_ret, cos, sin, dec_lanes, gn_gain)


GDN_CHUNK = 64
GDN_PREP = 128
GDN_HALO = 8
GDN_SOLVE_CHUNKS = 4


def _softplus(x):
    return jnp.maximum(x, 0.0) + jnp.log(1.0 + jnp.exp(-jnp.abs(x)))


def _head_block_diag(x):
    n, w = x.shape
    heads = w // HEAD_DIM
    rows = lax.broadcasted_iota(jnp.int32, (heads * n, w), 0) // n
    cols = lax.broadcasted_iota(jnp.int32, (heads * n, w), 1) // HEAD_DIM
    return jnp.where(rows == cols, jnp.concatenate([x] * heads, axis=0), 0.0)


def _unit_tri_inverses(mats):
    n, w = mats[0].shape
    ri = lax.broadcasted_iota(jnp.int32, (n, w), 0)
    ci = lax.broadcasted_iota(jnp.int32, (n, w), 1) % HEAD_DIM
    b16 = (ri // 16) == (ci // 16)
    b32 = (ri // 32) == (ci // 32)
    eye = jnp.where(ri == ci, 1.0, 0.0)
    ps = [jnp.where(b16, -a, 0.0) for a in mats]
    ts = [eye + p for p in ps]
    for _ in range(3):
        pbs = [p.astype(BF16) for p in ps]
        ps = [_dot(pb, _head_block_diag(pb)) for pb in pbs]
        ts = [t + _dot(t.astype(BF16), _head_block_diag(p.astype(BF16))) for t, p in zip(ts, ps)]
    for keep in (b32 & ~b16, ~b32):
        offs = [jnp.where(keep, a, 0.0).astype(BF16) for a in mats]
        tbs = [t.astype(BF16) for t in ts]
        mids = [_dot(tb, _head_block_diag(off)).astype(BF16) for tb, off in zip(tbs, offs)]
        ts = [t - _dot(mid, _head_block_diag(tb)) for t, mid, tb in zip(ts, mids, tbs)]
    return ts


def _gdn_prep_kernel(lat_ref, ctx_ref, abl_ref, abc_ref, cw_ref, par_ref, q_s, k_s, v_s, gb_s, *, n_ctx, n_lat):
    pt = GDN_PREP
    halo = GDN_HALO
    qkv_w = 3 * GDN_W
    gsum = _group_matrix(GDN_W, 1.0)
    lane128 = lax.broadcasted_iota(jnp.int32, (pt, LANES), 1)
    neg_a = -jnp.exp(par_ref[0:1, :])
    dt_bias = par_ref[1:2, :]
    taps = [cw_ref[j:j + 1, :] for j in range(GDN_CONV)]

    def prep(ref, ab_ref, n_rows, row_off):
        n_tiles = n_rows // pt

        def body(i, carry):
            t0 = pl.multiple_of(i * pt, pt)
            cur = ref[pl.ds(t0, pt), 0:qkv_w].astype(F32)
            p0 = pl.multiple_of(jnp.maximum(t0 - 16, 0), 16)
            n0 = pl.multiple_of(jnp.minimum(t0 + pt, n_rows - 16), 16)
            prev = ref[pl.ds(p0, 16), 0:qkv_w].astype(F32)[16 - halo:16]
            nxt = ref[pl.ds(n0, 16), 0:qkv_w].astype(F32)[0:halo]
            prev = jnp.where(i > 0, prev, 0.0)
            nxt = jnp.where(i < n_tiles - 1, nxt, 0.0)
            ext = jnp.concatenate([prev, cur, nxt], axis=0)
            rows = pt + 2 * halo
            acc = None
            for j in range(GDN_CONV):
                s = j - (GDN_CONV - 1) // 2
                sh = ext if s == 0 else pltpu.roll(ext, (rows - s) % rows, 0)
                term = sh[halo:halo + pt] * taps[j]
                acc = term if acc is None else acc + term
            act = _silu(acc)
            q, k, v = act[:, 0:GDN_W], act[:, GDN_W:2 * GDN_W], act[:, 2 * GDN_W:3 * GDN_W]
            q = q * lax.rsqrt(_split_dot(q * q, gsum) + EPS) * (HEAD_DIM ** -0.5)
            k = k * lax.rsqrt(_split_dot(k * k, gsum) + EPS)
            r0 = pl.multiple_of(row_off + t0, pt)
            q_s[pl.ds(r0, pt), :] = q.astype(BF16)
            k_s[pl.ds(r0, pt), :] = k.astype(BF16)
            v_s[pl.ds(r0, pt), :] = v.astype(BF16)
            ab = ab_ref[pl.ds(t0, pt), :]
            g = neg_a * _softplus(ab + dt_bias)
            gb_s[pl.ds(r0, pt), :] = jnp.where(lane128 < 2 * GDN_HEADS, g, _sigmoid(ab))
            return carry

        lax.fori_loop(0, n_tiles, body, 0)

    prep(ctx_ref, abc_ref, n_ctx, 0)
    prep(lat_ref, abl_ref, n_lat, n_ctx)


def _gdn_kernel(q_s, k_s, v_s, gb_s, zl_ref, zc_ref, ng_ref, ol_ref, oc_ref,
                o_s, dec_s, uf_s, ub_s, wf_s, wb_s, af_s, ab_s, qf_s, qb_s, kf_s, kb_s, sf_s, sb_s, *,
                n_ctx, n_lat, n_solve):
    u_s, w_s, a_s, qd_s, kd_s, st_s = (uf_s, ub_s), (wf_s, wb_s), (af_s, ab_s), (qf_s, qb_s), (kf_s, kb_s), (sf_s, sb_s)
    ch = GDN_CHUNK
    pt = GDN_PREP
    gmean = _group_matrix(GDN_W, 1.0 / HEAD_DIM)
    ri = lax.broadcasted_iota(jnp.int32, (ch, ch), 0)
    ci = lax.broadcasted_iota(jnp.int32, (ch, ch), 1)
    tri = tuple(jnp.where(m, 1.0, 0.0).astype(BF16) for m in (ri >= ci, ri <= ci))
    rp = lax.broadcasted_iota(jnp.int32, (ch, GDN_W), 0)
    cp = lax.broadcasted_iota(jnp.int32, (ch, GDN_W), 1) % HEAD_DIM
    incl = (rp >= cp, rp <= cp)
    strict = (rp > cp, rp < cp)
    ncc, nlc = n_ctx // ch, n_lat // ch
    head_blocks = (lax.broadcasted_iota(jnp.int32, (GDN_W, GDN_W), 0) // HEAD_DIM
                   == lax.broadcasted_iota(jnp.int32, (GDN_W, GDN_W), 1) // HEAD_DIM)

    def solve_chunks(cs):
        n_cs = len(cs)
        items = [(j, d) for j in range(n_cs) for d in range(2)]
        rows = [pl.ds(pl.multiple_of(c * ch, ch), ch) for c in cs]
        gbc = [gb_s[r, :] for r in rows]
        gcs = []
        for j in range(n_cs):
            per_dir = []
            for d in range(2):
                rem, acc = gbc[j], None
                for _ in range(3):
                    part = rem.astype(BF16)
                    term = _dot(tri[d], part)
                    acc = term if acc is None else acc + term
                    rem = rem - part.astype(F32)
                per_dir.append(acc)
            gcs.append(per_dir)
        gcs_t = [[g.T for g in per_dir] for per_dir in gcs]
        edge = (ch - 1, 0)

        def spread(mat, first):
            return jnp.concatenate([jnp.broadcast_to(mat[:, first + h:first + h + 1], (ch, HEAD_DIM))
                                    for h in range(GDN_HEADS)], axis=1)

        gc = [spread(gcs[j][d], d * GDN_HEADS) for j, d in items]
        gr = [jnp.concatenate([gcs_t[j][d][d * GDN_HEADS + h:d * GDN_HEADS + h + 1, :] for h in range(GDN_HEADS)],
                              axis=1) for j, d in items]
        gtot = [x[edge[d]:edge[d] + 1, :] for x, (_, d) in zip(gc, items)]
        beta = [spread(gbc[j], 2 * GDN_HEADS + d * GDN_HEADS) for j, d in items]
        decay = [jnp.where(incl[d], jnp.exp(jnp.minimum(gc[i] - gr[i], 0.0)), 0.0) for i, (_, d) in enumerate(items)]
        q = [q_s[r, :] for r in rows]
        k = [k_s[r, :] for r in rows]
        kf = [x.astype(F32) for x in k]
        v = [v_s[r, :].astype(F32) for r in rows]
        k_rows = [_head_block_diag(x) for x in k]
        qk = [_dot_nt(x, y) for x, y in zip(q, k_rows)]
        kb = [kf[j] * beta[i] for i, (j, _) in enumerate(items)]
        kk = [_dot_nt(kb[i].astype(BF16), k_rows[j]) for i, (j, _) in enumerate(items)]
        a = [jnp.where(strict[d], kk[i] * decay[i], 0.0) for i, (_, d) in enumerate(items)]
        t = [x.astype(BF16) for x in _unit_tri_inverses(a)]
        eg = [jnp.exp(x) for x in gc]
        u = [_dot(t[i], _head_block_diag((v[j] * beta[i]).astype(BF16))) for i, (j, _) in enumerate(items)]
        w = [_dot(t[i], _head_block_diag((kb[i] * eg[i]).astype(BF16))) for i in range(len(items))]
        for i, (j, d) in enumerate(items):
            u_s[d][rows[j], :] = u[i].astype(BF16)
            w_s[d][rows[j], :] = w[i].astype(BF16)
            a_s[d][rows[j], :] = (qk[j] * decay[i]).astype(BF16)
            qd_s[d][rows[j], :] = (q[j].astype(F32) * eg[i]).astype(BF16)
            kd_s[d][rows[j], :] = (kf[j] * jnp.exp(gtot[i] - gc[i])).astype(BF16)
            dec_s[pl.ds(cs[j], 1), d * GDN_W:(d + 1) * GDN_W] = jnp.exp(gtot[i])

    def scan_step(cf, cb):
        cs = (cf, cb)
        rows = [pl.ds(pl.multiple_of(c * ch, ch), ch) for c in cs]
        s = [st[...] for st in st_s]
        lhs = [jnp.concatenate([w_s[d][rows[d], :], qd_s[d][rows[d], :]], axis=0) for d in range(2)]
        prod = [_dot(lhs[d], s[d].astype(BF16)) for d in range(2)]
        vb = [(u_s[d][rows[d], :].astype(F32) - prod[d][0:ch]).astype(BF16) for d in range(2)]
        vbd = [jnp.where(head_blocks, jnp.concatenate([x] * GDN_HEADS, axis=0), 0.0) for x in vb]
        av = [_dot(a_s[d][rows[d], :], vbd[d]) for d in range(2)]
        upd = [_dot_tn(kd_s[d][rows[d], :], vb[d]) for d in range(2)]
        for d in range(2):
            dec = dec_s[pl.ds(cs[d], 1), d * GDN_W:(d + 1) * GDN_W]
            st_s[d][...] = s[d] * dec + jnp.where(head_blocks, upd[d], 0.0)
        for d in range(2):
            o_s[rows[d], :] += prod[d][ch:2 * ch] + av[d]

    def solve_body(n, carry):
        solve_chunks([n * n_solve + j for j in range(n_solve)])
        return carry

    lax.fori_loop(0, (ncc + nlc) // n_solve, solve_body, 0)

    for st in st_s:
        st[...] = jnp.zeros_like(st)
    o_s[...] = jnp.zeros_like(o_s)

    def scan_ctx(n, carry):
        scan_step(n, ncc - 1 - n)
        return carry

    def scan_lat(n, carry):
        scan_step(ncc + n, ncc + nlc - 1 - n)
        return carry

    lax.fori_loop(0, ncc, scan_ctx, 0)
    lax.fori_loop(0, nlc, scan_lat, 0, unroll=2)

    ng = ng_ref[...]

    def finish(z_ref, out_ref, n_rows, row_off):
        def body(i, carry):
            t0 = pl.multiple_of(i * pt, pt)
            o = o_s[pl.ds(pl.multiple_of(row_off + t0, pt), pt), :]
            z = z_ref[pl.ds(t0, pt), :].astype(F32)
            y = o * lax.rsqrt(_split_dot(o * o, gmean) + EPS) * ng * _silu(z)
            out_ref[pl.ds(t0, pt), :] = y.astype(out_ref.dtype)
            return carry

        lax.fori_loop(0, n_rows // pt, body, 0)

    finish(zc_ref, oc_ref, n_ctx, 0)
    finish(zl_ref, ol_ref, n_lat, n_ctx)


def gdn_mixer(p_gdn, p_ab, conv_w, a_log, dt_bias, norm_gain, *, n_batch, n_lat, n_ctx):
    width = p_gdn.shape[1]
    ctx_blk0 = (n_batch * n_lat) // n_ctx
    t_all = n_lat + n_ctx
    cw = jnp.pad(conv_w.astype(F32), ((0, 8 - GDN_CONV), (0, 0)))
    par = jnp.zeros((8, LANES), F32)
    par = par.at[0, :2 * GDN_HEADS].set(a_log.reshape(-1)).at[1, :2 * GDN_HEADS].set(dt_bias.reshape(-1))
    ng = jnp.tile(norm_gain.astype(F32).reshape(1, HEAD_DIM), (1, GDN_HEADS))
    seq = lambda w: pl.BlockSpec((t_all, w), lambda b: (b, 0))
    q, k, v, gb = pl.pallas_call(
        functools.partial(_gdn_prep_kernel, n_ctx=n_ctx, n_lat=n_lat),
        out_shape=(jax.ShapeDtypeStruct((n_batch * t_all, GDN_W), BF16),) * 3
        + (jax.ShapeDtypeStruct((n_batch * t_all, LANES), F32),),
        grid=(n_batch,),
        in_specs=[pl.BlockSpec((n_lat, width), lambda b: (b, 0)),
                  pl.BlockSpec((n_ctx, width), lambda b: (ctx_blk0 + b, 0)),
                  pl.BlockSpec((n_lat, LANES), lambda b: (b, 0)),
                  pl.BlockSpec((n_ctx, LANES), lambda b: (ctx_blk0 + b, 0)),
                  pl.BlockSpec(cw.shape, lambda b: (0, 0)),
                  pl.BlockSpec(par.shape, lambda b: (0, 0))],
        out_specs=(seq(GDN_W), seq(GDN_W), seq(GDN_W), seq(LANES)),
        compiler_params=_cparams("parallel"),
        name="gdn_prep",
    )(p_gdn, p_gdn, p_ab, p_ab, cw, par)
    zcol = 3 * GDN_W // GDN_W
    once = lambda w: pl.BlockSpec((t_all, w), lambda b: (b, 0), pipeline_mode=pl.Buffered(1))
    n_solve = GDN_SOLVE_CHUNKS if (t_all // GDN_CHUNK) % GDN_SOLVE_CHUNKS == 0 else 1
    return pl.pallas_call(
        functools.partial(_gdn_kernel, n_ctx=n_ctx, n_lat=n_lat, n_solve=n_solve),
        out_shape=(jax.ShapeDtypeStruct((n_batch * n_lat, GDN_W), BF16),
                   jax.ShapeDtypeStruct((n_batch * n_ctx, GDN_W), BF16)),
        grid=(n_batch,),
        in_specs=[once(GDN_W), once(GDN_W), once(GDN_W), once(LANES),
                  pl.BlockSpec((n_lat, GDN_W), lambda b: (b, zcol), pipeline_mode=pl.Buffered(1)),
                  pl.BlockSpec((n_ctx, GDN_W), lambda b: (ctx_blk0 + b, zcol)),
                  pl.BlockSpec((1, GDN_W), lambda b: (0, 0))],
        out_specs=(pl.BlockSpec((n_lat, GDN_W), lambda b: (b, 0)),
                   pl.BlockSpec((n_ctx, GDN_W), lambda b: (b, 0))),
        scratch_shapes=[pltpu.VMEM((t_all, GDN_W), F32), pltpu.VMEM((t_all // GDN_CHUNK, 2 * GDN_W), F32)]
        + [pltpu.VMEM((t_all, GDN_W), BF16)] * 10
        + [pltpu.VMEM((GDN_W, GDN_W), F32)] * 2,
        compiler_params=_cparams("parallel"),
        name="gdn",
    )(q, k, v, gb, p_gdn, p_gdn, ng)


SWA_BLOCK = 128
SWA_BAND = 3 * SWA_BLOCK
SWA_Q_PER_STEP = 4


def _swa_kernel(sink_ref, q_ref, kl_ref, vl_ref, kc_ref, vc_ref, cos_ref, sin_ref, o_ref, *, band, n_lat, n_sub):
    blk = SWA_BLOCK
    kc = kc_ref[...]
    vc = vc_ref[...]
    for sub in range(n_sub):
        rows = slice(sub * blk, (sub + 1) * blk)
        o_ref[rows, :] = _swa_block(pl.program_id(1) * n_sub + sub, q_ref[rows, :].astype(F32), sink_ref, kl_ref,
                                    vl_ref, kc, vc, cos_ref, sin_ref, band=band, n_lat=n_lat).astype(o_ref.dtype)


def _swa_block(i, q, sink_ref, kl_ref, vl_ref, kc, vc, cos_ref, sin_ref, *, band, n_lat):
    blk = SWA_BLOCK
    if band:
        q0 = pl.multiple_of(i * blk, blk)
        start = pl.multiple_of(jnp.clip((i - 1) * blk, 0, n_lat - SWA_BAND), blk)
        q = _rope(q, cos_ref[pl.ds(q0, blk), :], sin_ref[pl.ds(q0, blk), :])
        kb = _rope(kl_ref[pl.ds(start, SWA_BAND), :].astype(F32),
                   cos_ref[pl.ds(start, SWA_BAND), :], sin_ref[pl.ds(start, SWA_BAND), :]).astype(BF16)
        vb = vl_ref[pl.ds(start, SWA_BAND), :]
        keys = jnp.concatenate([kb, kc], axis=0)
        vals = jnp.concatenate([vb, vc], axis=0)
    else:
        keys, vals = kc, vc
    qb = (q * (HEAD_DIM ** -0.5 * LOG2E)).astype(BF16)
    n_keys = keys.shape[0]
    if band:
        q_pos = q0 + lax.broadcasted_iota(jnp.int32, (blk, 1), 0)
        col_id = lax.broadcasted_iota(jnp.int32, (1, n_keys), 1)
        valid = (col_id >= SWA_BAND) | (jnp.abs(start + col_id - q_pos) <= WINDOW)
        bias = jnp.where(valid, 0.0, NEG_BIG)
    v_lane = lax.broadcasted_iota(jnp.int32, vals.shape, 1) // HEAD_DIM
    outs = []
    for g in range(SWA_KV_HEADS):
        ksl = slice(g * HEAD_DIM, (g + 1) * HEAD_DIM)
        v_ext = jnp.where(v_lane == g, vals, 1.0)
        den_lane = (1 - g) * HEAD_DIM
        q4 = jnp.concatenate([qb[:, (g * SWA_GROUP + r) * HEAD_DIM:(g * SWA_GROUP + r + 1) * HEAD_DIM]
                              for r in range(SWA_GROUP)], axis=0)
        s = _dot_nt(q4, keys[:, ksl])
        es, ms, sinks = [], [], []
        for r in range(SWA_GROUP):
            sr = s[r * blk:(r + 1) * blk]
            if band:
                sr = sr + bias
            sink2 = sink_ref[g * SWA_GROUP + r] * LOG2E
            m = jnp.maximum(jnp.max(sr, axis=-1, keepdims=True), sink2)
            es.append(jnp.exp2(sr - m).astype(BF16))
            ms.append(m)
            sinks.append(sink2)
        ov = _dot(jnp.concatenate(es, axis=0), v_ext)
        for r in range(SWA_GROUP):
            ovr = ov[r * blk:(r + 1) * blk]
            den = ovr[:, den_lane:den_lane + 1] + jnp.exp2(sinks[r] - ms[r])
            outs.append(ovr[:, ksl] * (1.0 / den))
    return jnp.concatenate(outs, axis=1)


def swa_mixer(p_swa, cos, sin, sink, *, n_batch, n_lat, n_ctx):
    blk = SWA_BLOCK
    ctx_row0 = n_batch * n_lat
    kcol, vcol = SWA_W // SWA_KV_W, SWA_W // SWA_KV_W + 1
    sink = sink.astype(F32)

    def call(band):
        n_q = n_lat if band else n_ctx
        n_sub = min(SWA_Q_PER_STEP, n_q // blk)
        qblk = n_sub * blk
        nb = n_q // qblk
        qrow0 = 0 if band else ctx_row0 // qblk
        grid_spec = pltpu.PrefetchScalarGridSpec(
            num_scalar_prefetch=1,
            grid=(n_batch, nb),
            in_specs=[pl.BlockSpec((qblk, SWA_W), lambda b, i, s: (qrow0 + b * nb + i, 0)),
                      pl.BlockSpec((n_lat, SWA_KV_W), lambda b, i, s: (b, kcol)),
                      pl.BlockSpec((n_lat, SWA_KV_W), lambda b, i, s: (b, vcol)),
                      pl.BlockSpec((n_ctx, SWA_KV_W), lambda b, i, s: (ctx_row0 // n_ctx + b, kcol)),
                      pl.BlockSpec((n_ctx, SWA_KV_W), lambda b, i, s: (ctx_row0 // n_ctx + b, vcol)),
                      pl.BlockSpec((n_lat, LANES), lambda b, i, s: (0, 0)),
                      pl.BlockSpec((n_lat, LANES), lambda b, i, s: (0, 0))],
            out_specs=pl.BlockSpec((qblk, SWA_W), lambda b, i, s: (b * nb + i, 0)),
        )
        return pl.pallas_call(
            functools.partial(_swa_kernel, band=band, n_lat=n_lat, n_sub=n_sub),
            out_shape=jax.ShapeDtypeStruct((n_batch * n_q, SWA_W), BF16),
            grid_spec=grid_spec,
            compiler_params=_cparams("parallel", "arbitrary"),
            name="swa_latent" if band else "swa_context",
        )(sink, p_swa, p_swa, p_swa, p_swa, p_swa, cos, sin)

    return call(True), call(False)


MOE_ROW_TILE = 512
MOE_FF_TILE = 1792
MOE_DOWN_TILE = 1024


def _row_tile(n_lat, n_ctx_rows):
    for tm in (512, 256, 128):
        if n_lat % tm == 0 and n_ctx_rows % tm == 0:
            return tm
    raise ValueError("sequence lengths must be multiples of 128")


def kernel(x, c, ctx, c_ctx, ada_w, ada_b, norm1_g, w_in, ret_decay, ret_gn_g, gdn_conv_w, gdn_a_log, gdn_dt_bias,
           gdn_norm_g, swa_sink, w_out, norm2_g, ffn_w1, ffn_w3, ffn_w2, router_w, moe_w1, moe_w3, moe_w2, final_g):
    b, s, d = x.shape
    n_ctx = ctx.shape[1]
    depth = ada_w.shape[0]
    nl, ncx = b * s, b * n_ctx
    tm = _row_tile(s, ncx)
    seg = _seg_map(nl // tm, s // tm, b)

    cvec = jnp.zeros((8, d), F32).at[:b].set(c).at[b].set(c_ctx)
    mod_all = adaln(cvec, ada_w, ada_b).reshape(depth, 8, 6, d)
    xall = jnp.concatenate([x.reshape(nl, d), ctx.reshape(ncx, d)], axis=0)
    cos, sin = rope_tables(s)
    fg = final_g.reshape(1, d)

    ret_cols, gdn_cols = 4 * RET_W, 4 * GDN_W
    ab_cols = 4 * GDN_HEADS
    for layer in range(depth):
        last = layer == depth - 1
        mod = mod_all[layer]
        w = w_in[layer].astype(BF16)
        w_ret = w[:, :ret_cols]
        w_gdn = w[:, ret_cols:ret_cols + gdn_cols]
        w_ab = jnp.pad(w[:, ret_cols + gdn_cols:ret_cols + gdn_cols + ab_cols], ((0, 0), (0, LANES - ab_cols)))
        w_swa = w[:, ret_cols + gdn_cols + ab_cols:]
        p_ret, p_gdn, p_swa, p_ab = in_proj(xall, mod, norm1_g[layer].reshape(1, d), w_ret, w_gdn, w_swa, w_ab,
                                            tm=tm, seg=seg)
        dec_lanes = jnp.repeat(ret_decay[layer].astype(F32), HEAD_DIM, axis=1)
        o_ret = retention_mixer(p_ret, cos, sin, dec_lanes, ret_gn_g[layer].reshape(1, RET_W),
                                n_batch=b, n_lat=s, n_ctx=n_ctx)
        o_gdn = gdn_mixer(p_gdn, p_ab, gdn_conv_w[layer], gdn_a_log[layer], gdn_dt_bias[layer], gdn_norm_g[layer],
                          n_batch=b, n_lat=s, n_ctx=n_ctx)
        o_swa = swa_mixer(p_swa, cos, sin, swa_sink[layer], n_batch=b, n_lat=s, n_ctx=n_ctx)

        n_rows = nl if last else nl + ncx
        is_moe = layer % 2 == 1
        i = layer // 2
        rw = None
        if is_moe:
            rw = jnp.pad(router_w[i].astype(F32), ((0, 0), (0, LANES - N_EXPERTS)))
        res = out_proj(o_ret, o_gdn, o_swa, xall, mod, w_out[layer].astype(BF16), norm2_g[layer].reshape(1, d), rw,
                       tm=tm, seg=seg, n_rows=n_rows, n_lat_rows=nl, tok_dtype=F32 if is_moe else BF16)
        if is_moe:
            x_new, tokens, ridx, rgate, cnt = res
            xall = moe_layer(tokens, ridx, rgate, cnt, x_new, mod, moe_w1, moe_w3, moe_w2, fg,
                             li=i, tm=tm, tmx=MOE_ROW_TILE, tf=MOE_FF_TILE, seg=seg, last=last)
        else:
            x_new, tokens = res
            xall = dense_ffn(tokens, x_new, mod, ffn_w1[i].astype(BF16), ffn_w3[i].astype(BF16),
                             ffn_w2[i].astype(BF16), fg, tm=tm, seg=seg, last=last)
    return xall[:nl].reshape(b, s, d)
```

```python
import functools

import jax
import jax.numpy as jnp
import numpy as np
from jax import lax
from jax.experimental import pallas as pl
from jax.experimental.pallas import tpu as pltpu

F32 = jnp.float32
BF16 = jnp.bfloat16

HEAD_DIM = 64
GRID_W = 64
ROPE_BASE = 10000.0
EPS = 1e-6
RET_HEADS = 4
GDN_HEADS = 4
GDN_CONV = 5
SWA_HEADS = 8
SWA_KV_HEADS = 2
SWA_GROUP = SWA_HEADS // SWA_KV_HEADS
WINDOW = 128
N_EXPERTS = 8
RET_W = RET_HEADS * HEAD_DIM
GDN_W = GDN_HEADS * HEAD_DIM
SWA_W = SWA_HEADS * HEAD_DIM
SWA_KV_W = SWA_KV_HEADS * HEAD_DIM
LANES = 128
VMEM_LIMIT = 56 * 1024 * 1024
NEG_BIG = -1e30
LOG2E = 1.4426950408889634


def _cparams(*sem):
    return pltpu.CompilerParams(dimension_semantics=sem, vmem_limit_bytes=VMEM_LIMIT)


def _dot(a, b):
    return jnp.dot(a, b, preferred_element_type=F32)


def _dot_nt(a, b):
    return lax.dot_general(a, b, (((1,), (1,)), ((), ())), preferred_element_type=F32)


def _dot_tn(a, b):
    return lax.dot_general(a, b, (((0,), (0,)), ((), ())), preferred_element_type=F32)


def _split_dot(x, m_bf16, terms=2):
    acc = None
    rem = x
    for _ in range(terms):
        part = rem.astype(BF16)
        d = _dot(part, m_bf16)
        acc = d if acc is None else acc + d
        rem = rem - part.astype(F32)
    return acc


def _sigmoid(x):
    return 1.0 / (1.0 + jnp.exp(-x))


def _silu(x):
    return x * _sigmoid(x)


def _rms_mod(x, gain, shift, scale):
    ms = jnp.mean(x * x, axis=-1, keepdims=True)
    y = x * lax.rsqrt(ms + EPS) * gain
    return y * (1.0 + scale) + shift


def _seg_map(n_lat_tiles, tiles_per_batch, n_batch):
    def seg(i):
        return jnp.where(i < n_lat_tiles, i // tiles_per_batch, n_batch)
    return seg


def _adaln_kernel(c_ref, w_ref, b_ref, o_ref):
    h = _silu(c_ref[...]).astype(BF16)
    o_ref[0] = _dot(h, w_ref[0].astype(BF16)) + b_ref[0]


def adaln(cvec, ada_w, ada_b):
    depth, d, n6 = ada_w.shape
    tn = 1536 if n6 % 1536 == 0 else n6
    rows = cvec.shape[0]
    return pl.pallas_call(
        _adaln_kernel,
        out_shape=jax.ShapeDtypeStruct((depth, rows, n6), F32),
        grid=(depth, n6 // tn),
        in_specs=[pl.BlockSpec((rows, d), lambda l, j: (0, 0)),
                  pl.BlockSpec((1, d, tn), lambda l, j: (l, 0, j)),
                  pl.BlockSpec((1, 1, tn), lambda l, j: (l, 0, j))],
        out_specs=pl.BlockSpec((1, rows, tn), lambda l, j: (l, 0, j)),
        compiler_params=_cparams("parallel", "parallel"),
        name="adaln",
    )(cvec, ada_w, ada_b.reshape(depth, 1, n6))


def _stream_rows(refs, n_streams, n_lat_tiles):
    if n_streams == 1:
        return refs[0][...]
    return jnp.where(pl.program_id(0) < n_lat_tiles, refs[0][...], refs[1][...])


def _stream_specs(x_rows, tm, d, n_lat_tiles):
    if not isinstance(x_rows, tuple):
        return [pl.BlockSpec((tm, d), lambda i: (i, 0))], [x_rows]
    return ([pl.BlockSpec((tm, d), lambda i: (jnp.minimum(i, n_lat_tiles - 1), 0)),
             pl.BlockSpec((tm, d), lambda i: (jnp.maximum(i - n_lat_tiles, 0), 0))], list(x_rows))


def _inproj_kernel(*refs, n_streams, n_lat_tiles):
    mod_ref, g_ref, wr_ref, wg_ref, ws_ref, wab_ref, pr_ref, pg_ref, ps_ref, pab_ref = refs[n_streams:]
    x = _stream_rows(refs, n_streams, n_lat_tiles)
    h = _rms_mod(x, g_ref[...], mod_ref[0, 0:1, :], mod_ref[0, 1:2, :]).astype(BF16)
    for w_ref, o_ref in ((wr_ref, pr_ref), (wg_ref, pg_ref), (ws_ref, ps_ref), (wab_ref, pab_ref)):
        n = w_ref.shape[1]
        step = 256 if n % 256 == 0 else n
        for c in range(0, n, step):
            o_ref[:, c:c + step] = _dot(h, w_ref[:, c:c + step]).astype(o_ref.dtype)


def in_proj(x_rows, mod, gain, w_ret, w_gdn, w_swa, w_ab, *, tm, seg, n, n_lat_rows):
    d = mod.shape[-1]
    full = lambda i: (0, 0)
    row = lambda i: (i, 0)
    x_specs, x_args = _stream_specs(x_rows, tm, d, n_lat_rows // tm)
    outs = (jax.ShapeDtypeStruct((n, w_ret.shape[1]), BF16), jax.ShapeDtypeStruct((n, w_gdn.shape[1]), BF16),
            jax.ShapeDtypeStruct((n, w_swa.shape[1]), BF16), jax.ShapeDtypeStruct((n, w_ab.shape[1]), F32))
    return pl.pallas_call(
        functools.partial(_inproj_kernel, n_streams=len(x_args), n_lat_tiles=n_lat_rows // tm),
        out_shape=outs,
        grid=(n // tm,),
        in_specs=x_specs + [
                  pl.BlockSpec((1, 6, d), lambda i: (seg(i), 0, 0)),
                  pl.BlockSpec((1, d), full),
                  pl.BlockSpec(w_ret.shape, full), pl.BlockSpec(w_gdn.shape, full),
                  pl.BlockSpec(w_swa.shape, full), pl.BlockSpec(w_ab.shape, full)],
        out_specs=tuple(pl.BlockSpec((tm, o.shape[1]), row) for o in outs),
        compiler_params=_cparams("parallel"),
        name="in_proj",
    )(*x_args, mod, gain, w_ret, w_gdn, w_swa, w_ab)


def _outproj_kernel(*refs, tm, moe, n_lat_tiles, has_ctx, n_streams):
    n_mix = 6 if has_ctx else 3
    mix = refs[:n_mix]
    x_in = _stream_rows(refs[n_mix:], n_streams, n_lat_tiles)
    refs = refs[n_mix + n_streams:]
    if moe:
        (mod_ref, w_ref, g_ref, rw_ref, xo_ref, tok_ref, ridx_ref, rgate_ref, cnt_ref, base_ref,
         earlier_ref) = refs
    else:
        mod_ref, w_ref, g_ref, xo_ref, tok_ref = refs
    if has_ctx:
        is_lat = pl.program_id(0) < n_lat_tiles
        o_ret, o_gdn, o_swa = (jnp.where(is_lat, mix[2 * k][...], mix[2 * k + 1][...]) for k in range(3))
    else:
        o_ret, o_gdn, o_swa = (m[...] for m in mix)
    y = _dot(o_ret, w_ref[0:RET_W, :])
    y = y + _dot(o_gdn, w_ref[RET_W:RET_W + GDN_W, :])
    y = y + _dot(o_swa, w_ref[RET_W + GDN_W:, :])
    x = x_in + mod_ref[0, 2:3, :] * y
    xo_ref[...] = x
    t = _rms_mod(x, g_ref[...], mod_ref[0, 3:4, :], mod_ref[0, 4:5, :])
    tok_ref[...] = t.astype(tok_ref.dtype)
    if not moe:
        return

    @pl.when(pl.program_id(0) == 0)
    def _():
        base_ref[...] = jnp.zeros_like(base_ref)
        r_i = lax.broadcasted_iota(jnp.int32, (tm, tm), 0)
        c_i = lax.broadcasted_iota(jnp.int32, (tm, tm), 1)
        earlier_ref[...] = jnp.where(r_i < c_i, 1.0, 0.0).astype(BF16)

    rw = rw_ref[...]
    rw_hi = rw.astype(BF16)
    rw_lo = (rw - rw_hi.astype(F32)).astype(BF16)
    t_hi = t.astype(BF16)
    t_lo = (t - t_hi.astype(F32)).astype(BF16)
    logits = _dot(t_hi, rw_hi) + _dot(t_hi, rw_lo) + _dot(t_lo, rw_hi)
    lt = logits.T[0:N_EXPERTS, :]
    e_id = lax.broadcasted_iota(jnp.int32, lt.shape, 0)
    m1 = jnp.max(lt, axis=0, keepdims=True)
    i1 = jnp.min(jnp.where(lt == m1, e_id, N_EXPERTS), axis=0, keepdims=True)
    rest = jnp.where(e_id == i1, NEG_BIG, lt)
    m2 = jnp.max(rest, axis=0, keepdims=True)
    i2 = jnp.min(jnp.where(rest == m2, e_id, N_EXPERTS), axis=0, keepdims=True)
    e2 = jnp.exp(m2 - m1)
    g1 = 1.0 / (1.0 + e2)
    g2 = e2 * g1
    oh1 = jnp.where(e_id == i1, 1.0, 0.0)
    oh2 = jnp.where(e_id == i2, 1.0, 0.0)
    cum = _dot(jnp.concatenate([oh1, oh2], axis=0).astype(BF16), earlier_ref[...])
    cum1, cum2 = cum[0:N_EXPERTS], cum[N_EXPERTS:2 * N_EXPERTS]
    cnt1 = jnp.sum(oh1, axis=1, keepdims=True)
    cnt2 = jnp.sum(oh2, axis=1, keepdims=True)
    base = base_ref[:, 0:1]
    rank1 = jnp.sum(oh1 * (base + cum1), axis=0, keepdims=True)
    rank2 = jnp.sum(oh2 * (base + cnt1 + cum2), axis=0, keepdims=True)
    base = base + cnt1 + cnt2
    base_ref[...] = jnp.broadcast_to(base, base_ref.shape)
    cnt_ref[...] = jnp.broadcast_to(base, cnt_ref.shape).astype(jnp.int32)
    row8 = lax.broadcasted_iota(jnp.int32, (8, tm), 0)
    r1 = rank1.astype(jnp.int32)
    r2 = rank2.astype(jnp.int32)
    ridx_ref[...] = jnp.where(row8 == 0, i1, jnp.where(row8 == 1, i2, jnp.where(row8 == 2, r1,
                              jnp.where(row8 == 3, r2, 0))))
    rgate_ref[...] = jnp.where(row8 == 0, g1, jnp.where(row8 == 1, g2, 0.0))


def out_proj(o_ret, o_gdn, o_swa, x_rows, mod, w_out, gain2, router_w, *, tm, seg, n_rows, n_lat_rows, tok_dtype):
    d = mod.shape[-1]
    moe = router_w is not None
    has_ctx = n_rows > n_lat_rows
    n_lat_tiles = n_lat_rows // tm
    full = lambda i: (0, 0)
    row = lambda i: (i, 0)
    lat_row = lambda i: (jnp.minimum(i, n_lat_tiles - 1), 0)
    ctx_row = lambda i: (jnp.maximum(i - n_lat_tiles, 0), 0)
    in_specs, args = [], []
    for pair, width in ((o_ret, RET_W), (o_gdn, GDN_W), (o_swa, SWA_W)):
        in_specs.append(pl.BlockSpec((tm, width), lat_row))
        args.append(pair[0])
        if has_ctx:
            in_specs.append(pl.BlockSpec((tm, width), ctx_row))
            args.append(pair[1])
    x_specs, x_args = _stream_specs(x_rows, tm, d, n_lat_tiles)
    in_specs += x_specs + [pl.BlockSpec((1, 6, d), lambda i: (seg(i), 0, 0)),
                           pl.BlockSpec(w_out.shape, full), pl.BlockSpec((1, d), full)]
    args += x_args + [mod, w_out, gain2]
    outs = [jax.ShapeDtypeStruct((n_rows, d), F32), jax.ShapeDtypeStruct((n_rows, d), tok_dtype)]
    out_specs = [pl.BlockSpec((tm, d), row), pl.BlockSpec((tm, d), row)]
    scratch = []
    if moe:
        in_specs.append(pl.BlockSpec(router_w.shape, full))
        args.append(router_w)
        col = lambda i: (0, i)
        outs += [jax.ShapeDtypeStruct((8, n_rows), jnp.int32), jax.ShapeDtypeStruct((8, n_rows), F32),
                 jax.ShapeDtypeStruct((N_EXPERTS, LANES), jnp.int32)]
        out_specs += [pl.BlockSpec((8, tm), col), pl.BlockSpec((8, tm), col), pl.BlockSpec((N_EXPERTS, LANES), full)]
        scratch = [pltpu.VMEM((N_EXPERTS, LANES), F32), pltpu.VMEM((tm, tm), BF16)]
    return pl.pallas_call(
        functools.partial(_outproj_kernel, tm=tm, moe=moe, n_lat_tiles=n_lat_tiles, has_ctx=has_ctx,
                          n_streams=len(x_args)),
        out_shape=tuple(outs),
        grid=(n_rows // tm,),
        in_specs=in_specs,
        out_specs=tuple(out_specs),
        scratch_shapes=scratch,
        compiler_params=_cparams("arbitrary"),
        name="out_proj_router" if moe else "out_proj",
    )(*args)


def _final_norm(x, gain):
    ms = jnp.mean(x * x, axis=-1, keepdims=True)
    return x * lax.rsqrt(ms + EPS) * gain


def _dense_ffn_kernel(tok_ref, x_ref, mod_ref, w1_ref, w3_ref, w2_ref, fg_ref, o_ref, *, last):
    t = tok_ref[...]
    h = (_silu(_dot(t, w1_ref[...])) * _dot(t, w3_ref[...])).astype(BF16)
    x = x_ref[...] + mod_ref[0, 5:6, :] * _dot(h, w2_ref[...])
    if last:
        x = _final_norm(x, fg_ref[...])
    o_ref[...] = x


def dense_ffn(tokens, xall, mod, w1, w3, w2, final_g, *, tm, seg, last):
    n, d = tokens.shape
    f = w1.shape[1]
    full = lambda i: (0, 0)
    row = lambda i: (i, 0)
    return pl.pallas_call(
        functools.partial(_dense_ffn_kernel, last=last),
        out_shape=jax.ShapeDtypeStruct((n, d), F32),
        grid=(n // tm,),
        in_specs=[pl.BlockSpec((tm, d), row), pl.BlockSpec((tm, d), row),
                  pl.BlockSpec((1, 6, d), lambda i: (seg(i), 0, 0)),
                  pl.BlockSpec((d, f), full, pipeline_mode=pl.Buffered(1)),
                  pl.BlockSpec((d, f), full, pipeline_mode=pl.Buffered(1)),
                  pl.BlockSpec((f, d), full, pipeline_mode=pl.Buffered(1)),
                  pl.BlockSpec((1, d), full)],
        out_specs=pl.BlockSpec((tm, d), row),
        compiler_params=_cparams("parallel"),
        name="dense_ffn",
    )(tokens, xall, mod, w1, w3, w2, final_g)


def _dispatch_kernel(pad_ref, dest_ref, tok_ref, xs_ref, zero_s, sem, zsem, *, tm, tmx):
    @pl.when(pl.program_id(0) == 0)
    def _():
        zero_s[...] = jnp.zeros_like(zero_s)

        def for_each_zero_copy(fn):
            for e in range(N_EXPERTS):
                @pl.when(pad_ref[N_EXPERTS + e] > 0)
                def _(e=e):
                    row = pl.multiple_of(pad_ref[e], tmx)
                    fn(pltpu.make_async_copy(zero_s, xs_ref.at[pl.ds(row, tmx)], zsem))
            for t in range(N_EXPERTS):
                @pl.when(t < pad_ref[2 * N_EXPERTS + 1])
                def _(t=t):
                    row = pl.multiple_of(pad_ref[2 * N_EXPERTS] + t * tmx, tmx)
                    fn(pltpu.make_async_copy(zero_s, xs_ref.at[pl.ds(row, tmx)], zsem))

        for_each_zero_copy(lambda c: c.start())
        for_each_zero_copy(lambda c: c.wait())

    def issue(r, carry):
        for k in range(2):
            dst = dest_ref[0, 0, 2 * r + k]
            pltpu.make_async_copy(tok_ref.at[pl.ds(r, 1)], xs_ref.at[pl.ds(dst, 1)], sem).start()
        return carry

    lax.fori_loop(0, tm, issue, 0, unroll=16)
    for _ in range(2):
        pltpu.make_async_copy(tok_ref, xs_ref.at[pl.ds(0, tm)], sem).wait()


def moe_dispatch(tokens, dest3, pad_info, n_slots, *, tm, tmx):
    n, d = tokens.shape
    grid_spec = pltpu.PrefetchScalarGridSpec(
        num_scalar_prefetch=1,
        grid=(n // tm,),
        in_specs=[pl.BlockSpec((1, 1, 2 * tm), lambda i, pad: (i, 0, 0), memory_space=pltpu.SMEM),
                  pl.BlockSpec((tm, d), lambda i, pad: (i, 0))],
        out_specs=pl.BlockSpec(memory_space=pl.ANY),
        scratch_shapes=[pltpu.VMEM((tmx, d), tokens.dtype), pltpu.SemaphoreType.DMA, pltpu.SemaphoreType.DMA],
    )
    return pl.pallas_call(
        functools.partial(_dispatch_kernel, tm=tm, tmx=tmx),
        out_shape=jax.ShapeDtypeStruct((n_slots, d), tokens.dtype),
        grid_spec=grid_spec,
        compiler_params=_cparams("arbitrary"),
        name="moe_dispatch",
    )(pad_info, dest3, tokens)


def _tile_expert(te_ref, nu_ref, i):
    return te_ref[jnp.clip(i, 0, nu_ref[0] - 1)]


def _moe_up_kernel(te_ref, nu_ref, xs_ref, w1_ref, w3_ref, h_ref, w1b_s, w3b_s):
    i = pl.program_id(1)

    @pl.when((i == 0) | (_tile_expert(te_ref, nu_ref, i) != _tile_expert(te_ref, nu_ref, i - 1)))
    def _():
        w1b_s[...] = w1_ref[0, 0].astype(BF16)
        w3b_s[...] = w3_ref[0, 0].astype(BF16)

    @pl.when(i < nu_ref[0])
    def _():
        x = xs_ref[...].astype(BF16)
        h_ref[...] = (_silu(_dot(x, w1b_s[...])) * _dot(x, w3b_s[...])).astype(h_ref.dtype)

    @pl.when(i >= nu_ref[0])
    def _():
        h_ref[...] = jnp.zeros_like(h_ref)


def _moe_down_kernel(te_ref, nu_ref, h_ref, w2_ref, y_ref, w2b_s):
    i = pl.program_id(1)

    @pl.when((i == 0) | (_tile_expert(te_ref, nu_ref, i) != _tile_expert(te_ref, nu_ref, i - 1)))
    def _():
        w2b_s[...] = w2_ref[0, 0].astype(BF16)

    @pl.when(i < nu_ref[0])
    def _():
        y_ref[...] = _dot(h_ref[...], w2b_s[...])

    @pl.when(i >= nu_ref[0])
    def _():
        y_ref[...] = jnp.zeros_like(y_ref)


def moe_ffn(xs, tile_expert, n_used, w1, w3, w2, *, li, tm, tf, tn):
    r, d = xs.shape
    f = w1.shape[3]

    def tile(i, nu):
        return jnp.minimum(i, nu[0] - 1)

    up_spec = pltpu.PrefetchScalarGridSpec(
        num_scalar_prefetch=2,
        grid=(f // tf, r // tm),
        in_specs=[pl.BlockSpec((tm, d), lambda j, i, te, nu: (tile(i, nu), 0)),
                  pl.BlockSpec((1, 1, d, tf), lambda j, i, te, nu: (li, te[tile(i, nu)], 0, j)),
                  pl.BlockSpec((1, 1, d, tf), lambda j, i, te, nu: (li, te[tile(i, nu)], 0, j))],
        out_specs=pl.BlockSpec((tm, tf), lambda j, i, te, nu: (i, j)),
        scratch_shapes=[pltpu.VMEM((d, tf), BF16), pltpu.VMEM((d, tf), BF16)],
    )
    hidden = pl.pallas_call(
        _moe_up_kernel,
        out_shape=jax.ShapeDtypeStruct((r, f), BF16),
        grid_spec=up_spec,
        compiler_params=_cparams("arbitrary", "arbitrary"),
        name="moe_up",
    )(tile_expert, n_used, xs, w1, w3)
    down_spec = pltpu.PrefetchScalarGridSpec(
        num_scalar_prefetch=2,
        grid=(d // tn, r // tm),
        in_specs=[pl.BlockSpec((tm, f), lambda n, i, te, nu: (tile(i, nu), 0)),
                  pl.BlockSpec((1, 1, f, tn), lambda n, i, te, nu: (li, te[tile(i, nu)], 0, n))],
        out_specs=pl.BlockSpec((tm, tn), lambda n, i, te, nu: (i, n)),
        scratch_shapes=[pltpu.VMEM((f, tn), BF16)],
    )
    return pl.pallas_call(
        _moe_down_kernel,
        out_shape=jax.ShapeDtypeStruct((r, d), F32),
        grid_spec=down_spec,
        compiler_params=_cparams("arbitrary", "arbitrary"),
        name="moe_down",
    )(tile_expert, n_used, hidden, w2)


def _combine_kernel(dest_ref, gate_ref, x_ref, mod_ref, fg_ref, ys_ref, o_ref, buf_ref, sem, *, tm, last):
    def issue(r, carry):
        for k in range(2):
            src = dest_ref[0, 0, 2 * r + k]
            pltpu.make_async_copy(ys_ref.at[pl.ds(src, 1)], buf_ref.at[k, pl.ds(r, 1)], sem).start()
        return carry

    lax.fori_loop(0, tm, issue, 0, unroll=16)
    for k in range(2):
        pltpu.make_async_copy(ys_ref.at[pl.ds(0, tm)], buf_ref.at[k], sem).wait()
    gate = gate_ref[...]
    y = gate[:, 0:1] * buf_ref[0] + gate[:, 1:2] * buf_ref[1]
    x = x_ref[...] + mod_ref[0, 5:6, :] * y
    if last:
        x = _final_norm(x, fg_ref[...])
    o_ref[...] = x


def moe_combine(ys, dest3, rgate, xall, mod, final_g, *, tm, seg, last):
    n, d = xall.shape
    return pl.pallas_call(
        functools.partial(_combine_kernel, tm=tm, last=last),
        out_shape=jax.ShapeDtypeStruct((n, d), F32),
        grid=(n // tm,),
        in_specs=[pl.BlockSpec((1, 1, 2 * tm), lambda i: (i, 0, 0), memory_space=pltpu.SMEM),
                  pl.BlockSpec((tm, 8), lambda i: (i, 0)),
                  pl.BlockSpec((tm, d), lambda i: (i, 0)),
                  pl.BlockSpec((1, 6, d), lambda i: (seg(i), 0, 0)),
                  pl.BlockSpec((1, d), lambda i: (0, 0)),
                  pl.BlockSpec(memory_space=pl.ANY)],
        out_specs=pl.BlockSpec((tm, d), lambda i: (i, 0)),
        scratch_shapes=[pltpu.VMEM((2, tm, d), F32), pltpu.SemaphoreType.DMA],
        compiler_params=_cparams("arbitrary"),
        name="moe_combine",
    )(dest3, rgate, xall, mod, final_g, ys)


def moe_layer(tokens, ridx, rgate, cnt, xall, mod, w1, w3, w2, final_g, *, li, tm, tmx, tf, seg, last):
    n = tokens.shape[0]
    counts = cnt[:, 0]
    padded = (counts + tmx - 1) // tmx * tmx
    pend = jnp.cumsum(padded)
    pstart = pend - padded
    group_start = sum(jnp.where(ridx[0:2] == e, pstart[e], 0) for e in range(N_EXPERTS))
    dest = (group_start + ridx[2:4]).T
    dest3 = dest.reshape(n // tm, 1, 2 * tm).astype(jnp.int32)
    rgate = rgate.T
    n_tiles = (2 * n) // tmx + N_EXPERTS
    n_used = (pend[-1] // tmx).astype(jnp.int32).reshape(1)
    tile_start = jnp.arange(n_tiles, dtype=jnp.int32) * tmx
    tile_expert = jnp.minimum(jnp.sum(tile_start[:, None] >= pend[None, :], axis=1), N_EXPERTS - 1).astype(jnp.int32)
    pad_info = jnp.concatenate([pend - tmx, (padded > 0).astype(jnp.int32), pend[-1:], n_tiles - n_used]
                               ).astype(jnp.int32)
    xs = moe_dispatch(tokens, dest3, pad_info, n_tiles * tmx, tm=tm, tmx=tmx)
    ys = moe_ffn(xs, tile_expert, n_used, w1, w3, w2, li=li, tm=tmx, tf=tf, tn=MOE_DOWN_TILE)
    return moe_combine(ys, dest3, rgate, xall, mod, final_g, tm=tm, seg=seg, last=last)


def rope_tables(seq):
    rows = seq // GRID_W
    row = jnp.repeat(jnp.arange(rows, dtype=F32), GRID_W)
    col = jnp.tile(jnp.arange(GRID_W, dtype=F32), rows)
    n_freq = HEAD_DIM // 4
    inv = ROPE_BASE ** (-jnp.arange(n_freq, dtype=F32) / n_freq)
    ang = jnp.concatenate([row[:, None] * inv, col[:, None] * inv], axis=-1)
    cos, sin = jnp.cos(ang), jnp.sin(ang)
    cos64 = jnp.concatenate([cos, cos], axis=-1)
    sin64 = jnp.concatenate([-sin, sin], axis=-1)
    return jnp.tile(cos64, (1, 2)), jnp.tile(sin64, (1, 2))


def _tile_lanes(t, width):
    reps = width // t.shape[1]
    return t if reps == 1 else jnp.concatenate([t] * reps, axis=1)


def _rope(t, cos, sin):
    w = t.shape[1]
    lane = lax.broadcasted_iota(jnp.int32, t.shape, 1) % HEAD_DIM
    half = HEAD_DIM // 2
    rot = jnp.where(lane < half, pltpu.roll(t, w - half, 1), pltpu.roll(t, half, 1))
    return t * _tile_lanes(cos, w) + rot * _tile_lanes(sin, w)


def _group_matrix(width, value):
    r = lax.broadcasted_iota(jnp.int32, (width, width), 0) // HEAD_DIM
    c = lax.broadcasted_iota(jnp.int32, (width, width), 1) // HEAD_DIM
    return jnp.where(r == c, value, 0.0).astype(BF16)


RET_CHUNK = 128


def _retention_kernel(lat_ref, ctx_ref, cos_ref, sin_ref, dec_ref, gn_ref, ol_ref, oc_ref,
                      q_s, k_s, bn_s, st_s, o_s, dm_s, *, n_ctx, n_lat):
    ch = RET_CHUNK
    nc, nl = n_ctx // ch, n_lat // ch
    lg = -jnp.exp(dec_ref[...])
    lgf, lgb = lg[0:1], lg[1:2]
    pos = lax.broadcasted_iota(jnp.int32, (ch, 1), 0).astype(F32)
    wkf = jnp.exp((ch - 1 - pos) * lgf)
    wkb = jnp.exp(pos * lgb)
    wqf = jnp.exp((pos + 1) * lgf)
    wqb = jnp.exp((ch - pos) * lgb)
    cdf = jnp.exp(ch * lgf)
    cdb = jnp.exp(ch * lgb)
    blockmask = (lax.broadcasted_iota(jnp.int32, (RET_W, RET_W), 0) // HEAD_DIM
                 == lax.broadcasted_iota(jnp.int32, (RET_W, RET_W), 1) // HEAD_DIM)
    gmean = _group_matrix(RET_W, 1.0 / HEAD_DIM)
    diff = (lax.broadcasted_iota(jnp.int32, (ch, ch), 0) - lax.broadcasted_iota(jnp.int32, (ch, ch), 1)).astype(F32)
    gn = gn_ref[...]

    def load_qkv(ref, t0):
        return (ref[pl.ds(t0, ch), 0:RET_W].astype(F32), ref[pl.ds(t0, ch), RET_W:2 * RET_W].astype(F32),
                ref[pl.ds(t0, ch), 2 * RET_W:3 * RET_W])

    def reverse_step(ref, t0, c, roped):
        q, k, v = load_qkv(ref, t0)
        if roped:
            cos, sin = cos_ref[pl.ds(t0, ch), :], sin_ref[pl.ds(t0, ch), :]
            q, k = _rope(q, cos, sin), _rope(k, cos, sin)
        k = k * (HEAD_DIM ** -0.5)
        r0 = pl.multiple_of(c * ch, ch)
        q_s[pl.ds(r0, ch), :] = q.astype(BF16)
        k_s[pl.ds(r0, ch), :] = k.astype(BF16)
        st = st_s[...]
        bn_s[c] = st.astype(BF16)
        kvb = _dot_tn((k * wkb).astype(BF16), v)
        st_s[...] = cdb * st + jnp.where(blockmask, kvb, 0.0)

    st_s[...] = jnp.zeros_like(st_s)

    def rev_ctx(n, carry):
        c = nc - 1 - n
        reverse_step(ctx_ref, pl.multiple_of(c * ch, ch), c, False)
        return carry

    def rev_lat(n, carry):
        c = nl - 1 - n
        reverse_step(lat_ref, pl.multiple_of(c * ch, ch), nc + c, True)
        return carry

    lax.fori_loop(0, nc, rev_ctx, 0)
    lax.fori_loop(0, nl, rev_lat, 0)

    for h in range(RET_HEADS):
        lf, lb = lgf[:, h * HEAD_DIM:h * HEAD_DIM + 1], lgb[:, h * HEAD_DIM:h * HEAD_DIM + 1]
        dm_s[h] = jnp.where(diff == 0.0, 2.0, jnp.exp(jnp.abs(diff) * jnp.where(diff > 0.0, lf, lb)))

    def forward_step(ref, t0, c):
        r0 = pl.multiple_of(c * ch, ch)
        q = q_s[pl.ds(r0, ch), :]
        k = k_s[pl.ds(r0, ch), :]
        v = ref[pl.ds(t0, ch), 2 * RET_W:3 * RET_W]
        sls = [slice(h * HEAD_DIM, (h + 1) * HEAD_DIM) for h in range(RET_HEADS)]
        scores = [_dot_nt(q[:, sl], k[:, sl]) for sl in sls]
        probs = [(s * dm_s[h]).astype(BF16) for h, s in enumerate(scores)]
        heads = [_dot(p, v[:, sl]) for p, sl in zip(probs, sls)]
        qf = q.astype(F32)
        st = st_s[...]
        o = jnp.concatenate(heads, axis=1)
        o_s[pl.ds(r0, ch), :] = (o + _dot((qf * wqf).astype(BF16), st.astype(BF16))
                                 + _dot((qf * wqb).astype(BF16), bn_s[c]))
        kvf = _dot_tn((k.astype(F32) * wkf).astype(BF16), v)
        st_s[...] = cdf * st + jnp.where(blockmask, kvf, 0.0)

    st_s[...] = jnp.zeros_like(st_s)

    def fwd_ctx(n, carry):
        forward_step(ctx_ref, pl.multiple_of(n * ch, ch), n)
        return carry

    def fwd_lat(n, carry):
        forward_step(lat_ref, pl.multiple_of(n * ch, ch), nc + n)
        return carry

    lax.fori_loop(0, nc, fwd_ctx, 0)
    lax.fori_loop(0, nl, fwd_lat, 0, unroll=2)

    def finish(ref, out_ref, n_rows, row_off):
        ft = 2 * ch

        def body(i, carry):
            t0 = pl.multiple_of(i * ft, ft)
            o = o_s[pl.ds(pl.multiple_of(row_off + t0, ft), ft), :]
            g = ref[pl.ds(t0, ft), 3 * RET_W:4 * RET_W].astype(F32)
            xc = o - _split_dot(o, gmean)
            var = _split_dot(xc * xc, gmean)
            out_ref[pl.ds(t0, ft), :] = (xc * lax.rsqrt(var + EPS) * gn * _silu(g)).astype(out_ref.dtype)
            return carry

        lax.fori_loop(0, n_rows // ft, body, 0)

    finish(ctx_ref, oc_ref, n_ctx, 0)
    finish(lat_ref, ol_ref, n_lat, n_ctx)


def retention_mixer(p_ret, cos, sin, dec_lanes, gn_gain, *, n_batch, n_lat, n_ctx):
    n = p_ret.shape[0]
    width = p_ret.shape[1]
    ctx_blk0 = (n_batch * n_lat) // n_ctx
    t_all = n_lat + n_ctx
    nch = t_all // RET_CHUNK
    return pl.pallas_call(
        functools.partial(_retention_kernel, n_ctx=n_ctx, n_lat=n_lat),
        out_shape=(jax.ShapeDtypeStruct((n_batch * n_lat, RET_W), BF16),
                   jax.ShapeDtypeStruct((n_batch * n_ctx, RET_W), BF16)),
        grid=(n_batch,),
        in_specs=[pl.BlockSpec((n_lat, width), lambda b: (b, 0)),
                  pl.BlockSpec((n_ctx, width), lambda b: (ctx_blk0 + b, 0)),
                  pl.BlockSpec((n_lat, LANES), lambda b: (0, 0)),
                  pl.BlockSpec((n_lat, LANES), lambda b: (0, 0)),
                  pl.BlockSpec((2, RET_W), lambda b: (0, 0)),
                  pl.BlockSpec((1, RET_W), lambda b: (0, 0))],
        out_specs=(pl.BlockSpec((n_lat, RET_W), lambda b: (b, 0)),
                   pl.BlockSpec((n_ctx, RET_W), lambda b: (b, 0))),
        scratch_shapes=[pltpu.VMEM((t_all, RET_W), BF16), pltpu.VMEM((t_all, RET_W), BF16),
                        pltpu.VMEM((nch, RET_W, RET_W), BF16), pltpu.VMEM((RET_W, RET_W), F32),
                        pltpu.VMEM((t_all, RET_W), F32), pltpu.VMEM((RET_HEADS, RET_CHUNK, RET_CHUNK), F32)],
        compiler_params=_cparams("parallel"),
        name="retention",
    )(p_ret, p_ret, cos, sin, dec_lanes, gn_gain)


GDN_CHUNK = 64
GDN_PREP = 128
GDN_HALO = 8
GDN_SOLVE_CHUNKS = 4


def _softplus(x):
    return jnp.maximum(x, 0.0) + jnp.log(1.0 + jnp.exp(-jnp.abs(x)))


def _head_block_diag(x):
    n, w = x.shape
    heads = w // HEAD_DIM
    rows = lax.broadcasted_iota(jnp.int32, (heads * n, w), 0) // n
    cols = lax.broadcasted_iota(jnp.int32, (heads * n, w), 1) // HEAD_DIM
    return jnp.where(rows == cols, jnp.concatenate([x] * heads, axis=0), 0.0)


def _unit_tri_inverses(mats):
    n, w = mats[0].shape
    ri = lax.broadcasted_iota(jnp.int32, (n, w), 0)
    ci = lax.broadcasted_iota(jnp.int32, (n, w), 1) % HEAD_DIM
    b16 = (ri // 16) == (ci // 16)
    b32 = (ri // 32) == (ci // 32)
    eye = jnp.where(ri == ci, 1.0, 0.0)
    ps = [jnp.where(b16, -a, 0.0) for a in mats]
    ts = [eye + p for p in ps]
    for _ in range(3):
        pbs = [p.astype(BF16) for p in ps]
        ps = [_dot(pb, _head_block_diag(pb)) for pb in pbs]
        ts = [t + _dot(t.astype(BF16), _head_block_diag(p.astype(BF16))) for t, p in zip(ts, ps)]
    for keep in (b32 & ~b16, ~b32):
        offs = [jnp.where(keep, a, 0.0).astype(BF16) for a in mats]
        tbs = [t.astype(BF16) for t in ts]
        mids = [_dot(tb, _head_block_diag(off)).astype(BF16) for tb, off in zip(tbs, offs)]
        ts = [t - _dot(mid, _head_block_diag(tb)) for t, mid, tb in zip(ts, mids, tbs)]
    return ts


def _gdn_prep_kernel(lat_ref, ctx_ref, abl_ref, abc_ref, cw_ref, par_ref, q_s, k_s, v_s, gb_s, *, n_ctx, n_lat):
    pt = GDN_PREP
    halo = GDN_HALO
    qkv_w = 3 * GDN_W
    gsum = _group_matrix(GDN_W, 1.0)
    lane128 = lax.broadcasted_iota(jnp.int32, (pt, LANES), 1)
    neg_a = -jnp.exp(par_ref[0:1, :])
    dt_bias = par_ref[1:2, :]
    taps = [cw_ref[j:j + 1, :] for j in range(GDN_CONV)]

    def prep(ref, ab_ref, n_rows, row_off):
        n_tiles = n_rows // pt

        def body(i, carry):
            t0 = pl.multiple_of(i * pt, pt)
            cur = ref[pl.ds(t0, pt), 0:qkv_w].astype(F32)
            p0 = pl.multiple_of(jnp.maximum(t0 - 16, 0), 16)
            n0 = pl.multiple_of(jnp.minimum(t0 + pt, n_rows - 16), 16)
            prev = ref[pl.ds(p0, 16), 0:qkv_w].astype(F32)[16 - halo:16]
            nxt = ref[pl.ds(n0, 16), 0:qkv_w].astype(F32)[0:halo]
            prev = jnp.where(i > 0, prev, 0.0)
            nxt = jnp.where(i < n_tiles - 1, nxt, 0.0)
            ext = jnp.concatenate([prev, cur, nxt], axis=0)
            rows = pt + 2 * halo
            acc = None
            for j in range(GDN_CONV):
                s = j - (GDN_CONV - 1) // 2
                sh = ext if s == 0 else pltpu.roll(ext, (rows - s) % rows, 0)
                term = sh[halo:halo + pt] * taps[j]
                acc = term if acc is None else acc + term
            act = _silu(acc)
            q, k, v = act[:, 0:GDN_W], act[:, GDN_W:2 * GDN_W], act[:, 2 * GDN_W:3 * GDN_W]
            q = q * lax.rsqrt(_split_dot(q * q, gsum) + EPS) * (HEAD_DIM ** -0.5)
            k = k * lax.rsqrt(_split_dot(k * k, gsum) + EPS)
            r0 = pl.multiple_of(row_off + t0, pt)
            q_s[pl.ds(r0, pt), :] = q.astype(BF16)
            k_s[pl.ds(r0, pt), :] = k.astype(BF16)
            v_s[pl.ds(r0, pt), :] = v.astype(BF16)
            ab = ab_ref[pl.ds(t0, pt), :]
            g = neg_a * _softplus(ab + dt_bias)
            gb_s[pl.ds(r0, pt), :] = jnp.where(lane128 < 2 * GDN_HEADS, g, _sigmoid(ab))
            return carry

        lax.fori_loop(0, n_tiles, body, 0)

    prep(ctx_ref, abc_ref, n_ctx, 0)
    prep(lat_ref, abl_ref, n_lat, n_ctx)


def _gdn_kernel(q_s, k_s, v_s, gb_s, zl_ref, zc_ref, ng_ref, ol_ref, oc_ref,
                o_s, dec_s, uf_s, ub_s, wf_s, wb_s, af_s, ab_s, qf_s, qb_s, kf_s, kb_s, sf_s, sb_s, *,
                n_ctx, n_lat, n_solve):
    u_s, w_s, a_s, qd_s, kd_s, st_s = (uf_s, ub_s), (wf_s, wb_s), (af_s, ab_s), (qf_s, qb_s), (kf_s, kb_s), (sf_s, sb_s)
    ch = GDN_CHUNK
    pt = GDN_PREP
    gmean = _group_matrix(GDN_W, 1.0 / HEAD_DIM)
    ri = lax.broadcasted_iota(jnp.int32, (ch, ch), 0)
    ci = lax.broadcasted_iota(jnp.int32, (ch, ch), 1)
    tri = tuple(jnp.where(m, 1.0, 0.0).astype(BF16) for m in (ri >= ci, ri <= ci))
    rp = lax.broadcasted_iota(jnp.int32, (ch, GDN_W), 0)
    cp = lax.broadcasted_iota(jnp.int32, (ch, GDN_W), 1) % HEAD_DIM
    incl = (rp >= cp, rp <= cp)
    strict = (rp > cp, rp < cp)
    ncc, nlc = n_ctx // ch, n_lat // ch
    head_blocks = (lax.broadcasted_iota(jnp.int32, (GDN_W, GDN_W), 0) // HEAD_DIM
                   == lax.broadcasted_iota(jnp.int32, (GDN_W, GDN_W), 1) // HEAD_DIM)

    def solve_chunks(cs):
        n_cs = len(cs)
        items = [(j, d) for j in range(n_cs) for d in range(2)]
        rows = [pl.ds(pl.multiple_of(c * ch, ch), ch) for c in cs]
        gbc = [gb_s[r, :] for r in rows]
        gcs = []
        for j in range(n_cs):
            per_dir = []
            for d in range(2):
                rem, acc = gbc[j], None
                for _ in range(3):
                    part = rem.astype(BF16)
                    term = _dot(tri[d], part)
                    acc = term if acc is None else acc + term
                    rem = rem - part.astype(F32)
                per_dir.append(acc)
            gcs.append(per_dir)
        gcs_t = [[g.T for g in per_dir] for per_dir in gcs]
        edge = (ch - 1, 0)

        def spread(mat, first):
            return jnp.concatenate([jnp.broadcast_to(mat[:, first + h:first + h + 1], (ch, HEAD_DIM))
                                    for h in range(GDN_HEADS)], axis=1)

        gc = [spread(gcs[j][d], d * GDN_HEADS) for j, d in items]
        gr = [jnp.concatenate([gcs_t[j][d][d * GDN_HEADS + h:d * GDN_HEADS + h + 1, :] for h in range(GDN_HEADS)],
                              axis=1) for j, d in items]
        gtot = [x[edge[d]:edge[d] + 1, :] for x, (_, d) in zip(gc, items)]
        beta = [spread(gbc[j], 2 * GDN_HEADS + d * GDN_HEADS) for j, d in items]
        decay = [jnp.where(incl[d], jnp.exp(jnp.minimum(gc[i] - gr[i], 0.0)), 0.0) for i, (_, d) in enumerate(items)]
        q = [q_s[r, :] for r in rows]
        k = [k_s[r, :] for r in rows]
        kf = [x.astype(F32) for x in k]
        v = [v_s[r, :].astype(F32) for r in rows]
        k_rows = [_head_block_diag(x) for x in k]
        qk = [_dot_nt(x, y) for x, y in zip(q, k_rows)]
        kb = [kf[j] * beta[i] for i, (j, _) in enumerate(items)]
        kk = [_dot_nt(kb[i].astype(BF16), k_rows[j]) for i, (j, _) in enumerate(items)]
        a = [jnp.where(strict[d], kk[i] * decay[i], 0.0) for i, (_, d) in enumerate(items)]
        t = [x.astype(BF16) for x in _unit_tri_inverses(a)]
        eg = [jnp.exp(x) for x in gc]
        u = [_dot(t[i], _head_block_diag((v[j] * beta[i]).astype(BF16))) for i, (j, _) in enumerate(items)]
        w = [_dot(t[i], _head_block_diag((kb[i] * eg[i]).astype(BF16))) for i in range(len(items))]
        for i, (j, d) in enumerate(items):
            u_s[d][rows[j], :] = u[i].astype(BF16)
            w_s[d][rows[j], :] = w[i].astype(BF16)
            a_s[d][rows[j], :] = (qk[j] * decay[i]).astype(BF16)
            qd_s[d][rows[j], :] = (q[j].astype(F32) * eg[i]).astype(BF16)
            kd_s[d][rows[j], :] = (kf[j] * jnp.exp(gtot[i] - gc[i])).astype(BF16)
            dec_s[pl.ds(cs[j], 1), d * GDN_W:(d + 1) * GDN_W] = jnp.exp(gtot[i])

    def scan_step(cf, cb):
        cs = (cf, cb)
        rows = [pl.ds(pl.multiple_of(c * ch, ch), ch) for c in cs]
        s = [st[...] for st in st_s]
        lhs = [jnp.concatenate([w_s[d][rows[d], :], qd_s[d][rows[d], :]], axis=0) for d in range(2)]
        prod = [_dot(lhs[d], s[d].astype(BF16)) for d in range(2)]
        vb = [(u_s[d][rows[d], :].astype(F32) - prod[d][0:ch]).astype(BF16) for d in range(2)]
        vbd = [jnp.where(head_blocks, jnp.concatenate([x] * GDN_HEADS, axis=0), 0.0) for x in vb]
        av = [_dot(a_s[d][rows[d], :], vbd[d]) for d in range(2)]
        upd = [_dot_tn(kd_s[d][rows[d], :], vb[d]) for d in range(2)]
        for d in range(2):
            dec = dec_s[pl.ds(cs[d], 1), d * GDN_W:(d + 1) * GDN_W]
            st_s[d][...] = s[d] * dec + jnp.where(head_blocks, upd[d], 0.0)
        for d in range(2):
            o_s[rows[d], :] += prod[d][ch:2 * ch] + av[d]

    def solve_body(n, carry):
        solve_chunks([n * n_solve + j for j in range(n_solve)])
        return carry

    lax.fori_loop(0, (ncc + nlc) // n_solve, solve_body, 0)

    for st in st_s:
        st[...] = jnp.zeros_like(st)
    o_s[...] = jnp.zeros_like(o_s)

    def scan_ctx(n, carry):
        scan_step(n, ncc - 1 - n)
        return carry

    def scan_lat(n, carry):
        scan_step(ncc + n, ncc + nlc - 1 - n)
        return carry

    lax.fori_loop(0, ncc, scan_ctx, 0)
    lax.fori_loop(0, nlc, scan_lat, 0, unroll=2)

    ng = ng_ref[...]

    def finish(z_ref, out_ref, n_rows, row_off):
        def body(i, carry):
            t0 = pl.multiple_of(i * pt, pt)
            o = o_s[pl.ds(pl.multiple_of(row_off + t0, pt), pt), :]
            z = z_ref[pl.ds(t0, pt), :].astype(F32)
            y = o * lax.rsqrt(_split_dot(o * o, gmean) + EPS) * ng * _silu(z)
            out_ref[pl.ds(t0, pt), :] = y.astype(out_ref.dtype)
            return carry

        lax.fori_loop(0, n_rows // pt, body, 0)

    finish(zc_ref, oc_ref, n_ctx, 0)
    finish(zl_ref, ol_ref, n_lat, n_ctx)


def gdn_mixer(p_gdn, p_ab, conv_w, a_log, dt_bias, norm_gain, *, n_batch, n_lat, n_ctx):
    width = p_gdn.shape[1]
    ctx_blk0 = (n_batch * n_lat) // n_ctx
    t_all = n_lat + n_ctx
    cw = jnp.pad(conv_w.astype(F32), ((0, 8 - GDN_CONV), (0, 0)))
    par = jnp.zeros((8, LANES), F32)
    par = par.at[0, :2 * GDN_HEADS].set(a_log.reshape(-1)).at[1, :2 * GDN_HEADS].set(dt_bias.reshape(-1))
    ng = jnp.tile(norm_gain.astype(F32).reshape(1, HEAD_DIM), (1, GDN_HEADS))
    seq = lambda w: pl.BlockSpec((t_all, w), lambda b: (b, 0))
    q, k, v, gb = pl.pallas_call(
        functools.partial(_gdn_prep_kernel, n_ctx=n_ctx, n_lat=n_lat),
        out_shape=(jax.ShapeDtypeStruct((n_batch * t_all, GDN_W), BF16),) * 3
        + (jax.ShapeDtypeStruct((n_batch * t_all, LANES), F32),),
        grid=(n_batch,),
        in_specs=[pl.BlockSpec((n_lat, width), lambda b: (b, 0)),
                  pl.BlockSpec((n_ctx, width), lambda b: (ctx_blk0 + b, 0)),
                  pl.BlockSpec((n_lat, LANES), lambda b: (b, 0)),
                  pl.BlockSpec((n_ctx, LANES), lambda b: (ctx_blk0 + b, 0)),
                  pl.BlockSpec(cw.shape, lambda b: (0, 0)),
                  pl.BlockSpec(par.shape, lambda b: (0, 0))],
        out_specs=(seq(GDN_W), seq(GDN_W), seq(GDN_W), seq(LANES)),
        compiler_params=_cparams("parallel"),
        name="gdn_prep",
    )(p_gdn, p_gdn, p_ab, p_ab, cw, par)
    zcol = 3 * GDN_W // GDN_W
    once = lambda w: pl.BlockSpec((t_all, w), lambda b: (b, 0), pipeline_mode=pl.Buffered(1))
    n_solve = GDN_SOLVE_CHUNKS if (t_all // GDN_CHUNK) % GDN_SOLVE_CHUNKS == 0 else 1
    return pl.pallas_call(
        functools.partial(_gdn_kernel, n_ctx=n_ctx, n_lat=n_lat, n_solve=n_solve),
        out_shape=(jax.ShapeDtypeStruct((n_batch * n_lat, GDN_W), BF16),
                   jax.ShapeDtypeStruct((n_batch * n_ctx, GDN_W), BF16)),
        grid=(n_batch,),
        in_specs=[once(GDN_W), once(GDN_W), once(GDN_W), pl.BlockSpec((t_all, LANES), lambda b: (b, 0)),
                  pl.BlockSpec((n_lat, GDN_W), lambda b: (b, zcol)),
                  pl.BlockSpec((n_ctx, GDN_W), lambda b: (ctx_blk0 + b, zcol)),
                  pl.BlockSpec((1, GDN_W), lambda b: (0, 0))],
        out_specs=(pl.BlockSpec((n_lat, GDN_W), lambda b: (b, 0)),
                   pl.BlockSpec((n_ctx, GDN_W), lambda b: (b, 0))),
        scratch_shapes=[pltpu.VMEM((t_all, GDN_W), F32), pltpu.VMEM((t_all // GDN_CHUNK, 2 * GDN_W), F32)]
        + [pltpu.VMEM((t_all, GDN_W), BF16)] * 10
        + [pltpu.VMEM((GDN_W, GDN_W), F32)] * 2,
        compiler_params=_cparams("parallel"),
        name="gdn",
    )(q, k, v, gb, p_gdn, p_gdn, ng)


SWA_BLOCK = 128
SWA_BAND = 3 * SWA_BLOCK
SWA_Q_PER_STEP = 4


def _swa_kernel(sink_ref, q_ref, kl_ref, vl_ref, kc_ref, vc_ref, cos_ref, sin_ref, o_ref, *, band, n_lat, n_sub):
    blk = SWA_BLOCK
    kc = kc_ref[...]
    vc = vc_ref[...]
    for sub in range(n_sub):
        rows = slice(sub * blk, (sub + 1) * blk)
        o_ref[rows, :] = _swa_block(pl.program_id(1) * n_sub + sub, q_ref[rows, :].astype(F32), sink_ref, kl_ref,
                                    vl_ref, kc, vc, cos_ref, sin_ref, band=band, n_lat=n_lat).astype(o_ref.dtype)


def _swa_block(i, q, sink_ref, kl_ref, vl_ref, kc, vc, cos_ref, sin_ref, *, band, n_lat):
    blk = SWA_BLOCK
    if band:
        q0 = pl.multiple_of(i * blk, blk)
        start = pl.multiple_of(jnp.clip((i - 1) * blk, 0, n_lat - SWA_BAND), blk)
        q = _rope(q, cos_ref[pl.ds(q0, blk), :], sin_ref[pl.ds(q0, blk), :])
        kb = _rope(kl_ref[pl.ds(start, SWA_BAND), :].astype(F32),
                   cos_ref[pl.ds(start, SWA_BAND), :], sin_ref[pl.ds(start, SWA_BAND), :]).astype(BF16)
        vb = vl_ref[pl.ds(start, SWA_BAND), :]
        keys = jnp.concatenate([kb, kc], axis=0)
        vals = jnp.concatenate([vb, vc], axis=0)
    else:
        keys, vals = kc, vc
    qb = (q * (HEAD_DIM ** -0.5 * LOG2E)).astype(BF16)
    n_keys = keys.shape[0]
    if band:
        q_pos = q0 + lax.broadcasted_iota(jnp.int32, (blk, 1), 0)
        col_id = lax.broadcasted_iota(jnp.int32, (1, n_keys), 1)
        valid = (col_id >= SWA_BAND) | (jnp.abs(start + col_id - q_pos) <= WINDOW)
        bias = jnp.where(valid, 0.0, NEG_BIG)
    v_lane = lax.broadcasted_iota(jnp.int32, vals.shape, 1) // HEAD_DIM
    outs = []
    for g in range(SWA_KV_HEADS):
        ksl = slice(g * HEAD_DIM, (g + 1) * HEAD_DIM)
        v_ext = jnp.where(v_lane == g, vals, 1.0)
        den_lane = (1 - g) * HEAD_DIM
        q4 = jnp.concatenate([qb[:, (g * SWA_GROUP + r) * HEAD_DIM:(g * SWA_GROUP + r + 1) * HEAD_DIM]
                              for r in range(SWA_GROUP)], axis=0)
        s = _dot_nt(q4, keys[:, ksl])
        es, ms, sinks = [], [], []
        for r in range(SWA_GROUP):
            sr = s[r * blk:(r + 1) * blk]
            if band:
                sr = sr + bias
            sink2 = sink_ref[g * SWA_GROUP + r] * LOG2E
            m = jnp.maximum(jnp.max(sr, axis=-1, keepdims=True), sink2)
            es.append(jnp.exp2(sr - m).astype(BF16))
            ms.append(m)
            sinks.append(sink2)
        ov = _dot(jnp.concatenate(es, axis=0), v_ext)
        for r in range(SWA_GROUP):
            ovr = ov[r * blk:(r + 1) * blk]
            den = ovr[:, den_lane:den_lane + 1] + jnp.exp2(sinks[r] - ms[r])
            outs.append(ovr[:, ksl] * (1.0 / den))
    return jnp.concatenate(outs, axis=1)


def swa_mixer(p_swa, cos, sin, sink, *, n_batch, n_lat, n_ctx):
    blk = SWA_BLOCK
    ctx_row0 = n_batch * n_lat
    kcol, vcol = SWA_W // SWA_KV_W, SWA_W // SWA_KV_W + 1
    sink = sink.astype(F32)

    def call(band):
        n_q = n_lat if band else n_ctx
        n_sub = min(SWA_Q_PER_STEP, n_q // blk)
        qblk = n_sub * blk
        nb = n_q // qblk
        qrow0 = 0 if band else ctx_row0 // qblk
        grid_spec = pltpu.PrefetchScalarGridSpec(
            num_scalar_prefetch=1,
            grid=(n_batch, nb),
            in_specs=[pl.BlockSpec((qblk, SWA_W), lambda b, i, s: (qrow0 + b * nb + i, 0)),
                      pl.BlockSpec((n_lat, SWA_KV_W), lambda b, i, s: (b, kcol)),
                      pl.BlockSpec((n_lat, SWA_KV_W), lambda b, i, s: (b, vcol)),
                      pl.BlockSpec((n_ctx, SWA_KV_W), lambda b, i, s: (ctx_row0 // n_ctx + b, kcol)),
                      pl.BlockSpec((n_ctx, SWA_KV_W), lambda b, i, s: (ctx_row0 // n_ctx + b, vcol)),
                      pl.BlockSpec((n_lat, LANES), lambda b, i, s: (0, 0)),
                      pl.BlockSpec((n_lat, LANES), lambda b, i, s: (0, 0))],
            out_specs=pl.BlockSpec((qblk, SWA_W), lambda b, i, s: (b * nb + i, 0)),
        )
        return pl.pallas_call(
            functools.partial(_swa_kernel, band=band, n_lat=n_lat, n_sub=n_sub),
            out_shape=jax.ShapeDtypeStruct((n_batch * n_q, SWA_W), BF16),
            grid_spec=grid_spec,
            compiler_params=_cparams("parallel", "arbitrary"),
            name="swa_latent" if band else "swa_context",
        )(sink, p_swa, p_swa, p_swa, p_swa, p_swa, cos, sin)

    return call(True), call(False)


MOE_ROW_TILE = 512
MOE_FF_TILE = 1792
MOE_DOWN_TILE = 1024


def _row_tile(n_lat, n_ctx_rows):
    for tm in (512, 256, 128):
        if n_lat % tm == 0 and n_ctx_rows % tm == 0:
            return tm
    raise ValueError("sequence lengths must be multiples of 128")


def kernel(x, c, ctx, c_ctx, ada_w, ada_b, norm1_g, w_in, ret_decay, ret_gn_g, gdn_conv_w, gdn_a_log, gdn_dt_bias,
           gdn_norm_g, swa_sink, w_out, norm2_g, ffn_w1, ffn_w3, ffn_w2, router_w, moe_w1, moe_w3, moe_w2, final_g):
    b, s, d = x.shape
    n_ctx = ctx.shape[1]
    depth = ada_w.shape[0]
    nl, ncx = b * s, b * n_ctx
    tm = _row_tile(s, ncx)
    seg = _seg_map(nl // tm, s // tm, b)

    cvec = jnp.zeros((8, d), F32).at[:b].set(c).at[b].set(c_ctx)
    mod_all = adaln(cvec, ada_w, ada_b).reshape(depth, 8, 6, d)
    xall = (x.reshape(nl, d), ctx.reshape(ncx, d))
    cos, sin = rope_tables(s)
    fg = final_g.reshape(1, d)

    ret_cols, gdn_cols = 4 * RET_W, 4 * GDN_W
    ab_cols = 4 * GDN_HEADS
    for layer in range(depth):
        last = layer == depth - 1
        mod = mod_all[layer]
        w = w_in[layer].astype(BF16)
        w_ret = w[:, :ret_cols]
        w_gdn = w[:, ret_cols:ret_cols + gdn_cols]
        w_ab = jnp.pad(w[:, ret_cols + gdn_cols:ret_cols + gdn_cols + ab_cols], ((0, 0), (0, LANES - ab_cols)))
        w_swa = w[:, ret_cols + gdn_cols + ab_cols:]
        p_ret, p_gdn, p_swa, p_ab = in_proj(xall, mod, norm1_g[layer].reshape(1, d), w_ret, w_gdn, w_swa, w_ab,
                                            tm=tm, seg=seg, n=nl + ncx, n_lat_rows=nl)
        dec_lanes = jnp.repeat(ret_decay[layer].astype(F32), HEAD_DIM, axis=1)
        o_ret = retention_mixer(p_ret, cos, sin, dec_lanes, ret_gn_g[layer].reshape(1, RET_W),
                                n_batch=b, n_lat=s, n_ctx=n_ctx)
        o_gdn = gdn_mixer(p_gdn, p_ab, gdn_conv_w[layer], gdn_a_log[layer], gdn_dt_bias[layer], gdn_norm_g[layer],
                          n_batch=b, n_lat=s, n_ctx=n_ctx)
        o_swa = swa_mixer(p_swa, cos, sin, swa_sink[layer], n_batch=b, n_lat=s, n_ctx=n_ctx)

        n_rows = nl if last else nl + ncx
        is_moe = layer % 2 == 1
        i = layer // 2
        rw = None
        if is_moe:
            rw = jnp.pad(router_w[i].astype(F32), ((0, 0), (0, LANES - N_EXPERTS)))
        res = out_proj(o_ret, o_gdn, o_swa, xall, mod, w_out[layer].astype(BF16), norm2_g[layer].reshape(1, d), rw,
                       tm=tm, seg=seg, n_rows=n_rows, n_lat_rows=nl, tok_dtype=F32 if is_moe else BF16)
        if is_moe:
            x_new, tokens, ridx, rgate, cnt = res
            xall = moe_layer(tokens, ridx, rgate, cnt, x_new, mod, moe_w1, moe_w3, moe_w2, fg,
                             li=i, tm=tm, tmx=MOE_ROW_TILE, tf=MOE_FF_TILE, seg=seg, last=last)
        else:
            x_new, tokens = res
            xall = dense_ffn(tokens, x_new, mod, ffn_w1[i].astype(BF16), ffn_w3[i].astype(BF16),
                             ffn_w2[i].astype(BF16), fg, tm=tm, seg=seg, last=last)
    return xall[:nl].reshape(b, s, d)
```

```python
import functools

import jax
import jax.numpy as jnp
from jax import lax
from jax.experimental import pallas as pl
from jax.experimental.pallas import tpu as pltpu

F32 = jnp.float32
BF16 = jnp.bfloat16

HEAD_DIM = 64
GRID_W = 64
ROPE_BASE = 10000.0
EPS = 1e-6
RET_HEADS = 4
GDN_HEADS = 4
GDN_CONV = 5
SWA_HEADS = 8
SWA_KV_HEADS = 2
SWA_GROUP = SWA_HEADS // SWA_KV_HEADS
WINDOW = 128
N_EXPERTS = 8
RET_W = RET_HEADS * HEAD_DIM
GDN_W = GDN_HEADS * HEAD_DIM
SWA_W = SWA_HEADS * HEAD_DIM
SWA_KV_W = SWA_KV_HEADS * HEAD_DIM
LANES = 128
VMEM_LIMIT = 56 * 1024 * 1024
NEG_BIG = -1e30
LOG2E = 1.4426950408889634


def _cparams(*sem):
    return pltpu.CompilerParams(dimension_semantics=sem, vmem_limit_bytes=VMEM_LIMIT)


def _dot(a, b):
    return jnp.dot(a, b, preferred_element_type=F32)


def _dot_nt(a, b):
    return lax.dot_general(a, b, (((1,), (1,)), ((), ())), preferred_element_type=F32)


def _dot_tn(a, b):
    return lax.dot_general(a, b, (((0,), (0,)), ((), ())), preferred_element_type=F32)


def _split_dot(x, m_bf16, terms=2):
    acc = None
    rem = x
    for _ in range(terms):
        part = rem.astype(BF16)
        d = _dot(part, m_bf16)
        acc = d if acc is None else acc + d
        rem = rem - part.astype(F32)
    return acc


def _sigmoid(x):
    return 1.0 / (1.0 + jnp.exp(-x))


def _silu(x):
    return x * _sigmoid(x)


def _rms_mod(x, gain, shift, scale):
    ms = jnp.mean(x * x, axis=-1, keepdims=True)
    y = x * lax.rsqrt(ms + EPS) * gain
    return y * (1.0 + scale) + shift


def _seg_map(n_lat_tiles, tiles_per_batch, n_batch):
    def seg(i):
        return jnp.where(i < n_lat_tiles, i // tiles_per_batch, n_batch)
    return seg


def _adaln_kernel(c_ref, w_ref, b_ref, o_ref):
    h = _silu(c_ref[...]).astype(BF16)
    o_ref[0] = _dot(h, w_ref[0].astype(BF16)) + b_ref[0]


def adaln(cvec, ada_w, ada_b):
    depth, d, n6 = ada_w.shape
    tn = 1536 if n6 % 1536 == 0 else n6
    rows = cvec.shape[0]
    return pl.pallas_call(
        _adaln_kernel,
        out_shape=jax.ShapeDtypeStruct((depth, rows, n6), F32),
        grid=(depth, n6 // tn),
        in_specs=[pl.BlockSpec((rows, d), lambda l, j: (0, 0)),
                  pl.BlockSpec((1, d, tn), lambda l, j: (l, 0, j)),
                  pl.BlockSpec((1, 1, tn), lambda l, j: (l, 0, j))],
        out_specs=pl.BlockSpec((1, rows, tn), lambda l, j: (l, 0, j)),
        compiler_params=_cparams("parallel", "parallel"),
        name="adaln",
    )(cvec, ada_w, ada_b.reshape(depth, 1, n6))


def _stream_rows(refs, n_streams, n_lat_tiles):
    if n_streams == 1:
        return refs[0][...]
    return jnp.where(pl.program_id(0) < n_lat_tiles, refs[0][...], refs[1][...])


def _stream_specs(x_rows, tm, d, n_lat_tiles):
    if not isinstance(x_rows, tuple):
        return [pl.BlockSpec((tm, d), lambda i: (i, 0))], [x_rows]
    return ([pl.BlockSpec((tm, d), lambda i: (jnp.minimum(i, n_lat_tiles - 1), 0)),
             pl.BlockSpec((tm, d), lambda i: (jnp.maximum(i - n_lat_tiles, 0), 0))], list(x_rows))


def _inproj_kernel(*refs, n_streams, n_lat_tiles):
    mod_ref, g_ref, wr_ref, wg_ref, ws_ref, wab_ref, pr_ref, pg_ref, ps_ref, pab_ref = refs[n_streams:]
    x = _stream_rows(refs, n_streams, n_lat_tiles)
    h = _rms_mod(x, g_ref[...], mod_ref[0, 0:1, :], mod_ref[0, 1:2, :]).astype(BF16)
    for w_ref, o_ref in ((wr_ref, pr_ref), (wg_ref, pg_ref), (ws_ref, ps_ref), (wab_ref, pab_ref)):
        n = w_ref.shape[1]
        step = 256 if n % 256 == 0 else n
        for c in range(0, n, step):
            o_ref[:, c:c + step] = _dot(h, w_ref[:, c:c + step]).astype(o_ref.dtype)


def in_proj(x_rows, mod, gain, w_ret, w_gdn, w_swa, w_ab, *, tm, seg, n, n_lat_rows):
    d = mod.shape[-1]
    full = lambda i: (0, 0)
    row = lambda i: (i, 0)
    x_specs, x_args = _stream_specs(x_rows, tm, d, n_lat_rows // tm)
    outs = (jax.ShapeDtypeStruct((n, w_ret.shape[1]), BF16), jax.ShapeDtypeStruct((n, w_gdn.shape[1]), BF16),
            jax.ShapeDtypeStruct((n, w_swa.shape[1]), BF16), jax.ShapeDtypeStruct((n, w_ab.shape[1]), F32))
    return pl.pallas_call(
        functools.partial(_inproj_kernel, n_streams=len(x_args), n_lat_tiles=n_lat_rows // tm),
        out_shape=outs,
        grid=(n // tm,),
        in_specs=x_specs + [
                  pl.BlockSpec((1, 6, d), lambda i: (seg(i), 0, 0)),
                  pl.BlockSpec((1, d), full),
                  pl.BlockSpec(w_ret.shape, full), pl.BlockSpec(w_gdn.shape, full),
                  pl.BlockSpec(w_swa.shape, full), pl.BlockSpec(w_ab.shape, full)],
        out_specs=tuple(pl.BlockSpec((tm, o.shape[1]), row) for o in outs),
        compiler_params=_cparams("parallel"),
        name="in_proj",
    )(*x_args, mod, gain, w_ret, w_gdn, w_swa, w_ab)


def _outproj_kernel(*refs, tm, moe, n_lat_tiles, has_ctx, n_streams):
    n_mix = 6 if has_ctx else 3
    mix = refs[:n_mix]
    x_in = _stream_rows(refs[n_mix:], n_streams, n_lat_tiles)
    refs = refs[n_mix + n_streams:]
    if moe:
        (mod_ref, w_ref, g_ref, rw_ref, xo_ref, tok_ref, ridx_ref, rgate_ref, cnt_ref, base_ref,
         earlier_ref) = refs
    else:
        mod_ref, w_ref, g_ref, xo_ref, tok_ref = refs
    if has_ctx:
        is_lat = pl.program_id(0) < n_lat_tiles
        o_ret, o_gdn, o_swa = (jnp.where(is_lat, mix[2 * k][...], mix[2 * k + 1][...]) for k in range(3))
    else:
        o_ret, o_gdn, o_swa = (m[...] for m in mix)
    y = _dot(o_ret, w_ref[0:RET_W, :])
    y = y + _dot(o_gdn, w_ref[RET_W:RET_W + GDN_W, :])
    y = y + _dot(o_swa, w_ref[RET_W + GDN_W:, :])
    x = x_in + mod_ref[0, 2:3, :] * y
    xo_ref[...] = x
    t = _rms_mod(x, g_ref[...], mod_ref[0, 3:4, :], mod_ref[0, 4:5, :])
    tok_ref[...] = t.astype(tok_ref.dtype)
    if not moe:
        return

    @pl.when(pl.program_id(0) == 0)
    def _():
        base_ref[...] = jnp.zeros_like(base_ref)
        r_i = lax.broadcasted_iota(jnp.int32, (tm, tm), 0)
        c_i = lax.broadcasted_iota(jnp.int32, (tm, tm), 1)
        earlier_ref[...] = jnp.where(r_i < c_i, 1.0, 0.0).astype(BF16)

    rw = rw_ref[...]
    rw_hi = rw.astype(BF16)
    rw_lo = (rw - rw_hi.astype(F32)).astype(BF16)
    t_hi = t.astype(BF16)
    t_lo = (t - t_hi.astype(F32)).astype(BF16)
    logits = _dot(t_hi, rw_hi) + _dot(t_hi, rw_lo) + _dot(t_lo, rw_hi)
    lt = logits.T[0:N_EXPERTS, :]
    e_id = lax.broadcasted_iota(jnp.int32, lt.shape, 0)
    m1 = jnp.max(lt, axis=0, keepdims=True)
    i1 = jnp.min(jnp.where(lt == m1, e_id, N_EXPERTS), axis=0, keepdims=True)
    rest = jnp.where(e_id == i1, NEG_BIG, lt)
    m2 = jnp.max(rest, axis=0, keepdims=True)
    i2 = jnp.min(jnp.where(rest == m2, e_id, N_EXPERTS), axis=0, keepdims=True)
    e2 = jnp.exp(m2 - m1)
    g1 = 1.0 / (1.0 + e2)
    g2 = e2 * g1
    oh1 = jnp.where(e_id == i1, 1.0, 0.0)
    oh2 = jnp.where(e_id == i2, 1.0, 0.0)
    cum = _dot(jnp.concatenate([oh1, oh2], axis=0).astype(BF16), earlier_ref[...])
    cum1, cum2 = cum[0:N_EXPERTS], cum[N_EXPERTS:2 * N_EXPERTS]
    cnt1 = jnp.sum(oh1, axis=1, keepdims=True)
    cnt2 = jnp.sum(oh2, axis=1, keepdims=True)
    base = base_ref[:, 0:1]
    rank1 = jnp.sum(oh1 * (base + cum1), axis=0, keepdims=True)
    rank2 = jnp.sum(oh2 * (base + cnt1 + cum2), axis=0, keepdims=True)
    base = base + cnt1 + cnt2
    base_ref[...] = jnp.broadcast_to(base, base_ref.shape)
    cnt_ref[...] = jnp.broadcast_to(base, cnt_ref.shape).astype(jnp.int32)
    row8 = lax.broadcasted_iota(jnp.int32, (8, tm), 0)
    r1 = rank1.astype(jnp.int32)
    r2 = rank2.astype(jnp.int32)
    ridx_ref[...] = jnp.where(row8 == 0, i1, jnp.where(row8 == 1, i2, jnp.where(row8 == 2, r1,
                              jnp.where(row8 == 3, r2, 0))))
    rgate_ref[...] = jnp.where(row8 == 0, g1, jnp.where(row8 == 1, g2, 0.0))


def out_proj(o_ret, o_gdn, o_swa, x_rows, mod, w_out, gain2, router_w, *, tm, seg, n_rows, n_lat_rows, tok_dtype):
    d = mod.shape[-1]
    moe = router_w is not None
    has_ctx = n_rows > n_lat_rows
    n_lat_tiles = n_lat_rows // tm
    full = lambda i: (0, 0)
    row = lambda i: (i, 0)
    lat_row = lambda i: (jnp.minimum(i, n_lat_tiles - 1), 0)
    ctx_row = lambda i: (jnp.maximum(i - n_lat_tiles, 0), 0)
    in_specs, args = [], []
    for pair, width in ((o_ret, RET_W), (o_gdn, GDN_W), (o_swa, SWA_W)):
        in_specs.append(pl.BlockSpec((tm, width), lat_row))
        args.append(pair[0])
        if has_ctx:
            in_specs.append(pl.BlockSpec((tm, width), ctx_row))
            args.append(pair[1])
    x_specs, x_args = _stream_specs(x_rows, tm, d, n_lat_tiles)
    in_specs += x_specs + [pl.BlockSpec((1, 6, d), lambda i: (seg(i), 0, 0)),
                           pl.BlockSpec(w_out.shape, full), pl.BlockSpec((1, d), full)]
    args += x_args + [mod, w_out, gain2]
    outs = [jax.ShapeDtypeStruct((n_rows, d), F32), jax.ShapeDtypeStruct((n_rows, d), tok_dtype)]
    out_specs = [pl.BlockSpec((tm, d), row), pl.BlockSpec((tm, d), row)]
    scratch = []
    if moe:
        in_specs.append(pl.BlockSpec(router_w.shape, full))
        args.append(router_w)
        col = lambda i: (0, i)
        outs += [jax.ShapeDtypeStruct((8, n_rows), jnp.int32), jax.ShapeDtypeStruct((8, n_rows), F32),
                 jax.ShapeDtypeStruct((N_EXPERTS, LANES), jnp.int32)]
        out_specs += [pl.BlockSpec((8, tm), col), pl.BlockSpec((8, tm), col), pl.BlockSpec((N_EXPERTS, LANES), full)]
        scratch = [pltpu.VMEM((N_EXPERTS, LANES), F32), pltpu.VMEM((tm, tm), BF16)]
    return pl.pallas_call(
        functools.partial(_outproj_kernel, tm=tm, moe=moe, n_lat_tiles=n_lat_tiles, has_ctx=has_ctx,
                          n_streams=len(x_args)),
        out_shape=tuple(outs),
        grid=(n_rows // tm,),
        in_specs=in_specs,
        out_specs=tuple(out_specs),
        scratch_shapes=scratch,
        compiler_params=_cparams("arbitrary"),
        name="out_proj_router" if moe else "out_proj",
    )(*args)


def _final_norm(x, gain):
    ms = jnp.mean(x * x, axis=-1, keepdims=True)
    return x * lax.rsqrt(ms + EPS) * gain


def _dense_ffn_kernel(tok_ref, x_ref, mod_ref, w1_ref, w3_ref, w2_ref, fg_ref, o_ref, *, last):
    t = tok_ref[...]
    h = (_silu(_dot(t, w1_ref[...])) * _dot(t, w3_ref[...])).astype(BF16)
    x = x_ref[...] + mod_ref[0, 5:6, :] * _dot(h, w2_ref[...])
    if last:
        x = _final_norm(x, fg_ref[...])
    o_ref[...] = x


def dense_ffn(tokens, xall, mod, w1, w3, w2, final_g, *, tm, seg, last):
    n, d = tokens.shape
    f = w1.shape[1]
    full = lambda i: (0, 0)
    row = lambda i: (i, 0)
    return pl.pallas_call(
        functools.partial(_dense_ffn_kernel, last=last),
        out_shape=jax.ShapeDtypeStruct((n, d), F32),
        grid=(n // tm,),
        in_specs=[pl.BlockSpec((tm, d), row), pl.BlockSpec((tm, d), row),
                  pl.BlockSpec((1, 6, d), lambda i: (seg(i), 0, 0)),
                  pl.BlockSpec((d, f), full, pipeline_mode=pl.Buffered(1)),
                  pl.BlockSpec((d, f), full, pipeline_mode=pl.Buffered(1)),
                  pl.BlockSpec((f, d), full, pipeline_mode=pl.Buffered(1)),
                  pl.BlockSpec((1, d), full)],
        out_specs=pl.BlockSpec((tm, d), row),
        compiler_params=_cparams("parallel"),
        name="dense_ffn",
    )(tokens, xall, mod, w1, w3, w2, final_g)


def _dispatch_kernel(pad_ref, dest_ref, tok_ref, xs_ref, zero_s, sem, zsem, *, tm, tmx):
    @pl.when(pl.program_id(0) == 0)
    def _():
        zero_s[...] = jnp.zeros_like(zero_s)

        def for_each_zero_copy(fn):
            for e in range(N_EXPERTS):
                @pl.when(pad_ref[N_EXPERTS + e] > 0)
                def _(e=e):
                    row = pl.multiple_of(pad_ref[e], tmx)
                    fn(pltpu.make_async_copy(zero_s, xs_ref.at[pl.ds(row, tmx)], zsem))
            for t in range(N_EXPERTS):
                @pl.when(t < pad_ref[2 * N_EXPERTS + 1])
                def _(t=t):
                    row = pl.multiple_of(pad_ref[2 * N_EXPERTS] + t * tmx, tmx)
                    fn(pltpu.make_async_copy(zero_s, xs_ref.at[pl.ds(row, tmx)], zsem))

        for_each_zero_copy(lambda c: c.start())
        for_each_zero_copy(lambda c: c.wait())

    def issue(r, carry):
        for k in range(2):
            dst = dest_ref[0, 0, 2 * r + k]
            pltpu.make_async_copy(tok_ref.at[pl.ds(r, 1)], xs_ref.at[pl.ds(dst, 1)], sem).start()
        return carry

    lax.fori_loop(0, tm, issue, 0, unroll=16)
    for _ in range(2):
        pltpu.make_async_copy(tok_ref, xs_ref.at[pl.ds(0, tm)], sem).wait()


def moe_dispatch(tokens, dest3, pad_info, n_slots, *, tm, tmx):
    n, d = tokens.shape
    grid_spec = pltpu.PrefetchScalarGridSpec(
        num_scalar_prefetch=1,
        grid=(n // tm,),
        in_specs=[pl.BlockSpec((1, 1, 2 * tm), lambda i, pad: (i, 0, 0), memory_space=pltpu.SMEM),
                  pl.BlockSpec((tm, d), lambda i, pad: (i, 0))],
        out_specs=pl.BlockSpec(memory_space=pl.ANY),
        scratch_shapes=[pltpu.VMEM((tmx, d), tokens.dtype), pltpu.SemaphoreType.DMA, pltpu.SemaphoreType.DMA],
    )
    return pl.pallas_call(
        functools.partial(_dispatch_kernel, tm=tm, tmx=tmx),
        out_shape=jax.ShapeDtypeStruct((n_slots, d), tokens.dtype),
        grid_spec=grid_spec,
        compiler_params=_cparams("arbitrary"),
        name="moe_dispatch",
    )(pad_info, dest3, tokens)


def _tile_expert(te_ref, nu_ref, i):
    return te_ref[jnp.clip(i, 0, nu_ref[0] - 1)]


def _moe_up_kernel(te_ref, nu_ref, xs_ref, w1_ref, w3_ref, h_ref, w1b_s, w3b_s):
    i = pl.program_id(1)

    @pl.when((i == 0) | (_tile_expert(te_ref, nu_ref, i) != _tile_expert(te_ref, nu_ref, i - 1)))
    def _():
        w1b_s[...] = w1_ref[0, 0].astype(BF16)
        w3b_s[...] = w3_ref[0, 0].astype(BF16)

    @pl.when(i < nu_ref[0])
    def _():
        x = xs_ref[...].astype(BF16)
        h_ref[...] = (_silu(_dot(x, w1b_s[...])) * _dot(x, w3b_s[...])).astype(h_ref.dtype)

    @pl.when(i >= nu_ref[0])
    def _():
        h_ref[...] = jnp.zeros_like(h_ref)


def _moe_down_kernel(te_ref, nu_ref, h_ref, w2_ref, y_ref, w2b_s):
    i = pl.program_id(1)

    @pl.when((i == 0) | (_tile_expert(te_ref, nu_ref, i) != _tile_expert(te_ref, nu_ref, i - 1)))
    def _():
        w2b_s[...] = w2_ref[0, 0].astype(BF16)

    @pl.when(i < nu_ref[0])
    def _():
        y_ref[...] = _dot(h_ref[...], w2b_s[...])

    @pl.when(i >= nu_ref[0])
    def _():
        y_ref[...] = jnp.zeros_like(y_ref)


def moe_ffn(xs, tile_expert, n_used, w1, w3, w2, *, li, tm, tf, tn):
    r, d = xs.shape
    f = w1.shape[3]

    def tile(i, nu):
        return jnp.minimum(i, nu[0] - 1)

    up_spec = pltpu.PrefetchScalarGridSpec(
        num_scalar_prefetch=2,
        grid=(f // tf, r // tm),
        in_specs=[pl.BlockSpec((tm, d), lambda j, i, te, nu: (tile(i, nu), 0)),
                  pl.BlockSpec((1, 1, d, tf), lambda j, i, te, nu: (li, te[tile(i, nu)], 0, j)),
                  pl.BlockSpec((1, 1, d, tf), lambda j, i, te, nu: (li, te[tile(i, nu)], 0, j))],
        out_specs=pl.BlockSpec((tm, tf), lambda j, i, te, nu: (i, j)),
        scratch_shapes=[pltpu.VMEM((d, tf), BF16), pltpu.VMEM((d, tf), BF16)],
    )
    hidden = pl.pallas_call(
        _moe_up_kernel,
        out_shape=jax.ShapeDtypeStruct((r, f), BF16),
        grid_spec=up_spec,
        compiler_params=_cparams("arbitrary", "arbitrary"),
        name="moe_up",
    )(tile_expert, n_used, xs, w1, w3)
    down_spec = pltpu.PrefetchScalarGridSpec(
        num_scalar_prefetch=2,
        grid=(d // tn, r // tm),
        in_specs=[pl.BlockSpec((tm, f), lambda n, i, te, nu: (tile(i, nu), 0)),
                  pl.BlockSpec((1, 1, f, tn), lambda n, i, te, nu: (li, te[tile(i, nu)], 0, n))],
        out_specs=pl.BlockSpec((tm, tn), lambda n, i, te, nu: (i, n)),
        scratch_shapes=[pltpu.VMEM((f, tn), BF16)],
    )
    return pl.pallas_call(
        _moe_down_kernel,
        out_shape=jax.ShapeDtypeStruct((r, d), F32),
        grid_spec=down_spec,
        compiler_params=_cparams("arbitrary", "arbitrary"),
        name="moe_down",
    )(tile_expert, n_used, hidden, w2)


def _combine_kernel(dest_ref, next_ref, gate_ref, x_ref, mod_ref, fg_ref, ys_ref, o_ref, buf_ref, sem, *, tm, last):
    i = pl.program_id(0)
    slot = i % 2

    def start_gather(idx_ref, s):
        def issue(r, carry):
            for k in range(2):
                src = idx_ref[0, 0, 2 * r + k]
                pltpu.make_async_copy(ys_ref.at[pl.ds(src, 1)], buf_ref.at[s, k, pl.ds(r, 1)], sem.at[s]).start()
            return carry

        lax.fori_loop(0, tm, issue, 0, unroll=16)

    @pl.when(i == 0)
    def _():
        start_gather(dest_ref, 0)

    @pl.when(i + 1 < pl.num_programs(0))
    def _():
        start_gather(next_ref, 1 - slot)

    for k in range(2):
        pltpu.make_async_copy(ys_ref.at[pl.ds(0, tm)], buf_ref.at[slot, k], sem.at[slot]).wait()
    gate = gate_ref[...]
    y = gate[:, 0:1] * buf_ref[slot, 0] + gate[:, 1:2] * buf_ref[slot, 1]
    x = x_ref[...] + mod_ref[0, 5:6, :] * y
    if last:
        x = _final_norm(x, fg_ref[...])
    o_ref[...] = x


def moe_combine(ys, dest3, rgate, xall, mod, final_g, *, tm, seg, last):
    n, d = xall.shape
    return pl.pallas_call(
        functools.partial(_combine_kernel, tm=tm, last=last),
        out_shape=jax.ShapeDtypeStruct((n, d), F32),
        grid=(n // tm,),
        in_specs=[pl.BlockSpec((1, 1, 2 * tm), lambda i: (i, 0, 0), memory_space=pltpu.SMEM),
                  pl.BlockSpec((1, 1, 2 * tm), lambda i: (jnp.minimum(i + 1, n // tm - 1), 0, 0),
                               memory_space=pltpu.SMEM),
                  pl.BlockSpec((tm, 8), lambda i: (i, 0)),
                  pl.BlockSpec((tm, d), lambda i: (i, 0)),
                  pl.BlockSpec((1, 6, d), lambda i: (seg(i), 0, 0)),
                  pl.BlockSpec((1, d), lambda i: (0, 0)),
                  pl.BlockSpec(memory_space=pl.ANY)],
        out_specs=pl.BlockSpec((tm, d), lambda i: (i, 0)),
        scratch_shapes=[pltpu.VMEM((2, 2, tm, d), F32), pltpu.SemaphoreType.DMA((2,))],
        compiler_params=_cparams("arbitrary"),
        name="moe_combine",
    )(dest3, dest3, rgate, xall, mod, final_g, ys)


def moe_layer(tokens, ridx, rgate, cnt, xall, mod, w1, w3, w2, final_g, *, li, tm, tmx, tf, seg, last):
    n = tokens.shape[0]
    counts = cnt[:, 0]
    padded = (counts + tmx - 1) // tmx * tmx
    pend = jnp.cumsum(padded)
    pstart = pend - padded
    group_start = sum(jnp.where(ridx[0:2] == e, pstart[e], 0) for e in range(N_EXPERTS))
    dest = (group_start + ridx[2:4]).T
    dest3 = dest.reshape(n // tm, 1, 2 * tm).astype(jnp.int32)
    rgate = rgate.T
    n_tiles = (2 * n) // tmx + N_EXPERTS
    n_used = (pend[-1] // tmx).astype(jnp.int32).reshape(1)
    tile_start = jnp.arange(n_tiles, dtype=jnp.int32) * tmx
    tile_expert = jnp.minimum(jnp.sum(tile_start[:, None] >= pend[None, :], axis=1), N_EXPERTS - 1).astype(jnp.int32)
    pad_info = jnp.concatenate([pend - tmx, (padded > 0).astype(jnp.int32), pend[-1:], n_tiles - n_used]
                               ).astype(jnp.int32)
    xs = moe_dispatch(tokens, dest3, pad_info, n_tiles * tmx, tm=tm, tmx=tmx)
    ys = moe_ffn(xs, tile_expert, n_used, w1, w3, w2, li=li, tm=tmx, tf=tf, tn=MOE_DOWN_TILE)
    return moe_combine(ys, dest3, rgate, xall, mod, final_g, tm=tm, seg=seg, last=last)


def rope_tables(seq):
    rows = seq // GRID_W
    row = jnp.repeat(jnp.arange(rows, dtype=F32), GRID_W)
    col = jnp.tile(jnp.arange(GRID_W, dtype=F32), rows)
    n_freq = HEAD_DIM // 4
    inv = ROPE_BASE ** (-jnp.arange(n_freq, dtype=F32) / n_freq)
    ang = jnp.concatenate([row[:, None] * inv, col[:, None] * inv], axis=-1)
    cos, sin = jnp.cos(ang), jnp.sin(ang)
    cos64 = jnp.concatenate([cos, cos], axis=-1)
    sin64 = jnp.concatenate([-sin, sin], axis=-1)
    return jnp.tile(cos64, (1, 2)), jnp.tile(sin64, (1, 2))


def _tile_lanes(t, width):
    reps = width // t.shape[1]
    return t if reps == 1 else jnp.concatenate([t] * reps, axis=1)


def _rope(t, cos, sin):
    w = t.shape[1]
    lane = lax.broadcasted_iota(jnp.int32, t.shape, 1) % HEAD_DIM
    half = HEAD_DIM // 2
    rot = jnp.where(lane < half, pltpu.roll(t, w - half, 1), pltpu.roll(t, half, 1))
    return t * _tile_lanes(cos, w) + rot * _tile_lanes(sin, w)


def _group_matrix(width, value):
    r = lax.broadcasted_iota(jnp.int32, (width, width), 0) // HEAD_DIM
    c = lax.broadcasted_iota(jnp.int32, (width, width), 1) // HEAD_DIM
    return jnp.where(r == c, value, 0.0).astype(BF16)


RET_CHUNK = 128


def _retention_kernel(lat_ref, ctx_ref, cos_ref, sin_ref, dec_ref, gn_ref, ol_ref, oc_ref,
                      q_s, k_s, bn_s, st_s, o_s, dm_s, *, n_ctx, n_lat):
    ch = RET_CHUNK
    nc, nl = n_ctx // ch, n_lat // ch
    lg = -jnp.exp(dec_ref[...])
    lgf, lgb = lg[0:1], lg[1:2]
    pos = lax.broadcasted_iota(jnp.int32, (ch, 1), 0).astype(F32)
    wkf = jnp.exp((ch - 1 - pos) * lgf)
    wkb = jnp.exp(pos * lgb)
    wqf = jnp.exp((pos + 1) * lgf)
    wqb = jnp.exp((ch - pos) * lgb)
    cdf = jnp.exp(ch * lgf)
    cdb = jnp.exp(ch * lgb)
    blockmask = (lax.broadcasted_iota(jnp.int32, (RET_W, RET_W), 0) // HEAD_DIM
                 == lax.broadcasted_iota(jnp.int32, (RET_W, RET_W), 1) // HEAD_DIM)
    gmean = _group_matrix(RET_W, 1.0 / HEAD_DIM)
    diff = (lax.broadcasted_iota(jnp.int32, (ch, ch), 0) - lax.broadcasted_iota(jnp.int32, (ch, ch), 1)).astype(F32)
    gn = gn_ref[...]

    def load_qkv(ref, t0):
        return (ref[pl.ds(t0, ch), 0:RET_W].astype(F32), ref[pl.ds(t0, ch), RET_W:2 * RET_W].astype(F32),
                ref[pl.ds(t0, ch), 2 * RET_W:3 * RET_W])

    def reverse_step(ref, t0, c, roped):
        q, k, v = load_qkv(ref, t0)
        if roped:
            cos, sin = cos_ref[pl.ds(t0, ch), :], sin_ref[pl.ds(t0, ch), :]
            q, k = _rope(q, cos, sin), _rope(k, cos, sin)
        k = k * (HEAD_DIM ** -0.5)
        r0 = pl.multiple_of(c * ch, ch)
        q_s[pl.ds(r0, ch), :] = q.astype(BF16)
        k_s[pl.ds(r0, ch), :] = k.astype(BF16)
        st = st_s[...]
        bn_s[c] = st.astype(BF16)
        kvb = _dot_tn((k * wkb).astype(BF16), v)
        st_s[...] = cdb * st + jnp.where(blockmask, kvb, 0.0)

    st_s[...] = jnp.zeros_like(st_s)

    def rev_ctx(n, carry):
        c = nc - 1 - n
        reverse_step(ctx_ref, pl.multiple_of(c * ch, ch), c, False)
        return carry

    def rev_lat(n, carry):
        c = nl - 1 - n
        reverse_step(lat_ref, pl.multiple_of(c * ch, ch), nc + c, True)
        return carry

    lax.fori_loop(0, nc, rev_ctx, 0)
    lax.fori_loop(0, nl, rev_lat, 0)

    for h in range(RET_HEADS):
        lf, lb = lgf[:, h * HEAD_DIM:h * HEAD_DIM + 1], lgb[:, h * HEAD_DIM:h * HEAD_DIM + 1]
        dm_s[h] = jnp.where(diff == 0.0, 2.0, jnp.exp(jnp.abs(diff) * jnp.where(diff > 0.0, lf, lb)))

    def forward_step(ref, t0, c):
        r0 = pl.multiple_of(c * ch, ch)
        q = q_s[pl.ds(r0, ch), :]
        k = k_s[pl.ds(r0, ch), :]
        v = ref[pl.ds(t0, ch), 2 * RET_W:3 * RET_W]
        sls = [slice(h * HEAD_DIM, (h + 1) * HEAD_DIM) for h in range(RET_HEADS)]
        scores = [_dot_nt(q[:, sl], k[:, sl]) for sl in sls]
        probs = [(s * dm_s[h]).astype(BF16) for h, s in enumerate(scores)]
        heads = [_dot(p, v[:, sl]) for p, sl in zip(probs, sls)]
        qf = q.astype(F32)
        st = st_s[...]
        o = jnp.concatenate(heads, axis=1)
        o_s[pl.ds(r0, ch), :] = (o + _dot((qf * wqf).astype(BF16), st.astype(BF16))
                                 + _dot((qf * wqb).astype(BF16), bn_s[c]))
        kvf = _dot_tn((k.astype(F32) * wkf).astype(BF16), v)
        st_s[...] = cdf * st + jnp.where(blockmask, kvf, 0.0)

    st_s[...] = jnp.zeros_like(st_s)

    def fwd_ctx(n, carry):
        forward_step(ctx_ref, pl.multiple_of(n * ch, ch), n)
        return carry

    def fwd_lat(n, carry):
        forward_step(lat_ref, pl.multiple_of(n * ch, ch), nc + n)
        return carry

    lax.fori_loop(0, nc, fwd_ctx, 0)
    lax.fori_loop(0, nl, fwd_lat, 0, unroll=2)

    def finish(ref, out_ref, n_rows, row_off):
        ft = 2 * ch

        def body(i, carry):
            t0 = pl.multiple_of(i * ft, ft)
            o = o_s[pl.ds(pl.multiple_of(row_off + t0, ft), ft), :]
            g = ref[pl.ds(t0, ft), 3 * RET_W:4 * RET_W].astype(F32)
            xc = o - _split_dot(o, gmean)
            var = _split_dot(xc * xc, gmean)
            out_ref[pl.ds(t0, ft), :] = (xc * lax.rsqrt(var + EPS) * gn * _silu(g)).astype(out_ref.dtype)
            return carry

        lax.fori_loop(0, n_rows // ft, body, 0)

    finish(ctx_ref, oc_ref, n_ctx, 0)
    finish(lat_ref, ol_ref, n_lat, n_ctx)


def retention_mixer(p_ret, cos, sin, dec_lanes, gn_gain, *, n_batch, n_lat, n_ctx):
    n = p_ret.shape[0]
    width = p_ret.shape[1]
    ctx_blk0 = (n_batch * n_lat) // n_ctx
    t_all = n_lat + n_ctx
    nch = t_all // RET_CHUNK
    return pl.pallas_call(
        functools.partial(_retention_kernel, n_ctx=n_ctx, n_lat=n_lat),
        out_shape=(jax.ShapeDtypeStruct((n_batch * n_lat, RET_W), BF16),
                   jax.ShapeDtypeStruct((n_batch * n_ctx, RET_W), BF16)),
        grid=(n_batch,),
        in_specs=[pl.BlockSpec((n_lat, width), lambda b: (b, 0)),
                  pl.BlockSpec((n_ctx, width), lambda b: (ctx_blk0 + b, 0)),
                  pl.BlockSpec((n_lat, LANES), lambda b: (0, 0)),
                  pl.BlockSpec((n_lat, LANES), lambda b: (0, 0)),
                  pl.BlockSpec((2, RET_W), lambda b: (0, 0)),
                  pl.BlockSpec((1, RET_W), lambda b: (0, 0))],
        out_specs=(pl.BlockSpec((n_lat, RET_W), lambda b: (b, 0)),
                   pl.BlockSpec((n_ctx, RET_W), lambda b: (b, 0))),
        scratch_shapes=[pltpu.VMEM((t_all, RET_W), BF16), pltpu.VMEM((t_all, RET_W), BF16),
                        pltpu.VMEM((nch, RET_W, RET_W), BF16), pltpu.VMEM((RET_W, RET_W), F32),
                        pltpu.VMEM((t_all, RET_W), F32), pltpu.VMEM((RET_HEADS, RET_CHUNK, RET_CHUNK), F32)],
        compiler_params=_cparams("parallel"),
        name="retention",
    )(p_ret, p_ret, cos, sin, dec_lanes, gn_gain)


GDN_CHUNK = 64
GDN_PREP = 128
GDN_HALO = 8
GDN_SOLVE_CHUNKS = 4


def _softplus(x):
    return jnp.maximum(x, 0.0) + jnp.log(1.0 + jnp.exp(-jnp.abs(x)))


def _head_block_diag(x):
    n, w = x.shape
    heads = w // HEAD_DIM
    rows = lax.broadcasted_iota(jnp.int32, (heads * n, w), 0) // n
    cols = lax.broadcasted_iota(jnp.int32, (heads * n, w), 1) // HEAD_DIM
    return jnp.where(rows == cols, jnp.concatenate([x] * heads, axis=0), 0.0)


def _unit_tri_inverses(mats):
    n, w = mats[0].shape
    ri = lax.broadcasted_iota(jnp.int32, (n, w), 0)
    ci = lax.broadcasted_iota(jnp.int32, (n, w), 1) % HEAD_DIM
    b16 = (ri // 16) == (ci // 16)
    b32 = (ri // 32) == (ci // 32)
    eye = jnp.where(ri == ci, 1.0, 0.0)
    ps = [jnp.where(b16, -a, 0.0) for a in mats]
    ts = [eye + p for p in ps]
    for _ in range(3):
        pbs = [p.astype(BF16) for p in ps]
        ps = [_dot(pb, _head_block_diag(pb)) for pb in pbs]
        ts = [t + _dot(t.astype(BF16), _head_block_diag(p.astype(BF16))) for t, p in zip(ts, ps)]
    for keep in (b32 & ~b16, ~b32):
        offs = [jnp.where(keep, a, 0.0).astype(BF16) for a in mats]
        tbs = [t.astype(BF16) for t in ts]
        mids = [_dot(tb, _head_block_diag(off)).astype(BF16) for tb, off in zip(tbs, offs)]
        ts = [t - _dot(mid, _head_block_diag(tb)) for t, mid, tb in zip(ts, mids, tbs)]
    return ts


def _gdn_prep_kernel(lat_ref, ctx_ref, abl_ref, abc_ref, cw_ref, par_ref, q_s, k_s, v_s, gb_s, *, n_ctx, n_lat):
    pt = GDN_PREP
    halo = GDN_HALO
    qkv_w = 3 * GDN_W
    gsum = _group_matrix(GDN_W, 1.0)
    lane128 = lax.broadcasted_iota(jnp.int32, (pt, LANES), 1)
    neg_a = -jnp.exp(par_ref[0:1, :])
    dt_bias = par_ref[1:2, :]
    taps = [cw_ref[j:j + 1, :] for j in range(GDN_CONV)]

    def prep(ref, ab_ref, n_rows, row_off):
        n_tiles = n_rows // pt

        def body(i, carry):
            t0 = pl.multiple_of(i * pt, pt)
            cur = ref[pl.ds(t0, pt), 0:qkv_w].astype(F32)
            p0 = pl.multiple_of(jnp.maximum(t0 - 16, 0), 16)
            n0 = pl.multiple_of(jnp.minimum(t0 + pt, n_rows - 16), 16)
            prev = ref[pl.ds(p0, 16), 0:qkv_w].astype(F32)[16 - halo:16]
            nxt = ref[pl.ds(n0, 16), 0:qkv_w].astype(F32)[0:halo]
            prev = jnp.where(i > 0, prev, 0.0)
            nxt = jnp.where(i < n_tiles - 1, nxt, 0.0)
            ext = jnp.concatenate([prev, cur, nxt], axis=0)
            rows = pt + 2 * halo
            acc = None
            for j in range(GDN_CONV):
                s = j - (GDN_CONV - 1) // 2
                sh = ext if s == 0 else pltpu.roll(ext, (rows - s) % rows, 0)
                term = sh[halo:halo + pt] * taps[j]
                acc = term if acc is None else acc + term
            act = _silu(acc)
            q, k, v = act[:, 0:GDN_W], act[:, GDN_W:2 * GDN_W], act[:, 2 * GDN_W:3 * GDN_W]
            q = q * lax.rsqrt(_split_dot(q * q, gsum) + EPS) * (HEAD_DIM ** -0.5)
            k = k * lax.rsqrt(_split_dot(k * k, gsum) + EPS)
            r0 = pl.multiple_of(row_off + t0, pt)
            q_s[pl.ds(r0, pt), :] = q.astype(BF16)
            k_s[pl.ds(r0, pt), :] = k.astype(BF16)
            v_s[pl.ds(r0, pt), :] = v.astype(BF16)
            ab = ab_ref[pl.ds(t0, pt), :]
            g = neg_a * _softplus(ab + dt_bias)
            gb_s[pl.ds(r0, pt), :] = jnp.where(lane128 < 2 * GDN_HEADS, g, _sigmoid(ab))
            return carry

        lax.fori_loop(0, n_tiles, body, 0)

    prep(ctx_ref, abc_ref, n_ctx, 0)
    prep(lat_ref, abl_ref, n_lat, n_ctx)


def _gdn_kernel(q_s, k_s, v_s, gb_s, zl_ref, zc_ref, ng_ref, ol_ref, oc_ref,
                o_s, dec_s, uf_s, ub_s, wf_s, wb_s, af_s, ab_s, qf_s, qb_s, kf_s, kb_s, sf_s, sb_s, *,
                n_ctx, n_lat, n_solve):
    u_s, w_s, a_s, qd_s, kd_s, st_s = (uf_s, ub_s), (wf_s, wb_s), (af_s, ab_s), (qf_s, qb_s), (kf_s, kb_s), (sf_s, sb_s)
    ch = GDN_CHUNK
    pt = GDN_PREP
    gmean = _group_matrix(GDN_W, 1.0 / HEAD_DIM)
    ri = lax.broadcasted_iota(jnp.int32, (ch, ch), 0)
    ci = lax.broadcasted_iota(jnp.int32, (ch, ch), 1)
    tri = tuple(jnp.where(m, 1.0, 0.0).astype(BF16) for m in (ri >= ci, ri <= ci))
    rp = lax.broadcasted_iota(jnp.int32, (ch, GDN_W), 0)
    cp = lax.broadcasted_iota(jnp.int32, (ch, GDN_W), 1) % HEAD_DIM
    incl = (rp >= cp, rp <= cp)
    strict = (rp > cp, rp < cp)
    ncc, nlc = n_ctx // ch, n_lat // ch
    head_blocks = (lax.broadcasted_iota(jnp.int32, (GDN_W, GDN_W), 0) // HEAD_DIM
                   == lax.broadcasted_iota(jnp.int32, (GDN_W, GDN_W), 1) // HEAD_DIM)

    def solve_chunks(cs):
        n_cs = len(cs)
        items = [(j, d) for j in range(n_cs) for d in range(2)]
        rows = [pl.ds(pl.multiple_of(c * ch, ch), ch) for c in cs]
        gbc = [gb_s[r, :] for r in rows]
        gcs = []
        for j in range(n_cs):
            per_dir = []
            for d in range(2):
                rem, acc = gbc[j], None
                for _ in range(3):
                    part = rem.astype(BF16)
                    term = _dot(tri[d], part)
                    acc = term if acc is None else acc + term
                    rem = rem - part.astype(F32)
                per_dir.append(acc)
            gcs.append(per_dir)
        gcs_t = [[g.T for g in per_dir] for per_dir in gcs]
        edge = (ch - 1, 0)

        def spread(mat, first):
            return jnp.concatenate([jnp.broadcast_to(mat[:, first + h:first + h + 1], (ch, HEAD_DIM))
                                    for h in range(GDN_HEADS)], axis=1)

        gc = [spread(gcs[j][d], d * GDN_HEADS) for j, d in items]
        gr = [jnp.concatenate([gcs_t[j][d][d * GDN_HEADS + h:d * GDN_HEADS + h + 1, :] for h in range(GDN_HEADS)],
                              axis=1) for j, d in items]
        gtot = [x[edge[d]:edge[d] + 1, :] for x, (_, d) in zip(gc, items)]
        beta = [spread(gbc[j], 2 * GDN_HEADS + d * GDN_HEADS) for j, d in items]
        decay = [jnp.where(incl[d], jnp.exp(jnp.minimum(gc[i] - gr[i], 0.0)), 0.0) for i, (_, d) in enumerate(items)]
        q = [q_s[r, :] for r in rows]
        k = [k_s[r, :] for r in rows]
        kf = [x.astype(F32) for x in k]
        v = [v_s[r, :].astype(F32) for r in rows]
        k_rows = [_head_block_diag(x) for x in k]
        qk = [_dot_nt(x, y) for x, y in zip(q, k_rows)]
        kb = [kf[j] * beta[i] for i, (j, _) in enumerate(items)]
        kk = [_dot_nt(kb[i].astype(BF16), k_rows[j]) for i, (j, _) in enumerate(items)]
        a = [jnp.where(strict[d], kk[i] * decay[i], 0.0) for i, (_, d) in enumerate(items)]
        t = [x.astype(BF16) for x in _unit_tri_inverses(a)]
        eg = [jnp.exp(x) for x in gc]
        u = [_dot(t[i], _head_block_diag((v[j] * beta[i]).astype(BF16))) for i, (j, _) in enumerate(items)]
        w = [_dot(t[i], _head_block_diag((kb[i] * eg[i]).astype(BF16))) for i in range(len(items))]
        for i, (j, d) in enumerate(items):
            u_s[d][rows[j], :] = u[i].astype(BF16)
            w_s[d][rows[j], :] = w[i].astype(BF16)
            a_s[d][rows[j], :] = (qk[j] * decay[i]).astype(BF16)
            qd_s[d][rows[j], :] = (q[j].astype(F32) * eg[i]).astype(BF16)
            kd_s[d][rows[j], :] = (kf[j] * jnp.exp(gtot[i] - gc[i])).astype(BF16)
            dec_s[pl.ds(cs[j], 1), d * GDN_W:(d + 1) * GDN_W] = jnp.exp(gtot[i])

    def scan_step(cf, cb):
        cs = (cf, cb)
        rows = [pl.ds(pl.multiple_of(c * ch, ch), ch) for c in cs]
        s = [st[...] for st in st_s]
        lhs = [jnp.concatenate([w_s[d][rows[d], :], qd_s[d][rows[d], :]], axis=0) for d in range(2)]
        prod = [_dot(lhs[d], s[d].astype(BF16)) for d in range(2)]
        vb = [(u_s[d][rows[d], :].astype(F32) - prod[d][0:ch]).astype(BF16) for d in range(2)]
        vbd = [jnp.where(head_blocks, jnp.concatenate([x] * GDN_HEADS, axis=0), 0.0) for x in vb]
        av = [_dot(a_s[d][rows[d], :], vbd[d]) for d in range(2)]
        upd = [_dot_tn(kd_s[d][rows[d], :], vb[d]) for d in range(2)]
        for d in range(2):
            dec = dec_s[pl.ds(cs[d], 1), d * GDN_W:(d + 1) * GDN_W]
            st_s[d][...] = s[d] * dec + jnp.where(head_blocks, upd[d], 0.0)
        for d in range(2):
            o_s[rows[d], :] += prod[d][ch:2 * ch] + av[d]

    def solve_body(n, carry):
        solve_chunks([n * n_solve + j for j in range(n_solve)])
        return carry

    lax.fori_loop(0, (ncc + nlc) // n_solve, solve_body, 0)

    for st in st_s:
        st[...] = jnp.zeros_like(st)
    o_s[...] = jnp.zeros_like(o_s)

    def scan_ctx(n, carry):
        scan_step(n, ncc - 1 - n)
        return carry

    def scan_lat(n, carry):
        scan_step(ncc + n, ncc + nlc - 1 - n)
        return carry

    lax.fori_loop(0, ncc, scan_ctx, 0)
    lax.fori_loop(0, nlc, scan_lat, 0, unroll=2)

    ng = ng_ref[...]

    def finish(z_ref, out_ref, n_rows, row_off):
        def body(i, carry):
            t0 = pl.multiple_of(i * pt, pt)
            o = o_s[pl.ds(pl.multiple_of(row_off + t0, pt), pt), :]
            z = z_ref[pl.ds(t0, pt), :].astype(F32)
            y = o * lax.rsqrt(_split_dot(o * o, gmean) + EPS) * ng * _silu(z)
            out_ref[pl.ds(t0, pt), :] = y.astype(out_ref.dtype)
            return carry

        lax.fori_loop(0, n_rows // pt, body, 0)

    finish(zc_ref, oc_ref, n_ctx, 0)
    finish(zl_ref, ol_ref, n_lat, n_ctx)


def gdn_mixer(p_gdn, p_ab, conv_w, a_log, dt_bias, norm_gain, *, n_batch, n_lat, n_ctx):
    width = p_gdn.shape[1]
    ctx_blk0 = (n_batch * n_lat) // n_ctx
    t_all = n_lat + n_ctx
    cw = jnp.pad(conv_w.astype(F32), ((0, 8 - GDN_CONV), (0, 0)))
    par = jnp.zeros((8, LANES), F32)
    par = par.at[0, :2 * GDN_HEADS].set(a_log.reshape(-1)).at[1, :2 * GDN_HEADS].set(dt_bias.reshape(-1))
    ng = jnp.tile(norm_gain.astype(F32).reshape(1, HEAD_DIM), (1, GDN_HEADS))
    seq = lambda w: pl.BlockSpec((t_all, w), lambda b: (b, 0))
    q, k, v, gb = pl.pallas_call(
        functools.partial(_gdn_prep_kernel, n_ctx=n_ctx, n_lat=n_lat),
        out_shape=(jax.ShapeDtypeStruct((n_batch * t_all, GDN_W), BF16),) * 3
        + (jax.ShapeDtypeStruct((n_batch * t_all, LANES), F32),),
        grid=(n_batch,),
        in_specs=[pl.BlockSpec((n_lat, width), lambda b: (b, 0)),
                  pl.BlockSpec((n_ctx, width), lambda b: (ctx_blk0 + b, 0)),
                  pl.BlockSpec((n_lat, LANES), lambda b: (b, 0)),
                  pl.BlockSpec((n_ctx, LANES), lambda b: (ctx_blk0 + b, 0)),
                  pl.BlockSpec(cw.shape, lambda b: (0, 0)),
                  pl.BlockSpec(par.shape, lambda b: (0, 0))],
        out_specs=(seq(GDN_W), seq(GDN_W), seq(GDN_W), seq(LANES)),
        compiler_params=_cparams("parallel"),
        name="gdn_prep",
    )(p_gdn, p_gdn, p_ab, p_ab, cw, par)
    zcol = 3 * GDN_W // GDN_W
    once = lambda w: pl.BlockSpec((t_all, w), lambda b: (b, 0), pipeline_mode=pl.Buffered(1))
    n_solve = GDN_SOLVE_CHUNKS if (t_all // GDN_CHUNK) % GDN_SOLVE_CHUNKS == 0 else 1
    return pl.pallas_call(
        functools.partial(_gdn_kernel, n_ctx=n_ctx, n_lat=n_lat, n_solve=n_solve),
        out_shape=(jax.ShapeDtypeStruct((n_batch * n_lat, GDN_W), BF16),
                   jax.ShapeDtypeStruct((n_batch * n_ctx, GDN_W), BF16)),
        grid=(n_batch,),
        in_specs=[once(GDN_W), once(GDN_W), once(GDN_W), pl.BlockSpec((t_all, LANES), lambda b: (b, 0)),
                  pl.BlockSpec((n_lat, GDN_W), lambda b: (b, zcol)),
                  pl.BlockSpec((n_ctx, GDN_W), lambda b: (ctx_blk0 + b, zcol)),
                  pl.BlockSpec((1, GDN_W), lambda b: (0, 0))],
        out_specs=(pl.BlockSpec((n_lat, GDN_W), lambda b: (b, 0)),
                   pl.BlockSpec((n_ctx, GDN_W), lambda b: (b, 0))),
        scratch_shapes=[pltpu.VMEM((t_all, GDN_W), F32), pltpu.VMEM((t_all // GDN_CHUNK, 2 * GDN_W), F32)]
        + [pltpu.VMEM((t_all, GDN_W), BF16)] * 10
        + [pltpu.VMEM((GDN_W, GDN_W), F32)] * 2,
        compiler_params=_cparams("parallel"),
        name="gdn",
    )(q, k, v, gb, p_gdn, p_gdn, ng)


SWA_BLOCK = 128
SWA_BAND = 3 * SWA_BLOCK
SWA_Q_PER_STEP = 4


def _swa_kernel(sink_ref, q_ref, kl_ref, vl_ref, kc_ref, vc_ref, cos_ref, sin_ref, o_ref, *, band, n_lat, n_sub):
    blk = SWA_BLOCK
    kc = kc_ref[...]
    vc = vc_ref[...]
    for sub in range(n_sub):
        rows = slice(sub * blk, (sub + 1) * blk)
        o_ref[rows, :] = _swa_block(pl.program_id(1) * n_sub + sub, q_ref[rows, :].astype(F32), sink_ref, kl_ref,
                                    vl_ref, kc, vc, cos_ref, sin_ref, band=band, n_lat=n_lat).astype(o_ref.dtype)


def _swa_block(i, q, sink_ref, kl_ref, vl_ref, kc, vc, cos_ref, sin_ref, *, band, n_lat):
    blk = SWA_BLOCK
    if band:
        q0 = pl.multiple_of(i * blk, blk)
        start = pl.multiple_of(jnp.clip((i - 1) * blk, 0, n_lat - SWA_BAND), blk)
        q = _rope(q, cos_ref[pl.ds(q0, blk), :], sin_ref[pl.ds(q0, blk), :])
        kb = _rope(kl_ref[pl.ds(start, SWA_BAND), :].astype(F32),
                   cos_ref[pl.ds(start, SWA_BAND), :], sin_ref[pl.ds(start, SWA_BAND), :]).astype(BF16)
        vb = vl_ref[pl.ds(start, SWA_BAND), :]
        keys = jnp.concatenate([kb, kc], axis=0)
        vals = jnp.concatenate([vb, vc], axis=0)
    else:
        keys, vals = kc, vc
    qb = (q * (HEAD_DIM ** -0.5 * LOG2E)).astype(BF16)
    n_keys = keys.shape[0]
    if band:
        q_pos = q0 + lax.broadcasted_iota(jnp.int32, (blk, 1), 0)
        col_id = lax.broadcasted_iota(jnp.int32, (1, n_keys), 1)
        valid = (col_id >= SWA_BAND) | (jnp.abs(start + col_id - q_pos) <= WINDOW)
        bias = jnp.where(valid, 0.0, NEG_BIG)
    v_lane = lax.broadcasted_iota(jnp.int32, vals.shape, 1) // HEAD_DIM
    outs = []
    for g in range(SWA_KV_HEADS):
        ksl = slice(g * HEAD_DIM, (g + 1) * HEAD_DIM)
        v_ext = jnp.where(v_lane == g, vals, 1.0)
        den_lane = (1 - g) * HEAD_DIM
        q4 = jnp.concatenate([qb[:, (g * SWA_GROUP + r) * HEAD_DIM:(g * SWA_GROUP + r + 1) * HEAD_DIM]
                              for r in range(SWA_GROUP)], axis=0)
        s = _dot_nt(q4, keys[:, ksl])
        es, ms, sinks = [], [], []
        for r in range(SWA_GROUP):
            sr = s[r * blk:(r + 1) * blk]
            if band:
                sr = sr + bias
            sink2 = sink_ref[g * SWA_GROUP + r] * LOG2E
            m = jnp.maximum(jnp.max(sr, axis=-1, keepdims=True), sink2)
            es.append(jnp.exp2(sr - m).astype(BF16))
            ms.append(m)
            sinks.append(sink2)
        ov = _dot(jnp.concatenate(es, axis=0), v_ext)
        for r in range(SWA_GROUP):
            ovr = ov[r * blk:(r + 1) * blk]
            den = ovr[:, den_lane:den_lane + 1] + jnp.exp2(sinks[r] - ms[r])
            outs.append(ovr[:, ksl] * (1.0 / den))
    return jnp.concatenate(outs, axis=1)


def swa_mixer(p_swa, cos, sin, sink, *, n_batch, n_lat, n_ctx):
    blk = SWA_BLOCK
    ctx_row0 = n_batch * n_lat
    kcol, vcol = SWA_W // SWA_KV_W, SWA_W // SWA_KV_W + 1
    sink = sink.astype(F32)

    def call(band):
        n_q = n_lat if band else n_ctx
        n_sub = min(SWA_Q_PER_STEP, n_q // blk)
        qblk = n_sub * blk
        nb = n_q // qblk
        qrow0 = 0 if band else ctx_row0 // qblk
        grid_spec = pltpu.PrefetchScalarGridSpec(
            num_scalar_prefetch=1,
            grid=(n_batch, nb),
            in_specs=[pl.BlockSpec((qblk, SWA_W), lambda b, i, s: (qrow0 + b * nb + i, 0)),
                      pl.BlockSpec((n_lat, SWA_KV_W), lambda b, i, s: (b, kcol)),
                      pl.BlockSpec((n_lat, SWA_KV_W), lambda b, i, s: (b, vcol)),
                      pl.BlockSpec((n_ctx, SWA_KV_W), lambda b, i, s: (ctx_row0 // n_ctx + b, kcol)),
                      pl.BlockSpec((n_ctx, SWA_KV_W), lambda b, i, s: (ctx_row0 // n_ctx + b, vcol)),
                      pl.BlockSpec((n_lat, LANES), lambda b, i, s: (0, 0)),
                      pl.BlockSpec((n_lat, LANES), lambda b, i, s: (0, 0))],
            out_specs=pl.BlockSpec((qblk, SWA_W), lambda b, i, s: (b * nb + i, 0)),
        )
        return pl.pallas_call(
            functools.partial(_swa_kernel, band=band, n_lat=n_lat, n_sub=n_sub),
            out_shape=jax.ShapeDtypeStruct((n_batch * n_q, SWA_W), BF16),
            grid_spec=grid_spec,
            compiler_params=_cparams("parallel", "arbitrary"),
            name="swa_latent" if band else "swa_context",
        )(sink, p_swa, p_swa, p_swa, p_swa, p_swa, cos, sin)

    return call(True), call(False)


MOE_ROW_TILE = 512
MOE_FF_TILE = 1792
MOE_DOWN_TILE = 1024


def _row_tile(n_lat, n_ctx_rows):
    for tm in (512, 256, 128):
        if n_lat % tm == 0 and n_ctx_rows % tm == 0:
            return tm
    raise ValueError("sequence lengths must be multiples of 128")


def kernel(x, c, ctx, c_ctx, ada_w, ada_b, norm1_g, w_in, ret_decay, ret_gn_g, gdn_conv_w, gdn_a_log, gdn_dt_bias,
           gdn_norm_g, swa_sink, w_out, norm2_g, ffn_w1, ffn_w3, ffn_w2, router_w, moe_w1, moe_w3, moe_w2, final_g):
    b, s, d = x.shape
    n_ctx = ctx.shape[1]
    depth = ada_w.shape[0]
    nl, ncx = b * s, b * n_ctx
    tm = _row_tile(s, ncx)
    seg = _seg_map(nl // tm, s // tm, b)

    cvec = jnp.zeros((8, d), F32).at[:b].set(c).at[b].set(c_ctx)
    mod_all = adaln(cvec, ada_w, ada_b).reshape(depth, 8, 6, d)
    xall = (x.reshape(nl, d), ctx.reshape(ncx, d))
    cos, sin = rope_tables(s)
    fg = final_g.reshape(1, d)

    ret_cols, gdn_cols = 4 * RET_W, 4 * GDN_W
    ab_cols = 4 * GDN_HEADS
    for layer in range(depth):
        last = layer == depth - 1
        mod = mod_all[layer]
        w = w_in[layer].astype(BF16)
        w_ret = w[:, :ret_cols]
        w_gdn = w[:, ret_cols:ret_cols + gdn_cols]
        w_ab = jnp.pad(w[:, ret_cols + gdn_cols:ret_cols + gdn_cols + ab_cols], ((0, 0), (0, LANES - ab_cols)))
        w_swa = w[:, ret_cols + gdn_cols + ab_cols:]
        p_ret, p_gdn, p_swa, p_ab = in_proj(xall, mod, norm1_g[layer].reshape(1, d), w_ret, w_gdn, w_swa, w_ab,
                                            tm=tm, seg=seg, n=nl + ncx, n_lat_rows=nl)
        dec_lanes = jnp.repeat(ret_decay[layer].astype(F32), HEAD_DIM, axis=1)
        o_ret = retention_mixer(p_ret, cos, sin, dec_lanes, ret_gn_g[layer].reshape(1, RET_W),
                                n_batch=b, n_lat=s, n_ctx=n_ctx)
        o_gdn = gdn_mixer(p_gdn, p_ab, gdn_conv_w[layer], gdn_a_log[layer], gdn_dt_bias[layer], gdn_norm_g[layer],
                          n_batch=b, n_lat=s, n_ctx=n_ctx)
        o_swa = swa_mixer(p_swa, cos, sin, swa_sink[layer], n_batch=b, n_lat=s, n_ctx=n_ctx)

        n_rows = nl if last else nl + ncx
        is_moe = layer % 2 == 1
        i = layer // 2
        rw = None
        if is_moe:
            rw = jnp.pad(router_w[i].astype(F32), ((0, 0), (0, LANES - N_EXPERTS)))
        res = out_proj(o_ret, o_gdn, o_swa, xall, mod, w_out[layer].astype(BF16), norm2_g[layer].reshape(1, d), rw,
                       tm=tm, seg=seg, n_rows=n_rows, n_lat_rows=nl, tok_dtype=F32 if is_moe else BF16)
        if is_moe:
            x_new, tokens, ridx, rgate, cnt = res
            xall = moe_layer(tokens, ridx, rgate, cnt, x_new, mod, moe_w1, moe_w3, moe_w2, fg,
                             li=i, tm=tm, tmx=MOE_ROW_TILE, tf=MOE_FF_TILE, seg=seg, last=last)
        else:
            x_new, tokens = res
            xall = dense_ffn(tokens, x_new, mod, ffn_w1[i].astype(BF16), ffn_w3[i].astype(BF16),
                             ffn_w2[i].astype(BF16), fg, tm=tm, seg=seg, last=last)
    return xall[:nl].reshape(b, s, d)
```

```python
import functools

import jax
import jax.numpy as jnp
from jax import lax
from jax.experimental import pallas as pl
from jax.experimental.pallas import tpu as pltpu

F32 = jnp.float32
BF16 = jnp.bfloat16

HEAD_DIM = 64
GRID_W = 64
ROPE_BASE = 10000.0
EPS = 1e-6
RET_HEADS = 4
GDN_HEADS = 4
GDN_CONV = 5
SWA_HEADS = 8
SWA_KV_HEADS = 2
SWA_GROUP = SWA_HEADS // SWA_KV_HEADS
WINDOW = 128
N_EXPERTS = 8
RET_W = RET_HEADS * HEAD_DIM
GDN_W = GDN_HEADS * HEAD_DIM
SWA_W = SWA_HEADS * HEAD_DIM
SWA_KV_W = SWA_KV_HEADS * HEAD_DIM
LANES = 128
VMEM_LIMIT = 56 * 1024 * 1024
NEG_BIG = -1e30
LOG2E = 1.4426950408889634


def _cparams(*sem):
    return pltpu.CompilerParams(dimension_semantics=sem, vmem_limit_bytes=VMEM_LIMIT)


def _dot(a, b):
    return jnp.dot(a, b, preferred_element_type=F32)


def _dot_nt(a, b):
    return lax.dot_general(a, b, (((1,), (1,)), ((), ())), preferred_element_type=F32)


def _dot_tn(a, b):
    return lax.dot_general(a, b, (((0,), (0,)), ((), ())), preferred_element_type=F32)


def _split_dot(x, m_bf16, terms=2):
    acc = None
    rem = x
    for _ in range(terms):
        part = rem.astype(BF16)
        d = _dot(part, m_bf16)
        acc = d if acc is None else acc + d
        rem = rem - part.astype(F32)
    return acc


def _sigmoid(x):
    return 1.0 / (1.0 + jnp.exp(-x))


def _silu(x):
    return x * _sigmoid(x)


def _rms_mod(x, gain, shift, scale):
    ms = jnp.mean(x * x, axis=-1, keepdims=True)
    y = x * lax.rsqrt(ms + EPS) * gain
    return y * (1.0 + scale) + shift


def _seg_map(n_lat_tiles, tiles_per_batch, n_batch):
    def seg(i):
        return jnp.where(i < n_lat_tiles, i // tiles_per_batch, n_batch)
    return seg


def _adaln_kernel(c_ref, w_ref, b_ref, o_ref):
    h = _silu(c_ref[...]).astype(BF16)
    o_ref[0] = _dot(h, w_ref[0].astype(BF16)) + b_ref[0]


def adaln(cvec, ada_w, ada_b):
    depth, d, n6 = ada_w.shape
    tn = 1536 if n6 % 1536 == 0 else n6
    rows = cvec.shape[0]
    return pl.pallas_call(
        _adaln_kernel,
        out_shape=jax.ShapeDtypeStruct((depth, rows, n6), F32),
        grid=(depth, n6 // tn),
        in_specs=[pl.BlockSpec((rows, d), lambda l, j: (0, 0)),
                  pl.BlockSpec((1, d, tn), lambda l, j: (l, 0, j)),
                  pl.BlockSpec((1, 1, tn), lambda l, j: (l, 0, j))],
        out_specs=pl.BlockSpec((1, rows, tn), lambda l, j: (l, 0, j)),
        compiler_params=_cparams("parallel", "parallel"),
        name="adaln",
    )(cvec, ada_w, ada_b.reshape(depth, 1, n6))


def _stream_rows(refs, n_streams, n_lat_tiles):
    if n_streams == 1:
        return refs[0][...]
    return jnp.where(pl.program_id(0) < n_lat_tiles, refs[0][...], refs[1][...])


def _stream_specs(x_rows, tm, d, n_lat_tiles):
    if not isinstance(x_rows, tuple):
        return [pl.BlockSpec((tm, d), lambda i: (i, 0))], [x_rows]
    return ([pl.BlockSpec((tm, d), lambda i: (jnp.minimum(i, n_lat_tiles - 1), 0)),
             pl.BlockSpec((tm, d), lambda i: (jnp.maximum(i - n_lat_tiles, 0), 0))], list(x_rows))


def _inproj_kernel(*refs, n_streams, n_lat_tiles):
    mod_ref, g_ref, wr_ref, wg_ref, ws_ref, wab_ref, pr_ref, pg_ref, ps_ref, pab_ref = refs[n_streams:]
    x = _stream_rows(refs, n_streams, n_lat_tiles)
    h = _rms_mod(x, g_ref[...], mod_ref[0, 0:1, :], mod_ref[0, 1:2, :]).astype(BF16)
    for w_ref, o_ref in ((wr_ref, pr_ref), (wg_ref, pg_ref), (ws_ref, ps_ref), (wab_ref, pab_ref)):
        n = w_ref.shape[1]
        step = 256 if n % 256 == 0 else n
        for c in range(0, n, step):
            o_ref[:, c:c + step] = _dot(h, w_ref[:, c:c + step]).astype(o_ref.dtype)


def in_proj(x_rows, mod, gain, w_ret, w_gdn, w_swa, w_ab, *, tm, seg, n, n_lat_rows):
    d = mod.shape[-1]
    full = lambda i: (0, 0)
    row = lambda i: (i, 0)
    x_specs, x_args = _stream_specs(x_rows, tm, d, n_lat_rows // tm)
    outs = (jax.ShapeDtypeStruct((n, w_ret.shape[1]), BF16), jax.ShapeDtypeStruct((n, w_gdn.shape[1]), BF16),
            jax.ShapeDtypeStruct((n, w_swa.shape[1]), BF16), jax.ShapeDtypeStruct((n, w_ab.shape[1]), F32))
    return pl.pallas_call(
        functools.partial(_inproj_kernel, n_streams=len(x_args), n_lat_tiles=n_lat_rows // tm),
        out_shape=outs,
        grid=(n // tm,),
        in_specs=x_specs + [
                  pl.BlockSpec((1, 6, d), lambda i: (seg(i), 0, 0)),
                  pl.BlockSpec((1, d), full),
                  pl.BlockSpec(w_ret.shape, full), pl.BlockSpec(w_gdn.shape, full),
                  pl.BlockSpec(w_swa.shape, full), pl.BlockSpec(w_ab.shape, full)],
        out_specs=tuple(pl.BlockSpec((tm, o.shape[1]), row) for o in outs),
        compiler_params=_cparams("parallel"),
        name="in_proj",
    )(*x_args, mod, gain, w_ret, w_gdn, w_swa, w_ab)


def _outproj_kernel(*refs, tm, moe, n_lat_tiles, has_ctx, n_streams):
    n_mix = 6 if has_ctx else 3
    mix = refs[:n_mix]
    x_in = _stream_rows(refs[n_mix:], n_streams, n_lat_tiles)
    refs = refs[n_mix + n_streams:]
    if moe:
        (mod_ref, w_ref, g_ref, rw_ref, xo_ref, tok_ref, ridx_ref, rgate_ref, cnt_ref, base_ref,
         earlier_ref) = refs
    else:
        mod_ref, w_ref, g_ref, xo_ref, tok_ref = refs
    if has_ctx:
        is_lat = pl.program_id(0) < n_lat_tiles
        o_ret, o_gdn, o_swa = (jnp.where(is_lat, mix[2 * k][...], mix[2 * k + 1][...]) for k in range(3))
    else:
        o_ret, o_gdn, o_swa = (m[...] for m in mix)
    y = _dot(o_ret, w_ref[0:RET_W, :])
    y = y + _dot(o_gdn, w_ref[RET_W:RET_W + GDN_W, :])
    y = y + _dot(o_swa, w_ref[RET_W + GDN_W:, :])
    x = x_in + mod_ref[0, 2:3, :] * y
    xo_ref[...] = x
    t = _rms_mod(x, g_ref[...], mod_ref[0, 3:4, :], mod_ref[0, 4:5, :])
    tok_ref[...] = t.astype(tok_ref.dtype)
    if not moe:
        return

    @pl.when(pl.program_id(0) == 0)
    def _():
        base_ref[...] = jnp.zeros_like(base_ref)
        r_i = lax.broadcasted_iota(jnp.int32, (tm, tm), 0)
        c_i = lax.broadcasted_iota(jnp.int32, (tm, tm), 1)
        earlier_ref[...] = jnp.where(r_i < c_i, 1.0, 0.0).astype(BF16)

    rw = rw_ref[...]
    rw_hi = rw.astype(BF16)
    rw_lo = (rw - rw_hi.astype(F32)).astype(BF16)
    t_hi = t.astype(BF16)
    t_lo = (t - t_hi.astype(F32)).astype(BF16)
    logits = _dot(t_hi, rw_hi) + _dot(t_hi, rw_lo) + _dot(t_lo, rw_hi)
    lt = logits.T[0:N_EXPERTS, :]
    e_id = lax.broadcasted_iota(jnp.int32, lt.shape, 0)
    m1 = jnp.max(lt, axis=0, keepdims=True)
    i1 = jnp.min(jnp.where(lt == m1, e_id, N_EXPERTS), axis=0, keepdims=True)
    rest = jnp.where(e_id == i1, NEG_BIG, lt)
    m2 = jnp.max(rest, axis=0, keepdims=True)
    i2 = jnp.min(jnp.where(rest == m2, e_id, N_EXPERTS), axis=0, keepdims=True)
    e2 = jnp.exp(m2 - m1)
    g1 = 1.0 / (1.0 + e2)
    g2 = e2 * g1
    oh1 = jnp.where(e_id == i1, 1.0, 0.0)
    oh2 = jnp.where(e_id == i2, 1.0, 0.0)
    cum = _dot(jnp.concatenate([oh1, oh2], axis=0).astype(BF16), earlier_ref[...])
    cum1, cum2 = cum[0:N_EXPERTS], cum[N_EXPERTS:2 * N_EXPERTS]
    cnt1 = jnp.sum(oh1, axis=1, keepdims=True)
    cnt2 = jnp.sum(oh2, axis=1, keepdims=True)
    base = base_ref[:, 0:1]
    rank1 = jnp.sum(oh1 * (base + cum1), axis=0, keepdims=True)
    rank2 = jnp.sum(oh2 * (base + cnt1 + cum2), axis=0, keepdims=True)
    base = base + cnt1 + cnt2
    base_ref[...] = jnp.broadcast_to(base, base_ref.shape)
    cnt_ref[...] = jnp.broadcast_to(base, cnt_ref.shape).astype(jnp.int32)
    row8 = lax.broadcasted_iota(jnp.int32, (8, tm), 0)
    r1 = rank1.astype(jnp.int32)
    r2 = rank2.astype(jnp.int32)
    ridx_ref[...] = jnp.where(row8 == 0, i1, jnp.where(row8 == 1, i2, jnp.where(row8 == 2, r1,
                              jnp.where(row8 == 3, r2, 0))))
    rgate_ref[...] = jnp.where(row8 == 0, g1, jnp.where(row8 == 1, g2, 0.0))


def out_proj(o_ret, o_gdn, o_swa, x_rows, mod, w_out, gain2, router_w, *, tm, seg, n_rows, n_lat_rows, tok_dtype):
    d = mod.shape[-1]
    moe = router_w is not None
    has_ctx = n_rows > n_lat_rows
    n_lat_tiles = n_lat_rows // tm
    full = lambda i: (0, 0)
    row = lambda i: (i, 0)
    lat_row = lambda i: (jnp.minimum(i, n_lat_tiles - 1), 0)
    ctx_row = lambda i: (jnp.maximum(i - n_lat_tiles, 0), 0)
    in_specs, args = [], []
    for pair, width in ((o_ret, RET_W), (o_gdn, GDN_W), (o_swa, SWA_W)):
        in_specs.append(pl.BlockSpec((tm, width), lat_row))
        args.append(pair[0])
        if has_ctx:
            in_specs.append(pl.BlockSpec((tm, width), ctx_row))
            args.append(pair[1])
    x_specs, x_args = _stream_specs(x_rows, tm, d, n_lat_tiles)
    in_specs += x_specs + [pl.BlockSpec((1, 6, d), lambda i: (seg(i), 0, 0)),
                           pl.BlockSpec(w_out.shape, full), pl.BlockSpec((1, d), full)]
    args += x_args + [mod, w_out, gain2]
    outs = [jax.ShapeDtypeStruct((n_rows, d), F32), jax.ShapeDtypeStruct((n_rows, d), tok_dtype)]
    out_specs = [pl.BlockSpec((tm, d), row), pl.BlockSpec((tm, d), row)]
    scratch = []
    if moe:
        in_specs.append(pl.BlockSpec(router_w.shape, full))
        args.append(router_w)
        col = lambda i: (0, i)
        outs += [jax.ShapeDtypeStruct((8, n_rows), jnp.int32), jax.ShapeDtypeStruct((8, n_rows), F32),
                 jax.ShapeDtypeStruct((N_EXPERTS, LANES), jnp.int32)]
        out_specs += [pl.BlockSpec((8, tm), col), pl.BlockSpec((8, tm), col), pl.BlockSpec((N_EXPERTS, LANES), full)]
        scratch = [pltpu.VMEM((N_EXPERTS, LANES), F32), pltpu.VMEM((tm, tm), BF16)]
    return pl.pallas_call(
        functools.partial(_outproj_kernel, tm=tm, moe=moe, n_lat_tiles=n_lat_tiles, has_ctx=has_ctx,
                          n_streams=len(x_args)),
        out_shape=tuple(outs),
        grid=(n_rows // tm,),
        in_specs=in_specs,
        out_specs=tuple(out_specs),
        scratch_shapes=scratch,
        compiler_params=_cparams("arbitrary"),
        name="out_proj_router" if moe else "out_proj",
    )(*args)


def _final_norm(x, gain):
    ms = jnp.mean(x * x, axis=-1, keepdims=True)
    return x * lax.rsqrt(ms + EPS) * gain


def _dense_ffn_kernel(tok_ref, x_ref, mod_ref, w1_ref, w3_ref, w2_ref, fg_ref, o_ref, *, last):
    t = tok_ref[...]
    h = (_silu(_dot(t, w1_ref[...])) * _dot(t, w3_ref[...])).astype(BF16)
    x = x_ref[...] + mod_ref[0, 5:6, :] * _dot(h, w2_ref[...])
    if last:
        x = _final_norm(x, fg_ref[...])
    o_ref[...] = x


def dense_ffn(tokens, xall, mod, w1, w3, w2, final_g, *, tm, seg, last):
    n, d = tokens.shape
    f = w1.shape[1]
    full = lambda i: (0, 0)
    row = lambda i: (i, 0)
    return pl.pallas_call(
        functools.partial(_dense_ffn_kernel, last=last),
        out_shape=jax.ShapeDtypeStruct((n, d), F32),
        grid=(n // tm,),
        in_specs=[pl.BlockSpec((tm, d), row), pl.BlockSpec((tm, d), row),
                  pl.BlockSpec((1, 6, d), lambda i: (seg(i), 0, 0)),
                  pl.BlockSpec((d, f), full, pipeline_mode=pl.Buffered(1)),
                  pl.BlockSpec((d, f), full, pipeline_mode=pl.Buffered(1)),
                  pl.BlockSpec((f, d), full, pipeline_mode=pl.Buffered(1)),
                  pl.BlockSpec((1, d), full)],
        out_specs=pl.BlockSpec((tm, d), row),
        compiler_params=_cparams("parallel"),
        name="dense_ffn",
    )(tokens, xall, mod, w1, w3, w2, final_g)


def _dispatch_kernel(pad_ref, dest_ref, tok_ref, xs_ref, zero_s, sem, zsem, *, tm, tmx):
    @pl.when(pl.program_id(0) == 0)
    def _():
        zero_s[...] = jnp.zeros_like(zero_s)

        def for_each_zero_copy(fn):
            for e in range(N_EXPERTS):
                @pl.when(pad_ref[N_EXPERTS + e] > 0)
                def _(e=e):
                    row = pl.multiple_of(pad_ref[e], tmx)
                    fn(pltpu.make_async_copy(zero_s, xs_ref.at[pl.ds(row, tmx)], zsem))
            for t in range(N_EXPERTS):
                @pl.when(t < pad_ref[2 * N_EXPERTS + 1])
                def _(t=t):
                    row = pl.multiple_of(pad_ref[2 * N_EXPERTS] + t * tmx, tmx)
                    fn(pltpu.make_async_copy(zero_s, xs_ref.at[pl.ds(row, tmx)], zsem))

        for_each_zero_copy(lambda c: c.start())
        for_each_zero_copy(lambda c: c.wait())

    def issue(r, carry):
        for k in range(2):
            dst = dest_ref[0, 0, 2 * r + k]
            pltpu.make_async_copy(tok_ref.at[pl.ds(r, 1)], xs_ref.at[pl.ds(dst, 1)], sem).start()
        return carry

    lax.fori_loop(0, tm, issue, 0, unroll=16)
    for _ in range(2):
        pltpu.make_async_copy(tok_ref, xs_ref.at[pl.ds(0, tm)], sem).wait()


def moe_dispatch(tokens, dest3, pad_info, n_slots, *, tm, tmx):
    n, d = tokens.shape
    grid_spec = pltpu.PrefetchScalarGridSpec(
        num_scalar_prefetch=1,
        grid=(n // tm,),
        in_specs=[pl.BlockSpec((1, 1, 2 * tm), lambda i, pad: (i, 0, 0), memory_space=pltpu.SMEM),
                  pl.BlockSpec((tm, d), lambda i, pad: (i, 0))],
        out_specs=pl.BlockSpec(memory_space=pl.ANY),
        scratch_shapes=[pltpu.VMEM((tmx, d), tokens.dtype), pltpu.SemaphoreType.DMA, pltpu.SemaphoreType.DMA],
    )
    return pl.pallas_call(
        functools.partial(_dispatch_kernel, tm=tm, tmx=tmx),
        out_shape=jax.ShapeDtypeStruct((n_slots, d), tokens.dtype),
        grid_spec=grid_spec,
        compiler_params=_cparams("arbitrary"),
        name="moe_dispatch",
    )(pad_info, dest3, tokens)


def _tile_expert(te_ref, nu_ref, i):
    return te_ref[jnp.clip(i, 0, nu_ref[0] - 1)]


def _moe_up_kernel(te_ref, nu_ref, xs_ref, w1_ref, w3_ref, h_ref, w1b_s, w3b_s):
    i = pl.program_id(1)

    @pl.when((i == 0) | (_tile_expert(te_ref, nu_ref, i) != _tile_expert(te_ref, nu_ref, i - 1)))
    def _():
        w1b_s[...] = w1_ref[0, 0].astype(BF16)
        w3b_s[...] = w3_ref[0, 0].astype(BF16)

    @pl.when(i < nu_ref[0])
    def _():
        x = xs_ref[...].astype(BF16)
        h_ref[...] = (_silu(_dot(x, w1b_s[...])) * _dot(x, w3b_s[...])).astype(h_ref.dtype)

    @pl.when(i >= nu_ref[0])
    def _():
        h_ref[...] = jnp.zeros_like(h_ref)


def _moe_down_kernel(te_ref, nu_ref, h_ref, w2_ref, y_ref, w2b_s):
    i = pl.program_id(1)

    @pl.when((i == 0) | (_tile_expert(te_ref, nu_ref, i) != _tile_expert(te_ref, nu_ref, i - 1)))
    def _():
        w2b_s[...] = w2_ref[0, 0].astype(BF16)

    @pl.when(i < nu_ref[0])
    def _():
        y_ref[...] = _dot(h_ref[...], w2b_s[...])

    @pl.when(i >= nu_ref[0])
    def _():
        y_ref[...] = jnp.zeros_like(y_ref)


def moe_ffn(xs, tile_expert, n_used, w1, w3, w2, *, li, tm, tf, tn):
    r, d = xs.shape
    f = w1.shape[3]

    def tile(i, nu):
        return jnp.minimum(i, nu[0] - 1)

    up_spec = pltpu.PrefetchScalarGridSpec(
        num_scalar_prefetch=2,
        grid=(f // tf, r // tm),
        in_specs=[pl.BlockSpec((tm, d), lambda j, i, te, nu: (tile(i, nu), 0)),
                  pl.BlockSpec((1, 1, d, tf), lambda j, i, te, nu: (li, te[tile(i, nu)], 0, j)),
                  pl.BlockSpec((1, 1, d, tf), lambda j, i, te, nu: (li, te[tile(i, nu)], 0, j))],
        out_specs=pl.BlockSpec((tm, tf), lambda j, i, te, nu: (i, j)),
        scratch_shapes=[pltpu.VMEM((d, tf), BF16), pltpu.VMEM((d, tf), BF16)],
    )
    hidden = pl.pallas_call(
        _moe_up_kernel,
        out_shape=jax.ShapeDtypeStruct((r, f), BF16),
        grid_spec=up_spec,
        compiler_params=_cparams("arbitrary", "arbitrary"),
        name="moe_up",
    )(tile_expert, n_used, xs, w1, w3)
    down_spec = pltpu.PrefetchScalarGridSpec(
        num_scalar_prefetch=2,
        grid=(d // tn, r // tm),
        in_specs=[pl.BlockSpec((tm, f), lambda n, i, te, nu: (tile(i, nu), 0)),
                  pl.BlockSpec((1, 1, f, tn), lambda n, i, te, nu: (li, te[tile(i, nu)], 0, n))],
        out_specs=pl.BlockSpec((tm, tn), lambda n, i, te, nu: (i, n)),
        scratch_shapes=[pltpu.VMEM((f, tn), BF16)],
    )
    return pl.pallas_call(
        _moe_down_kernel,
        out_shape=jax.ShapeDtypeStruct((r, d), F32),
        grid_spec=down_spec,
        compiler_params=_cparams("arbitrary", "arbitrary"),
        name="moe_down",
    )(tile_expert, n_used, hidden, w2)


def _combine_kernel(dest_ref, next_ref, gate_ref, x_ref, mod_ref, fg_ref, ys_ref, o_ref, buf_ref, sem, *, tm, last):
    i = pl.program_id(0)
    slot = i % 2

    def start_gather(idx_ref, s):
        def issue(r, carry):
            for k in range(2):
                src = idx_ref[0, 0, 2 * r + k]
                pltpu.make_async_copy(ys_ref.at[pl.ds(src, 1)], buf_ref.at[s, k, pl.ds(r, 1)], sem.at[s]).start()
            return carry

        lax.fori_loop(0, tm, issue, 0, unroll=16)

    @pl.when(i == 0)
    def _():
        start_gather(dest_ref, 0)

    @pl.when(i + 1 < pl.num_programs(0))
    def _():
        start_gather(next_ref, 1 - slot)

    for k in range(2):
        pltpu.make_async_copy(ys_ref.at[pl.ds(0, tm)], buf_ref.at[slot, k], sem.at[slot]).wait()
    gate = gate_ref[...]
    y = gate[:, 0:1] * buf_ref[slot, 0] + gate[:, 1:2] * buf_ref[slot, 1]
    x = x_ref[...] + mod_ref[0, 5:6, :] * y
    if last:
        x = _final_norm(x, fg_ref[...])
    o_ref[...] = x


def moe_combine(ys, dest3, rgate, xall, mod, final_g, *, tm, seg, last):
    n, d = xall.shape
    return pl.pallas_call(
        functools.partial(_combine_kernel, tm=tm, last=last),
        out_shape=jax.ShapeDtypeStruct((n, d), F32),
        grid=(n // tm,),
        in_specs=[pl.BlockSpec((1, 1, 2 * tm), lambda i: (i, 0, 0), memory_space=pltpu.SMEM),
                  pl.BlockSpec((1, 1, 2 * tm), lambda i: (jnp.minimum(i + 1, n // tm - 1), 0, 0),
                               memory_space=pltpu.SMEM),
                  pl.BlockSpec((tm, 8), lambda i: (i, 0)),
                  pl.BlockSpec((tm, d), lambda i: (i, 0)),
                  pl.BlockSpec((1, 6, d), lambda i: (seg(i), 0, 0)),
                  pl.BlockSpec((1, d), lambda i: (0, 0)),
                  pl.BlockSpec(memory_space=pl.ANY)],
        out_specs=pl.BlockSpec((tm, d), lambda i: (i, 0)),
        scratch_shapes=[pltpu.VMEM((2, 2, tm, d), F32), pltpu.SemaphoreType.DMA((2,))],
        compiler_params=_cparams("arbitrary"),
        name="moe_combine",
    )(dest3, dest3, rgate, xall, mod, final_g, ys)


def moe_layer(tokens, ridx, rgate, cnt, xall, mod, w1, w3, w2, final_g, *, li, tm, tmx, tf, seg, last):
    n = tokens.shape[0]
    counts = cnt[:, 0]
    padded = (counts + tmx - 1) // tmx * tmx
    pend = jnp.cumsum(padded)
    pstart = pend - padded
    group_start = sum(jnp.where(ridx[0:2] == e, pstart[e], 0) for e in range(N_EXPERTS))
    dest = (group_start + ridx[2:4]).T
    dest3 = dest.reshape(n // tm, 1, 2 * tm).astype(jnp.int32)
    rgate = rgate.T
    n_tiles = (2 * n) // tmx + N_EXPERTS
    n_used = (pend[-1] // tmx).astype(jnp.int32).reshape(1)
    tile_start = jnp.arange(n_tiles, dtype=jnp.int32) * tmx
    tile_expert = jnp.minimum(jnp.sum(tile_start[:, None] >= pend[None, :], axis=1), N_EXPERTS - 1).astype(jnp.int32)
    pad_info = jnp.concatenate([pend - tmx, (padded > 0).astype(jnp.int32), pend[-1:], n_tiles - n_used]
                               ).astype(jnp.int32)
    xs = moe_dispatch(tokens, dest3, pad_info, n_tiles * tmx, tm=tm, tmx=tmx)
    ys = moe_ffn(xs, tile_expert, n_used, w1, w3, w2, li=li, tm=tmx, tf=tf, tn=MOE_DOWN_TILE)
    return moe_combine(ys, dest3, rgate, xall, mod, final_g, tm=tm, seg=seg, last=last)


def rope_tables(seq):
    rows = seq // GRID_W
    row = jnp.repeat(jnp.arange(rows, dtype=F32), GRID_W)
    col = jnp.tile(jnp.arange(GRID_W, dtype=F32), rows)
    n_freq = HEAD_DIM // 4
    inv = ROPE_BASE ** (-jnp.arange(n_freq, dtype=F32) / n_freq)
    ang = jnp.concatenate([row[:, None] * inv, col[:, None] * inv], axis=-1)
    cos, sin = jnp.cos(ang), jnp.sin(ang)
    cos64 = jnp.concatenate([cos, cos], axis=-1)
    sin64 = jnp.concatenate([-sin, sin], axis=-1)
    return jnp.tile(cos64, (1, 2)), jnp.tile(sin64, (1, 2))


def _tile_lanes(t, width):
    reps = width // t.shape[1]
    return t if reps == 1 else jnp.concatenate([t] * reps, axis=1)


def _rope(t, cos, sin):
    w = t.shape[1]
    lane = lax.broadcasted_iota(jnp.int32, t.shape, 1) % HEAD_DIM
    half = HEAD_DIM // 2
    rot = jnp.where(lane < half, pltpu.roll(t, w - half, 1), pltpu.roll(t, half, 1))
    return t * _tile_lanes(cos, w) + rot * _tile_lanes(sin, w)


def _group_matrix(width, value):
    r = lax.broadcasted_iota(jnp.int32, (width, width), 0) // HEAD_DIM
    c = lax.broadcasted_iota(jnp.int32, (width, width), 1) // HEAD_DIM
    return jnp.where(r == c, value, 0.0).astype(BF16)


RET_CHUNK = 128


def _retention_kernel(lat_ref, ctx_ref, cos_ref, sin_ref, dec_ref, gn_ref, ol_ref, oc_ref,
                      q_s, k_s, bn_s, st_s, o_s, dm_s, *, n_ctx, n_lat):
    ch = RET_CHUNK
    nc, nl = n_ctx // ch, n_lat // ch
    lg = -jnp.exp(dec_ref[...])
    lgf, lgb = lg[0:1], lg[1:2]
    pos = lax.broadcasted_iota(jnp.int32, (ch, 1), 0).astype(F32)
    wkf = jnp.exp((ch - 1 - pos) * lgf)
    wkb = jnp.exp(pos * lgb)
    wqf = jnp.exp((pos + 1) * lgf)
    wqb = jnp.exp((ch - pos) * lgb)
    cdf = jnp.exp(ch * lgf)
    cdb = jnp.exp(ch * lgb)
    blockmask = (lax.broadcasted_iota(jnp.int32, (RET_W, RET_W), 0) // HEAD_DIM
                 == lax.broadcasted_iota(jnp.int32, (RET_W, RET_W), 1) // HEAD_DIM)
    gmean = _group_matrix(RET_W, 1.0 / HEAD_DIM)
    diff = (lax.broadcasted_iota(jnp.int32, (ch, ch), 0) - lax.broadcasted_iota(jnp.int32, (ch, ch), 1)).astype(F32)
    gn = gn_ref[...]

    def load_qkv(ref, t0):
        return (ref[pl.ds(t0, ch), 0:RET_W].astype(F32), ref[pl.ds(t0, ch), RET_W:2 * RET_W].astype(F32),
                ref[pl.ds(t0, ch), 2 * RET_W:3 * RET_W])

    def reverse_step(ref, t0, c, roped):
        q, k, v = load_qkv(ref, t0)
        if roped:
            cos, sin = cos_ref[pl.ds(t0, ch), :], sin_ref[pl.ds(t0, ch), :]
            q, k = _rope(q, cos, sin), _rope(k, cos, sin)
        k = k * (HEAD_DIM ** -0.5)
        r0 = pl.multiple_of(c * ch, ch)
        q_s[pl.ds(r0, ch), :] = q.astype(BF16)
        k_s[pl.ds(r0, ch), :] = k.astype(BF16)
        st = st_s[...]
        bn_s[c] = st.astype(BF16)
        kvb = _dot_tn((k * wkb).astype(BF16), v)
        st_s[...] = cdb * st + jnp.where(blockmask, kvb, 0.0)

    st_s[...] = jnp.zeros_like(st_s)

    def rev_ctx(n, carry):
        c = nc - 1 - n
        reverse_step(ctx_ref, pl.multiple_of(c * ch, ch), c, False)
        return carry

    def rev_lat(n, carry):
        c = nl - 1 - n
        reverse_step(lat_ref, pl.multiple_of(c * ch, ch), nc + c, True)
        return carry

    lax.fori_loop(0, nc, rev_ctx, 0)
    lax.fori_loop(0, nl, rev_lat, 0)

    for h in range(RET_HEADS):
        lf, lb = lgf[:, h * HEAD_DIM:h * HEAD_DIM + 1], lgb[:, h * HEAD_DIM:h * HEAD_DIM + 1]
        dm_s[h] = jnp.where(diff == 0.0, 2.0, jnp.exp(jnp.abs(diff) * jnp.where(diff > 0.0, lf, lb)))

    def forward_step(ref, t0, c):
        r0 = pl.multiple_of(c * ch, ch)
        q = q_s[pl.ds(r0, ch), :]
        k = k_s[pl.ds(r0, ch), :]
        v = ref[pl.ds(t0, ch), 2 * RET_W:3 * RET_W]
        sls = [slice(h * HEAD_DIM, (h + 1) * HEAD_DIM) for h in range(RET_HEADS)]
        scores = [_dot_nt(q[:, sl], k[:, sl]) for sl in sls]
        probs = [(s * dm_s[h]).astype(BF16) for h, s in enumerate(scores)]
        heads = [_dot(p, v[:, sl]) for p, sl in zip(probs, sls)]
        qf = q.astype(F32)
        st = st_s[...]
        o = jnp.concatenate(heads, axis=1)
        o_s[pl.ds(r0, ch), :] = (o + _dot((qf * wqf).astype(BF16), st.astype(BF16))
                                 + _dot((qf * wqb).astype(BF16), bn_s[c]))
        kvf = _dot_tn((k.astype(F32) * wkf).astype(BF16), v)
        st_s[...] = cdf * st + jnp.where(blockmask, kvf, 0.0)

    st_s[...] = jnp.zeros_like(st_s)

    def fwd_ctx(n, carry):
        forward_step(ctx_ref, pl.multiple_of(n * ch, ch), n)
        return carry

    def fwd_lat(n, carry):
        forward_step(lat_ref, pl.multiple_of(n * ch, ch), nc + n)
        return carry

    lax.fori_loop(0, nc, fwd_ctx, 0)
    lax.fori_loop(0, nl, fwd_lat, 0, unroll=2)

    def finish(ref, out_ref, n_rows, row_off):
        ft = 2 * ch

        def body(i, carry):
            t0 = pl.multiple_of(i * ft, ft)
            o = o_s[pl.ds(pl.multiple_of(row_off + t0, ft), ft), :]
            g = ref[pl.ds(t0, ft), 3 * RET_W:4 * RET_W].astype(F32)
            xc = o - _split_dot(o, gmean)
            var = _split_dot(xc * xc, gmean)
            out_ref[pl.ds(t0, ft), :] = (xc * lax.rsqrt(var + EPS) * gn * _silu(g)).astype(out_ref.dtype)
            return carry

        lax.fori_loop(0, n_rows // ft, body, 0)

    finish(ctx_ref, oc_ref, n_ctx, 0)
    finish(lat_ref, ol_ref, n_lat, n_ctx)


def retention_mixer(p_ret, cos, sin, dec_lanes, gn_gain, *, n_batch, n_lat, n_ctx):
    n = p_ret.shape[0]
    width = p_ret.shape[1]
    ctx_blk0 = (n_batch * n_lat) // n_ctx
    t_all = n_lat + n_ctx
    nch = t_all // RET_CHUNK
    return pl.pallas_call(
        functools.partial(_retention_kernel, n_ctx=n_ctx, n_lat=n_lat),
        out_shape=(jax.ShapeDtypeStruct((n_batch * n_lat, RET_W), BF16),
                   jax.ShapeDtypeStruct((n_batch * n_ctx, RET_W), BF16)),
        grid=(n_batch,),
        in_specs=[pl.BlockSpec((n_lat, width), lambda b: (b, 0)),
                  pl.BlockSpec((n_ctx, width), lambda b: (ctx_blk0 + b, 0)),
                  pl.BlockSpec((n_lat, LANES), lambda b: (0, 0)),
                  pl.BlockSpec((n_lat, LANES), lambda b: (0, 0)),
                  pl.BlockSpec((2, RET_W), lambda b: (0, 0)),
                  pl.BlockSpec((1, RET_W), lambda b: (0, 0))],
        out_specs=(pl.BlockSpec((n_lat, RET_W), lambda b: (b, 0)),
                   pl.BlockSpec((n_ctx, RET_W), lambda b: (b, 0))),
        scratch_shapes=[pltpu.VMEM((t_all, RET_W), BF16), pltpu.VMEM((t_all, RET_W), BF16),
                        pltpu.VMEM((nch, RET_W, RET_W), BF16), pltpu.VMEM((RET_W, RET_W), F32),
                        pltpu.VMEM((t_all, RET_W), F32), pltpu.VMEM((RET_HEADS, RET_CHUNK, RET_CHUNK), F32)],
        compiler_params=_cparams("parallel"),
        name="retention",
    )(p_ret, p_ret, cos, sin, dec_lanes, gn_gain)


GDN_CHUNK = 64
GDN_PREP = 128
GDN_HALO = 8
GDN_SOLVE_CHUNKS = 4


def _softplus(x):
    return jnp.maximum(x, 0.0) + jnp.log(1.0 + jnp.exp(-jnp.abs(x)))


def _head_block_diag(x):
    n, w = x.shape
    heads = w // HEAD_DIM
    rows = lax.broadcasted_iota(jnp.int32, (heads * n, w), 0) // n
    cols = lax.broadcasted_iota(jnp.int32, (heads * n, w), 1) // HEAD_DIM
    return jnp.where(rows == cols, jnp.concatenate([x] * heads, axis=0), 0.0)


def _unit_tri_inverses(mats):
    n, w = mats[0].shape
    ri = lax.broadcasted_iota(jnp.int32, (n, w), 0)
    ci = lax.broadcasted_iota(jnp.int32, (n, w), 1) % HEAD_DIM
    b16 = (ri // 16) == (ci // 16)
    b32 = (ri // 32) == (ci // 32)
    eye = jnp.where(ri == ci, 1.0, 0.0)
    ps = [jnp.where(b16, -a, 0.0) for a in mats]
    ts = [eye + p for p in ps]
    for _ in range(3):
        pbs = [p.astype(BF16) for p in ps]
        ps = [_dot(pb, _head_block_diag(pb)) for pb in pbs]
        ts = [t + _dot(t.astype(BF16), _head_block_diag(p.astype(BF16))) for t, p in zip(ts, ps)]
    for keep in (b32 & ~b16, ~b32):
        offs = [jnp.where(keep, a, 0.0).astype(BF16) for a in mats]
        tbs = [t.astype(BF16) for t in ts]
        mids = [_dot(tb, _head_block_diag(off)).astype(BF16) for tb, off in zip(tbs, offs)]
        ts = [t - _dot(mid, _head_block_diag(tb)) for t, mid, tb in zip(ts, mids, tbs)]
    return ts


def _gdn_prep_kernel(lat_ref, ctx_ref, abl_ref, abc_ref, cw_ref, par_ref, q_s, k_s, v_s, gb_s, *, n_ctx, n_lat):
    pt = GDN_PREP
    halo = GDN_HALO
    qkv_w = 3 * GDN_W
    gsum = _group_matrix(GDN_W, 1.0)
    lane128 = lax.broadcasted_iota(jnp.int32, (pt, LANES), 1)
    neg_a = -jnp.exp(par_ref[0:1, :])
    dt_bias = par_ref[1:2, :]
    taps = [cw_ref[j:j + 1, :] for j in range(GDN_CONV)]

    def prep(ref, ab_ref, n_rows, row_off):
        n_tiles = n_rows // pt

        def body(i, carry):
            t0 = pl.multiple_of(i * pt, pt)
            cur = ref[pl.ds(t0, pt), 0:qkv_w].astype(F32)
            p0 = pl.multiple_of(jnp.maximum(t0 - 16, 0), 16)
            n0 = pl.multiple_of(jnp.minimum(t0 + pt, n_rows - 16), 16)
            prev = ref[pl.ds(p0, 16), 0:qkv_w].astype(F32)[16 - halo:16]
            nxt = ref[pl.ds(n0, 16), 0:qkv_w].astype(F32)[0:halo]
            prev = jnp.where(i > 0, prev, 0.0)
            nxt = jnp.where(i < n_tiles - 1, nxt, 0.0)
            ext = jnp.concatenate([prev, cur, nxt], axis=0)
            rows = pt + 2 * halo
            acc = None
            for j in range(GDN_CONV):
                s = j - (GDN_CONV - 1) // 2
                sh = ext if s == 0 else pltpu.roll(ext, (rows - s) % rows, 0)
                term = sh[halo:halo + pt] * taps[j]
                acc = term if acc is None else acc + term
            act = _silu(acc)
            q, k, v = act[:, 0:GDN_W], act[:, GDN_W:2 * GDN_W], act[:, 2 * GDN_W:3 * GDN_W]
            q = q * lax.rsqrt(_split_dot(q * q, gsum) + EPS) * (HEAD_DIM ** -0.5)
            k = k * lax.rsqrt(_split_dot(k * k, gsum) + EPS)
            r0 = pl.multiple_of(row_off + t0, pt)
            q_s[pl.ds(r0, pt), :] = q.astype(BF16)
            k_s[pl.ds(r0, pt), :] = k.astype(BF16)
            v_s[pl.ds(r0, pt), :] = v.astype(BF16)
            ab = ab_ref[pl.ds(t0, pt), :]
            g = neg_a * _softplus(ab + dt_bias)
            gb_s[pl.ds(r0, pt), :] = jnp.where(lane128 < 2 * GDN_HEADS, g, _sigmoid(ab))
            return carry

        lax.fori_loop(0, n_tiles, body, 0)

    prep(ctx_ref, abc_ref, n_ctx, 0)
    prep(lat_ref, abl_ref, n_lat, n_ctx)


def _gdn_kernel(q_s, k_s, v_s, gb_s, zl_ref, zc_ref, ng_ref, ol_ref, oc_ref,
                o_s, dec_s, uf_s, ub_s, wf_s, wb_s, af_s, ab_s, qf_s, qb_s, kf_s, kb_s, sf_s, sb_s, *,
                n_ctx, n_lat, n_solve):
    u_s, w_s, a_s, qd_s, kd_s, st_s = (uf_s, ub_s), (wf_s, wb_s), (af_s, ab_s), (qf_s, qb_s), (kf_s, kb_s), (sf_s, sb_s)
    ch = GDN_CHUNK
    pt = GDN_PREP
    gmean = _group_matrix(GDN_W, 1.0 / HEAD_DIM)
    ri = lax.broadcasted_iota(jnp.int32, (ch, ch), 0)
    ci = lax.broadcasted_iota(jnp.int32, (ch, ch), 1)
    tri = tuple(jnp.where(m, 1.0, 0.0).astype(BF16) for m in (ri >= ci, ri <= ci))
    rp = lax.broadcasted_iota(jnp.int32, (ch, GDN_W), 0)
    cp = lax.broadcasted_iota(jnp.int32, (ch, GDN_W), 1) % HEAD_DIM
    incl = (rp >= cp, rp <= cp)
    strict = (rp > cp, rp < cp)
    ncc, nlc = n_ctx // ch, n_lat // ch
    head_blocks = (lax.broadcasted_iota(jnp.int32, (GDN_W, GDN_W), 0) // HEAD_DIM
                   == lax.broadcasted_iota(jnp.int32, (GDN_W, GDN_W), 1) // HEAD_DIM)

    def solve_chunks(cs):
        n_cs = len(cs)
        items = [(j, d) for j in range(n_cs) for d in range(2)]
        rows = [pl.ds(pl.multiple_of(c * ch, ch), ch) for c in cs]
        gbc = [gb_s[r, :] for r in rows]
        gcs = []
        for j in range(n_cs):
            per_dir = []
            for d in range(2):
                rem, acc = gbc[j], None
                for _ in range(3):
                    part = rem.astype(BF16)
                    term = _dot(tri[d], part)
                    acc = term if acc is None else acc + term
                    rem = rem - part.astype(F32)
                per_dir.append(acc)
            gcs.append(per_dir)
        gcs_t = [[g.T for g in per_dir] for per_dir in gcs]
        edge = (ch - 1, 0)

        def spread(mat, first):
            return jnp.concatenate([jnp.broadcast_to(mat[:, first + h:first + h + 1], (ch, HEAD_DIM))
                                    for h in range(GDN_HEADS)], axis=1)

        gc = [spread(gcs[j][d], d * GDN_HEADS) for j, d in items]
        gr = [jnp.concatenate([gcs_t[j][d][d * GDN_HEADS + h:d * GDN_HEADS + h + 1, :] for h in range(GDN_HEADS)],
                              axis=1) for j, d in items]
        gtot = [x[edge[d]:edge[d] + 1, :] for x, (_, d) in zip(gc, items)]
        beta = [spread(gbc[j], 2 * GDN_HEADS + d * GDN_HEADS) for j, d in items]
        decay = [jnp.where(incl[d], jnp.exp(jnp.minimum(gc[i] - gr[i], 0.0)), 0.0) for i, (_, d) in enumerate(items)]
        q = [q_s[r, :] for r in rows]
        k = [k_s[r, :] for r in rows]
        kf = [x.astype(F32) for x in k]
        v = [v_s[r, :].astype(F32) for r in rows]
        k_rows = [_head_block_diag(x) for x in k]
        qk = [_dot_nt(x, y) for x, y in zip(q, k_rows)]
        kb = [kf[j] * beta[i] for i, (j, _) in enumerate(items)]
        kk = [_dot_nt(kb[i].astype(BF16), k_rows[j]) for i, (j, _) in enumerate(items)]
        a = [jnp.where(strict[d], kk[i] * decay[i], 0.0) for i, (_, d) in enumerate(items)]
        t = [x.astype(BF16) for x in _unit_tri_inverses(a)]
        eg = [jnp.exp(x) for x in gc]
        u = [_dot(t[i], _head_block_diag((v[j] * beta[i]).astype(BF16))) for i, (j, _) in enumerate(items)]
        w = [_dot(t[i], _head_block_diag((kb[i] * eg[i]).astype(BF16))) for i in range(len(items))]
        for i, (j, d) in enumerate(items):
            u_s[d][rows[j], :] = u[i].astype(BF16)
            w_s[d][rows[j], :] = w[i].astype(BF16)
            a_s[d][rows[j], :] = (qk[j] * decay[i]).astype(BF16)
            qd_s[d][rows[j], :] = (q[j].astype(F32) * eg[i]).astype(BF16)
            kd_s[d][rows[j], :] = (kf[j] * jnp.exp(gtot[i] - gc[i])).astype(BF16)
            dec_s[pl.ds(cs[j], 1), d * GDN_W:(d + 1) * GDN_W] = jnp.exp(gtot[i])

    def scan_step(cf, cb):
        cs = (cf, cb)
        rows = [pl.ds(pl.multiple_of(c * ch, ch), ch) for c in cs]
        s = [st[...] for st in st_s]
        lhs = [jnp.concatenate([w_s[d][rows[d], :], qd_s[d][rows[d], :]], axis=0) for d in range(2)]
        prod = [_dot(lhs[d], s[d].astype(BF16)) for d in range(2)]
        vb = [(u_s[d][rows[d], :].astype(F32) - prod[d][0:ch]).astype(BF16) for d in range(2)]
        vbd = [jnp.where(head_blocks, jnp.concatenate([x] * GDN_HEADS, axis=0), 0.0) for x in vb]
        av = [_dot(a_s[d][rows[d], :], vbd[d]) for d in range(2)]
        upd = [_dot_tn(kd_s[d][rows[d], :], vb[d]) for d in range(2)]
        for d in range(2):
            dec = dec_s[pl.ds(cs[d], 1), d * GDN_W:(d + 1) * GDN_W]
            st_s[d][...] = s[d] * dec + jnp.where(head_blocks, upd[d], 0.0)
        for d in range(2):
            o_s[rows[d], :] += prod[d][ch:2 * ch] + av[d]

    def solve_body(n, carry):
        solve_chunks([n * n_solve + j for j in range(n_solve)])
        return carry

    lax.fori_loop(0, (ncc + nlc) // n_solve, solve_body, 0)

    for st in st_s:
        st[...] = jnp.zeros_like(st)
    o_s[...] = jnp.zeros_like(o_s)

    def scan_ctx(n, carry):
        scan_step(n, ncc - 1 - n)
        return carry

    def scan_lat(n, carry):
        scan_step(ncc + n, ncc + nlc - 1 - n)
        return carry

    lax.fori_loop(0, ncc, scan_ctx, 0)
    lax.fori_loop(0, nlc, scan_lat, 0, unroll=2)

    ng = ng_ref[...]

    def finish(z_ref, out_ref, n_rows, row_off):
        def body(i, carry):
            t0 = pl.multiple_of(i * pt, pt)
            o = o_s[pl.ds(pl.multiple_of(row_off + t0, pt), pt), :]
            z = z_ref[pl.ds(t0, pt), :].astype(F32)
            y = o * lax.rsqrt(_split_dot(o * o, gmean) + EPS) * ng * _silu(z)
            out_ref[pl.ds(t0, pt), :] = y.astype(out_ref.dtype)
            return carry

        lax.fori_loop(0, n_rows // pt, body, 0)

    finish(zc_ref, oc_ref, n_ctx, 0)
    finish(zl_ref, ol_ref, n_lat, n_ctx)


def gdn_mixer(p_gdn, p_ab, conv_w, a_log, dt_bias, norm_gain, *, n_batch, n_lat, n_ctx):
    width = p_gdn.shape[1]
    ctx_blk0 = (n_batch * n_lat) // n_ctx
    t_all = n_lat + n_ctx
    cw = jnp.pad(conv_w.astype(F32), ((0, 8 - GDN_CONV), (0, 0)))
    par = jnp.zeros((8, LANES), F32)
    par = par.at[0, :2 * GDN_HEADS].set(a_log.reshape(-1)).at[1, :2 * GDN_HEADS].set(dt_bias.reshape(-1))
    ng = jnp.tile(norm_gain.astype(F32).reshape(1, HEAD_DIM), (1, GDN_HEADS))
    seq = lambda w: pl.BlockSpec((t_all, w), lambda b: (b, 0))
    q, k, v, gb = pl.pallas_call(
        functools.partial(_gdn_prep_kernel, n_ctx=n_ctx, n_lat=n_lat),
        out_shape=(jax.ShapeDtypeStruct((n_batch * t_all, GDN_W), BF16),) * 3
        + (jax.ShapeDtypeStruct((n_batch * t_all, LANES), F32),),
        grid=(n_batch,),
        in_specs=[pl.BlockSpec((n_lat, width), lambda b: (b, 0)),
                  pl.BlockSpec((n_ctx, width), lambda b: (ctx_blk0 + b, 0)),
                  pl.BlockSpec((n_lat, LANES), lambda b: (b, 0)),
                  pl.BlockSpec((n_ctx, LANES), lambda b: (ctx_blk0 + b, 0)),
                  pl.BlockSpec(cw.shape, lambda b: (0, 0)),
                  pl.BlockSpec(par.shape, lambda b: (0, 0))],
        out_specs=(seq(GDN_W), seq(GDN_W), seq(GDN_W), seq(LANES)),
        compiler_params=_cparams("parallel"),
        name="gdn_prep",
    )(p_gdn, p_gdn, p_ab, p_ab, cw, par)
    zcol = 3 * GDN_W // GDN_W
    once = lambda w: pl.BlockSpec((t_all, w), lambda b: (b, 0), pipeline_mode=pl.Buffered(1))
    n_solve = GDN_SOLVE_CHUNKS if (t_all // GDN_CHUNK) % GDN_SOLVE_CHUNKS == 0 else 1
    return pl.pallas_call(
        functools.partial(_gdn_kernel, n_ctx=n_ctx, n_lat=n_lat, n_solve=n_solve),
        out_shape=(jax.ShapeDtypeStruct((n_batch * n_lat, GDN_W), BF16),
                   jax.ShapeDtypeStruct((n_batch * n_ctx, GDN_W), BF16)),
        grid=(n_batch,),
        in_specs=[once(GDN_W), once(GDN_W), once(GDN_W), pl.BlockSpec((t_all, LANES), lambda b: (b, 0)),
                  pl.BlockSpec((n_lat, GDN_W), lambda b: (b, zcol)),
                  pl.BlockSpec((n_ctx, GDN_W), lambda b: (ctx_blk0 + b, zcol)),
                  pl.BlockSpec((1, GDN_W), lambda b: (0, 0))],
        out_specs=(pl.BlockSpec((n_lat, GDN_W), lambda b: (b, 0)),
                   pl.BlockSpec((n_ctx, GDN_W), lambda b: (b, 0))),
        scratch_shapes=[pltpu.VMEM((t_all, GDN_W), F32), pltpu.VMEM((t_all // GDN_CHUNK, 2 * GDN_W), F32)]
        + [pltpu.VMEM((t_all, GDN_W), BF16)] * 10
        + [pltpu.VMEM((GDN_W, GDN_W), F32)] * 2,
        compiler_params=_cparams("parallel"),
        name="gdn",
    )(q, k, v, gb, p_gdn, p_gdn, ng)


SWA_BLOCK = 128
SWA_BAND = 3 * SWA_BLOCK
SWA_Q_PER_STEP = 8


def _swa_kernel(sink_ref, q_ref, kl_ref, vl_ref, kc_ref, vc_ref, cos_ref, sin_ref, o_ref, *, band, n_lat, n_sub):
    blk = SWA_BLOCK
    kc = kc_ref[...]
    vc = vc_ref[...]
    for sub in range(n_sub):
        rows = slice(sub * blk, (sub + 1) * blk)
        o_ref[rows, :] = _swa_block(pl.program_id(1) * n_sub + sub, q_ref[rows, :].astype(F32), sink_ref, kl_ref,
                                    vl_ref, kc, vc, cos_ref, sin_ref, band=band, n_lat=n_lat).astype(o_ref.dtype)


def _swa_block(i, q, sink_ref, kl_ref, vl_ref, kc, vc, cos_ref, sin_ref, *, band, n_lat):
    blk = SWA_BLOCK
    if band:
        q0 = pl.multiple_of(i * blk, blk)
        start = pl.multiple_of(jnp.clip((i - 1) * blk, 0, n_lat - SWA_BAND), blk)
        q = _rope(q, cos_ref[pl.ds(q0, blk), :], sin_ref[pl.ds(q0, blk), :])
        kb = _rope(kl_ref[pl.ds(start, SWA_BAND), :].astype(F32),
                   cos_ref[pl.ds(start, SWA_BAND), :], sin_ref[pl.ds(start, SWA_BAND), :]).astype(BF16)
        vb = vl_ref[pl.ds(start, SWA_BAND), :]
        keys = jnp.concatenate([kb, kc], axis=0)
        vals = jnp.concatenate([vb, vc], axis=0)
    else:
        keys, vals = kc, vc
    qb = (q * (HEAD_DIM ** -0.5 * LOG2E)).astype(BF16)
    n_keys = keys.shape[0]
    if band:
        q_pos = q0 + lax.broadcasted_iota(jnp.int32, (blk, 1), 0)
        col_id = lax.broadcasted_iota(jnp.int32, (1, n_keys), 1)
        valid = (col_id >= SWA_BAND) | (jnp.abs(start + col_id - q_pos) <= WINDOW)
        bias = jnp.where(valid, 0.0, NEG_BIG)
    v_lane = lax.broadcasted_iota(jnp.int32, vals.shape, 1) // HEAD_DIM
    outs = []
    for g in range(SWA_KV_HEADS):
        ksl = slice(g * HEAD_DIM, (g + 1) * HEAD_DIM)
        v_ext = jnp.where(v_lane == g, vals, 1.0)
        den_lane = (1 - g) * HEAD_DIM
        q4 = jnp.concatenate([qb[:, (g * SWA_GROUP + r) * HEAD_DIM:(g * SWA_GROUP + r + 1) * HEAD_DIM]
                              for r in range(SWA_GROUP)], axis=0)
        s = _dot_nt(q4, keys[:, ksl])
        es, ms, sinks = [], [], []
        for r in range(SWA_GROUP):
            sr = s[r * blk:(r + 1) * blk]
            if band:
                sr = sr + bias
            sink2 = sink_ref[g * SWA_GROUP + r] * LOG2E
            m = jnp.maximum(jnp.max(sr, axis=-1, keepdims=True), sink2)
            es.append(jnp.exp2(sr - m).astype(BF16))
            ms.append(m)
            sinks.append(sink2)
        ov = _dot(jnp.concatenate(es, axis=0), v_ext)
        for r in range(SWA_GROUP):
            ovr = ov[r * blk:(r + 1) * blk]
            den = ovr[:, den_lane:den_lane + 1] + jnp.exp2(sinks[r] - ms[r])
            outs.append(ovr[:, ksl] * (1.0 / den))
    return jnp.concatenate(outs, axis=1)


def swa_mixer(p_swa, cos, sin, sink, *, n_batch, n_lat, n_ctx):
    blk = SWA_BLOCK
    ctx_row0 = n_batch * n_lat
    kcol, vcol = SWA_W // SWA_KV_W, SWA_W // SWA_KV_W + 1
    sink = sink.astype(F32)

    def call(band):
        n_q = n_lat if band else n_ctx
        n_sub = min(SWA_Q_PER_STEP, n_q // blk)
        qblk = n_sub * blk
        nb = n_q // qblk
        qrow0 = 0 if band else ctx_row0 // qblk
        grid_spec = pltpu.PrefetchScalarGridSpec(
            num_scalar_prefetch=1,
            grid=(n_batch, nb),
            in_specs=[pl.BlockSpec((qblk, SWA_W), lambda b, i, s: (qrow0 + b * nb + i, 0)),
                      pl.BlockSpec((n_lat, SWA_KV_W), lambda b, i, s: (b, kcol)),
                      pl.BlockSpec((n_lat, SWA_KV_W), lambda b, i, s: (b, vcol)),
                      pl.BlockSpec((n_ctx, SWA_KV_W), lambda b, i, s: (ctx_row0 // n_ctx + b, kcol)),
                      pl.BlockSpec((n_ctx, SWA_KV_W), lambda b, i, s: (ctx_row0 // n_ctx + b, vcol)),
                      pl.BlockSpec((n_lat, LANES), lambda b, i, s: (0, 0)),
                      pl.BlockSpec((n_lat, LANES), lambda b, i, s: (0, 0))],
            out_specs=pl.BlockSpec((qblk, SWA_W), lambda b, i, s: (b * nb + i, 0)),
        )
        return pl.pallas_call(
            functools.partial(_swa_kernel, band=band, n_lat=n_lat, n_sub=n_sub),
            out_shape=jax.ShapeDtypeStruct((n_batch * n_q, SWA_W), BF16),
            grid_spec=grid_spec,
            compiler_params=_cparams("parallel", "arbitrary"),
            name="swa_latent" if band else "swa_context",
        )(sink, p_swa, p_swa, p_swa, p_swa, p_swa, cos, sin)

    return call(True), call(False)


MOE_ROW_TILE = 512
MOE_FF_TILE = 1792
MOE_DOWN_TILE = 1024


def _row_tile(n_lat, n_ctx_rows):
    for tm in (512, 256, 128):
        if n_lat % tm == 0 and n_ctx_rows % tm == 0:
            return tm
    raise ValueError("sequence lengths must be multiples of 128")


def kernel(x, c, ctx, c_ctx, ada_w, ada_b, norm1_g, w_in, ret_decay, ret_gn_g, gdn_conv_w, gdn_a_log, gdn_dt_bias,
           gdn_norm_g, swa_sink, w_out, norm2_g, ffn_w1, ffn_w3, ffn_w2, router_w, moe_w1, moe_w3, moe_w2, final_g):
    b, s, d = x.shape
    n_ctx = ctx.shape[1]
    depth = ada_w.shape[0]
    nl, ncx = b * s, b * n_ctx
    tm = _row_tile(s, ncx)
    seg = _seg_map(nl // tm, s // tm, b)

    cvec = jnp.zeros((8, d), F32).at[:b].set(c).at[b].set(c_ctx)
    mod_all = adaln(cvec, ada_w, ada_b).reshape(depth, 8, 6, d)
    xall = (x.reshape(nl, d), ctx.reshape(ncx, d))
    cos, sin = rope_tables(s)
    fg = final_g.reshape(1, d)

    ret_cols, gdn_cols = 4 * RET_W, 4 * GDN_W
    ab_cols = 4 * GDN_HEADS
    for layer in range(depth):
        last = layer == depth - 1
        mod = mod_all[layer]
        w = w_in[layer].astype(BF16)
        w_ret = w[:, :ret_cols]
        w_gdn = w[:, ret_cols:ret_cols + gdn_cols]
        w_ab = jnp.pad(w[:, ret_cols + gdn_cols:ret_cols + gdn_cols + ab_cols], ((0, 0), (0, LANES - ab_cols)))
        w_swa = w[:, ret_cols + gdn_cols + ab_cols:]
        p_ret, p_gdn, p_swa, p_ab = in_proj(xall, mod, norm1_g[layer].reshape(1, d), w_ret, w_gdn, w_swa, w_ab,
                                            tm=tm, seg=seg, n=nl + ncx, n_lat_rows=nl)
        dec_lanes = jnp.repeat(ret_decay[layer].astype(F32), HEAD_DIM, axis=1)
        o_ret = retention_mixer(p_ret, cos, sin, dec_lanes, ret_gn_g[layer].reshape(1, RET_W),
                                n_batch=b, n_lat=s, n_ctx=n_ctx)
        o_gdn = gdn_mixer(p_gdn, p_ab, gdn_conv_w[layer], gdn_a_log[layer], gdn_dt_bias[layer], gdn_norm_g[layer],
                          n_batch=b, n_lat=s, n_ctx=n_ctx)
        o_swa = swa_mixer(p_swa, cos, sin, swa_sink[layer], n_batch=b, n_lat=s, n_ctx=n_ctx)

        n_rows = nl if last else nl + ncx
        is_moe = layer % 2 == 1
        i = layer // 2
        rw = None
        if is_moe:
            rw = jnp.pad(router_w[i].astype(F32), ((0, 0), (0, LANES - N_EXPERTS)))
        res = out_proj(o_ret, o_gdn, o_swa, xall, mod, w_out[layer].astype(BF16), norm2_g[layer].reshape(1, d), rw,
                       tm=tm, seg=seg, n_rows=n_rows, n_lat_rows=nl, tok_dtype=F32 if is_moe else BF16)
        if is_moe:
            x_new, tokens, ridx, rgate, cnt = res
            xall = moe_layer(tokens, ridx, rgate, cnt, x_new, mod, moe_w1, moe_w3, moe_w2, fg,
                             li=i, tm=tm, tmx=MOE_ROW_TILE, tf=MOE_FF_TILE, seg=seg, last=last)
        else:
            x_new, tokens = res
            xall = dense_ffn(tokens, x_new, mod, ffn_w1[i].astype(BF16), ffn_w3[i].astype(BF16),
                             ffn_w2[i].astype(BF16), fg, tm=tm, seg=seg, last=last)
    return xall[:nl].reshape(b, s, d)
```
